```python
import jax, jax.numpy as jnp
from jax import lax
import numpy as np

D_MODEL = 1024
BATCH = 8
SEQ = 2048
DEPTH = 2
DEC_BATCH = 32
DEC_SEQ = 1
PAST_LEN = 8192
PAGE_SIZE = 128

N_POOL_LAYERS = (DEPTH + 1) // 2
N_NSA_LAYERS = DEPTH // 2

POOL_WINDOWS = (2, 4, 8, 16)
N_POOL_GROUPS = len(POOL_WINDOWS)
POOL_GROUP = D_MODEL // N_POOL_GROUPS
POOL_BUF = max(POOL_WINDOWS) - 1

N_HEADS = 16
HEAD_DIM = D_MODEL // N_HEADS
N_KV_HEADS = 4
GQA = N_HEADS // N_KV_HEADS
CMP_BLOCK = 32
SLC_BLOCK = 64
SLC_TOPK = 16
WINDOW = 512
Q_BLOCK = 64
N_BRANCH = 3
KV_W = N_KV_HEADS * HEAD_DIM
Q_W = N_HEADS * HEAD_DIM
NSA_IN_COLS = Q_W + 6 * KV_W + N_BRANCH * N_HEADS
FORCE_BONUS = 1000.0

D_FF = 3584
N_EXPERTS = 8
TOP_K = 2

EPS = 1e-6
NEG = -1e30

kernel_name = 'pool_nsa_hybrid_decode_step'


def rmsnorm(x, g):
    xf = x.astype(jnp.float32)
    y = xf * lax.rsqrt(jnp.mean(xf * xf, axis=-1, keepdims=True) + EPS)
    return (y * g.astype(jnp.float32)).astype(x.dtype)


def alibi_slopes():
    h = jnp.arange(1, N_HEADS + 1, dtype=jnp.float32)
    return jnp.exp2(-8.0 * h / N_HEADS).reshape(N_KV_HEADS, GQA)


def masked_softmax(s, mask, axes):
    s = jnp.where(mask, s, NEG)
    m = jnp.max(s, axis=axes, keepdims=True)
    p = jnp.where(mask, jnp.exp(s - m), 0.0)
    return p / jnp.maximum(jnp.sum(p, axis=axes, keepdims=True), 1e-30)


def pool_mix(h, buf, w_pool, pool_scale):
    B, Q, D = h.shape
    P = buf.shape[1]
    full = jnp.concatenate([buf, h], axis=1)
    c = jnp.pad(jnp.cumsum(full.astype(jnp.float32), axis=1), ((0, 0), (1, 0), (0, 0)))
    end = P + 1 + jnp.arange(Q)
    c_end = c[:, end]
    parts = []
    for g, w in enumerate(POOL_WINDOWS):
        sl = slice(g * POOL_GROUP, (g + 1) * POOL_GROUP)
        start = jnp.maximum(end - w, 0)
        cnt = (end - start).astype(jnp.float32)[None, :, None]
        parts.append((c_end[..., sl] - c[:, start, sl]) / cnt)
    pooled = (jnp.concatenate(parts, axis=-1) - h.astype(jnp.float32)).astype(h.dtype)
    mixed = jnp.einsum('bqgc,gce->bqge', pooled.reshape(B, Q, N_POOL_GROUPS, POOL_GROUP), w_pool)
    y = mixed.reshape(B, Q, D) * pool_scale
    return y, full[:, -POOL_BUF:]


def nsa_project(h, w_in):
    B, L, _ = h.shape
    z = h @ w_in
    q = z[..., :Q_W].reshape(B, L, N_KV_HEADS, GQA, HEAD_DIM)
    kv = z[..., Q_W:Q_W + 6 * KV_W].reshape(B, L, 6, N_KV_HEADS, HEAD_DIM)
    gates = jax.nn.sigmoid(z[..., Q_W + 6 * KV_W:].astype(jnp.float32)).reshape(B, L, N_BRANCH, N_KV_HEADS, GQA)
    return q, kv, gates


def compress(x, w_cmp):
    B, L = x.shape[:2]
    nc = L // CMP_BLOCK
    xb = x[:, :nc * CMP_BLOCK].reshape(B, nc, CMP_BLOCK, N_KV_HEADS, HEAD_DIM)
    return jnp.einsum('bnckd,ck->bnkd', xb, w_cmp)


def to_slc_blocks(x):
    B, L = x.shape[:2]
    ns = -(-L // SLC_BLOCK)
    x = jnp.pad(x, ((0, 0), (0, ns * SLC_BLOCK - L), (0, 0), (0, 0)))
    return x.reshape(B, ns, SLC_BLOCK, N_KV_HEADS, HEAD_DIM).transpose(0, 3, 1, 2, 4)


def nsa_attend(q, gates, q_pos, kc, vc, ks, vs, kw, vw, w_pos):
    f32 = jnp.float32
    scale = HEAD_DIM ** -0.5
    slopes = alibi_slopes()
    tq = q_pos.astype(f32)
    B, Q = q.shape[:2]

    nc = kc.shape[1]
    c_idx = jnp.arange(nc)
    c_end = (c_idx + 1) * CMP_BLOCK - 1
    c_mid = c_idx.astype(f32) * CMP_BLOCK + (CMP_BLOCK - 1) / 2
    s_c = jnp.einsum('bqkgd,bckd->bkgqc', q, kc, preferred_element_type=f32) * scale \
        - slopes[:, :, None, None] * (tq[:, None] - c_mid[None, :])
    p_c = masked_softmax(s_c, c_end[None, :] <= q_pos[:, None], -1)
    o_c = jnp.einsum('bkgqc,bckd->bqkgd', p_c, vc)

    ns = ks.shape[2]
    r = SLC_BLOCK // CMP_BLOCK
    imp = jnp.pad(p_c.sum(axis=2), ((0, 0), (0, 0), (0, 0), (0, ns * r - nc)))
    imp = imp.reshape(B, N_KV_HEADS, Q, ns, r).sum(-1)
    blk = q_pos // SLC_BLOCK
    j = jnp.arange(ns)[None, :]
    valid = j <= blk[:, None]
    forced = (j == 0) | (j == blk[:, None]) | (j == blk[:, None] - 1)
    score = jnp.where(valid, imp + jnp.where(forced, FORCE_BONUS, 0.0), -1.0)
    _, idx = lax.top_k(score, min(SLC_TOPK, ns))

    gather = jax.vmap(jax.vmap(lambda blocks, ids: blocks[ids]))
    gk = gather(ks, idx)
    gv = gather(vs, idx)
    pos = idx[..., None] * SLC_BLOCK + jnp.arange(SLC_BLOCK)
    dist = (tq[:, None, None] - pos.astype(f32))[:, :, None]
    s_s = jnp.einsum('bqkgd,bkqnsd->bkgqns', q, gk, preferred_element_type=f32) * scale \
        - slopes[:, :, None, None, None] * dist
    mask_s = (pos <= q_pos[:, None, None])[:, :, None]
    p_s = masked_softmax(s_s, mask_s, (-2, -1))
    o_s = jnp.einsum('bkgqns,bkqnsd->bqkgd', p_s, gv)

    wp = w_pos[None, :]
    tp = q_pos[:, None]
    s_w = jnp.einsum('bqkgd,bwkd->bkgqw', q, kw, preferred_element_type=f32) * scale \
        - slopes[:, :, None, None] * (tq[:, None] - w_pos.astype(f32)[None, :])
    mask_w = (wp <= tp) & (tp - wp < WINDOW) & (wp >= 0)
    p_w = masked_softmax(s_w, mask_w, -1)
    o_w = jnp.einsum('bkgqw,bwkd->bqkgd', p_w, vw)

    o = gates[:, :, 0, :, :, None] * o_c + gates[:, :, 1, :, :, None] * o_s + gates[:, :, 2, :, :, None] * o_w
    return o.astype(q.dtype)


def nsa_prompt(h, w_in, w_cmp_k, w_cmp_v, w_o):
    B, L, _ = h.shape
    q, kv, gates = nsa_project(h, w_in)
    kc = compress(kv[:, :, 0], w_cmp_k)
    vc = compress(kv[:, :, 1], w_cmp_v)
    ks = to_slc_blocks(kv[:, :, 2])
    vs = to_slc_blocks(kv[:, :, 3])
    kwp = jnp.pad(kv[:, :, 4:6], ((0, 0), (WINDOW, 0), (0, 0), (0, 0), (0, 0)))
    nqb = L // Q_BLOCK
    qb = q.reshape(B, nqb, Q_BLOCK, N_KV_HEADS, GQA, HEAD_DIM).transpose(1, 0, 2, 3, 4, 5)
    gb = gates.reshape(B, nqb, Q_BLOCK, N_BRANCH, N_KV_HEADS, GQA).transpose(1, 0, 2, 3, 4, 5)

    def one_block(args):
        qi, gi, i = args
        t0 = i * Q_BLOCK
        q_pos = t0 + jnp.arange(Q_BLOCK)
        band = lax.dynamic_slice_in_dim(kwp, t0, WINDOW + Q_BLOCK, axis=1)
        w_pos = t0 - WINDOW + jnp.arange(WINDOW + Q_BLOCK)
        return nsa_attend(qi, gi, q_pos, kc, vc, ks, vs, band[:, :, 0], band[:, :, 1], w_pos)

    o = lax.map(one_block, (qb, gb, jnp.arange(nqb)))
    o = o.transpose(1, 0, 2, 3, 4, 5).reshape(B, L, Q_W)
    wb = min(WINDOW, L)
    return o @ w_o, kv[:, :, :4], kv[:, L - wb:, 4:6]


def nsa_sample(h, past, win_buf, w_in, w_cmp_k, w_cmp_v, w_o):
    B, Q, _ = h.shape
    q, kv, gates = nsa_project(h, w_in)
    P = past.shape[1]
    full = jnp.concatenate([past, kv[:, :, :4]], axis=1)
    kc = compress(full[:, :, 0], w_cmp_k)
    vc = compress(full[:, :, 1], w_cmp_v)
    ks = to_slc_blocks(full[:, :, 2])
    vs = to_slc_blocks(full[:, :, 3])
    wb = win_buf.shape[1]
    win = jnp.concatenate([win_buf, kv[:, :, 4:6]], axis=1)
    w_pos = P - wb + jnp.arange(wb + Q)
    q_pos = P + jnp.arange(Q)
    o = nsa_attend(q, gates, q_pos, kc, vc, ks, vs, win[:, :, 0], win[:, :, 1], w_pos)
    keep = min(WINDOW, wb + Q)
    return o.reshape(B, Q, Q_W) @ w_o, kv[:, :, :4], win[:, wb + Q - keep:]


def swiglu(h, wg, wu, wd):
    return (jax.nn.silu(h @ wg) * (h @ wu)) @ wd


def moe(h, w_router, wg, wu, wd):
    logits = (h @ w_router).astype(jnp.float32)
    top_v, top_i = lax.top_k(logits, TOP_K)
    wts = jax.nn.softmax(top_v, axis=-1)
    gate = jnp.sum(jax.nn.one_hot(top_i, N_EXPERTS, dtype=jnp.float32) * wts[..., None], axis=-2)
    acc = jnp.zeros(h.shape, jnp.float32)
    for e in range(N_EXPERTS):
        acc = acc + gate[..., e, None] * swiglu(h, wg[e], wu[e], wd[e])
    return acc.astype(h.dtype)


def setup_inputs(seed: int = 0) -> dict:
    key = jax.random.key(seed)
    k = jax.random.split(key, 24)
    f32 = jnp.float32
    n_pages = PAST_LEN // PAGE_SIZE
    n_used = DEC_BATCH * n_pages
    n_phys = n_used + n_used // 4
    win_buf = min(WINDOW, PAST_LEN)

    def nrm(kk, shape, s):
        return jax.random.normal(kk, shape, f32) * s

    page_table = jax.random.permutation(k[5], n_phys)[:n_used].astype(jnp.int32).reshape(DEC_BATCH, n_pages)
    return {
        'x_prompt': nrm(k[0], (BATCH, SEQ, D_MODEL), 1.0),
        'x_sample': nrm(k[1], (DEC_BATCH, DEC_SEQ, D_MODEL), 1.0),
        'state_pool': nrm(k[2], (N_POOL_LAYERS, DEC_BATCH, POOL_BUF, D_MODEL), 1.0),
        'cache_kv': nrm(k[3], (N_NSA_LAYERS, n_phys, PAGE_SIZE, 4, N_KV_HEADS, HEAD_DIM), 1.0),
        'state_win': nrm(k[4], (N_NSA_LAYERS, DEC_BATCH, win_buf, 2, N_KV_HEADS, HEAD_DIM), 1.0),
        'page_table': page_table,
        'norm_mix': 1.0 + nrm(k[6], (DEPTH, D_MODEL), 0.05),
        'w_pool': nrm(k[7], (N_POOL_LAYERS, N_POOL_GROUPS, POOL_GROUP, POOL_GROUP), POOL_GROUP ** -0.5),
        'pool_scale': 0.5 + nrm(k[8], (N_POOL_LAYERS, D_MODEL), 0.05),
        'w_nsa_in': nrm(k[9], (N_NSA_LAYERS, D_MODEL, NSA_IN_COLS), D_MODEL ** -0.5),
        'w_cmp_k': (1.0 + nrm(k[10], (N_NSA_LAYERS, CMP_BLOCK, N_KV_HEADS), 0.1)) / CMP_BLOCK,
        'w_cmp_v': (1.0 + nrm(k[11], (N_NSA_LAYERS, CMP_BLOCK, N_KV_HEADS), 0.1)) / CMP_BLOCK,
        'w_nsa_out': nrm(k[12], (N_NSA_LAYERS, Q_W, D_MODEL), Q_W ** -0.5),
        'norm_ffn': 1.0 + nrm(k[13], (DEPTH, D_MODEL), 0.05),
        'w_ffn_gate': nrm(k[14], (N_POOL_LAYERS, D_MODEL, D_FF), D_MODEL ** -0.5),
        'w_ffn_up': nrm(k[15], (N_POOL_LAYERS, D_MODEL, D_FF), D_MODEL ** -0.5),
        'w_ffn_down': nrm(k[16], (N_POOL_LAYERS, D_FF, D_MODEL), D_FF ** -0.5),
        'w_router': nrm(k[17], (N_NSA_LAYERS, D_MODEL, N_EXPERTS), D_MODEL ** -0.5),
        'w_moe_gate': nrm(k[18], (N_NSA_LAYERS, N_EXPERTS, D_MODEL, D_FF), D_MODEL ** -0.5),
        'w_moe_up': nrm(k[19], (N_NSA_LAYERS, N_EXPERTS, D_MODEL, D_FF), D_MODEL ** -0.5),
        'w_moe_down': nrm(k[20], (N_NSA_LAYERS, N_EXPERTS, D_FF, D_MODEL), D_FF ** -0.5),
        'norm_final': 1.0 + nrm(k[21], (D_MODEL,), 0.05),
    }


def reference(x_prompt, x_sample, state_pool, cache_kv, state_win, page_table,
              norm_mix, w_pool, pool_scale, w_nsa_in, w_cmp_k, w_cmp_v, w_nsa_out,
              norm_ffn, w_ffn_gate, w_ffn_up, w_ffn_down,
              w_router, w_moe_gate, w_moe_up, w_moe_down, norm_final):
    xp, xs = x_prompt, x_sample
    pool_p, pool_s, kv_p, kv_s, win_p, win_s = [], [], [], [], [], []
    for i in range(DEPTH):
        j = i // 2
        hp = rmsnorm(xp, norm_mix[i])
        hs = rmsnorm(xs, norm_mix[i])
        if i % 2 == 0:
            yp, sp = pool_mix(hp, hp[:, :0], w_pool[j], pool_scale[j])
            ys, ss = pool_mix(hs, state_pool[j], w_pool[j], pool_scale[j])
            pool_p.append(sp)
            pool_s.append(ss)
        else:
            yp, kp, wp = nsa_prompt(hp, w_nsa_in[j], w_cmp_k[j], w_cmp_v[j], w_nsa_out[j])
            past = cache_kv[j, page_table]
            past = past.reshape(page_table.shape[0], -1, 4, N_KV_HEADS, HEAD_DIM)
            ys, ksn, wsn = nsa_sample(hs, past, state_win[j], w_nsa_in[j], w_cmp_k[j], w_cmp_v[j], w_nsa_out[j])
            kv_p.append(kp)
            kv_s.append(ksn)
            win_p.append(wp)
            win_s.append(wsn)
        xp = xp + yp
        xs = xs + ys
        hp = rmsnorm(xp, norm_ffn[i])
        hs = rmsnorm(xs, norm_ffn[i])
        if i % 2 == 0:
            xp = xp + swiglu(hp, w_ffn_gate[j], w_ffn_up[j], w_ffn_down[j])
            xs = xs + swiglu(hs, w_ffn_gate[j], w_ffn_up[j], w_ffn_down[j])
        else:
            xp = xp + moe(hp, w_router[j], w_moe_gate[j], w_moe_up[j], w_moe_down[j])
            xs = xs + moe(hs, w_router[j], w_moe_gate[j], w_moe_up[j], w_moe_down[j])
    y_prompt = rmsnorm(xp, norm_final)
    y_sample = rmsnorm(xs, norm_final)
    return (y_prompt, y_sample, jnp.stack(pool_p), jnp.stack(pool_s), jnp.stack(kv_p), jnp.stack(kv_s), jnp.stack(win_p), jnp.stack(win_s))
```

```python
import functools

import jax
import jax.numpy as jnp
from jax import lax
from jax.experimental import pallas as pl
from jax.experimental.pallas import tpu as pltpu

F32 = jnp.float32
BF16 = jnp.bfloat16

EPS = 1e-6
NEG = -1e30
POOL_WINDOWS = (2, 4, 8, 16)
POOL_HALO = 16
N_HEADS = 16
N_KV_HEADS = 4
GQA = N_HEADS // N_KV_HEADS
HEAD_DIM = 64
KV_W = N_KV_HEADS * HEAD_DIM
CMP_BLOCK = 32
SLC_BLOCK = 64
SLC_TOPK = 16
WINDOW = 512
N_BRANCH = 3
FORCE_BONUS = 1000.0
TOP_K = 2
LANES = 128
VMEM_LIMIT = 56 * 1024 * 1024


def _cparams(sem):
    return pltpu.CompilerParams(dimension_semantics=sem, vmem_limit_bytes=VMEM_LIMIT)


def _rmsnorm(x, g):
    ms = jnp.mean(x * x, axis=-1, keepdims=True)
    return (x * lax.rsqrt(ms + EPS)) * g


def _dot(a, b):
    return jnp.dot(a, b, preferred_element_type=F32)


def _dot_nt(a, b):
    return lax.dot_general(a, b, (((1,), (1,)), ((), ())), preferred_element_type=F32)


def _split(a):
    hi = a.astype(BF16)
    return hi, (a - hi.astype(F32)).astype(BF16)


def _dot3(a, b, nt=False):
    d = _dot_nt if nt else _dot
    ah, al = _split(a)
    bh, bl = _split(b)
    return d(ah, bh) + (d(ah, bl) + d(al, bh))


def _masked_softmax(s, mask):
    s = jnp.where(mask, s, NEG)
    m = jnp.max(s, axis=-1, keepdims=True)
    p = jnp.where(mask, jnp.exp(s - m), 0.0)
    return p / jnp.maximum(jnp.sum(p, axis=-1, keepdims=True), 1e-30)


def _head_slopes(rows, rows_per_head, first_head):
    h = first_head + lax.broadcasted_iota(jnp.int32, (rows, 1), 0) // rows_per_head
    return jnp.exp2(-0.5 * (h + 1).astype(F32))


def _pool_prompt_kernel(x_ref, halo_ref, g_ref, w_ref, sc_ref, o_ref, hl_ref, full_ref, *, tile):
    t = pl.program_id(1)
    g = g_ref[...]
    x = x_ref[0]
    h = _rmsnorm(x, g)
    hh = _rmsnorm(halo_ref[0], g)
    full_ref[0:POOL_HALO, :] = jnp.where(t > 0, hh, 0.0)
    full_ref[POOL_HALO:POOL_HALO + tile, :] = h
    row = t * tile + lax.broadcasted_iota(jnp.int32, (tile, 1), 0)
    group = x.shape[-1] // len(POOL_WINDOWS)
    parts = []
    for gi, w in enumerate(POOL_WINDOWS):
        cs = slice(gi * group, (gi + 1) * group)
        hg = h[:, cs]
        acc = hg
        for k in range(1, w):
            acc = acc + full_ref[POOL_HALO - k:POOL_HALO - k + tile, cs]
        cnt = jnp.minimum(row + 1, w).astype(F32)
        pooled = acc / cnt - hg
        parts.append(_dot(pooled.astype(BF16), w_ref[gi]))
    o_ref[0] = x + jnp.concatenate(parts, axis=-1) * sc_ref[...]

    @pl.when(t == pl.num_programs(1) - 1)
    def _():
        hl_ref[0] = h[tile - POOL_HALO:, :]


def _pool_prompt(x, g, w_pool, scale, tile=512):
    B, L, D = x.shape
    assert L % tile == 0 and tile % POOL_HALO == 0
    hb = tile // POOL_HALO
    return pl.pallas_call(
        functools.partial(_pool_prompt_kernel, tile=tile),
        grid=(B, L // tile),
        in_specs=[
            pl.BlockSpec((1, tile, D), lambda b, t: (b, t, 0)),
            pl.BlockSpec((1, POOL_HALO, D), lambda b, t: (b, jnp.maximum(t * hb - 1, 0), 0)),
            pl.BlockSpec((1, D), lambda b, t: (0, 0)),
            pl.BlockSpec(w_pool.shape, lambda b, t: (0, 0, 0)),
            pl.BlockSpec((1, D), lambda b, t: (0, 0)),
        ],
        out_specs=[
            pl.BlockSpec((1, tile, D), lambda b, t: (b, t, 0)),
            pl.BlockSpec((1, POOL_HALO, D), lambda b, t: (b, 0, 0)),
        ],
        out_shape=[jax.ShapeDtypeStruct((B, L, D), F32), jax.ShapeDtypeStruct((B, POOL_HALO, D), F32)],
        scratch_shapes=[pltpu.VMEM((tile + POOL_HALO, D), F32)],
        compiler_params=_cparams(("parallel", "arbitrary")),
        name="pool_prompt",
    )(x, x, g, w_pool, scale)


def _pool_sample_kernel(x_ref, st_ref, g_ref, w_ref, sc_ref, o_ref, h_ref):
    x = x_ref[...]
    h = _rmsnorm(x, g_ref[...])
    h_ref[...] = h
    P = st_ref.shape[0]
    group = x.shape[-1] // len(POOL_WINDOWS)
    parts = []
    for gi, w in enumerate(POOL_WINDOWS):
        cs = slice(gi * group, (gi + 1) * group)
        hg = h[:, cs]
        acc = hg
        for k in range(1, w):
            acc = acc + st_ref[P - k][:, cs]
        pooled = acc / float(w) - hg
        parts.append(_dot3(pooled, w_ref[gi]))
    o_ref[...] = x + jnp.concatenate(parts, axis=-1) * sc_ref[...]


def _pool_sample(x, state_t, g, w_pool, scale):
    B, D = x.shape
    assert state_t.shape[0] >= max(POOL_WINDOWS) - 1
    return pl.pallas_call(
        _pool_sample_kernel,
        out_shape=[jax.ShapeDtypeStruct((B, D), F32), jax.ShapeDtypeStruct((B, D), F32)],
        compiler_params=pltpu.CompilerParams(vmem_limit_bytes=VMEM_LIMIT),
        name="pool_sample",
    )(x, state_t, g, w_pool, scale)


def _ffn_kernel(*refs, n_experts, final_norm, precise):
    moe = n_experts > 1
    it = iter(refs)
    x_ref, g_ref = next(it), next(it)
    wr_ref = next(it) if moe else None
    wg_ref, wu_ref, wd_ref = next(it), next(it), next(it)
    gf_ref = next(it) if final_norm else None
    o_ref, h_scr, acc_scr = next(it), next(it), next(it)
    eacc_scr, gate_scr = (next(it), next(it)) if moe else (None, None)

    e, f = pl.program_id(1), pl.program_id(2)
    last_f = f == pl.num_programs(2) - 1

    @pl.when((e == 0) & (f == 0))
    def _():
        h = _rmsnorm(x_ref[...], g_ref[...])
        h_scr[...] = h.astype(h_scr.dtype)
        acc_scr[...] = jnp.zeros_like(acc_scr)
        if moe:
            logits = jnp.dot(h, wr_ref[...], preferred_element_type=F32, precision=lax.Precision.HIGHEST)
            lane = lax.broadcasted_iota(jnp.int32, logits.shape, 1)
            lg = jnp.where(lane < n_experts, logits, -jnp.inf)
            m1 = jnp.max(lg, axis=-1, keepdims=True)
            i1 = jnp.min(jnp.where(lg == m1, lane, LANES), axis=-1, keepdims=True)
            lg2 = jnp.where(lane == i1, -jnp.inf, lg)
            m2 = jnp.max(lg2, axis=-1, keepdims=True)
            i2 = jnp.min(jnp.where(lg2 == m2, lane, LANES), axis=-1, keepdims=True)
            e2 = jnp.exp(m2 - m1)
            den = 1.0 + e2
            gate_scr[...] = jnp.where(lane == i1, 1.0 / den, 0.0) + jnp.where(lane == i2, e2 / den, 0.0)

    mm = _dot3 if precise else _dot
    hb = h_scr[...]
    a = mm(hb, wg_ref[0])
    u = mm(hb, wu_ref[0])
    act = (a * jax.nn.sigmoid(a)) * u
    y = mm(act.astype(hb.dtype), wd_ref[0])

    if moe:
        @pl.when(f == 0)
        def _():
            eacc_scr[...] = y

        @pl.when(f > 0)
        def _():
            eacc_scr[...] += y

        @pl.when(last_f)
        def _():
            gate = gate_scr[...]
            lane = lax.broadcasted_iota(jnp.int32, gate.shape, 1)
            ge = jnp.sum(jnp.where(lane == e, gate, 0.0), axis=-1, keepdims=True)
            acc_scr[...] += ge * eacc_scr[...]
    else:
        acc_scr[...] += y

    @pl.when((e == pl.num_programs(1) - 1) & last_f)
    def _():
        out = x_ref[...] + acc_scr[...]
        if final_norm:
            out = _rmsnorm(out, gf_ref[...])
        o_ref[...] = out


def _ffn(x, g, wg, wu, wd, w_router=None, g_final=None, *, tm, tf=512):
    precise = wg.dtype == F32
    N, D = x.shape
    E, _, F = wg.shape
    assert N % tm == 0 and F % tf == 0
    moe = w_router is not None
    assert moe == (E > 1)
    final_norm = g_final is not None
    const2 = lambda i, e, f: (0, 0)
    args, in_specs = [x, g], [pl.BlockSpec((tm, D), lambda i, e, f: (i, 0)), pl.BlockSpec((1, D), const2)]
    if moe:
        args.append(w_router)
        in_specs.append(pl.BlockSpec(w_router.shape, const2))
    args += [wg, wu, wd]
    in_specs += [
        pl.BlockSpec((1, D, tf), lambda i, e, f: (e, 0, f)),
        pl.BlockSpec((1, D, tf), lambda i, e, f: (e, 0, f)),
        pl.BlockSpec((1, tf, D), lambda i, e, f: (e, f, 0)),
    ]
    if final_norm:
        args.append(g_final)
        in_specs.append(pl.BlockSpec((1, D), const2))
    scratch = [pltpu.VMEM((tm, D), F32 if precise else BF16), pltpu.VMEM((tm, D), F32)]
    if moe:
        scratch += [pltpu.VMEM((tm, D), F32), pltpu.VMEM((tm, LANES), F32)]
    return pl.pallas_call(
        functools.partial(_ffn_kernel, n_experts=E, final_norm=final_norm, precise=precise),
        grid=(N // tm, E, F // tf),
        in_specs=in_specs,
        out_specs=pl.BlockSpec((tm, D), lambda i, e, f: (i, 0)),
        out_shape=jax.ShapeDtypeStruct((N, D), F32),
        scratch_shapes=scratch,
        compiler_params=_cparams(("parallel", "arbitrary", "arbitrary")),
        name="moe_ffn" if moe else "dense_ffn",
    )(*args)


def _nsa_proj_kernel(x_ref, g_ref, wq_ref, wkv_ref, wgt_ref, wck_ref, wcv_ref,
                     q_ref, kv4_ref, kwin_ref, ks_ref, vs_ref, kw_ref, vw_ref, gt_ref, kc_ref, vc_ref):
    hb = _rmsnorm(x_ref[...], g_ref[...]).astype(BF16)
    q_ref[...] = _dot(hb, wq_ref[...]).astype(BF16)
    kv = _dot(hb, wkv_ref[...])
    kv4_ref[...] = kv[:, :4 * KV_W]
    kwin_ref[...] = kv[:, 4 * KV_W:]
    ks_ref[...] = kv[:, 2 * KV_W:3 * KV_W].astype(BF16)
    vs_ref[...] = kv[:, 3 * KV_W:4 * KV_W].astype(BF16)
    kw_ref[...] = kv[:, 4 * KV_W:5 * KV_W].astype(BF16)
    vw_ref[...] = kv[:, 5 * KV_W:].astype(BF16)
    gt_ref[...] = jax.nn.sigmoid(_dot(hb, wgt_ref[...]))
    tm = kv.shape[0]
    kc_ref[...] = (kv[:, :KV_W] * wck_ref[...]).reshape(tm // CMP_BLOCK, CMP_BLOCK, KV_W).sum(axis=1)
    vc_ref[...] = (kv[:, KV_W:2 * KV_W] * wcv_ref[...]).reshape(tm // CMP_BLOCK, CMP_BLOCK, KV_W).sum(axis=1)


def _nsa_proj(x, g, wq, wkv, wgt, wck, wcv, *, tm):
    N, D = x.shape
    assert N % tm == 0 and tm % (8 * CMP_BLOCK) == 0
    row = lambda w: pl.BlockSpec((tm, w), lambda i: (i, 0))
    full = lambda a: pl.BlockSpec(a.shape, lambda i: (0, 0))
    out_shape = [jax.ShapeDtypeStruct((N, N_HEADS * HEAD_DIM), BF16),
                 jax.ShapeDtypeStruct((N, 4 * KV_W), F32), jax.ShapeDtypeStruct((N, 2 * KV_W), F32)]
    out_shape += [jax.ShapeDtypeStruct((N, KV_W), BF16)] * 4
    out_shape += [jax.ShapeDtypeStruct((N, LANES), F32)]
    out_shape += [jax.ShapeDtypeStruct((N // CMP_BLOCK, KV_W), F32)] * 2
    out_specs = [row(N_HEADS * HEAD_DIM), row(4 * KV_W), row(2 * KV_W)] + [row(KV_W)] * 4 + [row(LANES)]
    out_specs += [pl.BlockSpec((tm // CMP_BLOCK, KV_W), lambda i: (i, 0))] * 2
    return pl.pallas_call(
        _nsa_proj_kernel,
        grid=(N // tm,),
        in_specs=[row(D), full(g), full(wq), full(wkv), full(wgt), full(wck), full(wcv)],
        out_specs=out_specs,
        out_shape=out_shape,
        compiler_params=_cparams(("parallel",)),
        name="nsa_proj_prompt",
    )(x, g, wq, wkv, wgt, wck, wcv)


def _nsa_proj_sample_kernel(x_ref, g_ref, w_ref, z_ref):
    z_ref[...] = _dot3(_rmsnorm(x_ref[...], g_ref[...]), w_ref[...])


def _nsa_proj_sample(x, g, w_in, tn=384):
    B, D = x.shape
    cols = w_in.shape[1]
    assert cols % tn == 0
    return pl.pallas_call(
        _nsa_proj_sample_kernel,
        grid=(cols // tn,),
        in_specs=[pl.BlockSpec((B, D), lambda j: (0, 0)), pl.BlockSpec((1, D), lambda j: (0, 0)),
                  pl.BlockSpec((D, tn), lambda j: (0, j))],
        out_specs=pl.BlockSpec((B, tn), lambda j: (0, j)),
        out_shape=jax.ShapeDtypeStruct((B, cols), F32),
        compiler_params=_cparams(("parallel",)),
        name="nsa_proj_sample",
    )(x, g, w_in)


def _nsa_prompt_kernel(q_ref, gt_ref, kc_ref, vc_ref, ks_ref, vs_ref, kw_ref, vw_ref, x_ref, wo_ref,
                       o_ref, o_scr, *, tq, kchunk, seq):
    t0 = pl.program_id(1) * tq
    rows = GQA * tq
    scale = HEAD_DIM ** -0.5
    n_blocks = seq // SLC_BLOCK
    top_k = min(SLC_TOPK, n_blocks)

    q = q_ref[0]
    gates = gt_ref[0]
    qpos_t = t0 + lax.broadcasted_iota(jnp.int32, (tq, 1), 0)
    qpos = jnp.concatenate([qpos_t] * GQA, axis=0)
    tq_f = qpos.astype(F32)

    lane = lax.broadcasted_iota(jnp.int32, (tq, LANES), 1)
    jblk = lane >> 1
    real = ((lane & 1) == 0) & (jblk < n_blocks)
    blk = qpos_t // SLC_BLOCK
    valid = real & (jblk <= blk)
    forced = (jblk == 0) | (jblk == blk) | (jblk == blk - 1)
    c_lane = lax.broadcasted_iota(jnp.int32, (1, LANES), 1)
    c_mid = c_lane.astype(F32) * CMP_BLOCK + (CMP_BLOCK - 1) / 2
    c_end = (c_lane + 1) * CMP_BLOCK - 1

    n_chunks = (t0 + tq + kchunk - 1) // kchunk
    wstart = pl.multiple_of(jnp.maximum(t0 - WINDOW, 0), tq)
    wlen = WINDOW + tq

    for kv in range(N_KV_HEADS):
        hs = slice(kv * HEAD_DIM, (kv + 1) * HEAD_DIM)
        qs = jnp.concatenate(
            [q[:, (kv * GQA + g) * HEAD_DIM:(kv * GQA + g + 1) * HEAD_DIM] for g in range(GQA)], axis=0)
        slope = _head_slopes(rows, tq, kv * GQA)

        s_c = _dot_nt(qs, kc_ref[0][:, hs]) * scale - slope * (tq_f - c_mid)
        p_c = _masked_softmax(s_c, c_end <= qpos)
        o_c = _dot(p_c.astype(BF16), vc_ref[0][:, hs])

        imp = p_c[0:tq]
        for g in range(1, GQA):
            imp = imp + p_c[g * tq:(g + 1) * tq]
        pair = imp + pltpu.roll(imp, LANES - 1, 1)
        score = jnp.where(valid, pair + jnp.where(forced, FORCE_BONUS, 0.0), -1.0)
        score = jnp.where(real, score, -2.0)
        rank = jnp.zeros((tq, LANES), jnp.int32)
        for k in range(n_blocks):
            col = score[:, 2 * k:2 * k + 1]
            beats = (col > score) | ((col == score) & (lane > 2 * k))
            rank = rank + beats.astype(jnp.int32)
        sel = jnp.where(real & (rank < top_k), 1.0, 0.0).astype(BF16)

        def chunk(c, carry):
            m, l, acc = carry
            k0 = pl.multiple_of(c * kchunk, kchunk)
            kb = ks_ref[0, pl.ds(k0, kchunk), hs]
            vb = vs_ref[0, pl.ds(k0, kchunk), hs]
            pos = k0 + lax.broadcasted_iota(jnp.int32, (1, kchunk), 1)
            s = _dot_nt(qs, kb) * scale - slope * (tq_f - pos.astype(F32))
            expand = (lax.broadcasted_iota(jnp.int32, (LANES, kchunk), 0)
                      == 2 * ((k0 + lax.broadcasted_iota(jnp.int32, (LANES, kchunk), 1)) // SLC_BLOCK))
            picked = _dot(sel, jnp.where(expand, 1.0, 0.0).astype(BF16))
            picked = jnp.where(pos <= qpos_t, picked, 0.0)
            mk = jnp.concatenate([picked] * GQA, axis=0) > 0.5
            s = jnp.where(mk, s, NEG)
            m_new = jnp.maximum(m, jnp.max(s, axis=-1, keepdims=True))
            alpha = jnp.exp(m - m_new)
            p = jnp.where(mk, jnp.exp(s - m_new), 0.0)
            l = alpha * l + jnp.sum(p, axis=-1, keepdims=True)
            acc = alpha * acc + _dot(p.astype(BF16), vb)
            return m_new, l, acc

        init = (jnp.full((rows, 1), NEG, F32), jnp.zeros((rows, 1), F32), jnp.zeros((rows, HEAD_DIM), F32))
        _, l_s, acc_s = lax.fori_loop(0, n_chunks, chunk, init)
        o_s = acc_s / jnp.maximum(l_s, 1e-30)

        kb = kw_ref[0, pl.ds(wstart, wlen), hs]
        vb = vw_ref[0, pl.ds(wstart, wlen), hs]
        wpos = wstart + lax.broadcasted_iota(jnp.int32, (1, wlen), 1)
        s_w = _dot_nt(qs, kb) * scale - slope * (tq_f - wpos.astype(F32))
        p_w = _masked_softmax(s_w, (wpos <= qpos) & (qpos - wpos < WINDOW))
        o_w = _dot(p_w.astype(BF16), vb)

        def gate_col(n):
            base = n * N_HEADS + kv * GQA
            return jnp.concatenate([gates[:, base + g:base + g + 1] for g in range(GQA)], axis=0)

        o = gate_col(0) * o_c + gate_col(1) * o_s + gate_col(2) * o_w
        for g in range(GQA):
            h = kv * GQA + g
            o_scr[:, h * HEAD_DIM:(h + 1) * HEAD_DIM] = o[g * tq:(g + 1) * tq]

    o_ref[0] = x_ref[0] + _dot(o_scr[...].astype(BF16), wo_ref[...])


def _nsa_prompt(q, gates, kc, vc, ks, vs, kw, vw, x, w_o, *, tq=128, kchunk=512):
    B, L, D = x.shape
    assert L % kchunk == 0 and kchunk % tq == 0 and L >= WINDOW + tq and L % SLC_BLOCK == 0
    assert 2 * (L // SLC_BLOCK) <= LANES and L // CMP_BLOCK <= LANES and kc.shape[1] == LANES
    tile = lambda w: pl.BlockSpec((1, tq, w), lambda b, t: (b, t, 0))
    whole = lambda a: pl.BlockSpec((1,) + a.shape[1:], lambda b, t: (b, 0, 0))
    return pl.pallas_call(
        functools.partial(_nsa_prompt_kernel, tq=tq, kchunk=kchunk, seq=L),
        grid=(B, L // tq),
        in_specs=[tile(N_HEADS * HEAD_DIM), tile(LANES), whole(kc), whole(vc), whole(ks), whole(vs),
                  whole(kw), whole(vw), tile(D), pl.BlockSpec(w_o.shape, lambda b, t: (0, 0))],
        out_specs=tile(D),
        out_shape=jax.ShapeDtypeStruct((B, L, D), F32),
        scratch_shapes=[pltpu.VMEM((tq, N_HEADS * HEAD_DIM), F32)],
        compiler_params=_cparams(("parallel", "arbitrary")),
        name="nsa_attn_prompt",
    )(q, gates, kc, vc, ks, vs, kw, vw, x, w_o)


def _cmp_sample_kernel(pt_ref, *refs, pages_per_step):
    page_refs, w_ref, o_ref = refs[:pages_per_step], refs[pages_per_step], refs[pages_per_step + 1]
    w = w_ref[...]
    page = w.shape[0]
    per_page = page // CMP_BLOCK
    for i, r in enumerate(page_refs):
        o_ref[0, i * per_page:(i + 1) * per_page, :] = (
            (r[0] * w).reshape(per_page, CMP_BLOCK, 2 * KV_W).sum(axis=1))


def _cmp_sample(page_table, cache, w_cat, pages_per_step=4):
    B, n_pages = page_table.shape
    _, page, width = cache.shape
    assert n_pages % pages_per_step == 0 and page % CMP_BLOCK == 0
    per_step = pages_per_step * page // CMP_BLOCK
    assert per_step % 8 == 0
    page_spec = lambda i: pl.BlockSpec(
        (1, page, 2 * KV_W), lambda b, p, pt: (pt[b, p * pages_per_step + i], 0, 0))
    return pl.pallas_call(
        functools.partial(_cmp_sample_kernel, pages_per_step=pages_per_step),
        grid_spec=pltpu.PrefetchScalarGridSpec(
            num_scalar_prefetch=1,
            grid=(B, n_pages // pages_per_step),
            in_specs=[page_spec(i) for i in range(pages_per_step)]
            + [pl.BlockSpec(w_cat.shape, lambda b, p, pt: (0, 0))],
            out_specs=pl.BlockSpec((1, per_step, 2 * KV_W), lambda b, p, pt: (b, p, 0)),
        ),
        out_shape=jax.ShapeDtypeStruct((B, n_pages * page // CMP_BLOCK, 2 * KV_W), F32),
        compiler_params=_cparams(("parallel", "arbitrary")),
        name="cmp_sample",
    )(page_table, *([cache] * pages_per_step), w_cat)


def _select_sample_kernel(q_ref, kcv_ref, oc_ref, idx_ref, *, past):
    scale = HEAD_DIM ** -0.5
    q = q_ref[0]
    kcv = kcv_ref[0]
    nc = kcv.shape[0]
    n_past_blocks = past // SLC_BLOCK
    row = lax.broadcasted_iota(jnp.int32, (N_HEADS, 1), 0)
    slope = _head_slopes(N_HEADS, 1, 0)
    c_lane = lax.broadcasted_iota(jnp.int32, (1, nc), 1)
    c_mid = c_lane.astype(F32) * CMP_BLOCK + (CMP_BLOCK - 1) / 2
    c_ok = (c_lane + 1) * CMP_BLOCK - 1 <= past
    bias = slope * (float(past) - c_mid)

    s_c = jnp.zeros((N_HEADS, nc), F32)
    for kv in range(N_KV_HEADS):
        s_kv = _dot3(q, kcv[:, kv * HEAD_DIM:(kv + 1) * HEAD_DIM], nt=True)
        s_c = jnp.where(row // GQA == kv, s_kv, s_c)
    p_c = _masked_softmax(s_c * scale - bias, c_ok)
    o_c = jnp.zeros((N_HEADS, HEAD_DIM), F32)
    for kv in range(N_KV_HEADS):
        o_kv = _dot3(p_c, kcv[:, KV_W + kv * HEAD_DIM:KV_W + (kv + 1) * HEAD_DIM])
        o_c = jnp.where(row // GQA == kv, o_kv, o_c)
    oc_ref[0] = o_c

    imp = p_c
    shift = 1
    while shift < GQA:
        imp = imp + pltpu.roll(imp, shift, 0)
        shift *= 2
    pair = imp + pltpu.roll(imp, nc - 1, 1)
    lane = lax.broadcasted_iota(jnp.int32, (N_HEADS, nc), 1)
    jblk = lane >> 1
    real = (lane & 1) == 0
    forced = (jblk == 0) | (jblk == n_past_blocks - 1)
    score = jnp.where(real, pair + jnp.where(forced, FORCE_BONUS, 0.0), -2.0)
    own = jnp.float32(FORCE_BONUS)
    rank = (own > score).astype(jnp.int32)
    for k in range(n_past_blocks):
        col = score[:, 2 * k:2 * k + 1]
        beats = (col > score) | ((col == score) & (lane > 2 * k))
        rank = rank + beats.astype(jnp.int32)
    own_rank = jnp.sum(jnp.where(real & (score >= own), 1, 0), axis=-1, keepdims=True)
    out_lane = lax.broadcasted_iota(jnp.int32, (N_HEADS, LANES), 1)
    idx = jnp.zeros((N_HEADS, LANES), jnp.int32)
    for r in range(SLC_TOPK):
        hit = jnp.sum(jnp.where(real & (rank == r), jblk, 0), axis=-1, keepdims=True)
        hit = hit + jnp.where(own_rank == r, n_past_blocks, 0)
        idx = jnp.where(out_lane == r, hit, idx)
    idx_ref[0] = idx


def _select_sample(q16, kcv, past):
    B = q16.shape[0]
    nc = kcv.shape[1]
    assert past % SLC_BLOCK == 0 and nc == past // CMP_BLOCK and nc % LANES == 0
    assert past // SLC_BLOCK + 1 > SLC_TOPK
    return pl.pallas_call(
        functools.partial(_select_sample_kernel, past=past),
        grid=(B,),
        in_specs=[pl.BlockSpec((1,) + q16.shape[1:], lambda b: (b, 0, 0)),
                  pl.BlockSpec((1,) + kcv.shape[1:], lambda b: (b, 0, 0))],
        out_specs=[pl.BlockSpec((1, N_HEADS, HEAD_DIM), lambda b: (b, 0, 0)),
                   pl.BlockSpec((1, N_HEADS, LANES), lambda b: (b, 0, 0))],
        out_shape=[jax.ShapeDtypeStruct((B, N_HEADS, HEAD_DIM), F32),
                   jax.ShapeDtypeStruct((B, N_HEADS, LANES), jnp.int32)],
        compiler_params=_cparams(("parallel",)),
        name="select_sample",
    )(q16, kcv)


def _attend_sample_kernel(pt_ref, idx_ref, q_ref, oc_ref, gt_ref, new_ref, win_ref, cache_ref,
                          o_ref, kbuf, vbuf, sems, *, past):
    b = pl.program_id(0)
    scale = HEAD_DIM ** -0.5
    n_past_blocks = past // SLC_BLOCK
    blocks_per_page = cache_ref.shape[1] // SLC_BLOCK
    nkeys = SLC_TOPK * SLC_BLOCK

    def block_copies(kv, n):
        blk = idx_ref[(b * N_KV_HEADS + kv) * SLC_TOPK + n]
        in_past = blk < n_past_blocks
        safe = jnp.minimum(blk, n_past_blocks - 1)
        page = pt_ref[b, safe // blocks_per_page]
        off = pl.multiple_of((safe % blocks_per_page) * SLC_BLOCK, SLC_BLOCK)
        dst = pl.ds(n * SLC_BLOCK, SLC_BLOCK)
        ck = pltpu.make_async_copy(cache_ref.at[page, pl.ds(off, SLC_BLOCK), 2, kv, :],
                                   kbuf.at[kv, dst, :], sems.at[0, kv, n])
        cv = pltpu.make_async_copy(cache_ref.at[page, pl.ds(off, SLC_BLOCK), 3, kv, :],
                                   vbuf.at[kv, dst, :], sems.at[1, kv, n])
        return blk, in_past, ck, cv

    for kv in range(N_KV_HEADS):
        for n in range(SLC_TOPK):
            _, in_past, ck, cv = block_copies(kv, n)

            @pl.when(in_past)
            def _():
                ck.start()
                cv.start()

            @pl.when(jnp.logical_not(in_past))
            def _():
                pad = jnp.zeros((SLC_BLOCK - 8, HEAD_DIM), F32)
                first = lax.broadcasted_iota(jnp.int32, (8, HEAD_DIM), 0) == 0
                kbuf[kv, n * SLC_BLOCK:n * SLC_BLOCK + 8, :] = jnp.where(first, new_ref[0, 2, kv:kv + 1, :], 0.0)
                vbuf[kv, n * SLC_BLOCK:n * SLC_BLOCK + 8, :] = jnp.where(first, new_ref[0, 3, kv:kv + 1, :], 0.0)
                kbuf[kv, n * SLC_BLOCK + 8:(n + 1) * SLC_BLOCK, :] = pad
                vbuf[kv, n * SLC_BLOCK + 8:(n + 1) * SLC_BLOCK, :] = pad

    q = q_ref[0]
    row = lax.broadcasted_iota(jnp.int32, (N_HEADS, 1), 0)
    slope = _head_slopes(N_HEADS, 1, 0)
    key_lane = lax.broadcasted_iota(jnp.int32, (1, nkeys), 1)

    win = win_ref[0]
    wb = win.shape[0]
    w_lane = lax.broadcasted_iota(jnp.int32, (1, wb), 1)
    wpos = past - wb + w_lane
    w_ok = (past - wpos < WINDOW) & (wpos >= 0)
    w_bias = slope * (past - wpos).astype(F32)
    s_w = jnp.zeros((N_HEADS, wb), F32)
    s_n = jnp.zeros((N_HEADS, 1), F32)
    for kv in range(N_KV_HEADS):
        mine = row // GQA == kv
        s_kv = _dot3(q, win[:, kv * HEAD_DIM:(kv + 1) * HEAD_DIM], nt=True)
        s_w = jnp.where(mine, s_kv, s_w)
        s_n = jnp.where(mine, jnp.sum(q * new_ref[0, 4, kv:kv + 1, :], axis=-1, keepdims=True), s_n)
    s_w = jnp.where(w_ok, s_w * scale - w_bias, NEG)
    s_n = s_n * scale
    m_w = jnp.maximum(jnp.max(s_w, axis=-1, keepdims=True), s_n)
    p_w = jnp.where(w_ok, jnp.exp(s_w - m_w), 0.0)
    p_n = jnp.exp(s_n - m_w)
    l_w = jnp.maximum(jnp.sum(p_w, axis=-1, keepdims=True) + p_n, 1e-30)
    p_w = p_w / l_w
    p_n = p_n / l_w
    o_w = jnp.zeros((N_HEADS, HEAD_DIM), F32)
    for kv in range(N_KV_HEADS):
        o_kv = (_dot3(p_w, win[:, KV_W + kv * HEAD_DIM:KV_W + (kv + 1) * HEAD_DIM])
                + p_n * new_ref[0, 5, kv:kv + 1, :])
        o_w = jnp.where(row // GQA == kv, o_kv, o_w)

    for kv in range(N_KV_HEADS):
        for n in range(SLC_TOPK):
            _, in_past, ck, cv = block_copies(kv, n)

            @pl.when(in_past)
            def _():
                ck.wait()
                cv.wait()

    s_s = jnp.zeros((N_HEADS, nkeys), F32)
    pos = jnp.zeros((N_HEADS, nkeys), jnp.int32)
    for kv in range(N_KV_HEADS):
        mine = row // GQA == kv
        s_kv = _dot3(q, kbuf[kv], nt=True)
        s_s = jnp.where(mine, s_kv, s_s)
        pos_kv = key_lane % SLC_BLOCK
        for n in range(SLC_TOPK):
            blk = idx_ref[(b * N_KV_HEADS + kv) * SLC_TOPK + n]
            pos_kv = pos_kv + jnp.where(key_lane // SLC_BLOCK == n, blk * SLC_BLOCK, 0)
        pos = jnp.where(mine, pos_kv, pos)
    p_s = _masked_softmax(s_s * scale - slope * (past - pos).astype(F32), pos <= past)
    o_s = jnp.zeros((N_HEADS, HEAD_DIM), F32)
    for kv in range(N_KV_HEADS):
        o_kv = _dot3(p_s, vbuf[kv])
        o_s = jnp.where(row // GQA == kv, o_kv, o_s)

    gt = jax.nn.sigmoid(gt_ref[0])
    o_ref[0] = gt[:, 0:1] * oc_ref[0] + gt[:, 1:2] * o_s + gt[:, 2:3] * o_w


def _attend_sample(page_table, idx, q16, o_c, gates, kv_new, win, cache5, past):
    B = q16.shape[0]
    assert win.shape[1] <= WINDOW and cache5.shape[1] % SLC_BLOCK == 0
    blk3 = lambda a: pl.BlockSpec((1,) + a.shape[1:], lambda b, pt, ix: (b,) + (0,) * (a.ndim - 1))
    return pl.pallas_call(
        functools.partial(_attend_sample_kernel, past=past),
        grid_spec=pltpu.PrefetchScalarGridSpec(
            num_scalar_prefetch=2,
            grid=(B,),
            in_specs=[blk3(q16), blk3(o_c), blk3(gates), blk3(kv_new), blk3(win),
                      pl.BlockSpec(memory_space=pl.ANY)],
            out_specs=pl.BlockSpec((1, N_HEADS, HEAD_DIM), lambda b, pt, ix: (b, 0, 0)),
            scratch_shapes=[pltpu.VMEM((N_KV_HEADS, SLC_TOPK * SLC_BLOCK, HEAD_DIM), F32),
                            pltpu.VMEM((N_KV_HEADS, SLC_TOPK * SLC_BLOCK, HEAD_DIM), F32),
                            pltpu.SemaphoreType.DMA((2, N_KV_HEADS, SLC_TOPK))],
        ),
        out_shape=jax.ShapeDtypeStruct((B, N_HEADS, HEAD_DIM), F32),
        compiler_params=_cparams(("arbitrary",)),
        name="attend_sample",
    )(page_table, idx, q16, o_c, gates, kv_new, win, cache5)


def _linear_res_kernel(x_ref, a_ref, w_ref, o_ref):
    o_ref[...] = x_ref[...] + _dot3(a_ref[...], w_ref[...])


def _linear_res(x, a, w):
    return pl.pallas_call(
        _linear_res_kernel,
        out_shape=jax.ShapeDtypeStruct(x.shape, F32),
        compiler_params=pltpu.CompilerParams(vmem_limit_bytes=VMEM_LIMIT),
        name="out_proj_sample",
    )(x, a, w)


def _expand_cmp_weights(w_cmp, rows):
    w = jnp.repeat(w_cmp, HEAD_DIM, axis=1)
    return jnp.tile(w, (rows // CMP_BLOCK, 1))


def _row(v):
    return v.reshape(1, -1)


def _prompt_mixers(x, norm_mix, w_pool, pool_scale, w_in, w_cmp_k, w_cmp_v, w_out, norm_ffn0, wfg, wfu, wfd):
    B, L, D = x.shape
    q_w = N_HEADS * HEAD_DIM
    x, h_last = _pool_prompt(x, _row(norm_mix[0]), w_pool.astype(BF16), _row(pool_scale))
    new_pool = h_last[None, :, POOL_HALO - (max(POOL_WINDOWS) - 1):, :]
    x = _ffn(x.reshape(B * L, D), _row(norm_ffn0), wfg.astype(BF16), wfu.astype(BF16), wfd.astype(BF16), tm=1024)

    tm = 512
    wq = w_in[:, :q_w].astype(BF16)
    wkv = w_in[:, q_w:q_w + 6 * KV_W].astype(BF16)
    wgt = jnp.pad(w_in[:, q_w + 6 * KV_W:], ((0, 0), (0, LANES - N_BRANCH * N_HEADS))).astype(BF16)
    q, kv4, kwin, ks, vs, kw, vw, gates, kc, vc = _nsa_proj(
        x, _row(norm_mix[1]), wq, wkv, wgt, _expand_cmp_weights(w_cmp_k, tm), _expand_cmp_weights(w_cmp_v, tm),
        tm=tm)
    new_kv = kv4.reshape(1, B, L, 4, N_KV_HEADS, HEAD_DIM)
    new_win = kwin.reshape(B, L, 2, N_KV_HEADS, HEAD_DIM)[None, :, L - min(WINDOW, L):]
    nc = L // CMP_BLOCK
    pad_c = lambda a: jnp.pad(a.reshape(B, nc, KV_W), ((0, 0), (0, LANES - nc), (0, 0))).astype(BF16)
    per_seq = lambda a: a.reshape(B, L, a.shape[-1])
    x = _nsa_prompt(per_seq(q), per_seq(gates), pad_c(kc), pad_c(vc), per_seq(ks), per_seq(vs),
                    per_seq(kw), per_seq(vw), per_seq(x), w_out.astype(BF16))
    return x.reshape(B * L, D), new_pool, new_kv, new_win


def _sample_mixers(x, state_pool, cache, state_win, page_table, norm_mix, w_pool, pool_scale, w_in, w_cmp_k,
                   w_cmp_v, w_out, norm_ffn0, wfg, wfu, wfd):
    SB, D = x.shape
    n_phys, page = cache.shape[:2]
    past = page_table.shape[1] * page
    q_w = N_HEADS * HEAD_DIM
    x, h = _pool_sample(x, state_pool.transpose(1, 0, 2), _row(norm_mix[0]), w_pool, _row(pool_scale))
    new_pool = jnp.concatenate([state_pool[:, 1:], h[:, None]], axis=1)[None]
    x = _ffn(x, _row(norm_ffn0), wfg, wfu, wfd, tm=SB)

    cols = w_in.shape[1]
    z = _nsa_proj_sample(x, _row(norm_mix[1]), jnp.pad(w_in, ((0, 0), (0, -cols % LANES))))
    kv_new = z[:, q_w:q_w + 6 * KV_W].reshape(SB, 6, N_KV_HEADS, HEAD_DIM)
    new_kv = kv_new[None, :, None, :4]
    keep = min(WINDOW, state_win.shape[1] + 1)
    new_win = jnp.concatenate([state_win, kv_new[:, None, 4:]], axis=1)[None, :, -keep:]

    w_cat = jnp.concatenate([_expand_cmp_weights(w_cmp_k, page), _expand_cmp_weights(w_cmp_v, page)], axis=1)
    kcv = _cmp_sample(page_table, cache.reshape(n_phys, page, 4 * KV_W), w_cat)
    q16 = z[:, :q_w].reshape(SB, N_HEADS, HEAD_DIM)
    o_c, idx = _select_sample(q16, kcv, past)
    idx = idx[:, GQA - 1::GQA, :SLC_TOPK].reshape(-1)
    gate_logits = z[:, q_w + 6 * KV_W:cols].reshape(SB, N_BRANCH, N_HEADS).transpose(0, 2, 1)
    gate_logits = jnp.pad(gate_logits, ((0, 0), (0, 0), (0, LANES - N_BRANCH)))
    o = _attend_sample(page_table, idx, q16, o_c, gate_logits, kv_new,
                       state_win.reshape(SB, state_win.shape[1], 2 * KV_W), cache, past)
    return _linear_res(x, o.reshape(SB, q_w), w_out), new_pool, new_kv, new_win


def kernel(x_prompt, x_sample, state_pool, cache_kv, state_win, page_table, norm_mix, w_pool, pool_scale,
           w_nsa_in, w_cmp_k, w_cmp_v, w_nsa_out, norm_ffn, w_ffn_gate, w_ffn_up, w_ffn_down, w_router,
           w_moe_gate, w_moe_up, w_moe_down, norm_final):
    assert x_sample.shape[1] == 1 and norm_mix.shape[0] == 2 and x_prompt.shape[-1] == N_HEADS * HEAD_DIM
    w_rt = jnp.pad(w_router[0], ((0, 0), (0, LANES - w_router.shape[-1])))
    moe_w = (w_moe_gate[0].astype(BF16), w_moe_up[0].astype(BF16), w_moe_down[0].astype(BF16))

    xp, new_pool_prompt, new_kv_prompt, new_win_prompt = _prompt_mixers(
        x_prompt, norm_mix, w_pool[0], pool_scale[0], w_nsa_in[0], w_cmp_k[0], w_cmp_v[0], w_nsa_out[0],
        norm_ffn[0], w_ffn_gate, w_ffn_up, w_ffn_down)
    xs, new_pool_sample, new_kv_sample, new_win_sample = _sample_mixers(
        x_sample[:, 0], state_pool[0], cache_kv[0], state_win[0], page_table, norm_mix, w_pool[0], pool_scale[0],
        w_nsa_in[0], w_cmp_k[0], w_cmp_v[0], w_nsa_out[0], norm_ffn[0], w_ffn_gate, w_ffn_up, w_ffn_down)

    yp = _ffn(xp, _row(norm_ffn[1]), *moe_w, w_rt, _row(norm_final), tm=1024)
    ys = _ffn(xs, _row(norm_ffn[1]), *moe_w, w_rt, _row(norm_final), tm=xs.shape[0])
    return (yp.reshape(x_prompt.shape), ys.reshape(x_sample.shape), new_pool_prompt, new_pool_sample,
            new_kv_prompt, new_kv_sample, new_win_prompt, new_win_sample)
```

```python
import functools

import jax
import jax.numpy as jnp
from jax import lax
from jax.experimental import pallas as pl
from jax.experimental.pallas import tpu as pltpu

F32 = jnp.float32
BF16 = jnp.bfloat16

EPS = 1e-6
NEG = -1e30
POOL_WINDOWS = (2, 4, 8, 16)
POOL_HALO = 16
N_HEADS = 16
N_KV_HEADS = 4
GQA = N_HEADS // N_KV_HEADS
HEAD_DIM = 64
KV_W = N_KV_HEADS * HEAD_DIM
CMP_BLOCK = 32
SLC_BLOCK = 64
SLC_TOPK = 16
WINDOW = 512
N_BRANCH = 3
FORCE_BONUS = 1000.0
TOP_K = 2
LANES = 128
VMEM_LIMIT = 56 * 1024 * 1024


def _cparams(sem):
    return pltpu.CompilerParams(dimension_semantics=sem, vmem_limit_bytes=VMEM_LIMIT)


def _rmsnorm(x, g):
    ms = jnp.mean(x * x, axis=-1, keepdims=True)
    return (x * lax.rsqrt(ms + EPS)) * g


def _dot(a, b):
    return jnp.dot(a, b, preferred_element_type=F32)


def _dot_nt(a, b):
    return lax.dot_general(a, b, (((1,), (1,)), ((), ())), preferred_element_type=F32)


def _split(a):
    hi = a.astype(BF16)
    return hi, (a - hi.astype(F32)).astype(BF16)


def _dot3(a, b, nt=False):
    d = _dot_nt if nt else _dot
    ah, al = _split(a)
    bh, bl = _split(b)
    return d(ah, bh) + (d(ah, bl) + d(al, bh))


def _masked_softmax(s, mask):
    s = jnp.where(mask, s, NEG)
    m = jnp.max(s, axis=-1, keepdims=True)
    p = jnp.where(mask, jnp.exp(s - m), 0.0)
    return p / jnp.maximum(jnp.sum(p, axis=-1, keepdims=True), 1e-30)


def _head_slopes(rows, rows_per_head, first_head):
    h = first_head + lax.broadcasted_iota(jnp.int32, (rows, 1), 0) // rows_per_head
    return jnp.exp2(-0.5 * (h + 1).astype(F32))


def _pool_prompt_kernel(x_ref, halo_ref, g_ref, w_ref, sc_ref, o_ref, hl_ref, full_ref, *, tile):
    t = pl.program_id(1)
    g = g_ref[...]
    x = x_ref[0]
    h = _rmsnorm(x, g)
    hh = _rmsnorm(halo_ref[0], g)
    full_ref[0:POOL_HALO, :] = jnp.where(t > 0, hh, 0.0)
    full_ref[POOL_HALO:POOL_HALO + tile, :] = h
    row = t * tile + lax.broadcasted_iota(jnp.int32, (tile, 1), 0)
    group = x.shape[-1] // len(POOL_WINDOWS)
    parts = []
    for gi, w in enumerate(POOL_WINDOWS):
        cs = slice(gi * group, (gi + 1) * group)
        hg = h[:, cs]
        acc = hg
        for k in range(1, w):
            acc = acc + full_ref[POOL_HALO - k:POOL_HALO - k + tile, cs]
        cnt = jnp.minimum(row + 1, w).astype(F32)
        pooled = acc / cnt - hg
        parts.append(_dot(pooled.astype(BF16), w_ref[gi]))
    o_ref[0] = x + jnp.concatenate(parts, axis=-1) * sc_ref[...]

    @pl.when(t == pl.num_programs(1) - 1)
    def _():
        hl_ref[0] = h[tile - POOL_HALO:, :]


def _pool_prompt(x, g, w_pool, scale, tile=512):
    B, L, D = x.shape
    assert L % tile == 0 and tile % POOL_HALO == 0
    hb = tile // POOL_HALO
    return pl.pallas_call(
        functools.partial(_pool_prompt_kernel, tile=tile),
        grid=(B, L // tile),
        in_specs=[
            pl.BlockSpec((1, tile, D), lambda b, t: (b, t, 0)),
            pl.BlockSpec((1, POOL_HALO, D), lambda b, t: (b, jnp.maximum(t * hb - 1, 0), 0)),
            pl.BlockSpec((1, D), lambda b, t: (0, 0)),
            pl.BlockSpec(w_pool.shape, lambda b, t: (0, 0, 0)),
            pl.BlockSpec((1, D), lambda b, t: (0, 0)),
        ],
        out_specs=[
            pl.BlockSpec((1, tile, D), lambda b, t: (b, t, 0)),
            pl.BlockSpec((1, POOL_HALO, D), lambda b, t: (b, 0, 0)),
        ],
        out_shape=[jax.ShapeDtypeStruct((B, L, D), F32), jax.ShapeDtypeStruct((B, POOL_HALO, D), F32)],
        scratch_shapes=[pltpu.VMEM((tile + POOL_HALO, D), F32)],
        compiler_params=_cparams(("parallel", "arbitrary")),
        name="pool_prompt",
    )(x, x, g, w_pool, scale)


def _pool_sample_kernel(x_ref, st_ref, g_ref, w_ref, sc_ref, o_ref, h_ref):
    x = x_ref[...]
    h = _rmsnorm(x, g_ref[...])
    h_ref[...] = h
    P = st_ref.shape[0]
    group = x.shape[-1] // len(POOL_WINDOWS)
    parts = []
    for gi, w in enumerate(POOL_WINDOWS):
        cs = slice(gi * group, (gi + 1) * group)
        hg = h[:, cs]
        acc = hg
        for k in range(1, w):
            acc = acc + st_ref[P - k][:, cs]
        pooled = acc / float(w) - hg
        parts.append(_dot3(pooled, w_ref[gi]))
    o_ref[...] = x + jnp.concatenate(parts, axis=-1) * sc_ref[...]


def _pool_sample(x, state_t, g, w_pool, scale):
    B, D = x.shape
    assert state_t.shape[0] >= max(POOL_WINDOWS) - 1
    return pl.pallas_call(
        _pool_sample_kernel,
        out_shape=[jax.ShapeDtypeStruct((B, D), F32), jax.ShapeDtypeStruct((B, D), F32)],
        compiler_params=pltpu.CompilerParams(vmem_limit_bytes=VMEM_LIMIT),
        name="pool_sample",
    )(x, state_t, g, w_pool, scale)


def _ffn_kernel(*refs, n_experts, final_norm, precise):
    moe = n_experts > 1
    it = iter(refs)
    x_ref, g_ref = next(it), next(it)
    wr_ref = next(it) if moe else None
    wg_ref, wu_ref, wd_ref = next(it), next(it), next(it)
    gf_ref = next(it) if final_norm else None
    o_ref, h_scr, acc_scr = next(it), next(it), next(it)
    eacc_scr, gate_scr = (next(it), next(it)) if moe else (None, None)

    e, f = pl.program_id(1), pl.program_id(2)
    last_f = f == pl.num_programs(2) - 1

    @pl.when((e == 0) & (f == 0))
    def _():
        h = _rmsnorm(x_ref[...], g_ref[...])
        h_scr[...] = h.astype(h_scr.dtype)
        acc_scr[...] = jnp.zeros_like(acc_scr)
        if moe:
            logits = jnp.dot(h, wr_ref[...], preferred_element_type=F32, precision=lax.Precision.HIGHEST)
            lane = lax.broadcasted_iota(jnp.int32, logits.shape, 1)
            lg = jnp.where(lane < n_experts, logits, -jnp.inf)
            m1 = jnp.max(lg, axis=-1, keepdims=True)
            i1 = jnp.min(jnp.where(lg == m1, lane, LANES), axis=-1, keepdims=True)
            lg2 = jnp.where(lane == i1, -jnp.inf, lg)
            m2 = jnp.max(lg2, axis=-1, keepdims=True)
            i2 = jnp.min(jnp.where(lg2 == m2, lane, LANES), axis=-1, keepdims=True)
            e2 = jnp.exp(m2 - m1)
            den = 1.0 + e2
            gate_scr[...] = jnp.where(lane == i1, 1.0 / den, 0.0) + jnp.where(lane == i2, e2 / den, 0.0)

    mm = _dot3 if precise else _dot
    hb = h_scr[...]
    a = mm(hb, wg_ref[0])
    u = mm(hb, wu_ref[0])
    act = (a * jax.nn.sigmoid(a)) * u
    y = mm(act.astype(hb.dtype), wd_ref[0])

    if moe:
        @pl.when(f == 0)
        def _():
            eacc_scr[...] = y

        @pl.when(f > 0)
        def _():
            eacc_scr[...] += y

        @pl.when(last_f)
        def _():
            gate = gate_scr[...]
            lane = lax.broadcasted_iota(jnp.int32, gate.shape, 1)
            ge = jnp.sum(jnp.where(lane == e, gate, 0.0), axis=-1, keepdims=True)
            acc_scr[...] += ge * eacc_scr[...]
    else:
        acc_scr[...] += y

    @pl.when((e == pl.num_programs(1) - 1) & last_f)
    def _():
        out = x_ref[...] + acc_scr[...]
        if final_norm:
            out = _rmsnorm(out, gf_ref[...])
        o_ref[...] = out


def _ffn(x, g, wg, wu, wd, w_router=None, g_final=None, *, tm, tf=512):
    precise = wg.dtype == F32
    N, D = x.shape
    E, _, F = wg.shape
    assert N % tm == 0 and F % tf == 0
    moe = w_router is not None
    assert moe == (E > 1)
    final_norm = g_final is not None
    const2 = lambda i, e, f: (0, 0)
    args, in_specs = [x, g], [pl.BlockSpec((tm, D), lambda i, e, f: (i, 0)), pl.BlockSpec((1, D), const2)]
    if moe:
        args.append(w_router)
        in_specs.append(pl.BlockSpec(w_router.shape, const2))
    args += [wg, wu, wd]
    in_specs += [
        pl.BlockSpec((1, D, tf), lambda i, e, f: (e, 0, f)),
        pl.BlockSpec((1, D, tf), lambda i, e, f: (e, 0, f)),
        pl.BlockSpec((1, tf, D), lambda i, e, f: (e, f, 0)),
    ]
    if final_norm:
        args.append(g_final)
        in_specs.append(pl.BlockSpec((1, D), const2))
    scratch = [pltpu.VMEM((tm, D), F32 if precise else BF16), pltpu.VMEM((tm, D), F32)]
    if moe:
        scratch += [pltpu.VMEM((tm, D), F32), pltpu.VMEM((tm, LANES), F32)]
    return pl.pallas_call(
        functools.partial(_ffn_kernel, n_experts=E, final_norm=final_norm, precise=precise),
        grid=(N // tm, E, F // tf),
        in_specs=in_specs,
        out_specs=pl.BlockSpec((tm, D), lambda i, e, f: (i, 0)),
        out_shape=jax.ShapeDtypeStruct((N, D), F32),
        scratch_shapes=scratch,
        compiler_params=_cparams(("parallel", "arbitrary", "arbitrary")),
        name="moe_ffn" if moe else "dense_ffn",
    )(*args)


def _nsa_proj_kernel(x_ref, g_ref, wq_ref, wkv_ref, wgt_ref, wck_ref, wcv_ref,
                     q_ref, kv4_ref, kwin_ref, ks_ref, vs_ref, kw_ref, vw_ref, gt_ref, kc_ref, vc_ref):
    hb = _rmsnorm(x_ref[...], g_ref[...]).astype(BF16)
    q_ref[...] = _dot(hb, wq_ref[...]).astype(BF16)
    kv = _dot(hb, wkv_ref[...])
    kv4_ref[...] = kv[:, :4 * KV_W]
    kwin_ref[...] = kv[:, 4 * KV_W:]
    ks_ref[...] = kv[:, 2 * KV_W:3 * KV_W].astype(BF16)
    vs_ref[...] = kv[:, 3 * KV_W:4 * KV_W].astype(BF16)
    kw_ref[...] = kv[:, 4 * KV_W:5 * KV_W].astype(BF16)
    vw_ref[...] = kv[:, 5 * KV_W:].astype(BF16)
    gt_ref[...] = jax.nn.sigmoid(_dot(hb, wgt_ref[...]))
    tm = kv.shape[0]
    kc_ref[...] = (kv[:, :KV_W] * wck_ref[...]).reshape(tm // CMP_BLOCK, CMP_BLOCK, KV_W).sum(axis=1)
    vc_ref[...] = (kv[:, KV_W:2 * KV_W] * wcv_ref[...]).reshape(tm // CMP_BLOCK, CMP_BLOCK, KV_W).sum(axis=1)


def _nsa_proj(x, g, wq, wkv, wgt, wck, wcv, *, tm):
    N, D = x.shape
    assert N % tm == 0 and tm % (8 * CMP_BLOCK) == 0
    row = lambda w: pl.BlockSpec((tm, w), lambda i: (i, 0))
    full = lambda a: pl.BlockSpec(a.shape, lambda i: (0, 0))
    out_shape = [jax.ShapeDtypeStruct((N, N_HEADS * HEAD_DIM), BF16),
                 jax.ShapeDtypeStruct((N, 4 * KV_W), F32), jax.ShapeDtypeStruct((N, 2 * KV_W), F32)]
    out_shape += [jax.ShapeDtypeStruct((N, KV_W), BF16)] * 4
    out_shape += [jax.ShapeDtypeStruct((N, LANES), F32)]
    out_shape += [jax.ShapeDtypeStruct((N // CMP_BLOCK, KV_W), F32)] * 2
    out_specs = [row(N_HEADS * HEAD_DIM), row(4 * KV_W), row(2 * KV_W)] + [row(KV_W)] * 4 + [row(LANES)]
    out_specs += [pl.BlockSpec((tm // CMP_BLOCK, KV_W), lambda i: (i, 0))] * 2
    return pl.pallas_call(
        _nsa_proj_kernel,
        grid=(N // tm,),
        in_specs=[row(D), full(g), full(wq), full(wkv), full(wgt), full(wck), full(wcv)],
        out_specs=out_specs,
        out_shape=out_shape,
        compiler_params=_cparams(("parallel",)),
        name="nsa_proj_prompt",
    )(x, g, wq, wkv, wgt, wck, wcv)


def _nsa_proj_sample_kernel(x_ref, g_ref, w_ref, z_ref):
    z_ref[...] = _dot3(_rmsnorm(x_ref[...], g_ref[...]), w_ref[...])


def _nsa_proj_sample(x, g, w_in, tn=384):
    B, D = x.shape
    cols = w_in.shape[1]
    assert cols % tn == 0
    return pl.pallas_call(
        _nsa_proj_sample_kernel,
        grid=(cols // tn,),
        in_specs=[pl.BlockSpec((B, D), lambda j: (0, 0)), pl.BlockSpec((1, D), lambda j: (0, 0)),
                  pl.BlockSpec((D, tn), lambda j: (0, j))],
        out_specs=pl.BlockSpec((B, tn), lambda j: (0, j)),
        out_shape=jax.ShapeDtypeStruct((B, cols), F32),
        compiler_params=_cparams(("parallel",)),
        name="nsa_proj_sample",
    )(x, g, w_in)


def _nsa_prompt_kernel(q_ref, gt_ref, kc_ref, vc_ref, ks_ref, vs_ref, kw_ref, vw_ref, x_ref, wo_ref,
                       o_ref, o_scr, *, tq, kchunk, seq):
    t0 = pl.program_id(1) * tq
    rows = GQA * tq
    scale = HEAD_DIM ** -0.5
    n_blocks = seq // SLC_BLOCK
    top_k = min(SLC_TOPK, n_blocks)

    q = q_ref[0]
    gates = gt_ref[0]
    qpos_t = t0 + lax.broadcasted_iota(jnp.int32, (tq, 1), 0)
    qpos = jnp.concatenate([qpos_t] * GQA, axis=0)
    tq_f = qpos.astype(F32)

    lane = lax.broadcasted_iota(jnp.int32, (tq, LANES), 1)
    jblk = lane >> 1
    real = ((lane & 1) == 0) & (jblk < n_blocks)
    blk = qpos_t // SLC_BLOCK
    valid = real & (jblk <= blk)
    forced = (jblk == 0) | (jblk == blk) | (jblk == blk - 1)
    c_lane = lax.broadcasted_iota(jnp.int32, (1, LANES), 1)
    c_mid = c_lane.astype(F32) * CMP_BLOCK + (CMP_BLOCK - 1) / 2
    c_end = (c_lane + 1) * CMP_BLOCK - 1

    n_chunks = (t0 + tq + kchunk - 1) // kchunk
    wstart = pl.multiple_of(jnp.maximum(t0 - WINDOW, 0), tq)
    wlen = WINDOW + tq

    for kv in range(N_KV_HEADS):
        hs = slice(kv * HEAD_DIM, (kv + 1) * HEAD_DIM)
        qs = jnp.concatenate(
            [q[:, (kv * GQA + g) * HEAD_DIM:(kv * GQA + g + 1) * HEAD_DIM] for g in range(GQA)], axis=0)
        slope = _head_slopes(rows, tq, kv * GQA)

        s_c = _dot_nt(qs, kc_ref[0][:, hs]) * scale - slope * (tq_f - c_mid)
        p_c = _masked_softmax(s_c, c_end <= qpos)
        o_c = _dot(p_c.astype(BF16), vc_ref[0][:, hs])

        imp = p_c[0:tq]
        for g in range(1, GQA):
            imp = imp + p_c[g * tq:(g + 1) * tq]
        pair = imp + pltpu.roll(imp, LANES - 1, 1)
        score = jnp.where(valid, pair + jnp.where(forced, FORCE_BONUS, 0.0), -1.0)
        score = jnp.where(real, score, -2.0)
        rank = jnp.zeros((tq, LANES), jnp.int32)
        for k in range(n_blocks):
            col = score[:, 2 * k:2 * k + 1]
            beats = (col > score) | ((col == score) & (lane > 2 * k))
            rank = rank + beats.astype(jnp.int32)
        sel = jnp.where(real & (rank < top_k), 1.0, 0.0).astype(BF16)

        def chunk(c, carry):
            m, l, acc = carry
            k0 = pl.multiple_of(c * kchunk, kchunk)
            kb = ks_ref[0, pl.ds(k0, kchunk), hs]
            vb = vs_ref[0, pl.ds(k0, kchunk), hs]
            pos = k0 + lax.broadcasted_iota(jnp.int32, (1, kchunk), 1)
            s = _dot_nt(qs, kb) * scale - slope * (tq_f - pos.astype(F32))
            expand = (lax.broadcasted_iota(jnp.int32, (LANES, kchunk), 0)
                      == 2 * ((k0 + lax.broadcasted_iota(jnp.int32, (LANES, kchunk), 1)) // SLC_BLOCK))
            picked = _dot(sel, jnp.where(expand, 1.0, 0.0).astype(BF16))
            picked = jnp.where(pos <= qpos_t, picked, 0.0)
            mk = jnp.concatenate([picked] * GQA, axis=0) > 0.5
            s = jnp.where(mk, s, NEG)
            m_new = jnp.maximum(m, jnp.max(s, axis=-1, keepdims=True))
            alpha = jnp.exp(m - m_new)
            p = jnp.where(mk, jnp.exp(s - m_new), 0.0)
            l = alpha * l + jnp.sum(p, axis=-1, keepdims=True)
            acc = alpha * acc + _dot(p.astype(BF16), vb)
            return m_new, l, acc

        init = (jnp.full((rows, 1), NEG, F32), jnp.zeros((rows, 1), F32), jnp.zeros((rows, HEAD_DIM), F32))
        _, l_s, acc_s = lax.fori_loop(0, n_chunks, chunk, init)
        o_s = acc_s / jnp.maximum(l_s, 1e-30)

        kb = kw_ref[0, pl.ds(wstart, wlen), hs]
        vb = vw_ref[0, pl.ds(wstart, wlen), hs]
        wpos = wstart + lax.broadcasted_iota(jnp.int32, (1, wlen), 1)
        s_w = _dot_nt(qs, kb) * scale - slope * (tq_f - wpos.astype(F32))
        p_w = _masked_softmax(s_w, (wpos <= qpos) & (qpos - wpos < WINDOW))
        o_w = _dot(p_w.astype(BF16), vb)

        def gate_col(n):
            base = n * N_HEADS + kv * GQA
            return jnp.concatenate([gates[:, base + g:base + g + 1] for g in range(GQA)], axis=0)

        o = gate_col(0) * o_c + gate_col(1) * o_s + gate_col(2) * o_w
        for g in range(GQA):
            h = kv * GQA + g
            o_scr[:, h * HEAD_DIM:(h + 1) * HEAD_DIM] = o[g * tq:(g + 1) * tq]

    o_ref[0] = x_ref[0] + _dot(o_scr[...].astype(BF16), wo_ref[...])


def _nsa_prompt(q, gates, kc, vc, ks, vs, kw, vw, x, w_o, *, tq=128, kchunk=512):
    B, L, D = x.shape
    assert L % kchunk == 0 and kchunk % tq == 0 and L >= WINDOW + tq and L % SLC_BLOCK == 0
    assert 2 * (L // SLC_BLOCK) <= LANES and L // CMP_BLOCK <= LANES and kc.shape[1] == LANES
    tile = lambda w: pl.BlockSpec((1, tq, w), lambda b, t: (b, t, 0))
    whole = lambda a: pl.BlockSpec((1,) + a.shape[1:], lambda b, t: (b, 0, 0))
    return pl.pallas_call(
        functools.partial(_nsa_prompt_kernel, tq=tq, kchunk=kchunk, seq=L),
        grid=(B, L // tq),
        in_specs=[tile(N_HEADS * HEAD_DIM), tile(LANES), whole(kc), whole(vc), whole(ks), whole(vs),
                  whole(kw), whole(vw), tile(D), pl.BlockSpec(w_o.shape, lambda b, t: (0, 0))],
        out_specs=tile(D),
        out_shape=jax.ShapeDtypeStruct((B, L, D), F32),
        scratch_shapes=[pltpu.VMEM((tq, N_HEADS * HEAD_DIM), F32)],
        compiler_params=_cparams(("parallel", "arbitrary")),
        name="nsa_attn_prompt",
    )(q, gates, kc, vc, ks, vs, kw, vw, x, w_o)


CMP_GROUP = 32


def _cmp_lane_token(lane, page):
    per_page = page // CMP_BLOCK
    group, n, p = lane // LANES, (lane % LANES) // CMP_GROUP, lane % CMP_GROUP
    return per_page * (CMP_GROUP * group + p) + n


def _cmp_sample_kernel(pt_ref, cache_ref, w_ref, o_ref, buf, sems, *, rows):
    b, g = pl.program_id(0), pl.program_id(1)
    n_groups = pl.num_programs(1)
    step = b * n_groups + g
    slot = step % 2

    def page_copy(bb, gg, sl, i):
        page = pt_ref[bb, gg * CMP_GROUP + i]
        return pltpu.make_async_copy(cache_ref.at[page, pl.ds(0, rows), :], buf.at[sl, i], sems.at[sl])

    def start_group(bb, gg, sl):
        for i in range(CMP_GROUP):
            page_copy(bb, gg, sl, i).start()

    @pl.when(step == 0)
    def _():
        start_group(b, g, slot)

    @pl.when(step + 1 < pl.num_programs(0) * n_groups)
    def _():
        nxt = step + 1
        start_group(nxt // n_groups, nxt % n_groups, 1 - slot)

    for i in range(CMP_GROUP):
        page_copy(b, g, slot, i).wait()

    lane = lax.broadcasted_iota(jnp.int32, (8, LANES), 1)

    def row_group(r, carry):
        rs = pl.ds(pl.multiple_of(r * 8, 8), 8)
        w = w_ref[rs, :]
        vals = [buf[slot, i, rs, :] * w for i in range(CMP_GROUP)]
        h = CMP_BLOCK // 2
        while h >= 1:
            lo = (lane % (2 * h)) < h
            half = len(vals) // 2
            vals = [jnp.where(lo, vals[i] + pltpu.roll(vals[i], LANES - h, 1),
                              vals[i + half] + pltpu.roll(vals[i + half], h, 1)) for i in range(half)]
            h //= 2
        o_ref[0, rs, :] = vals[0]
        return carry

    lax.fori_loop(0, rows // 8, row_group, 0)


def _cmp_sample(page_table, cache_t, w_rows):
    B, n_pages = page_table.shape
    _, _, page = cache_t.shape
    rows = w_rows.shape[0]
    assert n_pages % CMP_GROUP == 0 and page == LANES and page // CMP_BLOCK * CMP_GROUP == LANES
    assert CMP_GROUP == CMP_BLOCK and rows % 8 == 0
    return pl.pallas_call(
        functools.partial(_cmp_sample_kernel, rows=rows),
        grid_spec=pltpu.PrefetchScalarGridSpec(
            num_scalar_prefetch=1,
            grid=(B, n_pages // CMP_GROUP),
            in_specs=[pl.BlockSpec(memory_space=pl.ANY),
                      pl.BlockSpec(w_rows.shape, lambda b, g, pt: (0, 0))],
            out_specs=pl.BlockSpec((1, rows, LANES), lambda b, g, pt: (b, 0, g)),
            scratch_shapes=[pltpu.VMEM((2, CMP_GROUP, rows, page), F32), pltpu.SemaphoreType.DMA((2,))],
        ),
        out_shape=jax.ShapeDtypeStruct((B, rows, n_pages * page // CMP_BLOCK), F32),
        compiler_params=_cparams(("arbitrary", "arbitrary")),
        name="cmp_sample",
    )(page_table, cache_t, w_rows)


def _select_sample_kernel(q_ref, kcv_ref, oc_ref, idx_ref, *, past, page):
    scale = HEAD_DIM ** -0.5
    q = q_ref[0]
    nc = kcv_ref.shape[2]
    n_past_blocks = past // SLC_BLOCK
    per_block = SLC_BLOCK // CMP_BLOCK
    row = lax.broadcasted_iota(jnp.int32, (N_HEADS, 1), 0)
    slope = _head_slopes(N_HEADS, 1, 0)
    lane = lax.broadcasted_iota(jnp.int32, (1, nc), 1)
    tok = _cmp_lane_token(lane, page)
    c_mid = tok.astype(F32) * CMP_BLOCK + (CMP_BLOCK - 1) / 2
    c_ok = (tok + 1) * CMP_BLOCK - 1 <= past
    bias = slope * (float(past) - c_mid)

    s_c = jnp.zeros((N_HEADS, nc), F32)
    for kv in range(N_KV_HEADS):
        s_kv = _dot3(q, kcv_ref[0, kv * HEAD_DIM:(kv + 1) * HEAD_DIM, :])
        s_c = jnp.where(row // GQA == kv, s_kv, s_c)
    p_c = _masked_softmax(s_c * scale - bias, c_ok)
    o_c = jnp.zeros((N_HEADS, HEAD_DIM), F32)
    for kv in range(N_KV_HEADS):
        o_kv = _dot3(p_c, kcv_ref[0, KV_W + kv * HEAD_DIM:KV_W + (kv + 1) * HEAD_DIM, :], nt=True)
        o_c = jnp.where(row // GQA == kv, o_kv, o_c)
    oc_ref[0] = o_c

    imp = p_c
    shift = 1
    while shift < GQA:
        imp = imp + pltpu.roll(imp, shift, 0)
        shift *= 2
    pair = imp + pltpu.roll(imp, nc - CMP_GROUP, 1)
    jblk = tok // per_block
    real = tok % per_block == 0
    forced = (jblk == 0) | (jblk == n_past_blocks - 1)
    score = jnp.where(real, pair + jnp.where(forced, FORCE_BONUS, 0.0), -2.0)
    own = jnp.float32(FORCE_BONUS)
    rank = (own > score).astype(jnp.int32)
    pages_per_lane_tile = CMP_GROUP
    for k in range(n_past_blocks):
        k_page, k_n = (k * per_block) // (page // CMP_BLOCK), (k * per_block) % (page // CMP_BLOCK)
        k_lane = LANES * (k_page // pages_per_lane_tile) + CMP_GROUP * k_n + k_page % pages_per_lane_tile
        col = score[:, k_lane:k_lane + 1]
        beats = (col > score) | ((col == score) & (jblk > k))
        rank = rank + beats.astype(jnp.int32)
    own_rank = jnp.sum(jnp.where(real & (score >= own), 1, 0), axis=-1, keepdims=True)
    out_lane = lax.broadcasted_iota(jnp.int32, (N_HEADS, LANES), 1)
    idx = jnp.zeros((N_HEADS, LANES), jnp.int32)
    for r in range(SLC_TOPK):
        hit = jnp.sum(jnp.where(real & (rank == r), jblk, 0), axis=-1, keepdims=True)
        hit = hit + jnp.where(own_rank == r, n_past_blocks, 0)
        idx = jnp.where(out_lane == r, hit, idx)
    idx_ref[0] = idx


def _select_sample(q16, kcv, past, page):
    B = q16.shape[0]
    nc = kcv.shape[2]
    assert past % SLC_BLOCK == 0 and nc == past // CMP_BLOCK and nc % LANES == 0
    assert past // SLC_BLOCK + 1 > SLC_TOPK and page % SLC_BLOCK == 0
    return pl.pallas_call(
        functools.partial(_select_sample_kernel, past=past, page=page),
        grid=(B,),
        in_specs=[pl.BlockSpec((1,) + q16.shape[1:], lambda b: (b, 0, 0)),
                  pl.BlockSpec((1,) + kcv.shape[1:], lambda b: (b, 0, 0))],
        out_specs=[pl.BlockSpec((1, N_HEADS, HEAD_DIM), lambda b: (b, 0, 0)),
                   pl.BlockSpec((1, N_HEADS, LANES), lambda b: (b, 0, 0))],
        out_shape=[jax.ShapeDtypeStruct((B, N_HEADS, HEAD_DIM), F32),
                   jax.ShapeDtypeStruct((B, N_HEADS, LANES), jnp.int32)],
        compiler_params=_cparams(("parallel",)),
        name="select_sample",
    )(q16, kcv)


def _attend_sample_kernel(pt_ref, idx_ref, q_ref, oc_ref, gt_ref, new_ref, win_ref, cache_ref,
                          o_ref, kbuf, vbuf, sems, *, past):
    b = pl.program_id(0)
    scale = HEAD_DIM ** -0.5
    n_past_blocks = past // SLC_BLOCK
    page_rows = cache_ref.shape[2]
    blocks_per_page = page_rows // SLC_BLOCK
    nkeys = SLC_TOPK * page_rows

    def block_copies(kv, n):
        blk = idx_ref[(b * N_KV_HEADS + kv) * SLC_TOPK + n]
        in_past = blk < n_past_blocks
        page = pt_ref[b, jnp.minimum(blk, n_past_blocks - 1) // blocks_per_page]
        dst = pl.ds(n * page_rows, page_rows)
        ck = pltpu.make_async_copy(cache_ref.at[page, pl.ds((2 * N_KV_HEADS + kv) * HEAD_DIM, HEAD_DIM), :],
                                   kbuf.at[kv, :, dst], sems.at[0, kv, n])
        cv = pltpu.make_async_copy(cache_ref.at[page, pl.ds((3 * N_KV_HEADS + kv) * HEAD_DIM, HEAD_DIM), :],
                                   vbuf.at[kv, :, dst], sems.at[1, kv, n])
        return blk, in_past, ck, cv

    for kv in range(N_KV_HEADS):
        for n in range(SLC_TOPK):
            _, in_past, ck, cv = block_copies(kv, n)

            @pl.when(in_past)
            def _():
                ck.start()
                cv.start()

            @pl.when(jnp.logical_not(in_past))
            def _():
                kbuf[kv, :, n * page_rows:(n + 1) * page_rows] = jnp.zeros((HEAD_DIM, page_rows), F32)
                vbuf[kv, :, n * page_rows:(n + 1) * page_rows] = jnp.zeros((HEAD_DIM, page_rows), F32)

    q = q_ref[0]
    row = lax.broadcasted_iota(jnp.int32, (N_HEADS, 1), 0)
    slope = _head_slopes(N_HEADS, 1, 0)
    key_lane = lax.broadcasted_iota(jnp.int32, (1, nkeys), 1)

    wb = win_ref.shape[2]
    w_lane = lax.broadcasted_iota(jnp.int32, (1, wb), 1)
    wpos = past - wb + w_lane
    w_ok = (past - wpos < WINDOW) & (wpos >= 0)
    w_bias = slope * (past - wpos).astype(F32)
    s_w = jnp.zeros((N_HEADS, wb), F32)
    s_n = jnp.zeros((N_HEADS, 1), F32)
    for kv in range(N_KV_HEADS):
        mine = row // GQA == kv
        s_kv = _dot3(q, win_ref[0, kv * HEAD_DIM:(kv + 1) * HEAD_DIM, :])
        s_w = jnp.where(mine, s_kv, s_w)
        s_n = jnp.where(mine, jnp.sum(q * new_ref[0, 4, kv:kv + 1, :], axis=-1, keepdims=True), s_n)
    s_w = jnp.where(w_ok, s_w * scale - w_bias, NEG)
    s_n = s_n * scale
    m_w = jnp.maximum(jnp.max(s_w, axis=-1, keepdims=True), s_n)
    p_w = jnp.where(w_ok, jnp.exp(s_w - m_w), 0.0)
    p_n = jnp.exp(s_n - m_w)
    l_w = jnp.maximum(jnp.sum(p_w, axis=-1, keepdims=True) + p_n, 1e-30)
    p_w = p_w / l_w
    p_n = p_n / l_w
    o_w = jnp.zeros((N_HEADS, HEAD_DIM), F32)
    for kv in range(N_KV_HEADS):
        o_kv = (_dot3(p_w, win_ref[0, KV_W + kv * HEAD_DIM:KV_W + (kv + 1) * HEAD_DIM, :], nt=True)
                + p_n * new_ref[0, 5, kv:kv + 1, :])
        o_w = jnp.where(row // GQA == kv, o_kv, o_w)

    for kv in range(N_KV_HEADS):
        for n in range(SLC_TOPK):
            _, in_past, ck, cv = block_copies(kv, n)

            @pl.when(in_past)
            def _():
                ck.wait()
                cv.wait()

    s_s = jnp.zeros((N_HEADS, nkeys), F32)
    pos = jnp.zeros((N_HEADS, nkeys), jnp.int32)
    live = jnp.zeros((N_HEADS, nkeys), jnp.int32)
    own = jnp.zeros((N_HEADS, 1), jnp.int32)
    s_n = jnp.zeros((N_HEADS, 1), F32)
    in_slab = key_lane % page_rows
    for kv in range(N_KV_HEADS):
        mine = row // GQA == kv
        s_kv = _dot3(q, kbuf[kv])
        s_s = jnp.where(mine, s_kv, s_s)
        s_n = jnp.where(mine, jnp.sum(q * new_ref[0, 2, kv:kv + 1, :], axis=-1, keepdims=True), s_n)
        base_kv = jnp.zeros((1, nkeys), jnp.int32)
        half_kv = jnp.zeros((1, nkeys), jnp.int32)
        own_kv = jnp.int32(0)
        for n in range(SLC_TOPK):
            blk = idx_ref[(b * N_KV_HEADS + kv) * SLC_TOPK + n]
            here = key_lane // page_rows == n
            base_kv = jnp.where(here, (blk // blocks_per_page) * page_rows, base_kv)
            half_kv = jnp.where(here, blk % blocks_per_page, half_kv)
            own_kv = own_kv + (blk >= n_past_blocks).astype(jnp.int32)
        pos = jnp.where(mine, base_kv + in_slab, pos)
        live = jnp.where(mine, (in_slab // SLC_BLOCK == half_kv).astype(jnp.int32), live)
        own = jnp.where(mine, own_kv, own)
    s_ok = (live > 0) & (pos < past)
    n_ok = own > 0
    s_s = jnp.where(s_ok, s_s * scale - slope * (past - pos).astype(F32), NEG)
    s_n = jnp.where(n_ok, s_n * scale, NEG)
    m_s = jnp.maximum(jnp.max(s_s, axis=-1, keepdims=True), s_n)
    p_s = jnp.where(s_ok, jnp.exp(s_s - m_s), 0.0)
    p_n = jnp.where(n_ok, jnp.exp(s_n - m_s), 0.0)
    l_s = jnp.maximum(jnp.sum(p_s, axis=-1, keepdims=True) + p_n, 1e-30)
    p_s = p_s / l_s
    p_n = p_n / l_s
    o_s = jnp.zeros((N_HEADS, HEAD_DIM), F32)
    for kv in range(N_KV_HEADS):
        o_kv = _dot3(p_s, vbuf[kv], nt=True) + p_n * new_ref[0, 3, kv:kv + 1, :]
        o_s = jnp.where(row // GQA == kv, o_kv, o_s)

    gt = jax.nn.sigmoid(gt_ref[0])
    o_ref[0] = gt[:, 0:1] * oc_ref[0] + gt[:, 1:2] * o_s + gt[:, 2:3] * o_w


def _attend_sample(page_table, idx, q16, o_c, gates, kv_new, win_t, cache_t, past):
    B = q16.shape[0]
    page_rows = cache_t.shape[2]
    assert win_t.shape[2] <= WINDOW and page_rows % SLC_BLOCK == 0
    blk3 = lambda a: pl.BlockSpec((1,) + a.shape[1:], lambda b, pt, ix: (b,) + (0,) * (a.ndim - 1))
    return pl.pallas_call(
        functools.partial(_attend_sample_kernel, past=past),
        grid_spec=pltpu.PrefetchScalarGridSpec(
            num_scalar_prefetch=2,
            grid=(B,),
            in_specs=[blk3(q16), blk3(o_c), blk3(gates), blk3(kv_new), blk3(win_t),
                      pl.BlockSpec(memory_space=pl.ANY)],
            out_specs=pl.BlockSpec((1, N_HEADS, HEAD_DIM), lambda b, pt, ix: (b, 0, 0)),
            scratch_shapes=[pltpu.VMEM((N_KV_HEADS, HEAD_DIM, SLC_TOPK * page_rows), F32),
                            pltpu.VMEM((N_KV_HEADS, HEAD_DIM, SLC_TOPK * page_rows), F32),
                            pltpu.SemaphoreType.DMA((2, N_KV_HEADS, SLC_TOPK))],
        ),
        out_shape=jax.ShapeDtypeStruct((B, N_HEADS, HEAD_DIM), F32),
        compiler_params=_cparams(("arbitrary",)),
        name="attend_sample",
    )(page_table, idx, q16, o_c, gates, kv_new, win_t, cache_t)


def _linear_res_kernel(x_ref, a_ref, w_ref, o_ref):
    o_ref[...] = x_ref[...] + _dot3(a_ref[...], w_ref[...])


def _linear_res(x, a, w):
    return pl.pallas_call(
        _linear_res_kernel,
        out_shape=jax.ShapeDtypeStruct(x.shape, F32),
        compiler_params=pltpu.CompilerParams(vmem_limit_bytes=VMEM_LIMIT),
        name="out_proj_sample",
    )(x, a, w)


def _expand_cmp_weights(w_cmp, rows):
    w = jnp.repeat(w_cmp, HEAD_DIM, axis=1)
    return jnp.tile(w, (rows // CMP_BLOCK, 1))


def _row(v):
    return v.reshape(1, -1)


def _prompt_mixers(x, norm_mix, w_pool, pool_scale, w_in, w_cmp_k, w_cmp_v, w_out, norm_ffn0, wfg, wfu, wfd):
    B, L, D = x.shape
    q_w = N_HEADS * HEAD_DIM
    x, h_last = _pool_prompt(x, _row(norm_mix[0]), w_pool.astype(BF16), _row(pool_scale))
    new_pool = h_last[None, :, POOL_HALO - (max(POOL_WINDOWS) - 1):, :]
    x = _ffn(x.reshape(B * L, D), _row(norm_ffn0), wfg.astype(BF16), wfu.astype(BF16), wfd.astype(BF16), tm=1024)

    tm = 512
    wq = w_in[:, :q_w].astype(BF16)
    wkv = w_in[:, q_w:q_w + 6 * KV_W].astype(BF16)
    wgt = jnp.pad(w_in[:, q_w + 6 * KV_W:], ((0, 0), (0, LANES - N_BRANCH * N_HEADS))).astype(BF16)
    q, kv4, kwin, ks, vs, kw, vw, gates, kc, vc = _nsa_proj(
        x, _row(norm_mix[1]), wq, wkv, wgt, _expand_cmp_weights(w_cmp_k, tm), _expand_cmp_weights(w_cmp_v, tm),
        tm=tm)
    new_kv = kv4.reshape(1, B, L, 4, N_KV_HEADS, HEAD_DIM)
    new_win = kwin.reshape(B, L, 2, N_KV_HEADS, HEAD_DIM)[None, :, L - min(WINDOW, L):]
    nc = L // CMP_BLOCK
    pad_c = lambda a: jnp.pad(a.reshape(B, nc, KV_W), ((0, 0), (0, LANES - nc), (0, 0))).astype(BF16)
    per_seq = lambda a: a.reshape(B, L, a.shape[-1])
    x = _nsa_prompt(per_seq(q), per_seq(gates), pad_c(kc), pad_c(vc), per_seq(ks), per_seq(vs),
                    per_seq(kw), per_seq(vw), per_seq(x), w_out.astype(BF16))
    return x.reshape(B * L, D), new_pool, new_kv, new_win


def _sample_mixers(x, state_pool, cache, state_win, page_table, norm_mix, w_pool, pool_scale, w_in, w_cmp_k,
                   w_cmp_v, w_out, norm_ffn0, wfg, wfu, wfd):
    SB, D = x.shape
    n_phys, page = cache.shape[:2]
    past = page_table.shape[1] * page
    q_w = N_HEADS * HEAD_DIM
    x, h = _pool_sample(x, state_pool.transpose(1, 0, 2), _row(norm_mix[0]), w_pool, _row(pool_scale))
    new_pool = jnp.concatenate([state_pool[:, 1:], h[:, None]], axis=1)[None]
    x = _ffn(x, _row(norm_ffn0), wfg, wfu, wfd, tm=SB)

    cols = w_in.shape[1]
    z = _nsa_proj_sample(x, _row(norm_mix[1]), jnp.pad(w_in, ((0, 0), (0, -cols % LANES))))
    kv_new = z[:, q_w:q_w + 6 * KV_W].reshape(SB, 6, N_KV_HEADS, HEAD_DIM)
    new_kv = kv_new[None, :, None, :4]
    keep = min(WINDOW, state_win.shape[1] + 1)
    new_win = jnp.concatenate([state_win, kv_new[:, None, 4:]], axis=1)[None, :, -keep:]

    cache_t = cache.transpose(0, 2, 3, 4, 1).reshape(n_phys, 4 * KV_W, page)
    win_t = state_win.transpose(0, 2, 3, 4, 1).reshape(SB, 2 * KV_W, state_win.shape[1])
    w_rows = jnp.concatenate([_expand_cmp_weights(w_cmp_k, page).T, _expand_cmp_weights(w_cmp_v, page).T], axis=0)
    kcv = _cmp_sample(page_table, cache_t, w_rows)
    q16 = z[:, :q_w].reshape(SB, N_HEADS, HEAD_DIM)
    o_c, idx = _select_sample(q16, kcv, past, page)
    idx = idx[:, GQA - 1::GQA, :SLC_TOPK].reshape(-1)
    gate_logits = z[:, q_w + 6 * KV_W:cols].reshape(SB, N_BRANCH, N_HEADS).transpose(0, 2, 1)
    gate_logits = jnp.pad(gate_logits, ((0, 0), (0, 0), (0, LANES - N_BRANCH)))
    o = _attend_sample(page_table, idx, q16, o_c, gate_logits, kv_new, win_t, cache_t, past)
    return _linear_res(x, o.reshape(SB, q_w), w_out), new_pool, new_kv, new_win


def kernel(x_prompt, x_sample, state_pool, cache_kv, state_win, page_table, norm_mix, w_pool, pool_scale,
           w_nsa_in, w_cmp_k, w_cmp_v, w_nsa_out, norm_ffn, w_ffn_gate, w_ffn_up, w_ffn_down, w_router,
           w_moe_gate, w_moe_up, w_moe_down, norm_final):
    assert x_sample.shape[1] == 1 and norm_mix.shape[0] == 2 and x_prompt.shape[-1] == N_HEADS * HEAD_DIM
    w_rt = jnp.pad(w_router[0], ((0, 0), (0, LANES - w_router.shape[-1])))
    moe_w = (w_moe_gate[0].astype(BF16), w_moe_up[0].astype(BF16), w_moe_down[0].astype(BF16))

    xp, new_pool_prompt, new_kv_prompt, new_win_prompt = _prompt_mixers(
        x_prompt, norm_mix, w_pool[0], pool_scale[0], w_nsa_in[0], w_cmp_k[0], w_cmp_v[0], w_nsa_out[0],
        norm_ffn[0], w_ffn_gate, w_ffn_up, w_ffn_down)
    xs, new_pool_sample, new_kv_sample, new_win_sample = _sample_mixers(
        x_sample[:, 0], state_pool[0], cache_kv[0], state_win[0], page_table, norm_mix, w_pool[0], pool_scale[0],
        w_nsa_in[0], w_cmp_k[0], w_cmp_v[0], w_nsa_out[0], norm_ffn[0], w_ffn_gate, w_ffn_up, w_ffn_down)

    yp = _ffn(xp, _row(norm_ffn[1]), *moe_w, w_rt, _row(norm_final), tm=1024)
    ys = _ffn(xs, _row(norm_ffn[1]), *moe_w, w_rt, _row(norm_final), tm=xs.shape[0])
    return (yp.reshape(x_prompt.shape), ys.reshape(x_sample.shape), new_pool_prompt, new_pool_sample,
            new_kv_prompt, new_kv_sample, new_win_prompt, new_win_sample)
```

```python
import functools

import jax
import jax.numpy as jnp
from jax import lax
from jax.experimental import pallas as pl
from jax.experimental.pallas import tpu as pltpu

F32 = jnp.float32
BF16 = jnp.bfloat16

EPS = 1e-6
NEG = -1e30
POOL_WINDOWS = (2, 4, 8, 16)
POOL_HALO = 16
N_HEADS = 16
N_KV_HEADS = 4
GQA = N_HEADS // N_KV_HEADS
HEAD_DIM = 64
KV_W = N_KV_HEADS * HEAD_DIM
CMP_BLOCK = 32
SLC_BLOCK = 64
SLC_TOPK = 16
WINDOW = 512
N_BRANCH = 3
FORCE_BONUS = 1000.0
TOP_K = 2
LANES = 128
VMEM_LIMIT = 56 * 1024 * 1024


def _cparams(sem):
    return pltpu.CompilerParams(dimension_semantics=sem, vmem_limit_bytes=VMEM_LIMIT)


def _rmsnorm(x, g):
    ms = jnp.mean(x * x, axis=-1, keepdims=True)
    return (x * lax.rsqrt(ms + EPS)) * g


def _dot(a, b):
    return jnp.dot(a, b, preferred_element_type=F32)


def _dot_nt(a, b):
    return lax.dot_general(a, b, (((1,), (1,)), ((), ())), preferred_element_type=F32)


def _split(a):
    hi = a.astype(BF16)
    return hi, (a - hi.astype(F32)).astype(BF16)


def _dot3(a, b, nt=False):
    d = _dot_nt if nt else _dot
    ah, al = _split(a)
    bh, bl = _split(b)
    return d(ah, bh) + (d(ah, bl) + d(al, bh))


def _masked_softmax(s, mask):
    s = jnp.where(mask, s, NEG)
    m = jnp.max(s, axis=-1, keepdims=True)
    p = jnp.where(mask, jnp.exp(s - m), 0.0)
    return p / jnp.maximum(jnp.sum(p, axis=-1, keepdims=True), 1e-30)


def _head_slopes(rows, rows_per_head, first_head):
    h = first_head + lax.broadcasted_iota(jnp.int32, (rows, 1), 0) // rows_per_head
    return jnp.exp2(-0.5 * (h + 1).astype(F32))


def _pool_prompt_kernel(x_ref, halo_ref, g_ref, w_ref, sc_ref, o_ref, hl_ref, full_ref, *, tile):
    t = pl.program_id(1)
    g = g_ref[...]
    x = x_ref[0]
    h = _rmsnorm(x, g)
    hh = _rmsnorm(halo_ref[0], g)
    full_ref[0:POOL_HALO, :] = jnp.where(t > 0, hh, 0.0)
    full_ref[POOL_HALO:POOL_HALO + tile, :] = h
    row = t * tile + lax.broadcasted_iota(jnp.int32, (tile, 1), 0)
    group = x.shape[-1] // len(POOL_WINDOWS)
    parts = []
    for gi, w in enumerate(POOL_WINDOWS):
        cs = slice(gi * group, (gi + 1) * group)
        hg = h[:, cs]
        acc = hg
        for k in range(1, w):
            acc = acc + full_ref[POOL_HALO - k:POOL_HALO - k + tile, cs]
        cnt = jnp.minimum(row + 1, w).astype(F32)
        pooled = acc / cnt - hg
        parts.append(_dot(pooled.astype(BF16), w_ref[gi]))
    o_ref[0] = x + jnp.concatenate(parts, axis=-1) * sc_ref[...]

    @pl.when(t == pl.num_programs(1) - 1)
    def _():
        hl_ref[0] = h[tile - POOL_HALO:, :]


def _pool_prompt(x, g, w_pool, scale, tile=512):
    B, L, D = x.shape
    assert L % tile == 0 and tile % POOL_HALO == 0
    hb = tile // POOL_HALO
    return pl.pallas_call(
        functools.partial(_pool_prompt_kernel, tile=tile),
        grid=(B, L // tile),
        in_specs=[
            pl.BlockSpec((1, tile, D), lambda b, t: (b, t, 0)),
            pl.BlockSpec((1, POOL_HALO, D), lambda b, t: (b, jnp.maximum(t * hb - 1, 0), 0)),
            pl.BlockSpec((1, D), lambda b, t: (0, 0)),
            pl.BlockSpec(w_pool.shape, lambda b, t: (0, 0, 0)),
            pl.BlockSpec((1, D), lambda b, t: (0, 0)),
        ],
        out_specs=[
            pl.BlockSpec((1, tile, D), lambda b, t: (b, t, 0)),
            pl.BlockSpec((1, POOL_HALO, D), lambda b, t: (b, 0, 0)),
        ],
        out_shape=[jax.ShapeDtypeStruct((B, L, D), F32), jax.ShapeDtypeStruct((B, POOL_HALO, D), F32)],
        scratch_shapes=[pltpu.VMEM((tile + POOL_HALO, D), F32)],
        compiler_params=_cparams(("parallel", "arbitrary")),
        name="pool_prompt",
    )(x, x, g, w_pool, scale)


def _pool_sample_kernel(x_ref, st_ref, g_ref, w_ref, sc_ref, o_ref, h_ref):
    x = x_ref[...]
    h = _rmsnorm(x, g_ref[...])
    h_ref[...] = h
    P = st_ref.shape[0]
    group = x.shape[-1] // len(POOL_WINDOWS)
    parts = []
    for gi, w in enumerate(POOL_WINDOWS):
        cs = slice(gi * group, (gi + 1) * group)
        hg = h[:, cs]
        acc = hg
        for k in range(1, w):
            acc = acc + st_ref[P - k][:, cs]
        pooled = acc / float(w) - hg
        parts.append(_dot3(pooled, w_ref[gi]))
    o_ref[...] = x + jnp.concatenate(parts, axis=-1) * sc_ref[...]


def _pool_sample(x, state_t, g, w_pool, scale):
    B, D = x.shape
    assert state_t.shape[0] >= max(POOL_WINDOWS) - 1
    return pl.pallas_call(
        _pool_sample_kernel,
        out_shape=[jax.ShapeDtypeStruct((B, D), F32), jax.ShapeDtypeStruct((B, D), F32)],
        compiler_params=pltpu.CompilerParams(vmem_limit_bytes=VMEM_LIMIT),
        name="pool_sample",
    )(x, state_t, g, w_pool, scale)


def _ffn_kernel(*refs, n_experts, final_norm, precise):
    moe = n_experts > 1
    it = iter(refs)
    x_ref, g_ref = next(it), next(it)
    wr_ref = next(it) if moe else None
    wg_ref, wu_ref, wd_ref = next(it), next(it), next(it)
    gf_ref = next(it) if final_norm else None
    o_ref, h_scr, acc_scr = next(it), next(it), next(it)
    eacc_scr, gate_scr = (next(it), next(it)) if moe else (None, None)

    e, f = pl.program_id(1), pl.program_id(2)
    last_f = f == pl.num_programs(2) - 1

    @pl.when((e == 0) & (f == 0))
    def _():
        h = _rmsnorm(x_ref[...], g_ref[...])
        h_scr[...] = h.astype(h_scr.dtype)
        acc_scr[...] = jnp.zeros_like(acc_scr)
        if moe:
            logits = jnp.dot(h, wr_ref[...], preferred_element_type=F32, precision=lax.Precision.HIGHEST)
            lane, i1, i2, w1, w2 = _top2_gates(logits, n_experts)
            gate_scr[...] = jnp.where(lane == i1, w1, 0.0) + jnp.where(lane == i2, w2, 0.0)

    mm = _dot3 if precise else _dot
    hb = h_scr[...]
    a = mm(hb, wg_ref[0])
    u = mm(hb, wu_ref[0])
    act = (a * jax.nn.sigmoid(a)) * u
    y = mm(act.astype(hb.dtype), wd_ref[0])

    if moe:
        @pl.when(f == 0)
        def _():
            eacc_scr[...] = y

        @pl.when(f > 0)
        def _():
            eacc_scr[...] += y

        @pl.when(last_f)
        def _():
            gate = gate_scr[...]
            lane = lax.broadcasted_iota(jnp.int32, gate.shape, 1)
            ge = jnp.sum(jnp.where(lane == e, gate, 0.0), axis=-1, keepdims=True)
            acc_scr[...] += ge * eacc_scr[...]
    else:
        acc_scr[...] += y

    @pl.when((e == pl.num_programs(1) - 1) & last_f)
    def _():
        out = x_ref[...] + acc_scr[...]
        if final_norm:
            out = _rmsnorm(out, gf_ref[...])
        o_ref[...] = out


def _ffn(x, g, wg, wu, wd, w_router=None, g_final=None, *, tm, tf=512):
    precise = wg.dtype == F32
    N, D = x.shape
    E, _, F = wg.shape
    assert N % tm == 0 and F % tf == 0
    moe = w_router is not None
    assert moe == (E > 1)
    final_norm = g_final is not None
    const2 = lambda i, e, f: (0, 0)
    args, in_specs = [x, g], [pl.BlockSpec((tm, D), lambda i, e, f: (i, 0)), pl.BlockSpec((1, D), const2)]
    if moe:
        args.append(w_router)
        in_specs.append(pl.BlockSpec(w_router.shape, const2))
    args += [wg, wu, wd]
    in_specs += [
        pl.BlockSpec((1, D, tf), lambda i, e, f: (e, 0, f)),
        pl.BlockSpec((1, D, tf), lambda i, e, f: (e, 0, f)),
        pl.BlockSpec((1, tf, D), lambda i, e, f: (e, f, 0)),
    ]
    if final_norm:
        args.append(g_final)
        in_specs.append(pl.BlockSpec((1, D), const2))
    scratch = [pltpu.VMEM((tm, D), F32 if precise else BF16), pltpu.VMEM((tm, D), F32)]
    if moe:
        scratch += [pltpu.VMEM((tm, D), F32), pltpu.VMEM((tm, LANES), F32)]
    return pl.pallas_call(
        functools.partial(_ffn_kernel, n_experts=E, final_norm=final_norm, precise=precise),
        grid=(N // tm, E, F // tf),
        in_specs=in_specs,
        out_specs=pl.BlockSpec((tm, D), lambda i, e, f: (i, 0)),
        out_shape=jax.ShapeDtypeStruct((N, D), F32),
        scratch_shapes=scratch,
        compiler_params=_cparams(("parallel", "arbitrary", "arbitrary")),
        name="moe_ffn" if moe else "dense_ffn",
    )(*args)


ROUTE_I1, ROUTE_I2, ROUTE_W1, ROUTE_W2, ROUTE_R1, ROUTE_R2 = range(6)


def _top2_gates(logits, n_experts):
    lane = lax.broadcasted_iota(jnp.int32, logits.shape, 1)
    lg = jnp.where(lane < n_experts, logits, -jnp.inf)
    m1 = jnp.max(lg, axis=-1, keepdims=True)
    i1 = jnp.min(jnp.where(lg == m1, lane, LANES), axis=-1, keepdims=True)
    lg2 = jnp.where(lane == i1, -jnp.inf, lg)
    m2 = jnp.max(lg2, axis=-1, keepdims=True)
    i2 = jnp.min(jnp.where(lg2 == m2, lane, LANES), axis=-1, keepdims=True)
    e2 = jnp.exp(m2 - m1)
    den = 1.0 + e2
    return lane, i1, i2, 1.0 / den, e2 / den


def _moe_route_kernel(x_ref, g_ref, wr_ref, tri_ref, route_ref, cnt_ref, carry, *, n_experts):
    @pl.when(pl.program_id(0) == 0)
    def _():
        carry[...] = jnp.zeros_like(carry)

    h = _rmsnorm(x_ref[...], g_ref[...])
    logits = jnp.dot(h, wr_ref[...], preferred_element_type=F32, precision=lax.Precision.HIGHEST)
    lane, i1, i2, w1, w2 = _top2_gates(logits, n_experts)
    member = jnp.where((lane == i1) | (lane == i2), 1.0, 0.0)
    before = _dot(tri_ref[...], member.astype(BF16)) + carry[0:1, :]
    r1 = jnp.sum(jnp.where(lane == i1, before, 0.0), axis=-1, keepdims=True)
    r2 = jnp.sum(jnp.where(lane == i2, before, 0.0), axis=-1, keepdims=True)
    rec = jnp.zeros(logits.shape, F32)
    for col, val in ((ROUTE_I1, i1.astype(F32)), (ROUTE_I2, i2.astype(F32)), (ROUTE_W1, w1), (ROUTE_W2, w2),
                     (ROUTE_R1, r1), (ROUTE_R2, r2)):
        rec = jnp.where(lane == col, val, rec)
    route_ref[...] = rec
    carry[...] = carry[...] + jnp.sum(member, axis=0, keepdims=True)
    cnt_ref[...] = carry[...]


def _moe_route(x, g, w_router, n_experts, tm=512):
    N, D = x.shape
    assert N % tm == 0
    tri = (jnp.arange(tm)[:, None] > jnp.arange(tm)[None, :]).astype(BF16)
    return pl.pallas_call(
        functools.partial(_moe_route_kernel, n_experts=n_experts),
        grid=(N // tm,),
        in_specs=[pl.BlockSpec((tm, D), lambda i: (i, 0)), pl.BlockSpec((1, D), lambda i: (0, 0)),
                  pl.BlockSpec(w_router.shape, lambda i: (0, 0)), pl.BlockSpec((tm, tm), lambda i: (0, 0))],
        out_specs=[pl.BlockSpec((tm, LANES), lambda i: (i, 0)), pl.BlockSpec((8, LANES), lambda i: (0, 0))],
        out_shape=[jax.ShapeDtypeStruct((N, LANES), F32), jax.ShapeDtypeStruct((8, LANES), F32)],
        scratch_shapes=[pltpu.VMEM((8, LANES), F32)],
        compiler_params=_cparams(("arbitrary",)),
        name="moe_route",
    )(x, g, w_router, tri)


def _row_gather(idx_of, src_hbm, dst_at, sem, rows):
    def copy(r):
        return pltpu.make_async_copy(src_hbm.at[pl.ds(idx_of(r), 1), :], dst_at(r), sem)

    def start():
        lax.fori_loop(0, rows, lambda r, c: (copy(r).start(), c)[1], 0, unroll=8)

    def wait():
        lax.fori_loop(0, rows, lambda r, c: (copy(r).wait(), c)[1], 0, unroll=8)

    return start, wait


def _moe_experts_kernel(te_ref, nu_ref, tos_ref, x_hbm, g_ref, wg_ref, wu_ref, wd_ref, o_ref,
                        xbuf, h_scr, acc_scr, sems, *, tile):
    i, f = pl.program_id(0), pl.program_id(1)
    n_used = nu_ref[0]
    used = i < n_used
    slot = i % 2

    def gather(t, sl):
        return _row_gather(lambda r: tos_ref[t * tile + r], x_hbm,
                           lambda r: xbuf.at[sl, pl.ds(r, 1), :], sems.at[sl], tile)

    @pl.when((f == 0) & (i == 0) & used)
    def _():
        gather(0, 0)[0]()

    @pl.when((f == 0) & used)
    def _():
        gather(i, slot)[1]()

        @pl.when(i + 1 < n_used)
        def _():
            gather(i + 1, 1 - slot)[0]()

        h_scr[...] = _rmsnorm(xbuf[slot], g_ref[...]).astype(BF16)
        acc_scr[...] = jnp.zeros_like(acc_scr)

    @pl.when(used)
    def _():
        hb = h_scr[...]
        a = _dot(hb, wg_ref[0])
        u = _dot(hb, wu_ref[0])
        act = (a * jax.nn.sigmoid(a)) * u
        acc_scr[...] += _dot(act.astype(BF16), wd_ref[0])

    @pl.when(f == pl.num_programs(1) - 1)
    def _():
        o_ref[...] = jnp.where(used, acc_scr[...], 0.0)


def _moe_experts(x, g, wg, wu, wd, tile_expert, n_used, token_of_slot, *, tile, tf=512):
    N, D = x.shape
    E, _, F = wg.shape
    n_slots = token_of_slot.shape[0]
    assert n_slots % tile == 0 and F % tf == 0
    n_tiles, nf = n_slots // tile, F // tf
    fidx = lambda i, f, nu: jnp.where(i < nu[0], f, nf - 1)
    return pl.pallas_call(
        functools.partial(_moe_experts_kernel, tile=tile),
        grid_spec=pltpu.PrefetchScalarGridSpec(
            num_scalar_prefetch=3,
            grid=(n_tiles, nf),
            in_specs=[pl.BlockSpec(memory_space=pl.ANY),
                      pl.BlockSpec((1, D), lambda i, f, te, nu, tos: (0, 0)),
                      pl.BlockSpec((1, D, tf), lambda i, f, te, nu, tos: (te[i], 0, fidx(i, f, nu))),
                      pl.BlockSpec((1, D, tf), lambda i, f, te, nu, tos: (te[i], 0, fidx(i, f, nu))),
                      pl.BlockSpec((1, tf, D), lambda i, f, te, nu, tos: (te[i], fidx(i, f, nu), 0))],
            out_specs=pl.BlockSpec((tile, D), lambda i, f, te, nu, tos: (i, 0)),
            scratch_shapes=[pltpu.VMEM((2, tile, D), F32), pltpu.VMEM((tile, D), BF16),
                            pltpu.VMEM((tile, D), F32), pltpu.SemaphoreType.DMA((2,))],
        ),
        out_shape=jax.ShapeDtypeStruct((n_slots, D), F32),
        compiler_params=_cparams(("arbitrary", "arbitrary")),
        name="moe_experts",
    )(tile_expert, n_used, token_of_slot, x, g, wg, wu, wd)


def _moe_combine_kernel(slot_ref, x_ref, route_ref, gf_ref, ys_hbm, o_ref, ybuf, sems, *, tile):
    i = pl.program_id(0)
    slot = i % 2

    def gather(t, sl):
        return _row_gather(lambda j: slot_ref[t * 2 * tile + j], ys_hbm,
                           lambda j: ybuf.at[sl, pl.ds(j, 1), :], sems.at[sl], 2 * tile)

    @pl.when(i == 0)
    def _():
        gather(0, 0)[0]()

    gather(i, slot)[1]()

    @pl.when(i + 1 < pl.num_programs(0))
    def _():
        gather(i + 1, 1 - slot)[0]()

    route = route_ref[...]
    w1, w2 = route[:, ROUTE_W1:ROUTE_W1 + 1], route[:, ROUTE_W2:ROUTE_W2 + 1]
    y = w1 * ybuf[slot, 0:tile, :] + w2 * ybuf[slot, tile:2 * tile, :]
    o_ref[...] = _rmsnorm(x_ref[...] + y, gf_ref[...])


def _moe_combine(x, route, g_final, ys, slots, *, tile=256):
    N, D = x.shape
    assert N % tile == 0
    return pl.pallas_call(
        functools.partial(_moe_combine_kernel, tile=tile),
        grid_spec=pltpu.PrefetchScalarGridSpec(
            num_scalar_prefetch=1,
            grid=(N // tile,),
            in_specs=[pl.BlockSpec((tile, D), lambda i, s: (i, 0)),
                      pl.BlockSpec((tile, LANES), lambda i, s: (i, 0)),
                      pl.BlockSpec((1, D), lambda i, s: (0, 0)),
                      pl.BlockSpec(memory_space=pl.ANY)],
            out_specs=pl.BlockSpec((tile, D), lambda i, s: (i, 0)),
            scratch_shapes=[pltpu.VMEM((2, 2 * tile, D), F32), pltpu.SemaphoreType.DMA((2,))],
        ),
        out_shape=jax.ShapeDtypeStruct((N, D), F32),
        compiler_params=_cparams(("arbitrary",)),
        name="moe_combine",
    )(slots, x, route, g_final, ys)


def _moe_prompt(x, g, w_router, wg, wu, wd, g_final, *, tile=512, ctile=256):
    N, D = x.shape
    E = wg.shape[0]
    route, counts = _moe_route(x, g, w_router, E)
    cnt = counts[0, :E].astype(jnp.int32)
    padded = (cnt + tile - 1) // tile * tile
    ends = jnp.cumsum(padded)
    off = ends - padded
    i1, i2 = route[:, ROUTE_I1].astype(jnp.int32), route[:, ROUTE_I2].astype(jnp.int32)
    slot1 = off[i1] + route[:, ROUTE_R1].astype(jnp.int32)
    slot2 = off[i2] + route[:, ROUTE_R2].astype(jnp.int32)
    n_slots = (TOP_K * N + E * (tile - 1)) // tile * tile
    rows = jnp.arange(N, dtype=jnp.int32)
    token_of_slot = jnp.zeros((n_slots,), jnp.int32).at[slot1].set(rows).at[slot2].set(rows)
    n_used = ends[-1:] // tile
    tile_start = jnp.minimum(jnp.arange(n_slots // tile, dtype=jnp.int32), n_used[0] - 1) * tile
    tile_expert = jnp.sum(tile_start[:, None] >= ends[None, :], axis=1).astype(jnp.int32)
    ys = _moe_experts(x, g, wg, wu, wd, tile_expert, n_used.astype(jnp.int32), token_of_slot, tile=tile)
    slots = jnp.stack([slot1.reshape(-1, ctile), slot2.reshape(-1, ctile)], axis=1).reshape(-1)
    return _moe_combine(x, route, g_final, ys, slots, tile=ctile)


def _nsa_proj_kernel(x_ref, g_ref, wq_ref, wkv_ref, wgt_ref, wck_ref, wcv_ref,
                     q_ref, kv4_ref, kwin_ref, ks_ref, vs_ref, kw_ref, vw_ref, gt_ref, kc_ref, vc_ref):
    hb = _rmsnorm(x_ref[...], g_ref[...]).astype(BF16)
    q_ref[...] = _dot(hb, wq_ref[...]).astype(BF16)
    kv = _dot(hb, wkv_ref[...])
    kv4_ref[...] = kv[:, :4 * KV_W]
    kwin_ref[...] = kv[:, 4 * KV_W:]
    ks_ref[...] = kv[:, 2 * KV_W:3 * KV_W].astype(BF16)
    vs_ref[...] = kv[:, 3 * KV_W:4 * KV_W].astype(BF16)
    kw_ref[...] = kv[:, 4 * KV_W:5 * KV_W].astype(BF16)
    vw_ref[...] = kv[:, 5 * KV_W:].astype(BF16)
    gt_ref[...] = jax.nn.sigmoid(_dot(hb, wgt_ref[...]))
    tm = kv.shape[0]
    kc_ref[...] = (kv[:, :KV_W] * wck_ref[...]).reshape(tm // CMP_BLOCK, CMP_BLOCK, KV_W).sum(axis=1)
    vc_ref[...] = (kv[:, KV_W:2 * KV_W] * wcv_ref[...]).reshape(tm // CMP_BLOCK, CMP_BLOCK, KV_W).sum(axis=1)


def _nsa_proj(x, g, wq, wkv, wgt, wck, wcv, *, tm):
    N, D = x.shape
    assert N % tm == 0 and tm % (8 * CMP_BLOCK) == 0
    row = lambda w: pl.BlockSpec((tm, w), lambda i: (i, 0))
    full = lambda a: pl.BlockSpec(a.shape, lambda i: (0, 0))
    out_shape = [jax.ShapeDtypeStruct((N, N_HEADS * HEAD_DIM), BF16),
                 jax.ShapeDtypeStruct((N, 4 * KV_W), F32), jax.ShapeDtypeStruct((N, 2 * KV_W), F32)]
    out_shape += [jax.ShapeDtypeStruct((N, KV_W), BF16)] * 4
    out_shape += [jax.ShapeDtypeStruct((N, LANES), F32)]
    out_shape += [jax.ShapeDtypeStruct((N // CMP_BLOCK, KV_W), F32)] * 2
    out_specs = [row(N_HEADS * HEAD_DIM), row(4 * KV_W), row(2 * KV_W)] + [row(KV_W)] * 4 + [row(LANES)]
    out_specs += [pl.BlockSpec((tm // CMP_BLOCK, KV_W), lambda i: (i, 0))] * 2
    return pl.pallas_call(
        _nsa_proj_kernel,
        grid=(N // tm,),
        in_specs=[row(D), full(g), full(wq), full(wkv), full(wgt), full(wck), full(wcv)],
        out_specs=out_specs,
        out_shape=out_shape,
        compiler_params=_cparams(("parallel",)),
        name="nsa_proj_prompt",
    )(x, g, wq, wkv, wgt, wck, wcv)


def _nsa_proj_sample_kernel(x_ref, g_ref, w_ref, z_ref):
    z_ref[...] = _dot3(_rmsnorm(x_ref[...], g_ref[...]), w_ref[...])


def _nsa_proj_sample(x, g, w_in, tn=384):
    B, D = x.shape
    cols = w_in.shape[1]
    assert cols % tn == 0
    return pl.pallas_call(
        _nsa_proj_sample_kernel,
        grid=(cols // tn,),
        in_specs=[pl.BlockSpec((B, D), lambda j: (0, 0)), pl.BlockSpec((1, D), lambda j: (0, 0)),
                  pl.BlockSpec((D, tn), lambda j: (0, j))],
        out_specs=pl.BlockSpec((B, tn), lambda j: (0, j)),
        out_shape=jax.ShapeDtypeStruct((B, cols), F32),
        compiler_params=_cparams(("parallel",)),
        name="nsa_proj_sample",
    )(x, g, w_in)


def _nsa_prompt_kernel(q_ref, gt_ref, kc_ref, vc_ref, ks_ref, vs_ref, kw_ref, vw_ref, x_ref, wo_ref,
                       o_ref, o_scr, *, tq, kchunk, seq):
    t0 = pl.program_id(1) * tq
    rows = GQA * tq
    scale = HEAD_DIM ** -0.5
    n_blocks = seq // SLC_BLOCK
    top_k = min(SLC_TOPK, n_blocks)

    q = q_ref[0]
    gates = gt_ref[0]
    qpos_t = t0 + lax.broadcasted_iota(jnp.int32, (tq, 1), 0)
    qpos = jnp.concatenate([qpos_t] * GQA, axis=0)
    tq_f = qpos.astype(F32)

    lane = lax.broadcasted_iota(jnp.int32, (tq, LANES), 1)
    jblk = lane >> 1
    real = ((lane & 1) == 0) & (jblk < n_blocks)
    blk = qpos_t // SLC_BLOCK
    valid = real & (jblk <= blk)
    forced = (jblk == 0) | (jblk == blk) | (jblk == blk - 1)
    c_lane = lax.broadcasted_iota(jnp.int32, (1, LANES), 1)
    c_mid = c_lane.astype(F32) * CMP_BLOCK + (CMP_BLOCK - 1) / 2
    c_end = (c_lane + 1) * CMP_BLOCK - 1

    n_chunks = (t0 + tq + kchunk - 1) // kchunk
    wstart = pl.multiple_of(jnp.maximum(t0 - WINDOW, 0), tq)
    wlen = WINDOW + tq

    for kv in range(N_KV_HEADS):
        hs = slice(kv * HEAD_DIM, (kv + 1) * HEAD_DIM)
        qs = jnp.concatenate(
            [q[:, (kv * GQA + g) * HEAD_DIM:(kv * GQA + g + 1) * HEAD_DIM] for g in range(GQA)], axis=0)
        slope = _head_slopes(rows, tq, kv * GQA)

        s_c = _dot_nt(qs, kc_ref[0][:, hs]) * scale - slope * (tq_f - c_mid)
        p_c = _masked_softmax(s_c, c_end <= qpos)
        o_c = _dot(p_c.astype(BF16), vc_ref[0][:, hs])

        imp = p_c[0:tq]
        for g in range(1, GQA):
            imp = imp + p_c[g * tq:(g + 1) * tq]
        pair = imp + pltpu.roll(imp, LANES - 1, 1)
        score = jnp.where(valid, pair + jnp.where(forced, FORCE_BONUS, 0.0), -1.0)
        score = jnp.where(real, score, -2.0)
        rank = jnp.zeros((tq, LANES), jnp.int32)
        for k in range(n_blocks):
            col = score[:, 2 * k:2 * k + 1]
            beats = (col > score) | ((col == score) & (lane > 2 * k))
            rank = rank + beats.astype(jnp.int32)
        sel = jnp.where(real & (rank < top_k), 1.0, 0.0).astype(BF16)

        def chunk(c, carry):
            m, l, acc = carry
            k0 = pl.multiple_of(c * kchunk, kchunk)
            kb = ks_ref[0, pl.ds(k0, kchunk), hs]
            vb = vs_ref[0, pl.ds(k0, kchunk), hs]
            pos = k0 + lax.broadcasted_iota(jnp.int32, (1, kchunk), 1)
            s = _dot_nt(qs, kb) * scale - slope * (tq_f - pos.astype(F32))
            expand = (lax.broadcasted_iota(jnp.int32, (LANES, kchunk), 0)
                      == 2 * ((k0 + lax.broadcasted_iota(jnp.int32, (LANES, kchunk), 1)) // SLC_BLOCK))
            picked = _dot(sel, jnp.where(expand, 1.0, 0.0).astype(BF16))
            picked = jnp.where(pos <= qpos_t, picked, 0.0)
            mk = jnp.concatenate([picked] * GQA, axis=0) > 0.5
            s = jnp.where(mk, s, NEG)
            m_new = jnp.maximum(m, jnp.max(s, axis=-1, keepdims=True))
            alpha = jnp.exp(m - m_new)
            p = jnp.where(mk, jnp.exp(s - m_new), 0.0)
            l = alpha * l + jnp.sum(p, axis=-1, keepdims=True)
            acc = alpha * acc + _dot(p.astype(BF16), vb)
            return m_new, l, acc

        init = (jnp.full((rows, 1), NEG, F32), jnp.zeros((rows, 1), F32), jnp.zeros((rows, HEAD_DIM), F32))
        _, l_s, acc_s = lax.fori_loop(0, n_chunks, chunk, init)
        o_s = acc_s / jnp.maximum(l_s, 1e-30)

        kb = kw_ref[0, pl.ds(wstart, wlen), hs]
        vb = vw_ref[0, pl.ds(wstart, wlen), hs]
        wpos = wstart + lax.broadcasted_iota(jnp.int32, (1, wlen), 1)
        s_w = _dot_nt(qs, kb) * scale - slope * (tq_f - wpos.astype(F32))
        p_w = _masked_softmax(s_w, (wpos <= qpos) & (qpos - wpos < WINDOW))
        o_w = _dot(p_w.astype(BF16), vb)

        def gate_col(n):
            base = n * N_HEADS + kv * GQA
            return jnp.concatenate([gates[:, base + g:base + g + 1] for g in range(GQA)], axis=0)

        o = gate_col(0) * o_c + gate_col(1) * o_s + gate_col(2) * o_w
        for g in range(GQA):
            h = kv * GQA + g
            o_scr[:, h * HEAD_DIM:(h + 1) * HEAD_DIM] = o[g * tq:(g + 1) * tq]

    o_ref[0] = x_ref[0] + _dot(o_scr[...].astype(BF16), wo_ref[...])


def _nsa_prompt(q, gates, kc, vc, ks, vs, kw, vw, x, w_o, *, tq=128, kchunk=512):
    B, L, D = x.shape
    assert L % kchunk == 0 and kchunk % tq == 0 and L >= WINDOW + tq and L % SLC_BLOCK == 0
    assert 2 * (L // SLC_BLOCK) <= LANES and L // CMP_BLOCK <= LANES and kc.shape[1] == LANES
    tile = lambda w: pl.BlockSpec((1, tq, w), lambda b, t: (b, t, 0))
    whole = lambda a: pl.BlockSpec((1,) + a.shape[1:], lambda b, t: (b, 0, 0))
    return pl.pallas_call(
        functools.partial(_nsa_prompt_kernel, tq=tq, kchunk=kchunk, seq=L),
        grid=(B, L // tq),
        in_specs=[tile(N_HEADS * HEAD_DIM), tile(LANES), whole(kc), whole(vc), whole(ks), whole(vs),
                  whole(kw), whole(vw), tile(D), pl.BlockSpec(w_o.shape, lambda b, t: (0, 0))],
        out_specs=tile(D),
        out_shape=jax.ShapeDtypeStruct((B, L, D), F32),
        scratch_shapes=[pltpu.VMEM((tq, N_HEADS * HEAD_DIM), F32)],
        compiler_params=_cparams(("parallel", "arbitrary")),
        name="nsa_attn_prompt",
    )(q, gates, kc, vc, ks, vs, kw, vw, x, w_o)


CMP_GROUP = 32
MXU_DEPTH = 256


def _cmp_sample_kernel(pt_ref, cache_ref, w_ref, s_ref, o_ref, buf, sems, *, rows):
    b, g = pl.program_id(0), pl.program_id(1)
    n_groups = pl.num_programs(1)
    step = b * n_groups + g
    slot = step % 2

    def page_copy(bb, gg, sl, i):
        page = pt_ref[bb, gg * CMP_GROUP + i]
        return pltpu.make_async_copy(cache_ref.at[page, pl.ds(0, rows), :], buf.at[sl, i], sems.at[sl])

    def start_group(bb, gg, sl):
        for i in range(CMP_GROUP):
            page_copy(bb, gg, sl, i).start()

    @pl.when(step == 0)
    def _():
        start_group(b, g, slot)

    @pl.when(step + 1 < pl.num_programs(0) * n_groups)
    def _():
        nxt = step + 1
        start_group(nxt // n_groups, nxt % n_groups, 1 - slot)

    for i in range(CMP_GROUP):
        page_copy(b, g, slot, i).wait()

    w = w_ref[...]
    acc = jnp.zeros((rows, LANES), F32)
    for j in range(CMP_GROUP // 2):
        p = jnp.concatenate([buf[slot, 2 * j] * w, buf[slot, 2 * j + 1] * w], axis=1)
        hi, lo = _split(p)
        r = _dot(jnp.concatenate([hi, lo], axis=0), s_ref[j])
        acc = acc + (r[:rows] + r[rows:])
    o_ref[0] = acc


def _cmp_sample(page_table, cache_t, w_rows):
    B, n_pages = page_table.shape
    _, _, page = cache_t.shape
    rows = w_rows.shape[0]
    per_page = page // CMP_BLOCK
    assert n_pages % CMP_GROUP == 0 and CMP_GROUP * per_page == LANES and 2 * page == MXU_DEPTH
    k = jnp.arange(2 * page)
    token = (2 * jnp.arange(CMP_GROUP // 2)[:, None] + k[None, :] // page) * per_page + (k[None, :] % page) // CMP_BLOCK
    block_sum = (token[:, :, None] == jnp.arange(LANES)[None, None, :]).astype(BF16)
    return pl.pallas_call(
        functools.partial(_cmp_sample_kernel, rows=rows),
        grid_spec=pltpu.PrefetchScalarGridSpec(
            num_scalar_prefetch=1,
            grid=(B, n_pages // CMP_GROUP),
            in_specs=[pl.BlockSpec(memory_space=pl.ANY),
                      pl.BlockSpec(w_rows.shape, lambda b, g, pt: (0, 0)),
                      pl.BlockSpec(block_sum.shape, lambda b, g, pt: (0, 0, 0))],
            out_specs=pl.BlockSpec((1, rows, LANES), lambda b, g, pt: (b, 0, g)),
            scratch_shapes=[pltpu.VMEM((2, CMP_GROUP, rows, page), F32), pltpu.SemaphoreType.DMA((2,))],
        ),
        out_shape=jax.ShapeDtypeStruct((B, rows, n_pages * per_page), F32),
        compiler_params=_cparams(("arbitrary", "arbitrary")),
        name="cmp_sample",
    )(page_table, cache_t, w_rows, block_sum)


def _select_sample_kernel(q_ref, kcv_ref, oc_ref, idx_ref, *, past):
    scale = HEAD_DIM ** -0.5
    q = q_ref[0]
    nc = kcv_ref.shape[2]
    n_past_blocks = past // SLC_BLOCK
    per_block = SLC_BLOCK // CMP_BLOCK
    row = lax.broadcasted_iota(jnp.int32, (N_HEADS, 1), 0)
    slope = _head_slopes(N_HEADS, 1, 0)
    tok = lax.broadcasted_iota(jnp.int32, (1, nc), 1)
    c_mid = tok.astype(F32) * CMP_BLOCK + (CMP_BLOCK - 1) / 2
    c_ok = (tok + 1) * CMP_BLOCK - 1 <= past
    bias = slope * (float(past) - c_mid)

    s_c = jnp.zeros((N_HEADS, nc), F32)
    for kv in range(N_KV_HEADS):
        s_kv = _dot3(q, kcv_ref[0, kv * HEAD_DIM:(kv + 1) * HEAD_DIM, :])
        s_c = jnp.where(row // GQA == kv, s_kv, s_c)
    p_c = _masked_softmax(s_c * scale - bias, c_ok)
    o_c = jnp.zeros((N_HEADS, HEAD_DIM), F32)
    for kv in range(N_KV_HEADS):
        o_kv = _dot3(p_c, kcv_ref[0, KV_W + kv * HEAD_DIM:KV_W + (kv + 1) * HEAD_DIM, :], nt=True)
        o_c = jnp.where(row // GQA == kv, o_kv, o_c)
    oc_ref[0] = o_c

    imp = p_c
    shift = 1
    while shift < GQA:
        imp = imp + pltpu.roll(imp, shift, 0)
        shift *= 2
    assert per_block == 2
    pair = imp + pltpu.roll(imp, nc - 1, 1)
    jblk = tok // per_block
    real = tok % per_block == 0
    forced = (jblk == 0) | (jblk == n_past_blocks - 1)
    score = jnp.where(real, pair + jnp.where(forced, FORCE_BONUS, 0.0), -2.0)
    own = jnp.float32(FORCE_BONUS)
    rank = (own > score).astype(jnp.int32)
    for k in range(n_past_blocks):
        col = score[:, per_block * k:per_block * k + 1]
        beats = (col > score) | ((col == score) & (jblk > k))
        rank = rank + beats.astype(jnp.int32)
    own_rank = jnp.sum(jnp.where(real & (score >= own), 1, 0), axis=-1, keepdims=True)
    out_lane = lax.broadcasted_iota(jnp.int32, (N_HEADS, LANES), 1)
    idx = jnp.zeros((N_HEADS, LANES), jnp.int32)
    for r in range(SLC_TOPK):
        hit = jnp.sum(jnp.where(real & (rank == r), jblk, 0), axis=-1, keepdims=True)
        hit = hit + jnp.where(own_rank == r, n_past_blocks, 0)
        idx = jnp.where(out_lane == r, hit, idx)
    idx_ref[0] = idx


def _select_sample(q16, kcv, past):
    B = q16.shape[0]
    nc = kcv.shape[2]
    assert past % SLC_BLOCK == 0 and nc == past // CMP_BLOCK and nc % LANES == 0
    assert past // SLC_BLOCK + 1 > SLC_TOPK
    return pl.pallas_call(
        functools.partial(_select_sample_kernel, past=past),
        grid=(B,),
        in_specs=[pl.BlockSpec((1,) + q16.shape[1:], lambda b: (b, 0, 0)),
                  pl.BlockSpec((1,) + kcv.shape[1:], lambda b: (b, 0, 0))],
        out_specs=[pl.BlockSpec((1, N_HEADS, HEAD_DIM), lambda b: (b, 0, 0)),
                   pl.BlockSpec((1, N_HEADS, LANES), lambda b: (b, 0, 0))],
        out_shape=[jax.ShapeDtypeStruct((B, N_HEADS, HEAD_DIM), F32),
                   jax.ShapeDtypeStruct((B, N_HEADS, LANES), jnp.int32)],
        compiler_params=_cparams(("parallel",)),
        name="select_sample",
    )(q16, kcv)


def _attend_sample_kernel(pt_ref, idx_ref, q_ref, oc_ref, gt_ref, new_ref, win_ref, cache_ref,
                          o_ref, kbuf, vbuf, sems, *, past):
    b = pl.program_id(0)
    scale = HEAD_DIM ** -0.5
    n_past_blocks = past // SLC_BLOCK
    page_rows = cache_ref.shape[2]
    blocks_per_page = page_rows // SLC_BLOCK
    nkeys = SLC_TOPK * page_rows

    def block_copies(kv, n):
        blk = idx_ref[(b * N_KV_HEADS + kv) * SLC_TOPK + n]
        in_past = blk < n_past_blocks
        page = pt_ref[b, jnp.minimum(blk, n_past_blocks - 1) // blocks_per_page]
        dst = pl.ds(n * page_rows, page_rows)
        ck = pltpu.make_async_copy(cache_ref.at[page, pl.ds((2 * N_KV_HEADS + kv) * HEAD_DIM, HEAD_DIM), :],
                                   kbuf.at[kv, :, dst], sems.at[0, kv, n])
        cv = pltpu.make_async_copy(cache_ref.at[page, pl.ds((3 * N_KV_HEADS + kv) * HEAD_DIM, HEAD_DIM), :],
                                   vbuf.at[kv, :, dst], sems.at[1, kv, n])
        return blk, in_past, ck, cv

    for kv in range(N_KV_HEADS):
        for n in range(SLC_TOPK):
            _, in_past, ck, cv = block_copies(kv, n)

            @pl.when(in_past)
            def _():
                ck.start()
                cv.start()

            @pl.when(jnp.logical_not(in_past))
            def _():
                kbuf[kv, :, n * page_rows:(n + 1) * page_rows] = jnp.zeros((HEAD_DIM, page_rows), F32)
                vbuf[kv, :, n * page_rows:(n + 1) * page_rows] = jnp.zeros((HEAD_DIM, page_rows), F32)

    q = q_ref[0]
    row = lax.broadcasted_iota(jnp.int32, (N_HEADS, 1), 0)
    slope = _head_slopes(N_HEADS, 1, 0)
    key_lane = lax.broadcasted_iota(jnp.int32, (1, nkeys), 1)

    wb = win_ref.shape[2]
    w_lane = lax.broadcasted_iota(jnp.int32, (1, wb), 1)
    wpos = past - wb + w_lane
    w_ok = (past - wpos < WINDOW) & (wpos >= 0)
    w_bias = slope * (past - wpos).astype(F32)
    s_w = jnp.zeros((N_HEADS, wb), F32)
    s_n = jnp.zeros((N_HEADS, 1), F32)
    for kv in range(N_KV_HEADS):
        mine = row // GQA == kv
        s_kv = _dot3(q, win_ref[0, kv * HEAD_DIM:(kv + 1) * HEAD_DIM, :])
        s_w = jnp.where(mine, s_kv, s_w)
        s_n = jnp.where(mine, jnp.sum(q * new_ref[0, 4, kv:kv + 1, :], axis=-1, keepdims=True), s_n)
    s_w = jnp.where(w_ok, s_w * scale - w_bias, NEG)
    s_n = s_n * scale
    m_w = jnp.maximum(jnp.max(s_w, axis=-1, keepdims=True), s_n)
    p_w = jnp.where(w_ok, jnp.exp(s_w - m_w), 0.0)
    p_n = jnp.exp(s_n - m_w)
    l_w = jnp.maximum(jnp.sum(p_w, axis=-1, keepdims=True) + p_n, 1e-30)
    p_w = p_w / l_w
    p_n = p_n / l_w
    o_w = jnp.zeros((N_HEADS, HEAD_DIM), F32)
    for kv in range(N_KV_HEADS):
        o_kv = (_dot3(p_w, win_ref[0, KV_W + kv * HEAD_DIM:KV_W + (kv + 1) * HEAD_DIM, :], nt=True)
                + p_n * new_ref[0, 5, kv:kv + 1, :])
        o_w = jnp.where(row // GQA == kv, o_kv, o_w)

    for kv in range(N_KV_HEADS):
        for n in range(SLC_TOPK):
            _, in_past, ck, cv = block_copies(kv, n)

            @pl.when(in_past)
            def _():
                ck.wait()
                cv.wait()

    s_s = jnp.zeros((N_HEADS, nkeys), F32)
    pos = jnp.zeros((N_HEADS, nkeys), jnp.int32)
    live = jnp.zeros((N_HEADS, nkeys), jnp.int32)
    own = jnp.zeros((N_HEADS, 1), jnp.int32)
    s_n = jnp.zeros((N_HEADS, 1), F32)
    in_slab = key_lane % page_rows
    for kv in range(N_KV_HEADS):
        mine = row // GQA == kv
        s_kv = _dot3(q, kbuf[kv])
        s_s = jnp.where(mine, s_kv, s_s)
        s_n = jnp.where(mine, jnp.sum(q * new_ref[0, 2, kv:kv + 1, :], axis=-1, keepdims=True), s_n)
        base_kv = jnp.zeros((1, nkeys), jnp.int32)
        half_kv = jnp.zeros((1, nkeys), jnp.int32)
        own_kv = jnp.int32(0)
        for n in range(SLC_TOPK):
            blk = idx_ref[(b * N_KV_HEADS + kv) * SLC_TOPK + n]
            here = key_lane // page_rows == n
            base_kv = jnp.where(here, (blk // blocks_per_page) * page_rows, base_kv)
            half_kv = jnp.where(here, blk % blocks_per_page, half_kv)
            own_kv = own_kv + (blk >= n_past_blocks).astype(jnp.int32)
        pos = jnp.where(mine, base_kv + in_slab, pos)
        live = jnp.where(mine, (in_slab // SLC_BLOCK == half_kv).astype(jnp.int32), live)
        own = jnp.where(mine, own_kv, own)
    s_ok = (live > 0) & (pos < past)
    n_ok = own > 0
    s_s = jnp.where(s_ok, s_s * scale - slope * (past - pos).astype(F32), NEG)
    s_n = jnp.where(n_ok, s_n * scale, NEG)
    m_s = jnp.maximum(jnp.max(s_s, axis=-1, keepdims=True), s_n)
    p_s = jnp.where(s_ok, jnp.exp(s_s - m_s), 0.0)
    p_n = jnp.where(n_ok, jnp.exp(s_n - m_s), 0.0)
    l_s = jnp.maximum(jnp.sum(p_s, axis=-1, keepdims=True) + p_n, 1e-30)
    p_s = p_s / l_s
    p_n = p_n / l_s
    o_s = jnp.zeros((N_HEADS, HEAD_DIM), F32)
    for kv in range(N_KV_HEADS):
        o_kv = _dot3(p_s, vbuf[kv], nt=True) + p_n * new_ref[0, 3, kv:kv + 1, :]
        o_s = jnp.where(row // GQA == kv, o_kv, o_s)

    gt = jax.nn.sigmoid(gt_ref[0])
    o_ref[0] = gt[:, 0:1] * oc_ref[0] + gt[:, 1:2] * o_s + gt[:, 2:3] * o_w


def _attend_sample(page_table, idx, q16, o_c, gates, kv_new, win_t, cache_t, past):
    B = q16.shape[0]
    page_rows = cache_t.shape[2]
    assert win_t.shape[2] <= WINDOW and page_rows % SLC_BLOCK == 0
    blk3 = lambda a: pl.BlockSpec((1,) + a.shape[1:], lambda b, pt, ix: (b,) + (0,) * (a.ndim - 1))
    return pl.pallas_call(
        functools.partial(_attend_sample_kernel, past=past),
        grid_spec=pltpu.PrefetchScalarGridSpec(
            num_scalar_prefetch=2,
            grid=(B,),
            in_specs=[blk3(q16), blk3(o_c), blk3(gates), blk3(kv_new), blk3(win_t),
                      pl.BlockSpec(memory_space=pl.ANY)],
            out_specs=pl.BlockSpec((1, N_HEADS, HEAD_DIM), lambda b, pt, ix: (b, 0, 0)),
            scratch_shapes=[pltpu.VMEM((N_KV_HEADS, HEAD_DIM, SLC_TOPK * page_rows), F32),
                            pltpu.VMEM((N_KV_HEADS, HEAD_DIM, SLC_TOPK * page_rows), F32),
                            pltpu.SemaphoreType.DMA((2, N_KV_HEADS, SLC_TOPK))],
        ),
        out_shape=jax.ShapeDtypeStruct((B, N_HEADS, HEAD_DIM), F32),
        compiler_params=_cparams(("arbitrary",)),
        name="attend_sample",
    )(page_table, idx, q16, o_c, gates, kv_new, win_t, cache_t)


def _linear_res_kernel(x_ref, a_ref, w_ref, o_ref):
    o_ref[...] = x_ref[...] + _dot3(a_ref[...], w_ref[...])


def _linear_res(x, a, w):
    return pl.pallas_call(
        _linear_res_kernel,
        out_shape=jax.ShapeDtypeStruct(x.shape, F32),
        compiler_params=pltpu.CompilerParams(vmem_limit_bytes=VMEM_LIMIT),
        name="out_proj_sample",
    )(x, a, w)


def _expand_cmp_weights(w_cmp, rows):
    w = jnp.repeat(w_cmp, HEAD_DIM, axis=1)
    return jnp.tile(w, (rows // CMP_BLOCK, 1))


def _row(v):
    return v.reshape(1, -1)


def _prompt_mixers(x, norm_mix, w_pool, pool_scale, w_in, w_cmp_k, w_cmp_v, w_out, norm_ffn0, wfg, wfu, wfd):
    B, L, D = x.shape
    q_w = N_HEADS * HEAD_DIM
    x, h_last = _pool_prompt(x, _row(norm_mix[0]), w_pool.astype(BF16), _row(pool_scale))
    new_pool = h_last[None, :, POOL_HALO - (max(POOL_WINDOWS) - 1):, :]
    x = _ffn(x.reshape(B * L, D), _row(norm_ffn0), wfg.astype(BF16), wfu.astype(BF16), wfd.astype(BF16), tm=1024)

    tm = 512
    wq = w_in[:, :q_w].astype(BF16)
    wkv = w_in[:, q_w:q_w + 6 * KV_W].astype(BF16)
    wgt = jnp.pad(w_in[:, q_w + 6 * KV_W:], ((0, 0), (0, LANES - N_BRANCH * N_HEADS))).astype(BF16)
    q, kv4, kwin, ks, vs, kw, vw, gates, kc, vc = _nsa_proj(
        x, _row(norm_mix[1]), wq, wkv, wgt, _expand_cmp_weights(w_cmp_k, tm), _expand_cmp_weights(w_cmp_v, tm),
        tm=tm)
    new_kv = kv4.reshape(1, B, L, 4, N_KV_HEADS, HEAD_DIM)
    new_win = kwin.reshape(B, L, 2, N_KV_HEADS, HEAD_DIM)[None, :, L - min(WINDOW, L):]
    nc = L // CMP_BLOCK
    pad_c = lambda a: jnp.pad(a.reshape(B, nc, KV_W), ((0, 0), (0, LANES - nc), (0, 0))).astype(BF16)
    per_seq = lambda a: a.reshape(B, L, a.shape[-1])
    x = _nsa_prompt(per_seq(q), per_seq(gates), pad_c(kc), pad_c(vc), per_seq(ks), per_seq(vs),
                    per_seq(kw), per_seq(vw), per_seq(x), w_out.astype(BF16))
    return x.reshape(B * L, D), new_pool, new_kv, new_win


def _sample_mixers(x, state_pool, cache, state_win, page_table, norm_mix, w_pool, pool_scale, w_in, w_cmp_k,
                   w_cmp_v, w_out, norm_ffn0, wfg, wfu, wfd):
    SB, D = x.shape
    n_phys, page = cache.shape[:2]
    past = page_table.shape[1] * page
    q_w = N_HEADS * HEAD_DIM
    x, h = _pool_sample(x, state_pool.transpose(1, 0, 2), _row(norm_mix[0]), w_pool, _row(pool_scale))
    new_pool = jnp.concatenate([state_pool[:, 1:], h[:, None]], axis=1)[None]
    x = _ffn(x, _row(norm_ffn0), wfg, wfu, wfd, tm=SB)

    cols = w_in.shape[1]
    z = _nsa_proj_sample(x, _row(norm_mix[1]), jnp.pad(w_in, ((0, 0), (0, -cols % LANES))))
    kv_new = z[:, q_w:q_w + 6 * KV_W].reshape(SB, 6, N_KV_HEADS, HEAD_DIM)
    new_kv = kv_new[None, :, None, :4]
    keep = min(WINDOW, state_win.shape[1] + 1)
    new_win = jnp.concatenate([state_win, kv_new[:, None, 4:]], axis=1)[None, :, -keep:]

    cache_t = cache.transpose(0, 2, 3, 4, 1).reshape(n_phys, 4 * KV_W, page)
    win_t = state_win.transpose(0, 2, 3, 4, 1).reshape(SB, 2 * KV_W, state_win.shape[1])
    w_rows = jnp.concatenate([_expand_cmp_weights(w_cmp_k, page).T, _expand_cmp_weights(w_cmp_v, page).T], axis=0)
    kcv = _cmp_sample(page_table, cache_t, w_rows)
    q16 = z[:, :q_w].reshape(SB, N_HEADS, HEAD_DIM)
    o_c, idx = _select_sample(q16, kcv, past)
    idx = idx[:, GQA - 1::GQA, :SLC_TOPK].reshape(-1)
    gate_logits = z[:, q_w + 6 * KV_W:cols].reshape(SB, N_BRANCH, N_HEADS).transpose(0, 2, 1)
    gate_logits = jnp.pad(gate_logits, ((0, 0), (0, 0), (0, LANES - N_BRANCH)))
    o = _attend_sample(page_table, idx, q16, o_c, gate_logits, kv_new, win_t, cache_t, past)
    return _linear_res(x, o.reshape(SB, q_w), w_out), new_pool, new_kv, new_win


def kernel(x_prompt, x_sample, state_pool, cache_kv, state_win, page_table, norm_mix, w_pool, pool_scale,
           w_nsa_in, w_cmp_k, w_cmp_v, w_nsa_out, norm_ffn, w_ffn_gate, w_ffn_up, w_ffn_down, w_router,
           w_moe_gate, w_moe_up, w_moe_down, norm_final):
    assert x_sample.shape[1] == 1 and norm_mix.shape[0] == 2 and x_prompt.shape[-1] == N_HEADS * HEAD_DIM
    w_rt = jnp.pad(w_router[0], ((0, 0), (0, LANES - w_router.shape[-1])))
    moe_w = (w_moe_gate[0].astype(BF16), w_moe_up[0].astype(BF16), w_moe_down[0].astype(BF16))

    xp, new_pool_prompt, new_kv_prompt, new_win_prompt = _prompt_mixers(
        x_prompt, norm_mix, w_pool[0], pool_scale[0], w_nsa_in[0], w_cmp_k[0], w_cmp_v[0], w_nsa_out[0],
        norm_ffn[0], w_ffn_gate, w_ffn_up, w_ffn_down)
    xs, new_pool_sample, new_kv_sample, new_win_sample = _sample_mixers(
        x_sample[:, 0], state_pool[0], cache_kv[0], state_win[0], page_table, norm_mix, w_pool[0], pool_scale[0],
        w_nsa_in[0], w_cmp_k[0], w_cmp_v[0], w_nsa_out[0], norm_ffn[0], w_ffn_gate, w_ffn_up, w_ffn_down)

    yp = _moe_prompt(xp, _row(norm_ffn[1]), w_rt, *moe_w, _row(norm_final))
    ys = _ffn(xs, _row(norm_ffn[1]), *moe_w, w_rt, _row(norm_final), tm=xs.shape[0])
    return (yp.reshape(x_prompt.shape), ys.reshape(x_sample.shape), new_pool_prompt, new_pool_sample,
            new_kv_prompt, new_kv_sample, new_win_prompt, new_win_sample)
```

```python
import functools

import jax
import jax.numpy as jnp
from jax import lax
from jax.experimental import pallas as pl
from jax.experimental.pallas import tpu as pltpu

F32 = jnp.float32
BF16 = jnp.bfloat16

EPS = 1e-6
NEG = -1e30
POOL_WINDOWS = (2, 4, 8, 16)
POOL_HALO = 16
N_HEADS = 16
N_KV_HEADS = 4
GQA = N_HEADS // N_KV_HEADS
HEAD_DIM = 64
KV_W = N_KV_HEADS * HEAD_DIM
CMP_BLOCK = 32
SLC_BLOCK = 64
SLC_TOPK = 16
WINDOW = 512
N_BRANCH = 3
FORCE_BONUS = 1000.0
TOP_K = 2
LANES = 128
VMEM_LIMIT = 56 * 1024 * 1024


def _cparams(sem):
    return pltpu.CompilerParams(dimension_semantics=sem, vmem_limit_bytes=VMEM_LIMIT)


def _rmsnorm(x, g):
    ms = jnp.mean(x * x, axis=-1, keepdims=True)
    return (x * lax.rsqrt(ms + EPS)) * g


def _dot(a, b):
    return jnp.dot(a, b, preferred_element_type=F32)


def _dot_nt(a, b):
    return lax.dot_general(a, b, (((1,), (1,)), ((), ())), preferred_element_type=F32)


def _split(a):
    hi = a.astype(BF16)
    return hi, (a - hi.astype(F32)).astype(BF16)


def _dot3(a, b, nt=False):
    d = _dot_nt if nt else _dot
    ah, al = _split(a)
    bh, bl = _split(b)
    return d(ah, bh) + (d(ah, bl) + d(al, bh))


def _masked_softmax(s, mask):
    s = jnp.where(mask, s, NEG)
    m = jnp.max(s, axis=-1, keepdims=True)
    p = jnp.where(mask, jnp.exp(s - m), 0.0)
    return p / jnp.maximum(jnp.sum(p, axis=-1, keepdims=True), 1e-30)


def _alibi_slope(h):
    return 2.0 ** (-8.0 * (h + 1) / N_HEADS)


def _head_slopes(rows, rows_per_head, first_head):
    h = lax.broadcasted_iota(jnp.int32, (rows, 1), 0) // rows_per_head
    slopes = jnp.zeros((rows, 1), F32)
    for j in range(rows // rows_per_head):
        slopes = jnp.where(h == j, _alibi_slope(first_head + j), slopes)
    return slopes


def _pool_prompt_kernel(x_ref, halo_ref, g_ref, w_ref, sc_ref, o_ref, hl_ref, full_ref, *, tile):
    t = pl.program_id(1)
    g = g_ref[...]
    x = x_ref[0]
    h = _rmsnorm(x, g)
    hh = _rmsnorm(halo_ref[0], g)
    full_ref[0:POOL_HALO, :] = jnp.where(t > 0, hh, 0.0)
    full_ref[POOL_HALO:POOL_HALO + tile, :] = h
    row = t * tile + lax.broadcasted_iota(jnp.int32, (tile, 1), 0)
    group = x.shape[-1] // len(POOL_WINDOWS)
    parts = []
    for gi, w in enumerate(POOL_WINDOWS):
        cs = slice(gi * group, (gi + 1) * group)
        hg = h[:, cs]
        acc = hg
        for k in range(1, w):
            acc = acc + full_ref[POOL_HALO - k:POOL_HALO - k + tile, cs]
        cnt = jnp.minimum(row + 1, w).astype(F32)
        pooled = acc / cnt - hg
        parts.append(_dot(pooled.astype(BF16), w_ref[gi]))
    o_ref[0] = x + jnp.concatenate(parts, axis=-1) * sc_ref[...]

    @pl.when(t == pl.num_programs(1) - 1)
    def _():
        hl_ref[0] = h[tile - POOL_HALO:, :]


def _pool_prompt(x, g, w_pool, scale, tile=512):
    B, L, D = x.shape
    assert L % tile == 0 and tile % POOL_HALO == 0
    hb = tile // POOL_HALO
    return pl.pallas_call(
        functools.partial(_pool_prompt_kernel, tile=tile),
        grid=(B, L // tile),
        in_specs=[
            pl.BlockSpec((1, tile, D), lambda b, t: (b, t, 0)),
            pl.BlockSpec((1, POOL_HALO, D), lambda b, t: (b, jnp.maximum(t * hb - 1, 0), 0)),
            pl.BlockSpec((1, D), lambda b, t: (0, 0)),
            pl.BlockSpec(w_pool.shape, lambda b, t: (0, 0, 0)),
            pl.BlockSpec((1, D), lambda b, t: (0, 0)),
        ],
        out_specs=[
            pl.BlockSpec((1, tile, D), lambda b, t: (b, t, 0)),
            pl.BlockSpec((1, POOL_HALO, D), lambda b, t: (b, 0, 0)),
        ],
        out_shape=[jax.ShapeDtypeStruct((B, L, D), F32), jax.ShapeDtypeStruct((B, POOL_HALO, D), F32)],
        scratch_shapes=[pltpu.VMEM((tile + POOL_HALO, D), F32)],
        compiler_params=_cparams(("parallel", "arbitrary")),
        name="pool_prompt",
    )(x, x, g, w_pool, scale)


def _pool_sample_kernel(x_ref, st_ref, g_ref, w_ref, sc_ref, o_ref, h_ref):
    x = x_ref[...]
    h = _rmsnorm(x, g_ref[...])
    h_ref[...] = h
    P = st_ref.shape[0]
    group = x.shape[-1] // len(POOL_WINDOWS)
    parts = []
    for gi, w in enumerate(POOL_WINDOWS):
        cs = slice(gi * group, (gi + 1) * group)
        hg = h[:, cs]
        acc = hg
        for k in range(1, w):
            acc = acc + st_ref[P - k][:, cs]
        pooled = acc / float(w) - hg
        parts.append(_dot3(pooled, w_ref[gi]))
    o_ref[...] = x + jnp.concatenate(parts, axis=-1) * sc_ref[...]


def _pool_sample(x, state_t, g, w_pool, scale):
    B, D = x.shape
    assert state_t.shape[0] >= max(POOL_WINDOWS) - 1
    return pl.pallas_call(
        _pool_sample_kernel,
        out_shape=[jax.ShapeDtypeStruct((B, D), F32), jax.ShapeDtypeStruct((B, D), F32)],
        compiler_params=pltpu.CompilerParams(vmem_limit_bytes=VMEM_LIMIT),
        name="pool_sample",
    )(x, state_t, g, w_pool, scale)


def _ffn_kernel(*refs, n_experts, final_norm, precise):
    moe = n_experts > 1
    it = iter(refs)
    x_ref, g_ref = next(it), next(it)
    wr_ref = next(it) if moe else None
    wg_ref, wu_ref, wd_ref = next(it), next(it), next(it)
    gf_ref = next(it) if final_norm else None
    o_ref, h_scr, acc_scr = next(it), next(it), next(it)
    eacc_scr, gate_scr = (next(it), next(it)) if moe else (None, None)

    e, f = pl.program_id(1), pl.program_id(2)
    last_f = f == pl.num_programs(2) - 1

    @pl.when((e == 0) & (f == 0))
    def _():
        h = _rmsnorm(x_ref[...], g_ref[...])
        h_scr[...] = h.astype(h_scr.dtype)
        acc_scr[...] = jnp.zeros_like(acc_scr)
        if moe:
            logits = jnp.dot(h, wr_ref[...], preferred_element_type=F32, precision=lax.Precision.HIGHEST)
            lane, i1, i2, w1, w2 = _top2_gates(logits, n_experts)
            gate_scr[...] = jnp.where(lane == i1, w1, 0.0) + jnp.where(lane == i2, w2, 0.0)

    mm = _dot3 if precise else _dot
    hb = h_scr[...]
    a = mm(hb, wg_ref[0])
    u = mm(hb, wu_ref[0])
    act = (a * jax.nn.sigmoid(a)) * u
    y = mm(act.astype(hb.dtype), wd_ref[0])

    if moe:
        @pl.when(f == 0)
        def _():
            eacc_scr[...] = y

        @pl.when(f > 0)
        def _():
            eacc_scr[...] += y

        @pl.when(last_f)
        def _():
            gate = gate_scr[...]
            lane = lax.broadcasted_iota(jnp.int32, gate.shape, 1)
            ge = jnp.sum(jnp.where(lane == e, gate, 0.0), axis=-1, keepdims=True)
            acc_scr[...] += ge * eacc_scr[...]
    else:
        acc_scr[...] += y

    @pl.when((e == pl.num_programs(1) - 1) & last_f)
    def _():
        out = x_ref[...] + acc_scr[...]
        if final_norm:
            out = _rmsnorm(out, gf_ref[...])
        o_ref[...] = out


def _ffn(x, g, wg, wu, wd, w_router=None, g_final=None, *, tm, tf=512):
    precise = wg.dtype == F32
    N, D = x.shape
    E, _, F = wg.shape
    assert N % tm == 0 and F % tf == 0
    moe = w_router is not None
    assert moe == (E > 1)
    final_norm = g_final is not None
    const2 = lambda i, e, f: (0, 0)
    args, in_specs = [x, g], [pl.BlockSpec((tm, D), lambda i, e, f: (i, 0)), pl.BlockSpec((1, D), const2)]
    if moe:
        args.append(w_router)
        in_specs.append(pl.BlockSpec(w_router.shape, const2))
    args += [wg, wu, wd]
    in_specs += [
        pl.BlockSpec((1, D, tf), lambda i, e, f: (e, 0, f)),
        pl.BlockSpec((1, D, tf), lambda i, e, f: (e, 0, f)),
        pl.BlockSpec((1, tf, D), lambda i, e, f: (e, f, 0)),
    ]
    if final_norm:
        args.append(g_final)
        in_specs.append(pl.BlockSpec((1, D), const2))
    scratch = [pltpu.VMEM((tm, D), F32 if precise else BF16), pltpu.VMEM((tm, D), F32)]
    if moe:
        scratch += [pltpu.VMEM((tm, D), F32), pltpu.VMEM((tm, LANES), F32)]
    return pl.pallas_call(
        functools.partial(_ffn_kernel, n_experts=E, final_norm=final_norm, precise=precise),
        grid=(N // tm, E, F // tf),
        in_specs=in_specs,
        out_specs=pl.BlockSpec((tm, D), lambda i, e, f: (i, 0)),
        out_shape=jax.ShapeDtypeStruct((N, D), F32),
        scratch_shapes=scratch,
        compiler_params=_cparams(("parallel", "arbitrary", "arbitrary")),
        name="moe_ffn" if moe else "dense_ffn",
    )(*args)


ROUTE_I1, ROUTE_I2, ROUTE_W1, ROUTE_W2, ROUTE_R1, ROUTE_R2 = range(6)


def _top2_gates(logits, n_experts):
    lane = lax.broadcasted_iota(jnp.int32, logits.shape, 1)
    lg = jnp.where(lane < n_experts, logits, -jnp.inf)
    m1 = jnp.max(lg, axis=-1, keepdims=True)
    i1 = jnp.min(jnp.where(lg == m1, lane, LANES), axis=-1, keepdims=True)
    lg2 = jnp.where(lane == i1, -jnp.inf, lg)
    m2 = jnp.max(lg2, axis=-1, keepdims=True)
    i2 = jnp.min(jnp.where(lg2 == m2, lane, LANES), axis=-1, keepdims=True)
    e2 = jnp.exp(m2 - m1)
    den = 1.0 + e2
    return lane, i1, i2, 1.0 / den, e2 / den


def _moe_route_kernel(x_ref, g_ref, wr_ref, tri_ref, route_ref, cnt_ref, carry, *, n_experts):
    @pl.when(pl.program_id(0) == 0)
    def _():
        carry[...] = jnp.zeros_like(carry)

    h = _rmsnorm(x_ref[...], g_ref[...])
    logits = jnp.dot(h, wr_ref[...], preferred_element_type=F32, precision=lax.Precision.HIGHEST)
    lane, i1, i2, w1, w2 = _top2_gates(logits, n_experts)
    member = jnp.where((lane == i1) | (lane == i2), 1.0, 0.0)
    before = _dot(tri_ref[...], member.astype(BF16)) + carry[0:1, :]
    r1 = jnp.sum(jnp.where(lane == i1, before, 0.0), axis=-1, keepdims=True)
    r2 = jnp.sum(jnp.where(lane == i2, before, 0.0), axis=-1, keepdims=True)
    rec = jnp.zeros(logits.shape, F32)
    for col, val in ((ROUTE_I1, i1.astype(F32)), (ROUTE_I2, i2.astype(F32)), (ROUTE_W1, w1), (ROUTE_W2, w2),
                     (ROUTE_R1, r1), (ROUTE_R2, r2)):
        rec = jnp.where(lane == col, val, rec)
    route_ref[...] = rec
    carry[...] = carry[...] + jnp.sum(member, axis=0, keepdims=True)
    cnt_ref[...] = carry[...]


def _moe_route(x, g, w_router, n_experts, tm=512):
    N, D = x.shape
    assert N % tm == 0
    tri = (jnp.arange(tm)[:, None] > jnp.arange(tm)[None, :]).astype(BF16)
    return pl.pallas_call(
        functools.partial(_moe_route_kernel, n_experts=n_experts),
        grid=(N // tm,),
        in_specs=[pl.BlockSpec((tm, D), lambda i: (i, 0)), pl.BlockSpec((1, D), lambda i: (0, 0)),
                  pl.BlockSpec(w_router.shape, lambda i: (0, 0)), pl.BlockSpec((tm, tm), lambda i: (0, 0))],
        out_specs=[pl.BlockSpec((tm, LANES), lambda i: (i, 0)), pl.BlockSpec((8, LANES), lambda i: (0, 0))],
        out_shape=[jax.ShapeDtypeStruct((N, LANES), F32), jax.ShapeDtypeStruct((8, LANES), F32)],
        scratch_shapes=[pltpu.VMEM((8, LANES), F32)],
        compiler_params=_cparams(("arbitrary",)),
        name="moe_route",
    )(x, g, w_router, tri)


def _row_gather(idx_of, src_hbm, dst_at, sem, rows):
    def copy(r):
        return pltpu.make_async_copy(src_hbm.at[pl.ds(idx_of(r), 1), :], dst_at(r), sem)

    def start():
        lax.fori_loop(0, rows, lambda r, c: (copy(r).start(), c)[1], 0, unroll=8)

    def wait():
        lax.fori_loop(0, rows, lambda r, c: (copy(r).wait(), c)[1], 0, unroll=8)

    return start, wait


def _moe_experts_kernel(te_ref, nu_ref, tos_ref, x_hbm, g_ref, wg_ref, wu_ref, wd_ref, o_ref,
                        xbuf, h_scr, acc_scr, sems, *, tile):
    i, f = pl.program_id(0), pl.program_id(1)
    n_used = nu_ref[0]
    used = i < n_used
    slot = i % 2

    def gather(t, sl):
        return _row_gather(lambda r: tos_ref[t * tile + r], x_hbm,
                           lambda r: xbuf.at[sl, pl.ds(r, 1), :], sems.at[sl], tile)

    @pl.when((f == 0) & (i == 0) & used)
    def _():
        gather(0, 0)[0]()

    @pl.when((f == 0) & used)
    def _():
        gather(i, slot)[1]()

        @pl.when(i + 1 < n_used)
        def _():
            gather(i + 1, 1 - slot)[0]()

        h_scr[...] = _rmsnorm(xbuf[slot], g_ref[...]).astype(BF16)
        acc_scr[...] = jnp.zeros_like(acc_scr)

    @pl.when(used)
    def _():
        hb = h_scr[...]
        a = _dot(hb, wg_ref[0])
        u = _dot(hb, wu_ref[0])
        act = (a * jax.nn.sigmoid(a)) * u
        acc_scr[...] += _dot(act.astype(BF16), wd_ref[0])

    @pl.when(f == pl.num_programs(1) - 1)
    def _():
        o_ref[...] = jnp.where(used, acc_scr[...], 0.0)


def _moe_experts(x, g, wg, wu, wd, tile_expert, n_used, token_of_slot, *, tile, tf=512):
    N, D = x.shape
    E, _, F = wg.shape
    n_slots = token_of_slot.shape[0]
    assert n_slots % tile == 0 and F % tf == 0
    n_tiles, nf = n_slots // tile, F // tf
    fidx = lambda i, f, nu: jnp.where(i < nu[0], f, nf - 1)
    return pl.pallas_call(
        functools.partial(_moe_experts_kernel, tile=tile),
        grid_spec=pltpu.PrefetchScalarGridSpec(
            num_scalar_prefetch=3,
            grid=(n_tiles, nf),
            in_specs=[pl.BlockSpec(memory_space=pl.ANY),
                      pl.BlockSpec((1, D), lambda i, f, te, nu, tos: (0, 0)),
                      pl.BlockSpec((1, D, tf), lambda i, f, te, nu, tos: (te[i], 0, fidx(i, f, nu))),
                      pl.BlockSpec((1, D, tf), lambda i, f, te, nu, tos: (te[i], 0, fidx(i, f, nu))),
                      pl.BlockSpec((1, tf, D), lambda i, f, te, nu, tos: (te[i], fidx(i, f, nu), 0))],
            out_specs=pl.BlockSpec((tile, D), lambda i, f, te, nu, tos: (i, 0)),
            scratch_shapes=[pltpu.VMEM((2, tile, D), F32), pltpu.VMEM((tile, D), BF16),
                            pltpu.VMEM((tile, D), F32), pltpu.SemaphoreType.DMA((2,))],
        ),
        out_shape=jax.ShapeDtypeStruct((n_slots, D), F32),
        compiler_params=_cparams(("arbitrary", "arbitrary")),
        name="moe_experts",
    )(tile_expert, n_used, token_of_slot, x, g, wg, wu, wd)


def _moe_combine_kernel(slot_ref, x_ref, route_ref, gf_ref, ys_hbm, o_ref, ybuf, sems, *, tile):
    i = pl.program_id(0)
    slot = i % 2

    def gather(t, sl):
        return _row_gather(lambda j: slot_ref[t * 2 * tile + j], ys_hbm,
                           lambda j: ybuf.at[sl, pl.ds(j, 1), :], sems.at[sl], 2 * tile)

    @pl.when(i == 0)
    def _():
        gather(0, 0)[0]()

    gather(i, slot)[1]()

    @pl.when(i + 1 < pl.num_programs(0))
    def _():
        gather(i + 1, 1 - slot)[0]()

    route = route_ref[...]
    w1, w2 = route[:, ROUTE_W1:ROUTE_W1 + 1], route[:, ROUTE_W2:ROUTE_W2 + 1]
    y = w1 * ybuf[slot, 0:tile, :] + w2 * ybuf[slot, tile:2 * tile, :]
    o_ref[...] = _rmsnorm(x_ref[...] + y, gf_ref[...])


def _moe_combine(x, route, g_final, ys, slots, *, tile=256):
    N, D = x.shape
    assert N % tile == 0
    return pl.pallas_call(
        functools.partial(_moe_combine_kernel, tile=tile),
        grid_spec=pltpu.PrefetchScalarGridSpec(
            num_scalar_prefetch=1,
            grid=(N // tile,),
            in_specs=[pl.BlockSpec((tile, D), lambda i, s: (i, 0)),
                      pl.BlockSpec((tile, LANES), lambda i, s: (i, 0)),
                      pl.BlockSpec((1, D), lambda i, s: (0, 0)),
                      pl.BlockSpec(memory_space=pl.ANY)],
            out_specs=pl.BlockSpec((tile, D), lambda i, s: (i, 0)),
            scratch_shapes=[pltpu.VMEM((2, 2 * tile, D), F32), pltpu.SemaphoreType.DMA((2,))],
        ),
        out_shape=jax.ShapeDtypeStruct((N, D), F32),
        compiler_params=_cparams(("arbitrary",)),
        name="moe_combine",
    )(slots, x, route, g_final, ys)


def _moe_prompt(x, g, w_router, wg, wu, wd, g_final, *, tile=512, ctile=256):
    N, D = x.shape
    E = wg.shape[0]
    route, counts = _moe_route(x, g, w_router, E)
    cnt = counts[0, :E].astype(jnp.int32)
    padded = (cnt + tile - 1) // tile * tile
    ends = jnp.cumsum(padded)
    off = ends - padded
    i1, i2 = route[:, ROUTE_I1].astype(jnp.int32), route[:, ROUTE_I2].astype(jnp.int32)
    slot1 = off[i1] + route[:, ROUTE_R1].astype(jnp.int32)
    slot2 = off[i2] + route[:, ROUTE_R2].astype(jnp.int32)
    n_slots = (TOP_K * N + E * (tile - 1)) // tile * tile
    rows = jnp.arange(N, dtype=jnp.int32)
    token_of_slot = jnp.zeros((n_slots,), jnp.int32).at[jnp.concatenate([slot1, slot2])].set(
        jnp.concatenate([rows, rows]), unique_indices=True)
    n_used = ends[-1:] // tile
    tile_start = jnp.minimum(jnp.arange(n_slots // tile, dtype=jnp.int32), n_used[0] - 1) * tile
    tile_expert = jnp.sum(tile_start[:, None] >= ends[None, :], axis=1).astype(jnp.int32)
    ys = _moe_experts(x, g, wg, wu, wd, tile_expert, n_used.astype(jnp.int32), token_of_slot, tile=tile)
    slots = jnp.stack([slot1.reshape(-1, ctile), slot2.reshape(-1, ctile)], axis=1).reshape(-1)
    return _moe_combine(x, route, g_final, ys, slots, tile=ctile)


def _nsa_proj_kernel(x_ref, g_ref, wq_ref, wkv_ref, wgt_ref, wck_ref, wcv_ref,
                     q_ref, kv4_ref, kwin_ref, ks_ref, vs_ref, kw_ref, vw_ref, gt_ref, kc_ref, vc_ref):
    hb = _rmsnorm(x_ref[...], g_ref[...]).astype(BF16)
    q_ref[...] = _dot(hb, wq_ref[...]).astype(BF16)
    kv = _dot(hb, wkv_ref[...])
    kv4_ref[...] = kv[:, :4 * KV_W]
    kwin_ref[...] = kv[:, 4 * KV_W:]
    ks_ref[...] = kv[:, 2 * KV_W:3 * KV_W].astype(BF16)
    vs_ref[...] = kv[:, 3 * KV_W:4 * KV_W].astype(BF16)
    kw_ref[...] = kv[:, 4 * KV_W:5 * KV_W].astype(BF16)
    vw_ref[...] = kv[:, 5 * KV_W:].astype(BF16)
    gt_ref[...] = jax.nn.sigmoid(_dot(hb, wgt_ref[...]))
    tm = kv.shape[0]
    kc_ref[...] = (kv[:, :KV_W] * wck_ref[...]).reshape(tm // CMP_BLOCK, CMP_BLOCK, KV_W).sum(axis=1)
    vc_ref[...] = (kv[:, KV_W:2 * KV_W] * wcv_ref[...]).reshape(tm // CMP_BLOCK, CMP_BLOCK, KV_W).sum(axis=1)


def _nsa_proj(x, g, wq, wkv, wgt, wck, wcv, *, tm):
    N, D = x.shape
    assert N % tm == 0 and tm % (8 * CMP_BLOCK) == 0
    row = lambda w: pl.BlockSpec((tm, w), lambda i: (i, 0))
    full = lambda a: pl.BlockSpec(a.shape, lambda i: (0, 0))
    out_shape = [jax.ShapeDtypeStruct((N, N_HEADS * HEAD_DIM), BF16),
                 jax.ShapeDtypeStruct((N, 4 * KV_W), F32), jax.ShapeDtypeStruct((N, 2 * KV_W), F32)]
    out_shape += [jax.ShapeDtypeStruct((N, KV_W), BF16)] * 4
    out_shape += [jax.ShapeDtypeStruct((N, LANES), F32)]
    out_shape += [jax.ShapeDtypeStruct((N // CMP_BLOCK, KV_W), F32)] * 2
    out_specs = [row(N_HEADS * HEAD_DIM), row(4 * KV_W), row(2 * KV_W)] + [row(KV_W)] * 4 + [row(LANES)]
    out_specs += [pl.BlockSpec((tm // CMP_BLOCK, KV_W), lambda i: (i, 0))] * 2
    return pl.pallas_call(
        _nsa_proj_kernel,
        grid=(N // tm,),
        in_specs=[row(D), full(g), full(wq), full(wkv), full(wgt), full(wck), full(wcv)],
        out_specs=out_specs,
        out_shape=out_shape,
        compiler_params=_cparams(("parallel",)),
        name="nsa_proj_prompt",
    )(x, g, wq, wkv, wgt, wck, wcv)


def _nsa_proj_sample_kernel(x_ref, g_ref, w_ref, z_ref):
    z_ref[...] = _dot3(_rmsnorm(x_ref[...], g_ref[...]), w_ref[...])


def _nsa_proj_sample(x, g, w_in, tn=384):
    B, D = x.shape
    cols = w_in.shape[1]
    assert cols % tn == 0
    return pl.pallas_call(
        _nsa_proj_sample_kernel,
        grid=(cols // tn,),
        in_specs=[pl.BlockSpec((B, D), lambda j: (0, 0)), pl.BlockSpec((1, D), lambda j: (0, 0)),
                  pl.BlockSpec((D, tn), lambda j: (0, j))],
        out_specs=pl.BlockSpec((B, tn), lambda j: (0, j)),
        out_shape=jax.ShapeDtypeStruct((B, cols), F32),
        compiler_params=_cparams(("parallel",)),
        name="nsa_proj_sample",
    )(x, g, w_in)


def _nsa_prompt_kernel(q_ref, gt_ref, kc_ref, vc_ref, ks_ref, vs_ref, kw_ref, vw_ref, x_ref, wo_ref,
                       o_ref, o_scr, sel_scr, *, tq, kchunk, seq):
    t0 = pl.program_id(1) * tq
    rows = GQA * tq
    scale = HEAD_DIM ** -0.5
    assert scale == 0.125
    n_blocks = seq // SLC_BLOCK
    top_k = min(SLC_TOPK, n_blocks)

    q = q_ref[0]
    gates = gt_ref[0]
    qpos_t = t0 + lax.broadcasted_iota(jnp.int32, (tq, 1), 0)
    qpos = jnp.concatenate([qpos_t] * GQA, axis=0)
    tq_f = qpos.astype(F32)

    lane = lax.broadcasted_iota(jnp.int32, (tq, LANES), 1)
    jblk = lane >> 1
    real = ((lane & 1) == 0) & (jblk < n_blocks)
    blk = qpos_t // SLC_BLOCK
    valid = real & (jblk <= blk)
    forced = (jblk == 0) | (jblk == blk) | (jblk == blk - 1)
    valid_f = jnp.where(valid, 1.0, 0.0)
    valid_b = valid_f.astype(BF16)
    bonus = jnp.where(forced, FORCE_BONUS, 0.0)
    floor = jnp.where(valid, 0.0, jnp.where(real, -1.0, -2.0))
    c_lane = lax.broadcasted_iota(jnp.int32, (1, LANES), 1)
    c_mid = c_lane.astype(F32) * CMP_BLOCK + (CMP_BLOCK - 1) / 2
    c_end = (c_lane + 1) * CMP_BLOCK - 1

    n_chunks = (t0 + tq + kchunk - 1) // kchunk
    wstart = pl.multiple_of(jnp.maximum(t0 - WINDOW, 0), tq)
    wlen = WINDOW + tq

    need_rank = (t0 + tq - 1) // SLC_BLOCK >= top_k
    wpos = wstart + lax.broadcasted_iota(jnp.int32, (1, wlen), 1)
    w_ok = (wpos <= qpos_t) & (qpos_t - wpos < WINDOW)
    w_dist = (wpos - qpos_t).astype(F32)

    def biased(ok, dist, kv):
        return jnp.concatenate(
            [jnp.where(ok, _alibi_slope(kv * GQA + g) * dist, NEG) for g in range(GQA)], axis=0)

    for kv in range(N_KV_HEADS):
        hs = slice(kv * HEAD_DIM, (kv + 1) * HEAD_DIM)
        qs = jnp.concatenate(
            [q[:, (kv * GQA + g) * HEAD_DIM:(kv * GQA + g + 1) * HEAD_DIM] for g in range(GQA)], axis=0) * scale
        slope = _head_slopes(rows, tq, kv * GQA)

        s_c = _dot_nt(qs, kc_ref[0][:, hs]) - slope * (tq_f - c_mid)
        p_c = _masked_softmax(s_c, c_end <= qpos)
        o_c = _dot(p_c.astype(BF16), vc_ref[0][:, hs])

        imp = p_c[0:tq]
        for g in range(1, GQA):
            imp = imp + p_c[g * tq:(g + 1) * tq]
        sel_scr[...] = valid_b

        @pl.when(need_rank)
        def _():
            pair = imp + pltpu.roll(imp, LANES - 1, 1)
            score = valid_f * (pair + bonus) + floor
            score_t = score.T[:2 * n_blocks]
            row_id = lax.broadcasted_iota(jnp.int32, score_t.shape, 0)
            rank_t = jnp.zeros(score_t.shape, jnp.int32)
            for k in range(n_blocks):
                other = score_t[2 * k:2 * k + 1, :]
                beats = (other > score_t) | ((other == score_t) & (row_id > 2 * k))
                rank_t = rank_t + beats.astype(jnp.int32)
            top_t = jnp.where(rank_t < top_k, 1.0, 0.0)
            top = jnp.concatenate([top_t, jnp.zeros((LANES - 2 * n_blocks, tq), F32)], axis=0).T
            sel_scr[...] = (top * valid_f).astype(BF16)

        sel = sel_scr[...]

        def chunk(c, carry):
            m, l, acc = carry
            k0 = pl.multiple_of(c * kchunk, kchunk)
            kb = ks_ref[0, pl.ds(k0, kchunk), hs]
            vb = vs_ref[0, pl.ds(k0, kchunk), hs]
            pos = k0 + lax.broadcasted_iota(jnp.int32, (1, kchunk), 1)
            expand = (lax.broadcasted_iota(jnp.int32, (LANES, kchunk), 0)
                      == 2 * ((k0 + lax.broadcasted_iota(jnp.int32, (LANES, kchunk), 1)) // SLC_BLOCK))
            picked = _dot(sel, jnp.where(expand, 1.0, 0.0).astype(BF16))
            ok = (picked > 0.5) & (pos <= qpos_t)
            s = _dot_nt(qs, kb) + biased(ok, (pos - qpos_t).astype(F32), kv)
            m_new = jnp.maximum(m, jnp.max(s, axis=-1, keepdims=True))
            alpha = jnp.exp(m - m_new)
            p = jnp.exp(s - m_new)
            l = alpha * l + jnp.sum(p, axis=-1, keepdims=True)
            acc = alpha * acc + _dot(p.astype(BF16), vb)
            return m_new, l, acc

        init = (jnp.full((rows, 1), NEG, F32), jnp.zeros((rows, 1), F32), jnp.zeros((rows, HEAD_DIM), F32))
        _, l_s, acc_s = lax.fori_loop(0, n_chunks, chunk, init)
        o_s = acc_s / l_s

        kb = kw_ref[0, pl.ds(wstart, wlen), hs]
        vb = vw_ref[0, pl.ds(wstart, wlen), hs]
        s_w = _dot_nt(qs, kb) + biased(w_ok, w_dist, kv)
        p_w = jnp.exp(s_w - jnp.max(s_w, axis=-1, keepdims=True))
        o_w = _dot(p_w.astype(BF16), vb) / jnp.sum(p_w, axis=-1, keepdims=True)

        def gate_col(n):
            base = n * N_HEADS + kv * GQA
            return jnp.concatenate([gates[:, base + g:base + g + 1] for g in range(GQA)], axis=0)

        o = gate_col(0) * o_c + gate_col(1) * o_s + gate_col(2) * o_w
        for g in range(GQA):
            h = kv * GQA + g
            o_scr[:, h * HEAD_DIM:(h + 1) * HEAD_DIM] = o[g * tq:(g + 1) * tq]

    o_ref[0] = x_ref[0] + _dot(o_scr[...].astype(BF16), wo_ref[...])


def _nsa_prompt(q, gates, kc, vc, ks, vs, kw, vw, x, w_o, *, tq=128, kchunk=512):
    B, L, D = x.shape
    assert L % kchunk == 0 and kchunk % tq == 0 and L >= WINDOW + tq and L % SLC_BLOCK == 0
    assert 2 * (L // SLC_BLOCK) <= LANES and L // CMP_BLOCK <= LANES and kc.shape[1] == LANES
    tile = lambda w: pl.BlockSpec((1, tq, w), lambda b, t: (b, t, 0))
    whole = lambda a: pl.BlockSpec((1,) + a.shape[1:], lambda b, t: (b, 0, 0))
    return pl.pallas_call(
        functools.partial(_nsa_prompt_kernel, tq=tq, kchunk=kchunk, seq=L),
        grid=(B, L // tq),
        in_specs=[tile(N_HEADS * HEAD_DIM), tile(LANES), whole(kc), whole(vc), whole(ks), whole(vs),
                  whole(kw), whole(vw), tile(D), pl.BlockSpec(w_o.shape, lambda b, t: (0, 0))],
        out_specs=tile(D),
        out_shape=jax.ShapeDtypeStruct((B, L, D), F32),
        scratch_shapes=[pltpu.VMEM((tq, N_HEADS * HEAD_DIM), F32), pltpu.VMEM((tq, LANES), BF16)],
        compiler_params=_cparams(("parallel", "arbitrary")),
        name="nsa_attn_prompt",
    )(q, gates, kc, vc, ks, vs, kw, vw, x, w_o)


CMP_GROUP = 32
MXU_DEPTH = 256


def _cmp_sample_kernel(pt_ref, cache_ref, w_ref, s_ref, o_ref, buf, sems, *, rows):
    b, g = pl.program_id(0), pl.program_id(1)
    n_groups = pl.num_programs(1)
    step = b * n_groups + g
    slot = step % 2

    def page_copy(bb, gg, sl, i):
        page = pt_ref[bb, gg * CMP_GROUP + i]
        return pltpu.make_async_copy(cache_ref.at[page, pl.ds(0, rows), :], buf.at[sl, i], sems.at[sl])

    def start_group(bb, gg, sl):
        for i in range(CMP_GROUP):
            page_copy(bb, gg, sl, i).start()

    @pl.when(step == 0)
    def _():
        start_group(b, g, slot)

    @pl.when(step + 1 < pl.num_programs(0) * n_groups)
    def _():
        nxt = step + 1
        start_group(nxt // n_groups, nxt % n_groups, 1 - slot)

    for i in range(CMP_GROUP):
        page_copy(b, g, slot, i).wait()

    w = w_ref[...]
    acc = jnp.zeros((rows, LANES), F32)
    for j in range(CMP_GROUP // 2):
        p = jnp.concatenate([buf[slot, 2 * j] * w, buf[slot, 2 * j + 1] * w], axis=1)
        hi, lo = _split(p)
        r = _dot(jnp.concatenate([hi, lo], axis=0), s_ref[j])
        acc = acc + (r[:rows] + r[rows:])
    o_ref[0] = acc


def _cmp_sample(page_table, cache_t, w_rows):
    B, n_pages = page_table.shape
    _, _, page = cache_t.shape
    rows = w_rows.shape[0]
    per_page = page // CMP_BLOCK
    assert n_pages % CMP_GROUP == 0 and CMP_GROUP * per_page == LANES and 2 * page == MXU_DEPTH
    k = jnp.arange(2 * page)
    token = (2 * jnp.arange(CMP_GROUP // 2)[:, None] + k[None, :] // page) * per_page + (k[None, :] % page) // CMP_BLOCK
    block_sum = (token[:, :, None] == jnp.arange(LANES)[None, None, :]).astype(BF16)
    return pl.pallas_call(
        functools.partial(_cmp_sample_kernel, rows=rows),
        grid_spec=pltpu.PrefetchScalarGridSpec(
            num_scalar_prefetch=1,
            grid=(B, n_pages // CMP_GROUP),
            in_specs=[pl.BlockSpec(memory_space=pl.ANY),
                      pl.BlockSpec(w_rows.shape, lambda b, g, pt: (0, 0)),
                      pl.BlockSpec(block_sum.shape, lambda b, g, pt: (0, 0, 0))],
            out_specs=pl.BlockSpec((1, rows, LANES), lambda b, g, pt: (b, 0, g)),
            scratch_shapes=[pltpu.VMEM((2, CMP_GROUP, rows, page), F32), pltpu.SemaphoreType.DMA((2,))],
        ),
        out_shape=jax.ShapeDtypeStruct((B, rows, n_pages * per_page), F32),
        compiler_params=_cparams(("arbitrary", "arbitrary")),
        name="cmp_sample",
    )(page_table, cache_t, w_rows, block_sum)


def _select_sample_kernel(q_ref, kcv_ref, oc_ref, idx_ref, *, past):
    scale = HEAD_DIM ** -0.5
    q = q_ref[0]
    nc = kcv_ref.shape[2]
    n_past_blocks = past // SLC_BLOCK
    per_block = SLC_BLOCK // CMP_BLOCK
    row = lax.broadcasted_iota(jnp.int32, (N_HEADS, 1), 0)
    slope = _head_slopes(N_HEADS, 1, 0)
    tok = lax.broadcasted_iota(jnp.int32, (1, nc), 1)
    c_mid = tok.astype(F32) * CMP_BLOCK + (CMP_BLOCK - 1) / 2
    c_ok = (tok + 1) * CMP_BLOCK - 1 <= past
    bias = slope * (float(past) - c_mid)

    s_c = jnp.zeros((N_HEADS, nc), F32)
    for kv in range(N_KV_HEADS):
        s_kv = _dot3(q, kcv_ref[0, kv * HEAD_DIM:(kv + 1) * HEAD_DIM, :])
        s_c = jnp.where(row // GQA == kv, s_kv, s_c)
    p_c = _masked_softmax(s_c * scale - bias, c_ok)
    o_c = jnp.zeros((N_HEADS, HEAD_DIM), F32)
    for kv in range(N_KV_HEADS):
        o_kv = _dot3(p_c, kcv_ref[0, KV_W + kv * HEAD_DIM:KV_W + (kv + 1) * HEAD_DIM, :], nt=True)
        o_c = jnp.where(row // GQA == kv, o_kv, o_c)
    oc_ref[0] = o_c

    imp = p_c
    shift = 1
    while shift < GQA:
        imp = imp + pltpu.roll(imp, shift, 0)
        shift *= 2
    assert per_block == 2
    pair = imp + pltpu.roll(imp, nc - 1, 1)
    jblk = tok // per_block
    real = tok % per_block == 0
    forced = (jblk == 0) | (jblk == n_past_blocks - 1)
    score = jnp.where(real, pair + jnp.where(forced, FORCE_BONUS, 0.0), -2.0)
    own = jnp.float32(FORCE_BONUS)
    rank = (own > score).astype(jnp.int32)
    for k in range(n_past_blocks):
        col = score[:, per_block * k:per_block * k + 1]
        beats = (col > score) | ((col == score) & (jblk > k))
        rank = rank + beats.astype(jnp.int32)
    own_rank = jnp.sum(jnp.where(real & (score >= own), 1, 0), axis=-1, keepdims=True)
    out_lane = lax.broadcasted_iota(jnp.int32, (N_HEADS, LANES), 1)
    idx = jnp.zeros((N_HEADS, LANES), jnp.int32)
    for r in range(SLC_TOPK):
        hit = jnp.sum(jnp.where(real & (rank == r), jblk, 0), axis=-1, keepdims=True)
        hit = hit + jnp.where(own_rank == r, n_past_blocks, 0)
        idx = jnp.where(out_lane == r, hit, idx)
    idx_ref[0] = idx


def _select_sample(q16, kcv, past):
    B = q16.shape[0]
    nc = kcv.shape[2]
    assert past % SLC_BLOCK == 0 and nc == past // CMP_BLOCK and nc % LANES == 0
    assert past // SLC_BLOCK + 1 > SLC_TOPK
    return pl.pallas_call(
        functools.partial(_select_sample_kernel, past=past),
        grid=(B,),
        in_specs=[pl.BlockSpec((1,) + q16.shape[1:], lambda b: (b, 0, 0)),
                  pl.BlockSpec((1,) + kcv.shape[1:], lambda b: (b, 0, 0))],
        out_specs=[pl.BlockSpec((1, N_HEADS, HEAD_DIM), lambda b: (b, 0, 0)),
                   pl.BlockSpec((1, N_HEADS, LANES), lambda b: (b, 0, 0))],
        out_shape=[jax.ShapeDtypeStruct((B, N_HEADS, HEAD_DIM), F32),
                   jax.ShapeDtypeStruct((B, N_HEADS, LANES), jnp.int32)],
        compiler_params=_cparams(("parallel",)),
        name="select_sample",
    )(q16, kcv)


def _attend_sample_kernel(pt_ref, idx_ref, q_ref, oc_ref, gt_ref, new_ref, win_ref, cache_ref,
                          o_ref, kbuf, vbuf, sems, *, past):
    b = pl.program_id(0)
    scale = HEAD_DIM ** -0.5
    n_past_blocks = past // SLC_BLOCK
    page_rows = cache_ref.shape[2]
    blocks_per_page = page_rows // SLC_BLOCK
    nkeys = SLC_TOPK * page_rows

    def block_copies(kv, n):
        blk = idx_ref[(b * N_KV_HEADS + kv) * SLC_TOPK + n]
        in_past = blk < n_past_blocks
        page = pt_ref[b, jnp.minimum(blk, n_past_blocks - 1) // blocks_per_page]
        dst = pl.ds(n * page_rows, page_rows)
        ck = pltpu.make_async_copy(cache_ref.at[page, pl.ds((2 * N_KV_HEADS + kv) * HEAD_DIM, HEAD_DIM), :],
                                   kbuf.at[kv, :, dst], sems.at[0, kv, n])
        cv = pltpu.make_async_copy(cache_ref.at[page, pl.ds((3 * N_KV_HEADS + kv) * HEAD_DIM, HEAD_DIM), :],
                                   vbuf.at[kv, :, dst], sems.at[1, kv, n])
        return blk, in_past, ck, cv

    for kv in range(N_KV_HEADS):
        for n in range(SLC_TOPK):
            _, in_past, ck, cv = block_copies(kv, n)

            @pl.when(in_past)
            def _():
                ck.start()
                cv.start()

            @pl.when(jnp.logical_not(in_past))
            def _():
                kbuf[kv, :, n * page_rows:(n + 1) * page_rows] = jnp.zeros((HEAD_DIM, page_rows), F32)
                vbuf[kv, :, n * page_rows:(n + 1) * page_rows] = jnp.zeros((HEAD_DIM, page_rows), F32)

    q = q_ref[0]
    row = lax.broadcasted_iota(jnp.int32, (N_HEADS, 1), 0)
    slope = _head_slopes(N_HEADS, 1, 0)
    key_lane = lax.broadcasted_iota(jnp.int32, (1, nkeys), 1)

    wb = win_ref.shape[2]
    w_lane = lax.broadcasted_iota(jnp.int32, (1, wb), 1)
    wpos = past - wb + w_lane
    w_ok = (past - wpos < WINDOW) & (wpos >= 0)
    w_bias = slope * (past - wpos).astype(F32)
    s_w = jnp.zeros((N_HEADS, wb), F32)
    s_n = jnp.zeros((N_HEADS, 1), F32)
    for kv in range(N_KV_HEADS):
        mine = row // GQA == kv
        s_kv = _dot3(q, win_ref[0, kv * HEAD_DIM:(kv + 1) * HEAD_DIM, :])
        s_w = jnp.where(mine, s_kv, s_w)
        s_n = jnp.where(mine, jnp.sum(q * new_ref[0, 4, kv:kv + 1, :], axis=-1, keepdims=True), s_n)
    s_w = jnp.where(w_ok, s_w * scale - w_bias, NEG)
    s_n = s_n * scale
    m_w = jnp.maximum(jnp.max(s_w, axis=-1, keepdims=True), s_n)
    p_w = jnp.where(w_ok, jnp.exp(s_w - m_w), 0.0)
    p_n = jnp.exp(s_n - m_w)
    l_w = jnp.maximum(jnp.sum(p_w, axis=-1, keepdims=True) + p_n, 1e-30)
    p_w = p_w / l_w
    p_n = p_n / l_w
    o_w = jnp.zeros((N_HEADS, HEAD_DIM), F32)
    for kv in range(N_KV_HEADS):
        o_kv = (_dot3(p_w, win_ref[0, KV_W + kv * HEAD_DIM:KV_W + (kv + 1) * HEAD_DIM, :], nt=True)
                + p_n * new_ref[0, 5, kv:kv + 1, :])
        o_w = jnp.where(row // GQA == kv, o_kv, o_w)

    for kv in range(N_KV_HEADS):
        for n in range(SLC_TOPK):
            _, in_past, ck, cv = block_copies(kv, n)

            @pl.when(in_past)
            def _():
                ck.wait()
                cv.wait()

    s_s = jnp.zeros((N_HEADS, nkeys), F32)
    pos = jnp.zeros((N_HEADS, nkeys), jnp.int32)
    live = jnp.zeros((N_HEADS, nkeys), jnp.int32)
    own = jnp.zeros((N_HEADS, 1), jnp.int32)
    s_n = jnp.zeros((N_HEADS, 1), F32)
    in_slab = key_lane % page_rows
    for kv in range(N_KV_HEADS):
        mine = row // GQA == kv
        s_kv = _dot3(q, kbuf[kv])
        s_s = jnp.where(mine, s_kv, s_s)
        s_n = jnp.where(mine, jnp.sum(q * new_ref[0, 2, kv:kv + 1, :], axis=-1, keepdims=True), s_n)
        base_kv = jnp.zeros((1, nkeys), jnp.int32)
        half_kv = jnp.zeros((1, nkeys), jnp.int32)
        own_kv = jnp.int32(0)
        for n in range(SLC_TOPK):
            blk = idx_ref[(b * N_KV_HEADS + kv) * SLC_TOPK + n]
            here = key_lane // page_rows == n
            base_kv = jnp.where(here, (blk // blocks_per_page) * page_rows, base_kv)
            half_kv = jnp.where(here, blk % blocks_per_page, half_kv)
            own_kv = own_kv + (blk >= n_past_blocks).astype(jnp.int32)
        pos = jnp.where(mine, base_kv + in_slab, pos)
        live = jnp.where(mine, (in_slab // SLC_BLOCK == half_kv).astype(jnp.int32), live)
        own = jnp.where(mine, own_kv, own)
    s_ok = (live > 0) & (pos < past)
    n_ok = own > 0
    s_s = jnp.where(s_ok, s_s * scale - slope * (past - pos).astype(F32), NEG)
    s_n = jnp.where(n_ok, s_n * scale, NEG)
    m_s = jnp.maximum(jnp.max(s_s, axis=-1, keepdims=True), s_n)
    p_s = jnp.where(s_ok, jnp.exp(s_s - m_s), 0.0)
    p_n = jnp.where(n_ok, jnp.exp(s_n - m_s), 0.0)
    l_s = jnp.maximum(jnp.sum(p_s, axis=-1, keepdims=True) + p_n, 1e-30)
    p_s = p_s / l_s
    p_n = p_n / l_s
    o_s = jnp.zeros((N_HEADS, HEAD_DIM), F32)
    for kv in range(N_KV_HEADS):
        o_kv = _dot3(p_s, vbuf[kv], nt=True) + p_n * new_ref[0, 3, kv:kv + 1, :]
        o_s = jnp.where(row // GQA == kv, o_kv, o_s)

    gt = jax.nn.sigmoid(gt_ref[0])
    o_ref[0] = gt[:, 0:1] * oc_ref[0] + gt[:, 1:2] * o_s + gt[:, 2:3] * o_w


def _attend_sample(page_table, idx, q16, o_c, gates, kv_new, win_t, cache_t, past):
    B = q16.shape[0]
    page_rows = cache_t.shape[2]
    assert win_t.shape[2] <= WINDOW and page_rows % SLC_BLOCK == 0
    blk3 = lambda a: pl.BlockSpec((1,) + a.shape[1:], lambda b, pt, ix: (b,) + (0,) * (a.ndim - 1))
    return pl.pallas_call(
        functools.partial(_attend_sample_kernel, past=past),
        grid_spec=pltpu.PrefetchScalarGridSpec(
            num_scalar_prefetch=2,
            grid=(B,),
            in_specs=[blk3(q16), blk3(o_c), blk3(gates), blk3(kv_new), blk3(win_t),
                      pl.BlockSpec(memory_space=pl.ANY)],
            out_specs=pl.BlockSpec((1, N_HEADS, HEAD_DIM), lambda b, pt, ix: (b, 0, 0)),
            scratch_shapes=[pltpu.VMEM((N_KV_HEADS, HEAD_DIM, SLC_TOPK * page_rows), F32),
                            pltpu.VMEM((N_KV_HEADS, HEAD_DIM, SLC_TOPK * page_rows), F32),
                            pltpu.SemaphoreType.DMA((2, N_KV_HEADS, SLC_TOPK))],
        ),
        out_shape=jax.ShapeDtypeStruct((B, N_HEADS, HEAD_DIM), F32),
        compiler_params=_cparams(("arbitrary",)),
        name="attend_sample",
    )(page_table, idx, q16, o_c, gates, kv_new, win_t, cache_t)


def _linear_res_kernel(x_ref, a_ref, w_ref, o_ref):
    o_ref[...] = x_ref[...] + _dot3(a_ref[...], w_ref[...])


def _linear_res(x, a, w):
    return pl.pallas_call(
        _linear_res_kernel,
        out_shape=jax.ShapeDtypeStruct(x.shape, F32),
        compiler_params=pltpu.CompilerParams(vmem_limit_bytes=VMEM_LIMIT),
        name="out_proj_sample",
    )(x, a, w)


def _expand_cmp_weights(w_cmp, rows):
    w = jnp.repeat(w_cmp, HEAD_DIM, axis=1)
    return jnp.tile(w, (rows // CMP_BLOCK, 1))


def _row(v):
    return v.reshape(1, -1)


def _prompt_mixers(x, norm_mix, w_pool, pool_scale, w_in, w_cmp_k, w_cmp_v, w_out, norm_ffn0, wfg, wfu, wfd):
    B, L, D = x.shape
    q_w = N_HEADS * HEAD_DIM
    x, h_last = _pool_prompt(x, _row(norm_mix[0]), w_pool.astype(BF16), _row(pool_scale))
    new_pool = h_last[None, :, POOL_HALO - (max(POOL_WINDOWS) - 1):, :]
    x = _ffn(x.reshape(B * L, D), _row(norm_ffn0), wfg.astype(BF16), wfu.astype(BF16), wfd.astype(BF16), tm=1024)

    tm = 512
    wq = w_in[:, :q_w].astype(BF16)
    wkv = w_in[:, q_w:q_w + 6 * KV_W].astype(BF16)
    wgt = jnp.pad(w_in[:, q_w + 6 * KV_W:], ((0, 0), (0, LANES - N_BRANCH * N_HEADS))).astype(BF16)
    q, kv4, kwin, ks, vs, kw, vw, gates, kc, vc = _nsa_proj(
        x, _row(norm_mix[1]), wq, wkv, wgt, _expand_cmp_weights(w_cmp_k, tm), _expand_cmp_weights(w_cmp_v, tm),
        tm=tm)
    new_kv = kv4.reshape(1, B, L, 4, N_KV_HEADS, HEAD_DIM)
    new_win = kwin.reshape(B, L, 2, N_KV_HEADS, HEAD_DIM)[None, :, L - min(WINDOW, L):]
    nc = L // CMP_BLOCK
    pad_c = lambda a: jnp.pad(a.reshape(B, nc, KV_W), ((0, 0), (0, LANES - nc), (0, 0))).astype(BF16)
    per_seq = lambda a: a.reshape(B, L, a.shape[-1])
    x = _nsa_prompt(per_seq(q), per_seq(gates), pad_c(kc), pad_c(vc), per_seq(ks), per_seq(vs),
                    per_seq(kw), per_seq(vw), per_seq(x), w_out.astype(BF16))
    return x.reshape(B * L, D), new_pool, new_kv, new_win


def _sample_mixers(x, state_pool, cache, state_win, page_table, norm_mix, w_pool, pool_scale, w_in, w_cmp_k,
                   w_cmp_v, w_out, norm_ffn0, wfg, wfu, wfd):
    SB, D = x.shape
    n_phys, page = cache.shape[:2]
    past = page_table.shape[1] * page
    q_w = N_HEADS * HEAD_DIM
    x, h = _pool_sample(x, state_pool.transpose(1, 0, 2), _row(norm_mix[0]), w_pool, _row(pool_scale))
    new_pool = jnp.concatenate([state_pool[:, 1:], h[:, None]], axis=1)[None]
    x = _ffn(x, _row(norm_ffn0), wfg, wfu, wfd, tm=SB)

    cols = w_in.shape[1]
    z = _nsa_proj_sample(x, _row(norm_mix[1]), jnp.pad(w_in, ((0, 0), (0, -cols % LANES))))
    kv_new = z[:, q_w:q_w + 6 * KV_W].reshape(SB, 6, N_KV_HEADS, HEAD_DIM)
    new_kv = kv_new[None, :, None, :4]
    keep = min(WINDOW, state_win.shape[1] + 1)
    new_win = jnp.concatenate([state_win, kv_new[:, None, 4:]], axis=1)[None, :, -keep:]

    cache_t = cache.transpose(0, 2, 3, 4, 1).reshape(n_phys, 4 * KV_W, page)
    win_t = state_win.transpose(0, 2, 3, 4, 1).reshape(SB, 2 * KV_W, state_win.shape[1])
    w_rows = jnp.concatenate([_expand_cmp_weights(w_cmp_k, page).T, _expand_cmp_weights(w_cmp_v, page).T], axis=0)
    kcv = _cmp_sample(page_table, cache_t, w_rows)
    q16 = z[:, :q_w].reshape(SB, N_HEADS, HEAD_DIM)
    o_c, idx = _select_sample(q16, kcv, past)
    idx = idx[:, GQA - 1::GQA, :SLC_TOPK].reshape(-1)
    gate_logits = z[:, q_w + 6 * KV_W:cols].reshape(SB, N_BRANCH, N_HEADS).transpose(0, 2, 1)
    gate_logits = jnp.pad(gate_logits, ((0, 0), (0, 0), (0, LANES - N_BRANCH)))
    o = _attend_sample(page_table, idx, q16, o_c, gate_logits, kv_new, win_t, cache_t, past)
    return _linear_res(x, o.reshape(SB, q_w), w_out), new_pool, new_kv, new_win


def kernel(x_prompt, x_sample, state_pool, cache_kv, state_win, page_table, norm_mix, w_pool, pool_scale,
           w_nsa_in, w_cmp_k, w_cmp_v, w_nsa_out, norm_ffn, w_ffn_gate, w_ffn_up, w_ffn_down, w_router,
           w_moe_gate, w_moe_up, w_moe_down, norm_final):
    assert x_sample.shape[1] == 1 and norm_mix.shape[0] == 2 and x_prompt.shape[-1] == N_HEADS * HEAD_DIM
    w_rt = jnp.pad(w_router[0], ((0, 0), (0, LANES - w_router.shape[-1])))
    moe_w = (w_moe_gate[0].astype(BF16), w_moe_up[0].astype(BF16), w_moe_down[0].astype(BF16))

    xp, new_pool_prompt, new_kv_prompt, new_win_prompt = _prompt_mixers(
        x_prompt, norm_mix, w_pool[0], pool_scale[0], w_nsa_in[0], w_cmp_k[0], w_cmp_v[0], w_nsa_out[0],
        norm_ffn[0], w_ffn_gate, w_ffn_up, w_ffn_down)
    xs, new_pool_sample, new_kv_sample, new_win_sample = _sample_mixers(
        x_sample[:, 0], state_pool[0], cache_kv[0], state_win[0], page_table, norm_mix, w_pool[0], pool_scale[0],
        w_nsa_in[0], w_cmp_k[0], w_cmp_v[0], w_nsa_out[0], norm_ffn[0], w_ffn_gate, w_ffn_up, w_ffn_down)

    yp = _moe_prompt(xp, _row(norm_ffn[1]), w_rt, *moe_w, _row(norm_final))
    ys = _ffn(xs, _row(norm_ffn[1]), *moe_w, w_rt, _row(norm_final), tm=xs.shape[0])
    return (yp.reshape(x_prompt.shape), ys.reshape(x_sample.shape), new_pool_prompt, new_pool_sample,
            new_kv_prompt, new_kv_sample, new_win_prompt, new_win_sample)
```

```python
import functools

import jax
import jax.numpy as jnp
from jax import lax
from jax.experimental import pallas as pl
from jax.experimental.pallas import tpu as pltpu

F32 = jnp.float32
BF16 = jnp.bfloat16

EPS = 1e-6
NEG = -1e30
POOL_WINDOWS = (2, 4, 8, 16)
POOL_HALO = 16
N_HEADS = 16
N_KV_HEADS = 4
GQA = N_HEADS // N_KV_HEADS
HEAD_DIM = 64
KV_W = N_KV_HEADS * HEAD_DIM
CMP_BLOCK = 32
SLC_BLOCK = 64
SLC_TOPK = 16
WINDOW = 512
N_BRANCH = 3
FORCE_BONUS = 1000.0
TOP_K = 2
LANES = 128
VMEM_LIMIT = 56 * 1024 * 1024


def _cparams(sem):
    return pltpu.CompilerParams(dimension_semantics=sem, vmem_limit_bytes=VMEM_LIMIT)


def _rmsnorm(x, g):
    ms = jnp.mean(x * x, axis=-1, keepdims=True)
    return (x * lax.rsqrt(ms + EPS)) * g


def _dot(a, b):
    return jnp.dot(a, b, preferred_element_type=F32)


def _dot_nt(a, b):
    return lax.dot_general(a, b, (((1,), (1,)), ((), ())), preferred_element_type=F32)


def _split(a):
    hi = a.astype(BF16)
    return hi, (a - hi.astype(F32)).astype(BF16)


def _dot3(a, b, nt=False):
    d = _dot_nt if nt else _dot
    ah, al = _split(a)
    bh, bl = _split(b)
    return d(ah, bh) + (d(ah, bl) + d(al, bh))


def _masked_softmax(s, mask):
    s = jnp.where(mask, s, NEG)
    m = jnp.max(s, axis=-1, keepdims=True)
    p = jnp.where(mask, jnp.exp(s - m), 0.0)
    return p / jnp.maximum(jnp.sum(p, axis=-1, keepdims=True), 1e-30)


def _alibi_slope(h):
    return 2.0 ** (-8.0 * (h + 1) / N_HEADS)


def _head_slopes(rows, rows_per_head, first_head):
    h = lax.broadcasted_iota(jnp.int32, (rows, 1), 0) // rows_per_head
    slopes = jnp.zeros((rows, 1), F32)
    for j in range(rows // rows_per_head):
        slopes = jnp.where(h == j, _alibi_slope(first_head + j), slopes)
    return slopes


def _pool_prompt_kernel(x_ref, halo_ref, g_ref, w_ref, sc_ref, o_ref, hl_ref, full_ref, *, tile):
    t = pl.program_id(1)
    g = g_ref[...]
    x = x_ref[0]
    h = _rmsnorm(x, g)
    hh = _rmsnorm(halo_ref[0], g)
    full_ref[0:POOL_HALO, :] = jnp.where(t > 0, hh, 0.0)
    full_ref[POOL_HALO:POOL_HALO + tile, :] = h
    row = t * tile + lax.broadcasted_iota(jnp.int32, (tile, 1), 0)
    group = x.shape[-1] // len(POOL_WINDOWS)
    parts = []
    for gi, w in enumerate(POOL_WINDOWS):
        cs = slice(gi * group, (gi + 1) * group)
        hg = h[:, cs]
        acc = hg
        for k in range(1, w):
            acc = acc + full_ref[POOL_HALO - k:POOL_HALO - k + tile, cs]
        cnt = jnp.minimum(row + 1, w).astype(F32)
        pooled = acc / cnt - hg
        parts.append(_dot(pooled.astype(BF16), w_ref[gi]))
    o_ref[0] = x + jnp.concatenate(parts, axis=-1) * sc_ref[...]

    @pl.when(t == pl.num_programs(1) - 1)
    def _():
        hl_ref[0] = h[tile - POOL_HALO:, :]


def _pool_prompt(x, g, w_pool, scale, tile=512):
    B, L, D = x.shape
    assert L % tile == 0 and tile % POOL_HALO == 0
    hb = tile // POOL_HALO
    return pl.pallas_call(
        functools.partial(_pool_prompt_kernel, tile=tile),
        grid=(B, L // tile),
        in_specs=[
            pl.BlockSpec((1, tile, D), lambda b, t: (b, t, 0)),
            pl.BlockSpec((1, POOL_HALO, D), lambda b, t: (b, jnp.maximum(t * hb - 1, 0), 0)),
            pl.BlockSpec((1, D), lambda b, t: (0, 0)),
            pl.BlockSpec(w_pool.shape, lambda b, t: (0, 0, 0)),
            pl.BlockSpec((1, D), lambda b, t: (0, 0)),
        ],
        out_specs=[
            pl.BlockSpec((1, tile, D), lambda b, t: (b, t, 0)),
            pl.BlockSpec((1, POOL_HALO, D), lambda b, t: (b, 0, 0)),
        ],
        out_shape=[jax.ShapeDtypeStruct((B, L, D), F32), jax.ShapeDtypeStruct((B, POOL_HALO, D), F32)],
        scratch_shapes=[pltpu.VMEM((tile + POOL_HALO, D), F32)],
        compiler_params=_cparams(("parallel", "arbitrary")),
        name="pool_prompt",
    )(x, x, g, w_pool, scale)


def _pool_sample_kernel(x_ref, st_ref, g_ref, w_ref, sc_ref, o_ref, h_ref):
    x = x_ref[...]
    h = _rmsnorm(x, g_ref[...])
    h_ref[...] = h
    P = st_ref.shape[0]
    group = x.shape[-1] // len(POOL_WINDOWS)
    parts = []
    for gi, w in enumerate(POOL_WINDOWS):
        cs = slice(gi * group, (gi + 1) * group)
        hg = h[:, cs]
        acc = hg
        for k in range(1, w):
            acc = acc + st_ref[P - k][:, cs]
        pooled = acc / float(w) - hg
        parts.append(_dot3(pooled, w_ref[gi]))
    o_ref[...] = x + jnp.concatenate(parts, axis=-1) * sc_ref[...]


def _pool_sample(x, state_t, g, w_pool, scale):
    B, D = x.shape
    assert state_t.shape[0] >= max(POOL_WINDOWS) - 1
    return pl.pallas_call(
        _pool_sample_kernel,
        out_shape=[jax.ShapeDtypeStruct((B, D), F32), jax.ShapeDtypeStruct((B, D), F32)],
        compiler_params=pltpu.CompilerParams(vmem_limit_bytes=VMEM_LIMIT),
        name="pool_sample",
    )(x, state_t, g, w_pool, scale)


def _ffn_kernel(*refs, n_experts, final_norm, precise):
    moe = n_experts > 1
    it = iter(refs)
    x_ref, g_ref = next(it), next(it)
    wr_ref = next(it) if moe else None
    wg_ref, wu_ref, wd_ref = next(it), next(it), next(it)
    gf_ref = next(it) if final_norm else None
    o_ref, h_scr, acc_scr = next(it), next(it), next(it)
    eacc_scr, gate_scr = (next(it), next(it)) if moe else (None, None)

    e, f = pl.program_id(1), pl.program_id(2)
    last_f = f == pl.num_programs(2) - 1

    @pl.when((e == 0) & (f == 0))
    def _():
        h = _rmsnorm(x_ref[...], g_ref[...])
        h_scr[...] = h.astype(h_scr.dtype)
        acc_scr[...] = jnp.zeros_like(acc_scr)
        if moe:
            logits = jnp.dot(h, wr_ref[...], preferred_element_type=F32, precision=lax.Precision.HIGHEST)
            lane, i1, i2, w1, w2 = _top2_gates(logits, n_experts)
            gate_scr[...] = jnp.where(lane == i1, w1, 0.0) + jnp.where(lane == i2, w2, 0.0)

    mm = _dot3 if precise else _dot
    hb = h_scr[...]
    a = mm(hb, wg_ref[0])
    u = mm(hb, wu_ref[0])
    act = (a * jax.nn.sigmoid(a)) * u
    y = mm(act.astype(hb.dtype), wd_ref[0])

    if moe:
        @pl.when(f == 0)
        def _():
            eacc_scr[...] = y

        @pl.when(f > 0)
        def _():
            eacc_scr[...] += y

        @pl.when(last_f)
        def _():
            gate = gate_scr[...]
            lane = lax.broadcasted_iota(jnp.int32, gate.shape, 1)
            ge = jnp.sum(jnp.where(lane == e, gate, 0.0), axis=-1, keepdims=True)
            acc_scr[...] += ge * eacc_scr[...]
    else:
        acc_scr[...] += y

    @pl.when((e == pl.num_programs(1) - 1) & last_f)
    def _():
        out = x_ref[...] + acc_scr[...]
        if final_norm:
            out = _rmsnorm(out, gf_ref[...])
        o_ref[...] = out


def _ffn(x, g, wg, wu, wd, w_router=None, g_final=None, *, tm, tf=512):
    precise = wg.dtype == F32
    N, D = x.shape
    E, _, F = wg.shape
    assert N % tm == 0 and F % tf == 0
    moe = w_router is not None
    assert moe == (E > 1)
    final_norm = g_final is not None
    const2 = lambda i, e, f: (0, 0)
    args, in_specs = [x, g], [pl.BlockSpec((tm, D), lambda i, e, f: (i, 0)), pl.BlockSpec((1, D), const2)]
    if moe:
        args.append(w_router)
        in_specs.append(pl.BlockSpec(w_router.shape, const2))
    args += [wg, wu, wd]
    in_specs += [
        pl.BlockSpec((1, D, tf), lambda i, e, f: (e, 0, f)),
        pl.BlockSpec((1, D, tf), lambda i, e, f: (e, 0, f)),
        pl.BlockSpec((1, tf, D), lambda i, e, f: (e, f, 0)),
    ]
    if final_norm:
        args.append(g_final)
        in_specs.append(pl.BlockSpec((1, D), const2))
    scratch = [pltpu.VMEM((tm, D), F32 if precise else BF16), pltpu.VMEM((tm, D), F32)]
    if moe:
        scratch += [pltpu.VMEM((tm, D), F32), pltpu.VMEM((tm, LANES), F32)]
    return pl.pallas_call(
        functools.partial(_ffn_kernel, n_experts=E, final_norm=final_norm, precise=precise),
        grid=(N // tm, E, F // tf),
        in_specs=in_specs,
        out_specs=pl.BlockSpec((tm, D), lambda i, e, f: (i, 0)),
        out_shape=jax.ShapeDtypeStruct((N, D), F32),
        scratch_shapes=scratch,
        compiler_params=_cparams(("parallel", "arbitrary", "arbitrary")),
        name="moe_ffn" if moe else "dense_ffn",
    )(*args)


ROUTE_I1, ROUTE_I2, ROUTE_W1, ROUTE_W2, ROUTE_R1, ROUTE_R2 = range(6)


def _top2_gates(logits, n_experts):
    lane = lax.broadcasted_iota(jnp.int32, logits.shape, 1)
    lg = jnp.where(lane < n_experts, logits, -jnp.inf)
    m1 = jnp.max(lg, axis=-1, keepdims=True)
    i1 = jnp.min(jnp.where(lg == m1, lane, LANES), axis=-1, keepdims=True)
    lg2 = jnp.where(lane == i1, -jnp.inf, lg)
    m2 = jnp.max(lg2, axis=-1, keepdims=True)
    i2 = jnp.min(jnp.where(lg2 == m2, lane, LANES), axis=-1, keepdims=True)
    e2 = jnp.exp(m2 - m1)
    den = 1.0 + e2
    return lane, i1, i2, 1.0 / den, e2 / den


def _moe_route_kernel(x_ref, g_ref, wr_ref, tri_ref, route_ref, cnt_ref, carry, *, n_experts):
    @pl.when(pl.program_id(0) == 0)
    def _():
        carry[...] = jnp.zeros_like(carry)

    h = _rmsnorm(x_ref[...], g_ref[...])
    logits = jnp.dot(h, wr_ref[...], preferred_element_type=F32, precision=lax.Precision.HIGHEST)
    lane, i1, i2, w1, w2 = _top2_gates(logits, n_experts)
    member = jnp.where((lane == i1) | (lane == i2), 1.0, 0.0)
    before = _dot(tri_ref[...], member.astype(BF16)) + carry[0:1, :]
    r1 = jnp.sum(jnp.where(lane == i1, before, 0.0), axis=-1, keepdims=True)
    r2 = jnp.sum(jnp.where(lane == i2, before, 0.0), axis=-1, keepdims=True)
    rec = jnp.zeros(logits.shape, F32)
    for col, val in ((ROUTE_I1, i1.astype(F32)), (ROUTE_I2, i2.astype(F32)), (ROUTE_W1, w1), (ROUTE_W2, w2),
                     (ROUTE_R1, r1), (ROUTE_R2, r2)):
        rec = jnp.where(lane == col, val, rec)
    route_ref[...] = rec
    carry[...] = carry[...] + jnp.sum(member, axis=0, keepdims=True)
    cnt_ref[...] = carry[...]


def _moe_route(x, g, w_router, n_experts, tm=512):
    N, D = x.shape
    assert N % tm == 0
    tri = (jnp.arange(tm)[:, None] > jnp.arange(tm)[None, :]).astype(BF16)
    return pl.pallas_call(
        functools.partial(_moe_route_kernel, n_experts=n_experts),
        grid=(N // tm,),
        in_specs=[pl.BlockSpec((tm, D), lambda i: (i, 0)), pl.BlockSpec((1, D), lambda i: (0, 0)),
                  pl.BlockSpec(w_router.shape, lambda i: (0, 0)), pl.BlockSpec((tm, tm), lambda i: (0, 0))],
        out_specs=[pl.BlockSpec((tm, LANES), lambda i: (i, 0)), pl.BlockSpec((8, LANES), lambda i: (0, 0))],
        out_shape=[jax.ShapeDtypeStruct((N, LANES), F32), jax.ShapeDtypeStruct((8, LANES), F32)],
        scratch_shapes=[pltpu.VMEM((8, LANES), F32)],
        compiler_params=_cparams(("arbitrary",)),
        name="moe_route",
    )(x, g, w_router, tri)


def _row_gather(idx_of, src_hbm, dst_at, sem, rows):
    def copy(r):
        return pltpu.make_async_copy(src_hbm.at[pl.ds(idx_of(r), 1), :], dst_at(r), sem)

    def start():
        lax.fori_loop(0, rows, lambda r, c: (copy(r).start(), c)[1], 0, unroll=8)

    def wait():
        lax.fori_loop(0, rows, lambda r, c: (copy(r).wait(), c)[1], 0, unroll=8)

    return start, wait, copy


def _moe_experts_kernel(te_ref, nu_ref, tos_ref, x_hbm, g_ref, wg_ref, wu_ref, wd_ref, o_ref,
                        xbuf, h_scr, acc_scr, sems, *, tile, nf):
    i, f = pl.program_id(0), pl.program_id(1)
    used = i < nu_ref[0]
    slot = i % 2
    per_step = tile // nf

    def gather(t, sl):
        return _row_gather(lambda r: tos_ref[t * tile + r], x_hbm,
                           lambda r: xbuf.at[sl, pl.ds(r, 1), :], sems.at[sl], tile)

    def prefetch_share():
        row_copy = gather(i + 1, 1 - slot)[2]
        for j in range(per_step):
            row_copy(f * per_step + j).start()

    @pl.when((f == 0) & (i == 0))
    def _():
        gather(0, 0)[0]()

    @pl.when(f == 0)
    def _():
        gather(i, slot)[1]()

    @pl.when((f == 0) & used)
    def _():
        h_scr[...] = _rmsnorm(xbuf[slot], g_ref[...]).astype(BF16)
        acc_scr[...] = jnp.zeros_like(acc_scr)

    @pl.when(used)
    def _():
        prefetch_share()
        hb = h_scr[...]
        a = _dot(hb, wg_ref[0])
        u = _dot(hb, wu_ref[0])
        act = (a * jax.nn.sigmoid(a)) * u
        acc_scr[...] += _dot(act.astype(BF16), wd_ref[0])

    @pl.when(jnp.logical_not(used))
    def _():
        prefetch_share()

    @pl.when(f == nf - 1)
    def _():
        o_ref[...] = jnp.where(used, acc_scr[...], 0.0)

    @pl.when((f == nf - 1) & (i == pl.num_programs(0) - 1))
    def _():
        gather(i + 1, 1 - slot)[1]()


def _moe_experts(x, g, wg, wu, wd, tile_expert, n_used, token_of_slot, *, tile, tf=512):
    N, D = x.shape
    E, _, F = wg.shape
    n_slots = token_of_slot.shape[0] - tile
    assert n_slots % tile == 0 and F % tf == 0
    n_tiles, nf = n_slots // tile, F // tf
    assert tile % nf == 0
    fidx = lambda i, f, nu: jnp.where(i < nu[0], f, nf - 1)
    return pl.pallas_call(
        functools.partial(_moe_experts_kernel, tile=tile, nf=nf),
        grid_spec=pltpu.PrefetchScalarGridSpec(
            num_scalar_prefetch=3,
            grid=(n_tiles, nf),
            in_specs=[pl.BlockSpec(memory_space=pl.ANY),
                      pl.BlockSpec((1, D), lambda i, f, te, nu, tos: (0, 0)),
                      pl.BlockSpec((1, D, tf), lambda i, f, te, nu, tos: (te[i], 0, fidx(i, f, nu))),
                      pl.BlockSpec((1, D, tf), lambda i, f, te, nu, tos: (te[i], 0, fidx(i, f, nu))),
                      pl.BlockSpec((1, tf, D), lambda i, f, te, nu, tos: (te[i], fidx(i, f, nu), 0))],
            out_specs=pl.BlockSpec((tile, D), lambda i, f, te, nu, tos: (i, 0)),
            scratch_shapes=[pltpu.VMEM((2, tile, D), F32), pltpu.VMEM((tile, D), BF16),
                            pltpu.VMEM((tile, D), F32), pltpu.SemaphoreType.DMA((2,))],
        ),
        out_shape=jax.ShapeDtypeStruct((n_slots, D), F32),
        compiler_params=_cparams(("arbitrary", "arbitrary")),
        name="moe_experts",
    )(tile_expert, n_used, token_of_slot, x, g, wg, wu, wd)


def _moe_combine_kernel(slot_ref, x_ref, route_ref, gf_ref, ys_hbm, o_ref, ybuf, sems, *, tile):
    i = pl.program_id(0)
    slot = i % 2

    def gather(t, sl):
        return _row_gather(lambda j: slot_ref[t * 2 * tile + j], ys_hbm,
                           lambda j: ybuf.at[sl, pl.ds(j, 1), :], sems.at[sl], 2 * tile)

    @pl.when(i == 0)
    def _():
        gather(0, 0)[0]()

    gather(i, slot)[1]()

    @pl.when(i + 1 < pl.num_programs(0))
    def _():
        gather(i + 1, 1 - slot)[0]()

    route = route_ref[...]
    w1, w2 = route[:, ROUTE_W1:ROUTE_W1 + 1], route[:, ROUTE_W2:ROUTE_W2 + 1]
    y = w1 * ybuf[slot, 0:tile, :] + w2 * ybuf[slot, tile:2 * tile, :]
    o_ref[...] = _rmsnorm(x_ref[...] + y, gf_ref[...])


def _moe_combine(x, route, g_final, ys, slots, *, tile=256):
    N, D = x.shape
    assert N % tile == 0
    return pl.pallas_call(
        functools.partial(_moe_combine_kernel, tile=tile),
        grid_spec=pltpu.PrefetchScalarGridSpec(
            num_scalar_prefetch=1,
            grid=(N // tile,),
            in_specs=[pl.BlockSpec((tile, D), lambda i, s: (i, 0)),
                      pl.BlockSpec((tile, LANES), lambda i, s: (i, 0)),
                      pl.BlockSpec((1, D), lambda i, s: (0, 0)),
                      pl.BlockSpec(memory_space=pl.ANY)],
            out_specs=pl.BlockSpec((tile, D), lambda i, s: (i, 0)),
            scratch_shapes=[pltpu.VMEM((2, 2 * tile, D), F32), pltpu.SemaphoreType.DMA((2,))],
        ),
        out_shape=jax.ShapeDtypeStruct((N, D), F32),
        compiler_params=_cparams(("arbitrary",)),
        name="moe_combine",
    )(slots, x, route, g_final, ys)


def _moe_prompt(x, g, w_router, wg, wu, wd, g_final, *, tile=448, ctile=256):
    N, D = x.shape
    E = wg.shape[0]
    route, counts = _moe_route(x, g, w_router, E)
    cnt = counts[0, :E].astype(jnp.int32)
    padded = (cnt + tile - 1) // tile * tile
    ends = jnp.cumsum(padded)
    off = ends - padded
    i1, i2 = route[:, ROUTE_I1].astype(jnp.int32), route[:, ROUTE_I2].astype(jnp.int32)
    slot1 = off[i1] + route[:, ROUTE_R1].astype(jnp.int32)
    slot2 = off[i2] + route[:, ROUTE_R2].astype(jnp.int32)
    n_slots = (TOP_K * N + E * (tile - 1)) // tile * tile
    rows = jnp.arange(N, dtype=jnp.int32)
    token_of_slot = jnp.zeros((n_slots + tile,), jnp.int32).at[jnp.concatenate([slot1, slot2])].set(
        jnp.concatenate([rows, rows]), unique_indices=True)
    n_used = ends[-1:] // tile
    tile_start = jnp.minimum(jnp.arange(n_slots // tile, dtype=jnp.int32), n_used[0] - 1) * tile
    tile_expert = jnp.sum(tile_start[:, None] >= ends[None, :], axis=1).astype(jnp.int32)
    ys = _moe_experts(x, g, wg, wu, wd, tile_expert, n_used.astype(jnp.int32), token_of_slot, tile=tile)
    slots = jnp.stack([slot1.reshape(-1, ctile), slot2.reshape(-1, ctile)], axis=1).reshape(-1)
    return _moe_combine(x, route, g_final, ys, slots, tile=ctile)


def _nsa_proj_kernel(x_ref, g_ref, wq_ref, wkv_ref, wgt_ref, wck_ref, wcv_ref,
                     q_ref, kvt_ref, ks_ref, vs_ref, kw_ref, vw_ref, gt_ref, kc_ref, vc_ref):
    hb = _rmsnorm(x_ref[...], g_ref[...]).astype(BF16)
    q_ref[...] = _dot(hb, wq_ref[...]).astype(BF16)
    kv = _dot(hb, wkv_ref[...])
    kvt_ref[0] = kv.T
    ks_ref[...] = kv[:, 2 * KV_W:3 * KV_W].astype(BF16)
    vs_ref[...] = kv[:, 3 * KV_W:4 * KV_W].astype(BF16)
    kw_ref[...] = kv[:, 4 * KV_W:5 * KV_W].astype(BF16)
    vw_ref[...] = kv[:, 5 * KV_W:].astype(BF16)
    gt_ref[...] = jax.nn.sigmoid(_dot(hb, wgt_ref[...]))
    tm = kv.shape[0]
    kc_ref[...] = (kv[:, :KV_W] * wck_ref[...]).reshape(tm // CMP_BLOCK, CMP_BLOCK, KV_W).sum(axis=1)
    vc_ref[...] = (kv[:, KV_W:2 * KV_W] * wcv_ref[...]).reshape(tm // CMP_BLOCK, CMP_BLOCK, KV_W).sum(axis=1)


def _nsa_proj(x, g, wq, wkv, wgt, wck, wcv, *, tm, seq):
    N, D = x.shape
    assert N % tm == 0 and tm % (8 * CMP_BLOCK) == 0 and seq % tm == 0 and N % seq == 0
    per_seq = seq // tm
    row = lambda w: pl.BlockSpec((tm, w), lambda i: (i, 0))
    full = lambda a: pl.BlockSpec(a.shape, lambda i: (0, 0))
    out_shape = [jax.ShapeDtypeStruct((N, N_HEADS * HEAD_DIM), BF16),
                 jax.ShapeDtypeStruct((N // seq, 6 * KV_W, seq), F32)]
    out_shape += [jax.ShapeDtypeStruct((N, KV_W), BF16)] * 4
    out_shape += [jax.ShapeDtypeStruct((N, LANES), F32)]
    out_shape += [jax.ShapeDtypeStruct((N // CMP_BLOCK, KV_W), F32)] * 2
    out_specs = [row(N_HEADS * HEAD_DIM),
                 pl.BlockSpec((1, 6 * KV_W, tm), lambda i: (i // per_seq, 0, i % per_seq))]
    out_specs += [row(KV_W)] * 4 + [row(LANES)]
    out_specs += [pl.BlockSpec((tm // CMP_BLOCK, KV_W), lambda i: (i, 0))] * 2
    return pl.pallas_call(
        _nsa_proj_kernel,
        grid=(N // tm,),
        in_specs=[row(D), full(g), full(wq), full(wkv), full(wgt), full(wck), full(wcv)],
        out_specs=out_specs,
        out_shape=out_shape,
        compiler_params=_cparams(("parallel",)),
        name="nsa_proj_prompt",
    )(x, g, wq, wkv, wgt, wck, wcv)


def _nsa_proj_sample_kernel(x_ref, g_ref, w_ref, z_ref):
    z_ref[...] = _dot3(_rmsnorm(x_ref[...], g_ref[...]), w_ref[...])


def _nsa_proj_sample(x, g, w_in, tn=384):
    B, D = x.shape
    cols = w_in.shape[1]
    assert cols % tn == 0
    return pl.pallas_call(
        _nsa_proj_sample_kernel,
        grid=(cols // tn,),
        in_specs=[pl.BlockSpec((B, D), lambda j: (0, 0)), pl.BlockSpec((1, D), lambda j: (0, 0)),
                  pl.BlockSpec((D, tn), lambda j: (0, j))],
        out_specs=pl.BlockSpec((B, tn), lambda j: (0, j)),
        out_shape=jax.ShapeDtypeStruct((B, cols), F32),
        compiler_params=_cparams(("parallel",)),
        name="nsa_proj_sample",
    )(x, g, w_in)


def _nsa_prompt_kernel(q_ref, gt_ref, kc_ref, vc_ref, ks_ref, vs_ref, kw_ref, vw_ref, x_ref, wo_ref,
                       o_ref, o_scr, sel_scr, *, tq, kchunk, seq):
    t0 = pl.program_id(1) * tq
    rows = GQA * tq
    scale = HEAD_DIM ** -0.5
    assert scale == 0.125
    n_blocks = seq // SLC_BLOCK
    top_k = min(SLC_TOPK, n_blocks)

    q = q_ref[0]
    gates = gt_ref[0]
    qpos_t = t0 + lax.broadcasted_iota(jnp.int32, (tq, 1), 0)
    qpos = jnp.concatenate([qpos_t] * GQA, axis=0)
    tq_f = qpos.astype(F32)

    lane = lax.broadcasted_iota(jnp.int32, (tq, LANES), 1)
    jblk = lane >> 1
    real = ((lane & 1) == 0) & (jblk < n_blocks)
    blk = qpos_t // SLC_BLOCK
    valid = real & (jblk <= blk)
    forced = (jblk == 0) | (jblk == blk) | (jblk == blk - 1)
    valid_f = jnp.where(valid, 1.0, 0.0)
    valid_b = valid_f.astype(BF16)
    bonus = jnp.where(forced, FORCE_BONUS, 0.0)
    floor = jnp.where(valid, 0.0, jnp.where(real, -1.0, -2.0))
    c_lane = lax.broadcasted_iota(jnp.int32, (1, LANES), 1)
    c_mid = c_lane.astype(F32) * CMP_BLOCK + (CMP_BLOCK - 1) / 2
    c_end = (c_lane + 1) * CMP_BLOCK - 1

    n_chunks = (t0 + tq + kchunk - 1) // kchunk
    wstart = pl.multiple_of(jnp.maximum(t0 - WINDOW, 0), tq)
    wlen = WINDOW + tq

    need_rank = (t0 + tq - 1) // SLC_BLOCK >= top_k
    wpos = wstart + lax.broadcasted_iota(jnp.int32, (1, wlen), 1)
    w_ok = (wpos <= qpos_t) & (qpos_t - wpos < WINDOW)
    w_dist = (wpos - qpos_t).astype(F32)

    def biased(ok, dist, kv):
        return jnp.concatenate(
            [jnp.where(ok, _alibi_slope(kv * GQA + g) * dist, NEG) for g in range(GQA)], axis=0)

    for kv in range(N_KV_HEADS):
        hs = slice(kv * HEAD_DIM, (kv + 1) * HEAD_DIM)
        qs = jnp.concatenate(
            [q[:, (kv * GQA + g) * HEAD_DIM:(kv * GQA + g + 1) * HEAD_DIM] for g in range(GQA)], axis=0) * scale
        slope = _head_slopes(rows, tq, kv * GQA)

        s_c = _dot_nt(qs, kc_ref[0][:, hs]) - slope * (tq_f - c_mid)
        p_c = _masked_softmax(s_c, c_end <= qpos)
        o_c = _dot(p_c.astype(BF16), vc_ref[0][:, hs])

        imp = p_c[0:tq]
        for g in range(1, GQA):
            imp = imp + p_c[g * tq:(g + 1) * tq]
        sel_scr[...] = valid_b

        @pl.when(need_rank)
        def _():
            pair = imp + pltpu.roll(imp, LANES - 1, 1)
            score = valid_f * (pair + bonus) + floor
            score_t = score.T[:2 * n_blocks]
            row_id = lax.broadcasted_iota(jnp.int32, score_t.shape, 0)
            rank_t = jnp.zeros(score_t.shape, jnp.int32)
            for k in range(n_blocks):
                other = score_t[2 * k:2 * k + 1, :]
                beats = (other > score_t) | ((other == score_t) & (row_id > 2 * k))
                rank_t = rank_t + beats.astype(jnp.int32)
            top_t = jnp.where(rank_t < top_k, 1.0, 0.0)
            top = jnp.concatenate([top_t, jnp.zeros((LANES - 2 * n_blocks, tq), F32)], axis=0).T
            sel_scr[...] = (top * valid_f).astype(BF16)

        sel = sel_scr[...]

        def chunk(c, carry):
            m, l, acc = carry
            k0 = pl.multiple_of(c * kchunk, kchunk)
            kb = ks_ref[0, pl.ds(k0, kchunk), hs]
            vb = vs_ref[0, pl.ds(k0, kchunk), hs]
            pos = k0 + lax.broadcasted_iota(jnp.int32, (1, kchunk), 1)
            expand = (lax.broadcasted_iota(jnp.int32, (LANES, kchunk), 0)
                      == 2 * ((k0 + lax.broadcasted_iota(jnp.int32, (LANES, kchunk), 1)) // SLC_BLOCK))
            picked = _dot(sel, jnp.where(expand, 1.0, 0.0).astype(BF16))
            ok = (picked > 0.5) & (pos <= qpos_t)
            s = _dot_nt(qs, kb) + biased(ok, (pos - qpos_t).astype(F32), kv)
            m_new = jnp.maximum(m, jnp.max(s, axis=-1, keepdims=True))
            alpha = jnp.exp(m - m_new)
            p = jnp.exp(s - m_new)
            l = alpha * l + jnp.sum(p, axis=-1, keepdims=True)
            acc = alpha * acc + _dot(p.astype(BF16), vb)
            return m_new, l, acc

        init = (jnp.full((rows, 1), NEG, F32), jnp.zeros((rows, 1), F32), jnp.zeros((rows, HEAD_DIM), F32))
        _, l_s, acc_s = lax.fori_loop(0, n_chunks, chunk, init)
        o_s = acc_s / l_s

        kb = kw_ref[0, pl.ds(wstart, wlen), hs]
        vb = vw_ref[0, pl.ds(wstart, wlen), hs]
        s_w = _dot_nt(qs, kb) + biased(w_ok, w_dist, kv)
        p_w = jnp.exp(s_w - jnp.max(s_w, axis=-1, keepdims=True))
        o_w = _dot(p_w.astype(BF16), vb) / jnp.sum(p_w, axis=-1, keepdims=True)

        def gate_col(n):
            base = n * N_HEADS + kv * GQA
            return jnp.concatenate([gates[:, base + g:base + g + 1] for g in range(GQA)], axis=0)

        o = gate_col(0) * o_c + gate_col(1) * o_s + gate_col(2) * o_w
        for g in range(GQA):
            h = kv * GQA + g
            o_scr[:, h * HEAD_DIM:(h + 1) * HEAD_DIM] = o[g * tq:(g + 1) * tq]

    o_ref[0] = x_ref[0] + _dot(o_scr[...].astype(BF16), wo_ref[...])


def _nsa_prompt(q, gates, kc, vc, ks, vs, kw, vw, x, w_o, *, tq=128, kchunk=512):
    B, L, D = x.shape
    assert L % kchunk == 0 and kchunk % tq == 0 and L >= WINDOW + tq and L % SLC_BLOCK == 0
    assert 2 * (L // SLC_BLOCK) <= LANES and L // CMP_BLOCK <= LANES and kc.shape[1] == LANES
    tile = lambda w: pl.BlockSpec((1, tq, w), lambda b, t: (b, t, 0))
    whole = lambda a: pl.BlockSpec((1,) + a.shape[1:], lambda b, t: (b, 0, 0))
    return pl.pallas_call(
        functools.partial(_nsa_prompt_kernel, tq=tq, kchunk=kchunk, seq=L),
        grid=(B, L // tq),
        in_specs=[tile(N_HEADS * HEAD_DIM), tile(LANES), whole(kc), whole(vc), whole(ks), whole(vs),
                  whole(kw), whole(vw), tile(D), pl.BlockSpec(w_o.shape, lambda b, t: (0, 0))],
        out_specs=tile(D),
        out_shape=jax.ShapeDtypeStruct((B, L, D), F32),
        scratch_shapes=[pltpu.VMEM((tq, N_HEADS * HEAD_DIM), F32), pltpu.VMEM((tq, LANES), BF16)],
        compiler_params=_cparams(("parallel", "arbitrary")),
        name="nsa_attn_prompt",
    )(q, gates, kc, vc, ks, vs, kw, vw, x, w_o)


CMP_GROUP = 32
MXU_DEPTH = 256


def _cmp_sample_kernel(pt_ref, cache_ref, w_ref, s_ref, o_ref, buf, sems, *, rows):
    b, g = pl.program_id(0), pl.program_id(1)
    n_groups = pl.num_programs(1)
    step = b * n_groups + g
    slot = step % 2

    def page_copy(bb, gg, sl, i):
        page = pt_ref[bb, gg * CMP_GROUP + i]
        return pltpu.make_async_copy(cache_ref.at[page, pl.ds(0, rows), :], buf.at[sl, i], sems.at[sl])

    def start_group(bb, gg, sl):
        for i in range(CMP_GROUP):
            page_copy(bb, gg, sl, i).start()

    @pl.when(step == 0)
    def _():
        start_group(b, g, slot)

    @pl.when(step + 1 < pl.num_programs(0) * n_groups)
    def _():
        nxt = step + 1
        start_group(nxt // n_groups, nxt % n_groups, 1 - slot)

    for i in range(CMP_GROUP):
        page_copy(b, g, slot, i).wait()

    w = w_ref[...]
    acc = jnp.zeros((rows, LANES), F32)
    for j in range(CMP_GROUP // 2):
        p = jnp.concatenate([buf[slot, 2 * j] * w, buf[slot, 2 * j + 1] * w], axis=1)
        hi, lo = _split(p)
        r = _dot(jnp.concatenate([hi, lo], axis=0), s_ref[j])
        acc = acc + (r[:rows] + r[rows:])
    o_ref[0] = acc


def _cmp_sample(page_table, cache_t, w_rows):
    B, n_pages = page_table.shape
    _, _, page = cache_t.shape
    rows = w_rows.shape[0]
    per_page = page // CMP_BLOCK
    assert n_pages % CMP_GROUP == 0 and CMP_GROUP * per_page == LANES and 2 * page == MXU_DEPTH
    k = jnp.arange(2 * page)
    token = (2 * jnp.arange(CMP_GROUP // 2)[:, None] + k[None, :] // page) * per_page + (k[None, :] % page) // CMP_BLOCK
    block_sum = (token[:, :, None] == jnp.arange(LANES)[None, None, :]).astype(BF16)
    return pl.pallas_call(
        functools.partial(_cmp_sample_kernel, rows=rows),
        grid_spec=pltpu.PrefetchScalarGridSpec(
            num_scalar_prefetch=1,
            grid=(B, n_pages // CMP_GROUP),
            in_specs=[pl.BlockSpec(memory_space=pl.ANY),
                      pl.BlockSpec(w_rows.shape, lambda b, g, pt: (0, 0)),
                      pl.BlockSpec(block_sum.shape, lambda b, g, pt: (0, 0, 0))],
            out_specs=pl.BlockSpec((1, rows, LANES), lambda b, g, pt: (b, 0, g)),
            scratch_shapes=[pltpu.VMEM((2, CMP_GROUP, rows, page), F32), pltpu.SemaphoreType.DMA((2,))],
        ),
        out_shape=jax.ShapeDtypeStruct((B, rows, n_pages * per_page), F32),
        compiler_params=_cparams(("arbitrary", "arbitrary")),
        name="cmp_sample",
    )(page_table, cache_t, w_rows, block_sum)


def _select_sample_kernel(q_ref, kcv_ref, oc_ref, idx_ref, *, past):
    scale = HEAD_DIM ** -0.5
    q = q_ref[0]
    nc = kcv_ref.shape[2]
    n_past_blocks = past // SLC_BLOCK
    per_block = SLC_BLOCK // CMP_BLOCK
    row = lax.broadcasted_iota(jnp.int32, (N_HEADS, 1), 0)
    slope = _head_slopes(N_HEADS, 1, 0)
    tok = lax.broadcasted_iota(jnp.int32, (1, nc), 1)
    c_mid = tok.astype(F32) * CMP_BLOCK + (CMP_BLOCK - 1) / 2
    c_ok = (tok + 1) * CMP_BLOCK - 1 <= past
    bias = slope * (float(past) - c_mid)

    s_c = jnp.zeros((N_HEADS, nc), F32)
    for kv in range(N_KV_HEADS):
        s_kv = _dot3(q, kcv_ref[0, kv * HEAD_DIM:(kv + 1) * HEAD_DIM, :])
        s_c = jnp.where(row // GQA == kv, s_kv, s_c)
    p_c = _masked_softmax(s_c * scale - bias, c_ok)
    o_c = jnp.zeros((N_HEADS, HEAD_DIM), F32)
    for kv in range(N_KV_HEADS):
        o_kv = _dot3(p_c, kcv_ref[0, KV_W + kv * HEAD_DIM:KV_W + (kv + 1) * HEAD_DIM, :], nt=True)
        o_c = jnp.where(row // GQA == kv, o_kv, o_c)
    oc_ref[0] = o_c

    imp = p_c
    shift = 1
    while shift < GQA:
        imp = imp + pltpu.roll(imp, shift, 0)
        shift *= 2
    assert per_block == 2
    pair = imp + pltpu.roll(imp, nc - 1, 1)
    jblk = tok // per_block
    real = tok % per_block == 0
    forced = (jblk == 0) | (jblk == n_past_blocks - 1)
    score = jnp.where(real, pair + jnp.where(forced, FORCE_BONUS, 0.0), -2.0)
    own = jnp.float32(FORCE_BONUS)
    rank = (own > score).astype(jnp.int32)
    for k in range(n_past_blocks):
        col = score[:, per_block * k:per_block * k + 1]
        beats = (col > score) | ((col == score) & (jblk > k))
        rank = rank + beats.astype(jnp.int32)
    own_rank = jnp.sum(jnp.where(real & (score >= own), 1, 0), axis=-1, keepdims=True)
    out_lane = lax.broadcasted_iota(jnp.int32, (N_HEADS, LANES), 1)
    idx = jnp.zeros((N_HEADS, LANES), jnp.int32)
    for r in range(SLC_TOPK):
        hit = jnp.sum(jnp.where(real & (rank == r), jblk, 0), axis=-1, keepdims=True)
        hit = hit + jnp.where(own_rank == r, n_past_blocks, 0)
        idx = jnp.where(out_lane == r, hit, idx)
    idx_ref[0] = idx


def _select_sample(q16, kcv, past):
    B = q16.shape[0]
    nc = kcv.shape[2]
    assert past % SLC_BLOCK == 0 and nc == past // CMP_BLOCK and nc % LANES == 0
    assert past // SLC_BLOCK + 1 > SLC_TOPK
    return pl.pallas_call(
        functools.partial(_select_sample_kernel, past=past),
        grid=(B,),
        in_specs=[pl.BlockSpec((1,) + q16.shape[1:], lambda b: (b, 0, 0)),
                  pl.BlockSpec((1,) + kcv.shape[1:], lambda b: (b, 0, 0))],
        out_specs=[pl.BlockSpec((1, N_HEADS, HEAD_DIM), lambda b: (b, 0, 0)),
                   pl.BlockSpec((1, N_HEADS, LANES), lambda b: (b, 0, 0))],
        out_shape=[jax.ShapeDtypeStruct((B, N_HEADS, HEAD_DIM), F32),
                   jax.ShapeDtypeStruct((B, N_HEADS, LANES), jnp.int32)],
        compiler_params=_cparams(("parallel",)),
        name="select_sample",
    )(q16, kcv)


def _attend_sample_kernel(pt_ref, idx_ref, q_ref, oc_ref, gt_ref, new_ref, win_ref, cache_ref,
                          o_ref, kbuf, vbuf, sems, *, past):
    b = pl.program_id(0)
    scale = HEAD_DIM ** -0.5
    n_past_blocks = past // SLC_BLOCK
    page_rows = cache_ref.shape[2]
    blocks_per_page = page_rows // SLC_BLOCK
    nkeys = SLC_TOPK * page_rows

    def block_copies(kv, n):
        blk = idx_ref[(b * N_KV_HEADS + kv) * SLC_TOPK + n]
        in_past = blk < n_past_blocks
        page = pt_ref[b, jnp.minimum(blk, n_past_blocks - 1) // blocks_per_page]
        dst = pl.ds(n * page_rows, page_rows)
        ck = pltpu.make_async_copy(cache_ref.at[page, pl.ds((2 * N_KV_HEADS + kv) * HEAD_DIM, HEAD_DIM), :],
                                   kbuf.at[kv, :, dst], sems.at[0, kv, n])
        cv = pltpu.make_async_copy(cache_ref.at[page, pl.ds((3 * N_KV_HEADS + kv) * HEAD_DIM, HEAD_DIM), :],
                                   vbuf.at[kv, :, dst], sems.at[1, kv, n])
        return blk, in_past, ck, cv

    for kv in range(N_KV_HEADS):
        for n in range(SLC_TOPK):
            _, in_past, ck, cv = block_copies(kv, n)

            @pl.when(in_past)
            def _():
                ck.start()
                cv.start()

            @pl.when(jnp.logical_not(in_past))
            def _():
                kbuf[kv, :, n * page_rows:(n + 1) * page_rows] = jnp.zeros((HEAD_DIM, page_rows), F32)
                vbuf[kv, :, n * page_rows:(n + 1) * page_rows] = jnp.zeros((HEAD_DIM, page_rows), F32)

    q = q_ref[0]
    row = lax.broadcasted_iota(jnp.int32, (N_HEADS, 1), 0)
    slope = _head_slopes(N_HEADS, 1, 0)
    key_lane = lax.broadcasted_iota(jnp.int32, (1, nkeys), 1)

    wb = win_ref.shape[2]
    w_lane = lax.broadcasted_iota(jnp.int32, (1, wb), 1)
    wpos = past - wb + w_lane
    w_ok = (past - wpos < WINDOW) & (wpos >= 0)
    w_bias = slope * (past - wpos).astype(F32)
    s_w = jnp.zeros((N_HEADS, wb), F32)
    s_n = jnp.zeros((N_HEADS, 1), F32)
    for kv in range(N_KV_HEADS):
        mine = row // GQA == kv
        s_kv = _dot3(q, win_ref[0, kv * HEAD_DIM:(kv + 1) * HEAD_DIM, :])
        s_w = jnp.where(mine, s_kv, s_w)
        s_n = jnp.where(mine, jnp.sum(q * new_ref[0, 4, kv:kv + 1, :], axis=-1, keepdims=True), s_n)
    s_w = jnp.where(w_ok, s_w * scale - w_bias, NEG)
    s_n = s_n * scale
    m_w = jnp.maximum(jnp.max(s_w, axis=-1, keepdims=True), s_n)
    p_w = jnp.where(w_ok, jnp.exp(s_w - m_w), 0.0)
    p_n = jnp.exp(s_n - m_w)
    l_w = jnp.maximum(jnp.sum(p_w, axis=-1, keepdims=True) + p_n, 1e-30)
    p_w = p_w / l_w
    p_n = p_n / l_w
    o_w = jnp.zeros((N_HEADS, HEAD_DIM), F32)
    for kv in range(N_KV_HEADS):
        o_kv = (_dot3(p_w, win_ref[0, KV_W + kv * HEAD_DIM:KV_W + (kv + 1) * HEAD_DIM, :], nt=True)
                + p_n * new_ref[0, 5, kv:kv + 1, :])
        o_w = jnp.where(row // GQA == kv, o_kv, o_w)

    for kv in range(N_KV_HEADS):
        for n in range(SLC_TOPK):
            _, in_past, ck, cv = block_copies(kv, n)

            @pl.when(in_past)
            def _():
                ck.wait()
                cv.wait()

    s_s = jnp.zeros((N_HEADS, nkeys), F32)
    pos = jnp.zeros((N_HEADS, nkeys), jnp.int32)
    live = jnp.zeros((N_HEADS, nkeys), jnp.int32)
    own = jnp.zeros((N_HEADS, 1), jnp.int32)
    s_n = jnp.zeros((N_HEADS, 1), F32)
    in_slab = key_lane % page_rows
    for kv in range(N_KV_HEADS):
        mine = row // GQA == kv
        s_kv = _dot3(q, kbuf[kv])
        s_s = jnp.where(mine, s_kv, s_s)
        s_n = jnp.where(mine, jnp.sum(q * new_ref[0, 2, kv:kv + 1, :], axis=-1, keepdims=True), s_n)
        base_kv = jnp.zeros((1, nkeys), jnp.int32)
        half_kv = jnp.zeros((1, nkeys), jnp.int32)
        own_kv = jnp.int32(0)
        for n in range(SLC_TOPK):
            blk = idx_ref[(b * N_KV_HEADS + kv) * SLC_TOPK + n]
            here = key_lane // page_rows == n
            base_kv = jnp.where(here, (blk // blocks_per_page) * page_rows, base_kv)
            half_kv = jnp.where(here, blk % blocks_per_page, half_kv)
            own_kv = own_kv + (blk >= n_past_blocks).astype(jnp.int32)
        pos = jnp.where(mine, base_kv + in_slab, pos)
        live = jnp.where(mine, (in_slab // SLC_BLOCK == half_kv).astype(jnp.int32), live)
        own = jnp.where(mine, own_kv, own)
    s_ok = (live > 0) & (pos < past)
    n_ok = own > 0
    s_s = jnp.where(s_ok, s_s * scale - slope * (past - pos).astype(F32), NEG)
    s_n = jnp.where(n_ok, s_n * scale, NEG)
    m_s = jnp.maximum(jnp.max(s_s, axis=-1, keepdims=True), s_n)
    p_s = jnp.where(s_ok, jnp.exp(s_s - m_s), 0.0)
    p_n = jnp.where(n_ok, jnp.exp(s_n - m_s), 0.0)
    l_s = jnp.maximum(jnp.sum(p_s, axis=-1, keepdims=True) + p_n, 1e-30)
    p_s = p_s / l_s
    p_n = p_n / l_s
    o_s = jnp.zeros((N_HEADS, HEAD_DIM), F32)
    for kv in range(N_KV_HEADS):
        o_kv = _dot3(p_s, vbuf[kv], nt=True) + p_n * new_ref[0, 3, kv:kv + 1, :]
        o_s = jnp.where(row // GQA == kv, o_kv, o_s)

    gt = jax.nn.sigmoid(gt_ref[0])
    o_ref[0] = gt[:, 0:1] * oc_ref[0] + gt[:, 1:2] * o_s + gt[:, 2:3] * o_w


def _attend_sample(page_table, idx, q16, o_c, gates, kv_new, win_t, cache_t, past):
    B = q16.shape[0]
    page_rows = cache_t.shape[2]
    assert win_t.shape[2] <= WINDOW and page_rows % SLC_BLOCK == 0
    blk3 = lambda a: pl.BlockSpec((1,) + a.shape[1:], lambda b, pt, ix: (b,) + (0,) * (a.ndim - 1))
    return pl.pallas_call(
        functools.partial(_attend_sample_kernel, past=past),
        grid_spec=pltpu.PrefetchScalarGridSpec(
            num_scalar_prefetch=2,
            grid=(B,),
            in_specs=[blk3(q16), blk3(o_c), blk3(gates), blk3(kv_new), blk3(win_t),
                      pl.BlockSpec(memory_space=pl.ANY)],
            out_specs=pl.BlockSpec((1, N_HEADS, HEAD_DIM), lambda b, pt, ix: (b, 0, 0)),
            scratch_shapes=[pltpu.VMEM((N_KV_HEADS, HEAD_DIM, SLC_TOPK * page_rows), F32),
                            pltpu.VMEM((N_KV_HEADS, HEAD_DIM, SLC_TOPK * page_rows), F32),
                            pltpu.SemaphoreType.DMA((2, N_KV_HEADS, SLC_TOPK))],
        ),
        out_shape=jax.ShapeDtypeStruct((B, N_HEADS, HEAD_DIM), F32),
        compiler_params=_cparams(("arbitrary",)),
        name="attend_sample",
    )(page_table, idx, q16, o_c, gates, kv_new, win_t, cache_t)


def _linear_res_kernel(x_ref, a_ref, w_ref, o_ref):
    o_ref[...] = x_ref[...] + _dot3(a_ref[...], w_ref[...])


def _linear_res(x, a, w):
    return pl.pallas_call(
        _linear_res_kernel,
        out_shape=jax.ShapeDtypeStruct(x.shape, F32),
        compiler_params=pltpu.CompilerParams(vmem_limit_bytes=VMEM_LIMIT),
        name="out_proj_sample",
    )(x, a, w)


def _expand_cmp_weights(w_cmp, rows):
    w = jnp.repeat(w_cmp, HEAD_DIM, axis=1)
    return jnp.tile(w, (rows // CMP_BLOCK, 1))


def _row(v):
    return v.reshape(1, -1)


def _prompt_mixers(x, norm_mix, w_pool, pool_scale, w_in, w_cmp_k, w_cmp_v, w_out, norm_ffn0, wfg, wfu, wfd):
    B, L, D = x.shape
    q_w = N_HEADS * HEAD_DIM
    x, h_last = _pool_prompt(x, _row(norm_mix[0]), w_pool.astype(BF16), _row(pool_scale))
    new_pool = h_last[None, :, POOL_HALO - (max(POOL_WINDOWS) - 1):, :]
    x = _ffn(x.reshape(B * L, D), _row(norm_ffn0), wfg.astype(BF16), wfu.astype(BF16), wfd.astype(BF16), tm=1024)

    tm = 512
    wq = w_in[:, :q_w].astype(BF16)
    wkv = w_in[:, q_w:q_w + 6 * KV_W].astype(BF16)
    wgt = jnp.pad(w_in[:, q_w + 6 * KV_W:], ((0, 0), (0, LANES - N_BRANCH * N_HEADS))).astype(BF16)
    q, kv_t, ks, vs, kw, vw, gates, kc, vc = _nsa_proj(
        x, _row(norm_mix[1]), wq, wkv, wgt, _expand_cmp_weights(w_cmp_k, tm), _expand_cmp_weights(w_cmp_v, tm),
        tm=tm, seq=L)
    kv_t = kv_t.reshape(B, 6, N_KV_HEADS, HEAD_DIM, L)
    new_kv = kv_t[:, :4].transpose(0, 4, 1, 2, 3)[None]
    new_win = kv_t[:, 4:, :, :, L - min(WINDOW, L):].transpose(0, 4, 1, 2, 3)[None]
    nc = L // CMP_BLOCK
    pad_c = lambda a: jnp.pad(a.reshape(B, nc, KV_W), ((0, 0), (0, LANES - nc), (0, 0))).astype(BF16)
    per_seq = lambda a: a.reshape(B, L, a.shape[-1])
    x = _nsa_prompt(per_seq(q), per_seq(gates), pad_c(kc), pad_c(vc), per_seq(ks), per_seq(vs),
                    per_seq(kw), per_seq(vw), per_seq(x), w_out.astype(BF16))
    return x.reshape(B * L, D), new_pool, new_kv, new_win


def _sample_mixers(x, state_pool, cache, state_win, page_table, norm_mix, w_pool, pool_scale, w_in, w_cmp_k,
                   w_cmp_v, w_out, norm_ffn0, wfg, wfu, wfd):
    SB, D = x.shape
    n_phys, page = cache.shape[:2]
    past = page_table.shape[1] * page
    q_w = N_HEADS * HEAD_DIM
    x, h = _pool_sample(x, state_pool.transpose(1, 0, 2), _row(norm_mix[0]), w_pool, _row(pool_scale))
    new_pool = jnp.concatenate([state_pool[:, 1:], h[:, None]], axis=1)[None]
    x = _ffn(x, _row(norm_ffn0), wfg, wfu, wfd, tm=SB)

    cols = w_in.shape[1]
    z = _nsa_proj_sample(x, _row(norm_mix[1]), jnp.pad(w_in, ((0, 0), (0, -cols % LANES))))
    kv_new = z[:, q_w:q_w + 6 * KV_W].reshape(SB, 6, N_KV_HEADS, HEAD_DIM)
    new_kv = kv_new[None, :, None, :4]
    keep = min(WINDOW, state_win.shape[1] + 1)
    new_win = jnp.concatenate([state_win, kv_new[:, None, 4:]], axis=1)[None, :, -keep:]

    cache_t = cache.transpose(0, 2, 3, 4, 1).reshape(n_phys, 4 * KV_W, page)
    win_t = state_win.transpose(0, 2, 3, 4, 1).reshape(SB, 2 * KV_W, state_win.shape[1])
    w_rows = jnp.concatenate([_expand_cmp_weights(w_cmp_k, page).T, _expand_cmp_weights(w_cmp_v, page).T], axis=0)
    kcv = _cmp_sample(page_table, cache_t, w_rows)
    q16 = z[:, :q_w].reshape(SB, N_HEADS, HEAD_DIM)
    o_c, idx = _select_sample(q16, kcv, past)
    idx = idx[:, GQA - 1::GQA, :SLC_TOPK].reshape(-1)
    gate_logits = z[:, q_w + 6 * KV_W:cols].reshape(SB, N_BRANCH, N_HEADS).transpose(0, 2, 1)
    gate_logits = jnp.pad(gate_logits, ((0, 0), (0, 0), (0, LANES - N_BRANCH)))
    o = _attend_sample(page_table, idx, q16, o_c, gate_logits, kv_new, win_t, cache_t, past)
    return _linear_res(x, o.reshape(SB, q_w), w_out), new_pool, new_kv, new_win


def kernel(x_prompt, x_sample, state_pool, cache_kv, state_win, page_table, norm_mix, w_pool, pool_scale,
           w_nsa_in, w_cmp_k, w_cmp_v, w_nsa_out, norm_ffn, w_ffn_gate, w_ffn_up, w_ffn_down, w_router,
           w_moe_gate, w_moe_up, w_moe_down, norm_final):
    assert x_sample.shape[1] == 1 and norm_mix.shape[0] == 2 and x_prompt.shape[-1] == N_HEADS * HEAD_DIM
    w_rt = jnp.pad(w_router[0], ((0, 0), (0, LANES - w_router.shape[-1])))
    moe_w = (w_moe_gate[0].astype(BF16), w_moe_up[0].astype(BF16), w_moe_down[0].astype(BF16))

    xp, new_pool_prompt, new_kv_prompt, new_win_prompt = _prompt_mixers(
        x_prompt, norm_mix, w_pool[0], pool_scale[0], w_nsa_in[0], w_cmp_k[0], w_cmp_v[0], w_nsa_out[0],
        norm_ffn[0], w_ffn_gate, w_ffn_up, w_ffn_down)
    xs, new_pool_sample, new_kv_sample, new_win_sample = _sample_mixers(
        x_sample[:, 0], state_pool[0], cache_kv[0], state_win[0], page_table, norm_mix, w_pool[0], pool_scale[0],
        w_nsa_in[0], w_cmp_k[0], w_cmp_v[0], w_nsa_out[0], norm_ffn[0], w_ffn_gate, w_ffn_up, w_ffn_down)

    yp = _moe_prompt(xp, _row(norm_ffn[1]), w_rt, *moe_w, _row(norm_final))
    ys = _ffn(xs, _row(norm_ffn[1]), *moe_w, w_rt, _row(norm_final), tm=xs.shape[0])
    return (yp.reshape(x_prompt.shape), ys.reshape(x_sample.shape), new_pool_prompt, new_pool_sample,
            new_kv_prompt, new_kv_sample, new_win_prompt, new_win_sample)
```

```python
import functools

import jax
import jax.numpy as jnp
from jax import lax
from jax.experimental import pallas as pl
from jax.experimental.pallas import tpu as pltpu

F32 = jnp.float32
BF16 = jnp.bfloat16

EPS = 1e-6
NEG = -1e30
POOL_WINDOWS = (2, 4, 8, 16)
POOL_HALO = 16
N_HEADS = 16
N_KV_HEADS = 4
GQA = N_HEADS // N_KV_HEADS
HEAD_DIM = 64
KV_W = N_KV_HEADS * HEAD_DIM
CMP_BLOCK = 32
SLC_BLOCK = 64
SLC_TOPK = 16
WINDOW = 512
N_BRANCH = 3
FORCE_BONUS = 1000.0
TOP_K = 2
LANES = 128
VMEM_LIMIT = 56 * 1024 * 1024


def _cparams(sem):
    return pltpu.CompilerParams(dimension_semantics=sem, vmem_limit_bytes=VMEM_LIMIT)


def _rmsnorm(x, g):
    ms = jnp.mean(x * x, axis=-1, keepdims=True)
    return (x * lax.rsqrt(ms + EPS)) * g


def _dot(a, b):
    return jnp.dot(a, b, preferred_element_type=F32)


def _dot_nt(a, b):
    return lax.dot_general(a, b, (((1,), (1,)), ((), ())), preferred_element_type=F32)


def _split(a):
    hi = a.astype(BF16)
    return hi, (a - hi.astype(F32)).astype(BF16)


def _dot3(a, b, nt=False):
    d = _dot_nt if nt else _dot
    ah, al = _split(a)
    bh, bl = _split(b)
    return d(ah, bh) + (d(ah, bl) + d(al, bh))


def _masked_softmax(s, mask):
    s = jnp.where(mask, s, NEG)
    m = jnp.max(s, axis=-1, keepdims=True)
    p = jnp.where(mask, jnp.exp(s - m), 0.0)
    return p / jnp.maximum(jnp.sum(p, axis=-1, keepdims=True), 1e-30)


def _alibi_slope(h):
    return 2.0 ** (-8.0 * (h + 1) / N_HEADS)


def _head_slopes(rows, rows_per_head, first_head):
    h = lax.broadcasted_iota(jnp.int32, (rows, 1), 0) // rows_per_head
    slopes = jnp.zeros((rows, 1), F32)
    for j in range(rows // rows_per_head):
        slopes = jnp.where(h == j, _alibi_slope(first_head + j), slopes)
    return slopes


def _pool_prompt_kernel(x_ref, halo_ref, g_ref, w_ref, sc_ref, o_ref, hl_ref, full_ref, *, tile):
    t = pl.program_id(1)
    g = g_ref[...]
    x = x_ref[0]
    h = _rmsnorm(x, g)
    hh = _rmsnorm(halo_ref[0], g)
    full_ref[0:POOL_HALO, :] = jnp.where(t > 0, hh, 0.0)
    full_ref[POOL_HALO:POOL_HALO + tile, :] = h
    row = t * tile + lax.broadcasted_iota(jnp.int32, (tile, 1), 0)
    group = x.shape[-1] // len(POOL_WINDOWS)
    parts = []
    for gi, w in enumerate(POOL_WINDOWS):
        cs = slice(gi * group, (gi + 1) * group)
        hg = h[:, cs]
        acc = hg
        for k in range(1, w):
            acc = acc + full_ref[POOL_HALO - k:POOL_HALO - k + tile, cs]
        cnt = jnp.minimum(row + 1, w).astype(F32)
        pooled = acc / cnt - hg
        parts.append(_dot(pooled.astype(BF16), w_ref[gi]))
    o_ref[0] = x + jnp.concatenate(parts, axis=-1) * sc_ref[...]

    @pl.when(t == pl.num_programs(1) - 1)
    def _():
        hl_ref[0] = h[tile - POOL_HALO:, :]


def _pool_prompt(x, g, w_pool, scale, tile=512):
    B, L, D = x.shape
    assert L % tile == 0 and tile % POOL_HALO == 0
    hb = tile // POOL_HALO
    return pl.pallas_call(
        functools.partial(_pool_prompt_kernel, tile=tile),
        grid=(B, L // tile),
        in_specs=[
            pl.BlockSpec((1, tile, D), lambda b, t: (b, t, 0)),
            pl.BlockSpec((1, POOL_HALO, D), lambda b, t: (b, jnp.maximum(t * hb - 1, 0), 0)),
            pl.BlockSpec((1, D), lambda b, t: (0, 0)),
            pl.BlockSpec(w_pool.shape, lambda b, t: (0, 0, 0)),
            pl.BlockSpec((1, D), lambda b, t: (0, 0)),
        ],
        out_specs=[
            pl.BlockSpec((1, tile, D), lambda b, t: (b, t, 0)),
            pl.BlockSpec((1, POOL_HALO, D), lambda b, t: (b, 0, 0)),
        ],
        out_shape=[jax.ShapeDtypeStruct((B, L, D), F32), jax.ShapeDtypeStruct((B, POOL_HALO, D), F32)],
        scratch_shapes=[pltpu.VMEM((tile + POOL_HALO, D), F32)],
        compiler_params=_cparams(("parallel", "arbitrary")),
        name="pool_prompt",
    )(x, x, g, w_pool, scale)


def _pool_sample_kernel(x_ref, st_ref, g_ref, w_ref, sc_ref, o_ref, h_ref):
    x = x_ref[...]
    h = _rmsnorm(x, g_ref[...])
    h_ref[...] = h
    P = st_ref.shape[0]
    group = x.shape[-1] // len(POOL_WINDOWS)
    parts = []
    for gi, w in enumerate(POOL_WINDOWS):
        cs = slice(gi * group, (gi + 1) * group)
        hg = h[:, cs]
        acc = hg
        for k in range(1, w):
            acc = acc + st_ref[P - k][:, cs]
        pooled = acc / float(w) - hg
        parts.append(_dot3(pooled, w_ref[gi]))
    o_ref[...] = x + jnp.concatenate(parts, axis=-1) * sc_ref[...]


def _pool_sample(x, state_t, g, w_pool, scale):
    B, D = x.shape
    assert state_t.shape[0] >= max(POOL_WINDOWS) - 1
    return pl.pallas_call(
        _pool_sample_kernel,
        out_shape=[jax.ShapeDtypeStruct((B, D), F32), jax.ShapeDtypeStruct((B, D), F32)],
        compiler_params=pltpu.CompilerParams(vmem_limit_bytes=VMEM_LIMIT),
        name="pool_sample",
    )(x, state_t, g, w_pool, scale)


def _ffn_kernel(*refs, n_experts, final_norm, precise):
    moe = n_experts > 1
    it = iter(refs)
    x_ref, g_ref = next(it), next(it)
    wr_ref = next(it) if moe else None
    wg_ref, wu_ref, wd_ref = next(it), next(it), next(it)
    gf_ref = next(it) if final_norm else None
    o_ref, h_scr, acc_scr = next(it), next(it), next(it)
    eacc_scr, gate_scr = (next(it), next(it)) if moe else (None, None)

    e, f = pl.program_id(1), pl.program_id(2)
    last_f = f == pl.num_programs(2) - 1

    @pl.when((e == 0) & (f == 0))
    def _():
        h = _rmsnorm(x_ref[...], g_ref[...])
        h_scr[...] = h.astype(h_scr.dtype)
        acc_scr[...] = jnp.zeros_like(acc_scr)
        if moe:
            logits = jnp.dot(h, wr_ref[...], preferred_element_type=F32, precision=lax.Precision.HIGHEST)
            lane, i1, i2, w1, w2 = _top2_gates(logits, n_experts)
            gate_scr[...] = jnp.where(lane == i1, w1, 0.0) + jnp.where(lane == i2, w2, 0.0)

    mm = _dot3 if precise else _dot
    hb = h_scr[...]
    a = mm(hb, wg_ref[0])
    u = mm(hb, wu_ref[0])
    act = (a * jax.nn.sigmoid(a)) * u
    y = mm(act.astype(hb.dtype), wd_ref[0])

    if moe:
        @pl.when(f == 0)
        def _():
            eacc_scr[...] = y

        @pl.when(f > 0)
        def _():
            eacc_scr[...] += y

        @pl.when(last_f)
        def _():
            gate = gate_scr[...]
            lane = lax.broadcasted_iota(jnp.int32, gate.shape, 1)
            ge = jnp.sum(jnp.where(lane == e, gate, 0.0), axis=-1, keepdims=True)
            acc_scr[...] += ge * eacc_scr[...]
    else:
        acc_scr[...] += y

    @pl.when((e == pl.num_programs(1) - 1) & last_f)
    def _():
        out = x_ref[...] + acc_scr[...]
        if final_norm:
            out = _rmsnorm(out, gf_ref[...])
        o_ref[...] = out


def _ffn(x, g, wg, wu, wd, w_router=None, g_final=None, *, tm, tf=512):
    precise = wg.dtype == F32
    N, D = x.shape
    E, _, F = wg.shape
    assert N % tm == 0 and F % tf == 0
    moe = w_router is not None
    assert moe == (E > 1)
    final_norm = g_final is not None
    const2 = lambda i, e, f: (0, 0)
    args, in_specs = [x, g], [pl.BlockSpec((tm, D), lambda i, e, f: (i, 0)), pl.BlockSpec((1, D), const2)]
    if moe:
        args.append(w_router)
        in_specs.append(pl.BlockSpec(w_router.shape, const2))
    args += [wg, wu, wd]
    in_specs += [
        pl.BlockSpec((1, D, tf), lambda i, e, f: (e, 0, f)),
        pl.BlockSpec((1, D, tf), lambda i, e, f: (e, 0, f)),
        pl.BlockSpec((1, tf, D), lambda i, e, f: (e, f, 0)),
    ]
    if final_norm:
        args.append(g_final)
        in_specs.append(pl.BlockSpec((1, D), const2))
    scratch = [pltpu.VMEM((tm, D), F32 if precise else BF16), pltpu.VMEM((tm, D), F32)]
    if moe:
        scratch += [pltpu.VMEM((tm, D), F32), pltpu.VMEM((tm, LANES), F32)]
    return pl.pallas_call(
        functools.partial(_ffn_kernel, n_experts=E, final_norm=final_norm, precise=precise),
        grid=(N // tm, E, F // tf),
        in_specs=in_specs,
        out_specs=pl.BlockSpec((tm, D), lambda i, e, f: (i, 0)),
        out_shape=jax.ShapeDtypeStruct((N, D), F32),
        scratch_shapes=scratch,
        compiler_params=_cparams(("parallel", "arbitrary", "arbitrary")),
        name="moe_ffn" if moe else "dense_ffn",
    )(*args)


ROUTE_I1, ROUTE_I2, ROUTE_W1, ROUTE_W2, ROUTE_R1, ROUTE_R2 = range(6)


def _top2_gates(logits, n_experts):
    lane = lax.broadcasted_iota(jnp.int32, logits.shape, 1)
    lg = jnp.where(lane < n_experts, logits, -jnp.inf)
    m1 = jnp.max(lg, axis=-1, keepdims=True)
    i1 = jnp.min(jnp.where(lg == m1, lane, LANES), axis=-1, keepdims=True)
    lg2 = jnp.where(lane == i1, -jnp.inf, lg)
    m2 = jnp.max(lg2, axis=-1, keepdims=True)
    i2 = jnp.min(jnp.where(lg2 == m2, lane, LANES), axis=-1, keepdims=True)
    e2 = jnp.exp(m2 - m1)
    den = 1.0 + e2
    return lane, i1, i2, 1.0 / den, e2 / den


def _moe_route_kernel(x_ref, g_ref, wr_ref, tri_ref, route_ref, cnt_ref, carry, *, n_experts):
    @pl.when(pl.program_id(0) == 0)
    def _():
        carry[...] = jnp.zeros_like(carry)

    h = _rmsnorm(x_ref[...], g_ref[...])
    logits = jnp.dot(h, wr_ref[...], preferred_element_type=F32, precision=lax.Precision.HIGHEST)
    lane, i1, i2, w1, w2 = _top2_gates(logits, n_experts)
    member = jnp.where((lane == i1) | (lane == i2), 1.0, 0.0)
    before = _dot(tri_ref[...], member.astype(BF16)) + carry[0:1, :]
    r1 = jnp.sum(jnp.where(lane == i1, before, 0.0), axis=-1, keepdims=True)
    r2 = jnp.sum(jnp.where(lane == i2, before, 0.0), axis=-1, keepdims=True)
    rec = jnp.zeros(logits.shape, F32)
    for col, val in ((ROUTE_I1, i1.astype(F32)), (ROUTE_I2, i2.astype(F32)), (ROUTE_W1, w1), (ROUTE_W2, w2),
                     (ROUTE_R1, r1), (ROUTE_R2, r2)):
        rec = jnp.where(lane == col, val, rec)
    route_ref[...] = rec
    carry[...] = carry[...] + jnp.sum(member, axis=0, keepdims=True)
    cnt_ref[...] = carry[...]


def _moe_route(x, g, w_router, n_experts, tm=512):
    N, D = x.shape
    assert N % tm == 0
    tri = (jnp.arange(tm)[:, None] > jnp.arange(tm)[None, :]).astype(BF16)
    return pl.pallas_call(
        functools.partial(_moe_route_kernel, n_experts=n_experts),
        grid=(N // tm,),
        in_specs=[pl.BlockSpec((tm, D), lambda i: (i, 0)), pl.BlockSpec((1, D), lambda i: (0, 0)),
                  pl.BlockSpec(w_router.shape, lambda i: (0, 0)), pl.BlockSpec((tm, tm), lambda i: (0, 0))],
        out_specs=[pl.BlockSpec((tm, LANES), lambda i: (i, 0)), pl.BlockSpec((8, LANES), lambda i: (0, 0))],
        out_shape=[jax.ShapeDtypeStruct((N, LANES), F32), jax.ShapeDtypeStruct((8, LANES), F32)],
        scratch_shapes=[pltpu.VMEM((8, LANES), F32)],
        compiler_params=_cparams(("arbitrary",)),
        name="moe_route",
    )(x, g, w_router, tri)


def _row_gather(idx_of, src_hbm, dst_at, sem, rows):
    def copy(r):
        return pltpu.make_async_copy(src_hbm.at[pl.ds(idx_of(r), 1), :], dst_at(r), sem)

    def start():
        lax.fori_loop(0, rows, lambda r, c: (copy(r).start(), c)[1], 0, unroll=8)

    def wait():
        lax.fori_loop(0, rows, lambda r, c: (copy(r).wait(), c)[1], 0, unroll=8)

    return start, wait, copy


def _moe_experts_kernel(te_ref, nu_ref, tos_ref, x_hbm, g_ref, wg_ref, wu_ref, wd_ref, o_ref,
                        xbuf, h_scr, acc_scr, sems, *, tile, nf):
    i, f = pl.program_id(0), pl.program_id(1)
    used = i < nu_ref[0]
    slot = i % 2
    per_step = tile // nf

    def gather(t, sl):
        return _row_gather(lambda r: tos_ref[t * tile + r], x_hbm,
                           lambda r: xbuf.at[sl, pl.ds(r, 1), :], sems.at[sl], tile)

    def prefetch_share():
        row_copy = gather(i + 1, 1 - slot)[2]
        for j in range(per_step):
            row_copy(f * per_step + j).start()

    @pl.when((f == 0) & (i == 0))
    def _():
        gather(0, 0)[0]()

    @pl.when(f == 0)
    def _():
        gather(i, slot)[1]()

    @pl.when((f == 0) & used)
    def _():
        h_scr[...] = _rmsnorm(xbuf[slot], g_ref[...]).astype(BF16)
        acc_scr[...] = jnp.zeros_like(acc_scr)

    @pl.when(used)
    def _():
        prefetch_share()
        hb = h_scr[...]
        a = _dot(hb, wg_ref[0])
        u = _dot(hb, wu_ref[0])
        act = (a * jax.nn.sigmoid(a)) * u
        acc_scr[...] += _dot(act.astype(BF16), wd_ref[0])

    @pl.when(jnp.logical_not(used))
    def _():
        prefetch_share()

    @pl.when(f == nf - 1)
    def _():
        o_ref[...] = jnp.where(used, acc_scr[...], 0.0)

    @pl.when((f == nf - 1) & (i == pl.num_programs(0) - 1))
    def _():
        gather(i + 1, 1 - slot)[1]()


def _moe_experts(x, g, wg, wu, wd, tile_expert, n_used, token_of_slot, *, tile, tf=512):
    N, D = x.shape
    E, _, F = wg.shape
    n_slots = token_of_slot.shape[0] - tile
    assert n_slots % tile == 0 and F % tf == 0
    n_tiles, nf = n_slots // tile, F // tf
    assert tile % nf == 0
    fidx = lambda i, f, nu: jnp.where(i < nu[0], f, nf - 1)
    return pl.pallas_call(
        functools.partial(_moe_experts_kernel, tile=tile, nf=nf),
        grid_spec=pltpu.PrefetchScalarGridSpec(
            num_scalar_prefetch=3,
            grid=(n_tiles, nf),
            in_specs=[pl.BlockSpec(memory_space=pl.ANY),
                      pl.BlockSpec((1, D), lambda i, f, te, nu, tos: (0, 0)),
                      pl.BlockSpec((1, D, tf), lambda i, f, te, nu, tos: (te[i], 0, fidx(i, f, nu))),
                      pl.BlockSpec((1, D, tf), lambda i, f, te, nu, tos: (te[i], 0, fidx(i, f, nu))),
                      pl.BlockSpec((1, tf, D), lambda i, f, te, nu, tos: (te[i], fidx(i, f, nu), 0))],
            out_specs=pl.BlockSpec((tile, D), lambda i, f, te, nu, tos: (i, 0)),
            scratch_shapes=[pltpu.VMEM((2, tile, D), F32), pltpu.VMEM((tile, D), BF16),
                            pltpu.VMEM((tile, D), F32), pltpu.SemaphoreType.DMA((2,))],
        ),
        out_shape=jax.ShapeDtypeStruct((n_slots, D), F32),
        compiler_params=_cparams(("arbitrary", "arbitrary")),
        name="moe_experts",
    )(tile_expert, n_used, token_of_slot, x, g, wg, wu, wd)


def _moe_combine_kernel(slot_ref, x_ref, route_ref, gf_ref, ys_hbm, o_ref, ybuf, sems, *, tile):
    i = pl.program_id(0)
    slot = i % 2

    def gather(t, sl):
        return _row_gather(lambda j: slot_ref[t * 2 * tile + j], ys_hbm,
                           lambda j: ybuf.at[sl, pl.ds(j, 1), :], sems.at[sl], 2 * tile)

    @pl.when(i == 0)
    def _():
        gather(0, 0)[0]()

    gather(i, slot)[1]()

    @pl.when(i + 1 < pl.num_programs(0))
    def _():
        gather(i + 1, 1 - slot)[0]()

    route = route_ref[...]
    w1, w2 = route[:, ROUTE_W1:ROUTE_W1 + 1], route[:, ROUTE_W2:ROUTE_W2 + 1]
    y = w1 * ybuf[slot, 0:tile, :] + w2 * ybuf[slot, tile:2 * tile, :]
    o_ref[...] = _rmsnorm(x_ref[...] + y, gf_ref[...])


def _moe_combine(x, route, g_final, ys, slots, *, tile=256):
    N, D = x.shape
    assert N % tile == 0
    return pl.pallas_call(
        functools.partial(_moe_combine_kernel, tile=tile),
        grid_spec=pltpu.PrefetchScalarGridSpec(
            num_scalar_prefetch=1,
            grid=(N // tile,),
            in_specs=[pl.BlockSpec((tile, D), lambda i, s: (i, 0)),
                      pl.BlockSpec((tile, LANES), lambda i, s: (i, 0)),
                      pl.BlockSpec((1, D), lambda i, s: (0, 0)),
                      pl.BlockSpec(memory_space=pl.ANY)],
            out_specs=pl.BlockSpec((tile, D), lambda i, s: (i, 0)),
            scratch_shapes=[pltpu.VMEM((2, 2 * tile, D), F32), pltpu.SemaphoreType.DMA((2,))],
        ),
        out_shape=jax.ShapeDtypeStruct((N, D), F32),
        compiler_params=_cparams(("arbitrary",)),
        name="moe_combine",
    )(slots, x, route, g_final, ys)


def _moe_prompt(x, g, w_router, wg, wu, wd, g_final, *, tile=896, ctile=256):
    N, D = x.shape
    E = wg.shape[0]
    route, counts = _moe_route(x, g, w_router, E)
    cnt = counts[0, :E].astype(jnp.int32)
    padded = (cnt + tile - 1) // tile * tile
    ends = jnp.cumsum(padded)
    off = ends - padded
    i1, i2 = route[:, ROUTE_I1].astype(jnp.int32), route[:, ROUTE_I2].astype(jnp.int32)
    slot1 = off[i1] + route[:, ROUTE_R1].astype(jnp.int32)
    slot2 = off[i2] + route[:, ROUTE_R2].astype(jnp.int32)
    n_slots = (TOP_K * N + E * (tile - 1)) // tile * tile
    rows = jnp.arange(N, dtype=jnp.int32)
    token_of_slot = jnp.zeros((n_slots + tile,), jnp.int32).at[jnp.concatenate([slot1, slot2])].set(
        jnp.concatenate([rows, rows]), unique_indices=True)
    n_used = ends[-1:] // tile
    tile_start = jnp.minimum(jnp.arange(n_slots // tile, dtype=jnp.int32), n_used[0] - 1) * tile
    tile_expert = jnp.sum(tile_start[:, None] >= ends[None, :], axis=1).astype(jnp.int32)
    ys = _moe_experts(x, g, wg, wu, wd, tile_expert, n_used.astype(jnp.int32), token_of_slot, tile=tile)
    slots = jnp.stack([slot1.reshape(-1, ctile), slot2.reshape(-1, ctile)], axis=1).reshape(-1)
    return _moe_combine(x, route, g_final, ys, slots, tile=ctile)


def _nsa_proj_kernel(x_ref, g_ref, wq_ref, wkv_ref, wgt_ref, wck_ref, wcv_ref,
                     q_ref, kvt_ref, ks_ref, vs_ref, kw_ref, vw_ref, gt_ref, kc_ref, vc_ref):
    hb = _rmsnorm(x_ref[...], g_ref[...]).astype(BF16)
    q_ref[...] = _dot(hb, wq_ref[...]).astype(BF16)
    kv = _dot(hb, wkv_ref[...])
    kvt_ref[0] = kv.T
    ks_ref[...] = kv[:, 2 * KV_W:3 * KV_W].astype(BF16)
    vs_ref[...] = kv[:, 3 * KV_W:4 * KV_W].astype(BF16)
    kw_ref[...] = kv[:, 4 * KV_W:5 * KV_W].astype(BF16)
    vw_ref[...] = kv[:, 5 * KV_W:].astype(BF16)
    gt_ref[...] = jax.nn.sigmoid(_dot(hb, wgt_ref[...]))
    tm = kv.shape[0]
    kc_ref[...] = (kv[:, :KV_W] * wck_ref[...]).reshape(tm // CMP_BLOCK, CMP_BLOCK, KV_W).sum(axis=1)
    vc_ref[...] = (kv[:, KV_W:2 * KV_W] * wcv_ref[...]).reshape(tm // CMP_BLOCK, CMP_BLOCK, KV_W).sum(axis=1)


def _nsa_proj(x, g, wq, wkv, wgt, wck, wcv, *, tm, seq):
    N, D = x.shape
    assert N % tm == 0 and tm % (8 * CMP_BLOCK) == 0 and seq % tm == 0 and N % seq == 0
    per_seq = seq // tm
    row = lambda w: pl.BlockSpec((tm, w), lambda i: (i, 0))
    full = lambda a: pl.BlockSpec(a.shape, lambda i: (0, 0))
    out_shape = [jax.ShapeDtypeStruct((N, N_HEADS * HEAD_DIM), BF16),
                 jax.ShapeDtypeStruct((N // seq, 6 * KV_W, seq), F32)]
    out_shape += [jax.ShapeDtypeStruct((N, KV_W), BF16)] * 4
    out_shape += [jax.ShapeDtypeStruct((N, LANES), F32)]
    out_shape += [jax.ShapeDtypeStruct((N // CMP_BLOCK, KV_W), F32)] * 2
    out_specs = [row(N_HEADS * HEAD_DIM),
                 pl.BlockSpec((1, 6 * KV_W, tm), lambda i: (i // per_seq, 0, i % per_seq))]
    out_specs += [row(KV_W)] * 4 + [row(LANES)]
    out_specs += [pl.BlockSpec((tm // CMP_BLOCK, KV_W), lambda i: (i, 0))] * 2
    return pl.pallas_call(
        _nsa_proj_kernel,
        grid=(N // tm,),
        in_specs=[row(D), full(g), full(wq), full(wkv), full(wgt), full(wck), full(wcv)],
        out_specs=out_specs,
        out_shape=out_shape,
        compiler_params=_cparams(("parallel",)),
        name="nsa_proj_prompt",
    )(x, g, wq, wkv, wgt, wck, wcv)


def _nsa_proj_sample_kernel(x_ref, g_ref, w_ref, z_ref):
    z_ref[...] = _dot3(_rmsnorm(x_ref[...], g_ref[...]), w_ref[...])


def _nsa_proj_sample(x, g, w_in, tn=384):
    B, D = x.shape
    cols = w_in.shape[1]
    assert cols % tn == 0
    return pl.pallas_call(
        _nsa_proj_sample_kernel,
        grid=(cols // tn,),
        in_specs=[pl.BlockSpec((B, D), lambda j: (0, 0)), pl.BlockSpec((1, D), lambda j: (0, 0)),
                  pl.BlockSpec((D, tn), lambda j: (0, j))],
        out_specs=pl.BlockSpec((B, tn), lambda j: (0, j)),
        out_shape=jax.ShapeDtypeStruct((B, cols), F32),
        compiler_params=_cparams(("parallel",)),
        name="nsa_proj_sample",
    )(x, g, w_in)


def _nsa_prompt_kernel(q_ref, gt_ref, kc_ref, vc_ref, ks_ref, vs_ref, kw_ref, vw_ref, x_ref, wo_ref,
                       o_ref, o_scr, sel_scr, *, tq, kchunk, seq):
    t0 = pl.program_id(1) * tq
    rows = GQA * tq
    scale = HEAD_DIM ** -0.5
    assert scale == 0.125
    n_blocks = seq // SLC_BLOCK
    top_k = min(SLC_TOPK, n_blocks)

    q = q_ref[0]
    gates = gt_ref[0]
    qpos_t = t0 + lax.broadcasted_iota(jnp.int32, (tq, 1), 0)
    qpos = jnp.concatenate([qpos_t] * GQA, axis=0)
    tq_f = qpos.astype(F32)

    lane = lax.broadcasted_iota(jnp.int32, (tq, LANES), 1)
    jblk = lane >> 1
    real = ((lane & 1) == 0) & (jblk < n_blocks)
    blk = qpos_t // SLC_BLOCK
    valid = real & (jblk <= blk)
    forced = (jblk == 0) | (jblk == blk) | (jblk == blk - 1)
    valid_f = jnp.where(valid, 1.0, 0.0)
    valid_b = valid_f.astype(BF16)
    bonus = jnp.where(forced, FORCE_BONUS, 0.0)
    floor = jnp.where(valid, 0.0, jnp.where(real, -1.0, -2.0))
    c_lane = lax.broadcasted_iota(jnp.int32, (1, LANES), 1)
    c_mid = c_lane.astype(F32) * CMP_BLOCK + (CMP_BLOCK - 1) / 2
    c_end = (c_lane + 1) * CMP_BLOCK - 1

    n_chunks = (t0 + tq + kchunk - 1) // kchunk
    wstart = pl.multiple_of(jnp.maximum(t0 - WINDOW, 0), tq)
    wlen = WINDOW + tq

    need_rank = (t0 + tq - 1) // SLC_BLOCK >= top_k
    wpos = wstart + lax.broadcasted_iota(jnp.int32, (1, wlen), 1)
    w_dist = jnp.where((wpos <= qpos_t) & (qpos_t - wpos < WINDOW), (wpos - qpos_t).astype(F32), NEG)

    def with_ones(v):
        return jnp.concatenate([v, jnp.ones_like(v)], axis=1)

    def biased(scores, dist, kv):
        return jnp.concatenate(
            [scores[g * tq:(g + 1) * tq] + _alibi_slope(kv * GQA + g) * dist for g in range(GQA)], axis=0)

    for kv in range(N_KV_HEADS):
        hs = slice(kv * HEAD_DIM, (kv + 1) * HEAD_DIM)
        qs = jnp.concatenate(
            [q[:, (kv * GQA + g) * HEAD_DIM:(kv * GQA + g + 1) * HEAD_DIM] for g in range(GQA)], axis=0) * scale
        slope = _head_slopes(rows, tq, kv * GQA)

        s_c = _dot_nt(qs, kc_ref[0][:, hs]) - slope * (tq_f - c_mid)
        p_c = _masked_softmax(s_c, c_end <= qpos)
        o_c = _dot(p_c.astype(BF16), vc_ref[0][:, hs])

        imp = p_c[0:tq]
        for g in range(1, GQA):
            imp = imp + p_c[g * tq:(g + 1) * tq]
        sel_scr[...] = valid_b

        @pl.when(need_rank)
        def _():
            pair = imp + pltpu.roll(imp, LANES - 1, 1)
            score = valid_f * (pair + bonus) + floor
            score_t = score.T[:2 * n_blocks]
            row_id = lax.broadcasted_iota(jnp.int32, score_t.shape, 0)
            rank_t = jnp.zeros(score_t.shape, jnp.int32)
            for k in range(n_blocks):
                other = score_t[2 * k:2 * k + 1, :]
                beats = (other > score_t) | ((other == score_t) & (row_id > 2 * k))
                rank_t = rank_t + beats.astype(jnp.int32)
            top_t = jnp.where(rank_t < top_k, 1.0, 0.0)
            top = jnp.concatenate([top_t, jnp.zeros((LANES - 2 * n_blocks, tq), F32)], axis=0).T
            sel_scr[...] = (top * valid_f).astype(BF16)

        sel = sel_scr[...]

        def chunk(c, carry):
            m, acc = carry
            k0 = pl.multiple_of(c * kchunk, kchunk)
            kb = ks_ref[0, pl.ds(k0, kchunk), hs]
            vb = with_ones(vs_ref[0, pl.ds(k0, kchunk), hs])
            pos = k0 + lax.broadcasted_iota(jnp.int32, (1, kchunk), 1)
            expand = (lax.broadcasted_iota(jnp.int32, (LANES, kchunk), 0)
                      == 2 * ((k0 + lax.broadcasted_iota(jnp.int32, (LANES, kchunk), 1)) // SLC_BLOCK))
            picked = _dot(sel, jnp.where(expand, 1.0, 0.0).astype(BF16))
            ok = (picked > 0.5) & (pos <= qpos_t)
            s = biased(_dot_nt(qs, kb), jnp.where(ok, (pos - qpos_t).astype(F32), NEG), kv)
            m_new = jnp.maximum(m, jnp.max(s, axis=-1, keepdims=True))
            p = jnp.exp(s - m_new)
            acc = jnp.exp(m - m_new) * acc + _dot(p.astype(BF16), vb)
            return m_new, acc

        init = (jnp.full((rows, 1), NEG, F32), jnp.zeros((rows, 2 * HEAD_DIM), F32))
        _, acc_s = lax.fori_loop(0, n_chunks, chunk, init)
        o_s = acc_s[:, :HEAD_DIM] / acc_s[:, HEAD_DIM:HEAD_DIM + 1]

        kb = kw_ref[0, pl.ds(wstart, wlen), hs]
        vb = with_ones(vw_ref[0, pl.ds(wstart, wlen), hs])
        s_w = biased(_dot_nt(qs, kb), w_dist, kv)
        p_w = jnp.exp(s_w - jnp.max(s_w, axis=-1, keepdims=True))
        pv = _dot(p_w.astype(BF16), vb)
        o_w = pv[:, :HEAD_DIM] / pv[:, HEAD_DIM:HEAD_DIM + 1]

        def gate_col(n):
            base = n * N_HEADS + kv * GQA
            return jnp.concatenate([gates[:, base + g:base + g + 1] for g in range(GQA)], axis=0)

        o = gate_col(0) * o_c + gate_col(1) * o_s + gate_col(2) * o_w
        for g in range(GQA):
            h = kv * GQA + g
            o_scr[:, h * HEAD_DIM:(h + 1) * HEAD_DIM] = o[g * tq:(g + 1) * tq]

    o_ref[0] = x_ref[0] + _dot(o_scr[...].astype(BF16), wo_ref[...])


def _nsa_prompt(q, gates, kc, vc, ks, vs, kw, vw, x, w_o, *, tq=128, kchunk=512):
    B, L, D = x.shape
    assert L % kchunk == 0 and kchunk % tq == 0 and L >= WINDOW + tq and L % SLC_BLOCK == 0
    assert 2 * (L // SLC_BLOCK) <= LANES and L // CMP_BLOCK <= LANES and kc.shape[1] == LANES
    tile = lambda w: pl.BlockSpec((1, tq, w), lambda b, t: (b, t, 0))
    whole = lambda a: pl.BlockSpec((1,) + a.shape[1:], lambda b, t: (b, 0, 0))
    return pl.pallas_call(
        functools.partial(_nsa_prompt_kernel, tq=tq, kchunk=kchunk, seq=L),
        grid=(B, L // tq),
        in_specs=[tile(N_HEADS * HEAD_DIM), tile(LANES), whole(kc), whole(vc), whole(ks), whole(vs),
                  whole(kw), whole(vw), tile(D), pl.BlockSpec(w_o.shape, lambda b, t: (0, 0))],
        out_specs=tile(D),
        out_shape=jax.ShapeDtypeStruct((B, L, D), F32),
        scratch_shapes=[pltpu.VMEM((tq, N_HEADS * HEAD_DIM), F32), pltpu.VMEM((tq, LANES), BF16)],
        compiler_params=_cparams(("parallel", "arbitrary")),
        name="nsa_attn_prompt",
    )(q, gates, kc, vc, ks, vs, kw, vw, x, w_o)


CMP_GROUP = 32
MXU_DEPTH = 256


def _cmp_sample_kernel(pt_ref, cache_ref, w_ref, s_ref, o_ref, buf, sems, *, rows):
    b, g = pl.program_id(0), pl.program_id(1)
    n_groups = pl.num_programs(1)
    step = b * n_groups + g
    slot = step % 2

    def page_copy(bb, gg, sl, i):
        page = pt_ref[bb, gg * CMP_GROUP + i]
        return pltpu.make_async_copy(cache_ref.at[page, pl.ds(0, rows), :], buf.at[sl, i], sems.at[sl])

    def start_group(bb, gg, sl):
        for i in range(CMP_GROUP):
            page_copy(bb, gg, sl, i).start()

    @pl.when(step == 0)
    def _():
        start_group(b, g, slot)

    @pl.when(step + 1 < pl.num_programs(0) * n_groups)
    def _():
        nxt = step + 1
        start_group(nxt // n_groups, nxt % n_groups, 1 - slot)

    for i in range(CMP_GROUP):
        page_copy(b, g, slot, i).wait()

    w = w_ref[...]
    acc = jnp.zeros((rows, LANES), F32)
    for j in range(CMP_GROUP // 2):
        p = jnp.concatenate([buf[slot, 2 * j] * w, buf[slot, 2 * j + 1] * w], axis=1)
        hi, lo = _split(p)
        r = _dot(jnp.concatenate([hi, lo], axis=0), s_ref[j])
        acc = acc + (r[:rows] + r[rows:])
    o_ref[0] = acc


def _cmp_sample(page_table, cache_t, w_rows):
    B, n_pages = page_table.shape
    _, _, page = cache_t.shape
    rows = w_rows.shape[0]
    per_page = page // CMP_BLOCK
    assert n_pages % CMP_GROUP == 0 and CMP_GROUP * per_page == LANES and 2 * page == MXU_DEPTH
    k = jnp.arange(2 * page)
    token = (2 * jnp.arange(CMP_GROUP // 2)[:, None] + k[None, :] // page) * per_page + (k[None, :] % page) // CMP_BLOCK
    block_sum = (token[:, :, None] == jnp.arange(LANES)[None, None, :]).astype(BF16)
    return pl.pallas_call(
        functools.partial(_cmp_sample_kernel, rows=rows),
        grid_spec=pltpu.PrefetchScalarGridSpec(
            num_scalar_prefetch=1,
            grid=(B, n_pages // CMP_GROUP),
            in_specs=[pl.BlockSpec(memory_space=pl.ANY),
                      pl.BlockSpec(w_rows.shape, lambda b, g, pt: (0, 0)),
                      pl.BlockSpec(block_sum.shape, lambda b, g, pt: (0, 0, 0))],
            out_specs=pl.BlockSpec((1, rows, LANES), lambda b, g, pt: (b, 0, g)),
            scratch_shapes=[pltpu.VMEM((2, CMP_GROUP, rows, page), F32), pltpu.SemaphoreType.DMA((2,))],
        ),
        out_shape=jax.ShapeDtypeStruct((B, rows, n_pages * per_page), F32),
        compiler_params=_cparams(("arbitrary", "arbitrary")),
        name="cmp_sample",
    )(page_table, cache_t, w_rows, block_sum)


def _select_sample_kernel(q_ref, kcv_ref, oc_ref, idx_ref, *, past):
    scale = HEAD_DIM ** -0.5
    q = q_ref[0]
    nc = kcv_ref.shape[2]
    n_past_blocks = past // SLC_BLOCK
    per_block = SLC_BLOCK // CMP_BLOCK
    row = lax.broadcasted_iota(jnp.int32, (N_HEADS, 1), 0)
    slope = _head_slopes(N_HEADS, 1, 0)
    tok = lax.broadcasted_iota(jnp.int32, (1, nc), 1)
    c_mid = tok.astype(F32) * CMP_BLOCK + (CMP_BLOCK - 1) / 2
    c_ok = (tok + 1) * CMP_BLOCK - 1 <= past
    bias = slope * (float(past) - c_mid)

    s_c = jnp.zeros((N_HEADS, nc), F32)
    for kv in range(N_KV_HEADS):
        s_kv = _dot3(q, kcv_ref[0, kv * HEAD_DIM:(kv + 1) * HEAD_DIM, :])
        s_c = jnp.where(row // GQA == kv, s_kv, s_c)
    p_c = _masked_softmax(s_c * scale - bias, c_ok)
    o_c = jnp.zeros((N_HEADS, HEAD_DIM), F32)
    for kv in range(N_KV_HEADS):
        o_kv = _dot3(p_c, kcv_ref[0, KV_W + kv * HEAD_DIM:KV_W + (kv + 1) * HEAD_DIM, :], nt=True)
        o_c = jnp.where(row // GQA == kv, o_kv, o_c)
    oc_ref[0] = o_c

    imp = p_c
    shift = 1
    while shift < GQA:
        imp = imp + pltpu.roll(imp, shift, 0)
        shift *= 2
    assert per_block == 2
    pair = imp + pltpu.roll(imp, nc - 1, 1)
    jblk = tok // per_block
    real = tok % per_block == 0
    forced = (jblk == 0) | (jblk == n_past_blocks - 1)
    score = jnp.where(real, pair + jnp.where(forced, FORCE_BONUS, 0.0), -2.0)
    own = jnp.float32(FORCE_BONUS)
    rank = (own > score).astype(jnp.int32)
    for k in range(n_past_blocks):
        col = score[:, per_block * k:per_block * k + 1]
        beats = (col > score) | ((col == score) & (jblk > k))
        rank = rank + beats.astype(jnp.int32)
    own_rank = jnp.sum(jnp.where(real & (score >= own), 1, 0), axis=-1, keepdims=True)
    out_lane = lax.broadcasted_iota(jnp.int32, (N_HEADS, LANES), 1)
    idx = jnp.zeros((N_HEADS, LANES), jnp.int32)
    for r in range(SLC_TOPK):
        hit = jnp.sum(jnp.where(real & (rank == r), jblk, 0), axis=-1, keepdims=True)
        hit = hit + jnp.where(own_rank == r, n_past_blocks, 0)
        idx = jnp.where(out_lane == r, hit, idx)
    idx_ref[0] = idx


def _select_sample(q16, kcv, past):
    B = q16.shape[0]
    nc = kcv.shape[2]
    assert past % SLC_BLOCK == 0 and nc == past // CMP_BLOCK and nc % LANES == 0
    assert past // SLC_BLOCK + 1 > SLC_TOPK
    return pl.pallas_call(
        functools.partial(_select_sample_kernel, past=past),
        grid=(B,),
        in_specs=[pl.BlockSpec((1,) + q16.shape[1:], lambda b: (b, 0, 0)),
                  pl.BlockSpec((1,) + kcv.shape[1:], lambda b: (b, 0, 0))],
        out_specs=[pl.BlockSpec((1, N_HEADS, HEAD_DIM), lambda b: (b, 0, 0)),
                   pl.BlockSpec((1, N_HEADS, LANES), lambda b: (b, 0, 0))],
        out_shape=[jax.ShapeDtypeStruct((B, N_HEADS, HEAD_DIM), F32),
                   jax.ShapeDtypeStruct((B, N_HEADS, LANES), jnp.int32)],
        compiler_params=_cparams(("parallel",)),
        name="select_sample",
    )(q16, kcv)


def _attend_sample_kernel(pt_ref, idx_ref, q_ref, oc_ref, gt_ref, new_ref, win_ref, cache_ref,
                          o_ref, kbuf, vbuf, sems, *, past):
    b = pl.program_id(0)
    scale = HEAD_DIM ** -0.5
    n_past_blocks = past // SLC_BLOCK
    page_rows = cache_ref.shape[2]
    blocks_per_page = page_rows // SLC_BLOCK
    nkeys = SLC_TOPK * page_rows

    def block_copies(kv, n):
        blk = idx_ref[(b * N_KV_HEADS + kv) * SLC_TOPK + n]
        in_past = blk < n_past_blocks
        page = pt_ref[b, jnp.minimum(blk, n_past_blocks - 1) // blocks_per_page]
        dst = pl.ds(n * page_rows, page_rows)
        ck = pltpu.make_async_copy(cache_ref.at[page, pl.ds((2 * N_KV_HEADS + kv) * HEAD_DIM, HEAD_DIM), :],
                                   kbuf.at[kv, :, dst], sems.at[0, kv, n])
        cv = pltpu.make_async_copy(cache_ref.at[page, pl.ds((3 * N_KV_HEADS + kv) * HEAD_DIM, HEAD_DIM), :],
                                   vbuf.at[kv, :, dst], sems.at[1, kv, n])
        return blk, in_past, ck, cv

    for kv in range(N_KV_HEADS):
        for n in range(SLC_TOPK):
            _, in_past, ck, cv = block_copies(kv, n)

            @pl.when(in_past)
            def _():
                ck.start()
                cv.start()

            @pl.when(jnp.logical_not(in_past))
            def _():
                kbuf[kv, :, n * page_rows:(n + 1) * page_rows] = jnp.zeros((HEAD_DIM, page_rows), F32)
                vbuf[kv, :, n * page_rows:(n + 1) * page_rows] = jnp.zeros((HEAD_DIM, page_rows), F32)

    q = q_ref[0]
    row = lax.broadcasted_iota(jnp.int32, (N_HEADS, 1), 0)
    slope = _head_slopes(N_HEADS, 1, 0)
    key_lane = lax.broadcasted_iota(jnp.int32, (1, nkeys), 1)

    wb = win_ref.shape[2]
    w_lane = lax.broadcasted_iota(jnp.int32, (1, wb), 1)
    wpos = past - wb + w_lane
    w_ok = (past - wpos < WINDOW) & (wpos >= 0)
    w_bias = slope * (past - wpos).astype(F32)
    s_w = jnp.zeros((N_HEADS, wb), F32)
    s_n = jnp.zeros((N_HEADS, 1), F32)
    for kv in range(N_KV_HEADS):
        mine = row // GQA == kv
        s_kv = _dot3(q, win_ref[0, kv * HEAD_DIM:(kv + 1) * HEAD_DIM, :])
        s_w = jnp.where(mine, s_kv, s_w)
        s_n = jnp.where(mine, jnp.sum(q * new_ref[0, 4, kv:kv + 1, :], axis=-1, keepdims=True), s_n)
    s_w = jnp.where(w_ok, s_w * scale - w_bias, NEG)
    s_n = s_n * scale
    m_w = jnp.maximum(jnp.max(s_w, axis=-1, keepdims=True), s_n)
    p_w = jnp.where(w_ok, jnp.exp(s_w - m_w), 0.0)
    p_n = jnp.exp(s_n - m_w)
    l_w = jnp.maximum(jnp.sum(p_w, axis=-1, keepdims=True) + p_n, 1e-30)
    p_w = p_w / l_w
    p_n = p_n / l_w
    o_w = jnp.zeros((N_HEADS, HEAD_DIM), F32)
    for kv in range(N_KV_HEADS):
        o_kv = (_dot3(p_w, win_ref[0, KV_W + kv * HEAD_DIM:KV_W + (kv + 1) * HEAD_DIM, :], nt=True)
                + p_n * new_ref[0, 5, kv:kv + 1, :])
        o_w = jnp.where(row // GQA == kv, o_kv, o_w)

    for kv in range(N_KV_HEADS):
        for n in range(SLC_TOPK):
            _, in_past, ck, cv = block_copies(kv, n)

            @pl.when(in_past)
            def _():
                ck.wait()
                cv.wait()

    s_s = jnp.zeros((N_HEADS, nkeys), F32)
    pos = jnp.zeros((N_HEADS, nkeys), jnp.int32)
    live = jnp.zeros((N_HEADS, nkeys), jnp.int32)
    own = jnp.zeros((N_HEADS, 1), jnp.int32)
    s_n = jnp.zeros((N_HEADS, 1), F32)
    in_slab = key_lane % page_rows
    for kv in range(N_KV_HEADS):
        mine = row // GQA == kv
        s_kv = _dot3(q, kbuf[kv])
        s_s = jnp.where(mine, s_kv, s_s)
        s_n = jnp.where(mine, jnp.sum(q * new_ref[0, 2, kv:kv + 1, :], axis=-1, keepdims=True), s_n)
        base_kv = jnp.zeros((1, nkeys), jnp.int32)
        half_kv = jnp.zeros((1, nkeys), jnp.int32)
        own_kv = jnp.int32(0)
        for n in range(SLC_TOPK):
            blk = idx_ref[(b * N_KV_HEADS + kv) * SLC_TOPK + n]
            here = key_lane // page_rows == n
            base_kv = jnp.where(here, (blk // blocks_per_page) * page_rows, base_kv)
            half_kv = jnp.where(here, blk % blocks_per_page, half_kv)
            own_kv = own_kv + (blk >= n_past_blocks).astype(jnp.int32)
        pos = jnp.where(mine, base_kv + in_slab, pos)
        live = jnp.where(mine, (in_slab // SLC_BLOCK == half_kv).astype(jnp.int32), live)
        own = jnp.where(mine, own_kv, own)
    s_ok = (live > 0) & (pos < past)
    n_ok = own > 0
    s_s = jnp.where(s_ok, s_s * scale - slope * (past - pos).astype(F32), NEG)
    s_n = jnp.where(n_ok, s_n * scale, NEG)
    m_s = jnp.maximum(jnp.max(s_s, axis=-1, keepdims=True), s_n)
    p_s = jnp.where(s_ok, jnp.exp(s_s - m_s), 0.0)
    p_n = jnp.where(n_ok, jnp.exp(s_n - m_s), 0.0)
    l_s = jnp.maximum(jnp.sum(p_s, axis=-1, keepdims=True) + p_n, 1e-30)
    p_s = p_s / l_s
    p_n = p_n / l_s
    o_s = jnp.zeros((N_HEADS, HEAD_DIM), F32)
    for kv in range(N_KV_HEADS):
        o_kv = _dot3(p_s, vbuf[kv], nt=True) + p_n * new_ref[0, 3, kv:kv + 1, :]
        o_s = jnp.where(row // GQA == kv, o_kv, o_s)

    gt = jax.nn.sigmoid(gt_ref[0])
    o_ref[0] = gt[:, 0:1] * oc_ref[0] + gt[:, 1:2] * o_s + gt[:, 2:3] * o_w


def _attend_sample(page_table, idx, q16, o_c, gates, kv_new, win_t, cache_t, past):
    B = q16.shape[0]
    page_rows = cache_t.shape[2]
    assert win_t.shape[2] <= WINDOW and page_rows % SLC_BLOCK == 0
    blk3 = lambda a: pl.BlockSpec((1,) + a.shape[1:], lambda b, pt, ix: (b,) + (0,) * (a.ndim - 1))
    return pl.pallas_call(
        functools.partial(_attend_sample_kernel, past=past),
        grid_spec=pltpu.PrefetchScalarGridSpec(
            num_scalar_prefetch=2,
            grid=(B,),
            in_specs=[blk3(q16), blk3(o_c), blk3(gates), blk3(kv_new), blk3(win_t),
                      pl.BlockSpec(memory_space=pl.ANY)],
            out_specs=pl.BlockSpec((1, N_HEADS, HEAD_DIM), lambda b, pt, ix: (b, 0, 0)),
            scratch_shapes=[pltpu.VMEM((N_KV_HEADS, HEAD_DIM, SLC_TOPK * page_rows), F32),
                            pltpu.VMEM((N_KV_HEADS, HEAD_DIM, SLC_TOPK * page_rows), F32),
                            pltpu.SemaphoreType.DMA((2, N_KV_HEADS, SLC_TOPK))],
        ),
        out_shape=jax.ShapeDtypeStruct((B, N_HEADS, HEAD_DIM), F32),
        compiler_params=_cparams(("arbitrary",)),
        name="attend_sample",
    )(page_table, idx, q16, o_c, gates, kv_new, win_t, cache_t)


def _linear_res_kernel(x_ref, a_ref, w_ref, o_ref):
    o_ref[...] = x_ref[...] + _dot3(a_ref[...], w_ref[...])


def _linear_res(x, a, w):
    return pl.pallas_call(
        _linear_res_kernel,
        out_shape=jax.ShapeDtypeStruct(x.shape, F32),
        compiler_params=pltpu.CompilerParams(vmem_limit_bytes=VMEM_LIMIT),
        name="out_proj_sample",
    )(x, a, w)


def _expand_cmp_weights(w_cmp, rows):
    w = jnp.repeat(w_cmp, HEAD_DIM, axis=1)
    return jnp.tile(w, (rows // CMP_BLOCK, 1))


def _row(v):
    return v.reshape(1, -1)


def _prompt_mixers(x, norm_mix, w_pool, pool_scale, w_in, w_cmp_k, w_cmp_v, w_out, norm_ffn0, wfg, wfu, wfd):
    B, L, D = x.shape
    q_w = N_HEADS * HEAD_DIM
    x, h_last = _pool_prompt(x, _row(norm_mix[0]), w_pool.astype(BF16), _row(pool_scale))
    new_pool = h_last[None, :, POOL_HALO - (max(POOL_WINDOWS) - 1):, :]
    x = _ffn(x.reshape(B * L, D), _row(norm_ffn0), wfg.astype(BF16), wfu.astype(BF16), wfd.astype(BF16), tm=1024)

    tm = 512
    wq = w_in[:, :q_w].astype(BF16)
    wkv = w_in[:, q_w:q_w + 6 * KV_W].astype(BF16)
    wgt = jnp.pad(w_in[:, q_w + 6 * KV_W:], ((0, 0), (0, LANES - N_BRANCH * N_HEADS))).astype(BF16)
    q, kv_t, ks, vs, kw, vw, gates, kc, vc = _nsa_proj(
        x, _row(norm_mix[1]), wq, wkv, wgt, _expand_cmp_weights(w_cmp_k, tm), _expand_cmp_weights(w_cmp_v, tm),
        tm=tm, seq=L)
    kv_t = kv_t.reshape(B, 6, N_KV_HEADS, HEAD_DIM, L)
    new_kv = kv_t[:, :4].transpose(0, 4, 1, 2, 3)[None]
    new_win = kv_t[:, 4:, :, :, L - min(WINDOW, L):].transpose(0, 4, 1, 2, 3)[None]
    nc = L // CMP_BLOCK
    pad_c = lambda a: jnp.pad(a.reshape(B, nc, KV_W), ((0, 0), (0, LANES - nc), (0, 0))).astype(BF16)
    per_seq = lambda a: a.reshape(B, L, a.shape[-1])
    x = _nsa_prompt(per_seq(q), per_seq(gates), pad_c(kc), pad_c(vc), per_seq(ks), per_seq(vs),
                    per_seq(kw), per_seq(vw), per_seq(x), w_out.astype(BF16))
    return x.reshape(B * L, D), new_pool, new_kv, new_win


def _sample_mixers(x, state_pool, cache, state_win, page_table, norm_mix, w_pool, pool_scale, w_in, w_cmp_k,
                   w_cmp_v, w_out, norm_ffn0, wfg, wfu, wfd):
    SB, D = x.shape
    n_phys, page = cache.shape[:2]
    past = page_table.shape[1] * page
    q_w = N_HEADS * HEAD_DIM
    x, h = _pool_sample(x, state_pool.transpose(1, 0, 2), _row(norm_mix[0]), w_pool, _row(pool_scale))
    new_pool = jnp.concatenate([state_pool[:, 1:], h[:, None]], axis=1)[None]
    x = _ffn(x, _row(norm_ffn0), wfg, wfu, wfd, tm=SB)

    cols = w_in.shape[1]
    z = _nsa_proj_sample(x, _row(norm_mix[1]), jnp.pad(w_in, ((0, 0), (0, -cols % LANES))))
    kv_new = z[:, q_w:q_w + 6 * KV_W].reshape(SB, 6, N_KV_HEADS, HEAD_DIM)
    new_kv = kv_new[None, :, None, :4]
    keep = min(WINDOW, state_win.shape[1] + 1)
    new_win = jnp.concatenate([state_win, kv_new[:, None, 4:]], axis=1)[None, :, -keep:]

    cache_t = cache.transpose(0, 2, 3, 4, 1).reshape(n_phys, 4 * KV_W, page)
    win_t = state_win.transpose(0, 2, 3, 4, 1).reshape(SB, 2 * KV_W, state_win.shape[1])
    w_rows = jnp.concatenate([_expand_cmp_weights(w_cmp_k, page).T, _expand_cmp_weights(w_cmp_v, page).T], axis=0)
    kcv = _cmp_sample(page_table, cache_t, w_rows)
    q16 = z[:, :q_w].reshape(SB, N_HEADS, HEAD_DIM)
    o_c, idx = _select_sample(q16, kcv, past)
    idx = idx[:, GQA - 1::GQA, :SLC_TOPK].reshape(-1)
    gate_logits = z[:, q_w + 6 * KV_W:cols].reshape(SB, N_BRANCH, N_HEADS).transpose(0, 2, 1)
    gate_logits = jnp.pad(gate_logits, ((0, 0), (0, 0), (0, LANES - N_BRANCH)))
    o = _attend_sample(page_table, idx, q16, o_c, gate_logits, kv_new, win_t, cache_t, past)
    return _linear_res(x, o.reshape(SB, q_w), w_out), new_pool, new_kv, new_win


def kernel(x_prompt, x_sample, state_pool, cache_kv, state_win, page_table, norm_mix, w_pool, pool_scale,
           w_nsa_in, w_cmp_k, w_cmp_v, w_nsa_out, norm_ffn, w_ffn_gate, w_ffn_up, w_ffn_down, w_router,
           w_moe_gate, w_moe_up, w_moe_down, norm_final):
    assert x_sample.shape[1] == 1 and norm_mix.shape[0] == 2 and x_prompt.shape[-1] == N_HEADS * HEAD_DIM
    w_rt = jnp.pad(w_router[0], ((0, 0), (0, LANES - w_router.shape[-1])))
    moe_w = (w_moe_gate[0].astype(BF16), w_moe_up[0].astype(BF16), w_moe_down[0].astype(BF16))

    xp, new_pool_prompt, new_kv_prompt, new_win_prompt = _prompt_mixers(
        x_prompt, norm_mix, w_pool[0], pool_scale[0], w_nsa_in[0], w_cmp_k[0], w_cmp_v[0], w_nsa_out[0],
        norm_ffn[0], w_ffn_gate, w_ffn_up, w_ffn_down)
    xs, new_pool_sample, new_kv_sample, new_win_sample = _sample_mixers(
        x_sample[:, 0], state_pool[0], cache_kv[0], state_win[0], page_table, norm_mix, w_pool[0], pool_scale[0],
        w_nsa_in[0], w_cmp_k[0], w_cmp_v[0], w_nsa_out[0], norm_ffn[0], w_ffn_gate, w_ffn_up, w_ffn_down)

    yp = _moe_prompt(xp, _row(norm_ffn[1]), w_rt, *moe_w, _row(norm_final))
    ys = _ffn(xs, _row(norm_ffn[1]), *moe_w, w_rt, _row(norm_final), tm=xs.shape[0])
    return (yp.reshape(x_prompt.shape), ys.reshape(x_sample.shape), new_pool_prompt, new_pool_sample,
            new_kv_prompt, new_kv_sample, new_win_prompt, new_win_sample)
```

```python
import functools

import jax
import jax.numpy as jnp
from jax import lax
from jax.experimental import pallas as pl
from jax.experimental.pallas import tpu as pltpu

F32 = jnp.float32
BF16 = jnp.bfloat16

EPS = 1e-6
NEG = -1e30
POOL_WINDOWS = (2, 4, 8, 16)
POOL_HALO = 16
N_HEADS = 16
N_KV_HEADS = 4
GQA = N_HEADS // N_KV_HEADS
HEAD_DIM = 64
KV_W = N_KV_HEADS * HEAD_DIM
CMP_BLOCK = 32
SLC_BLOCK = 64
SLC_TOPK = 16
WINDOW = 512
N_BRANCH = 3
FORCE_BONUS = 1000.0
TOP_K = 2
LANES = 128
VMEM_LIMIT = 56 * 1024 * 1024


def _cparams(sem):
    return pltpu.CompilerParams(dimension_semantics=sem, vmem_limit_bytes=VMEM_LIMIT)


def _rmsnorm(x, g):
    ms = jnp.mean(x * x, axis=-1, keepdims=True)
    return (x * lax.rsqrt(ms + EPS)) * g


def _dot(a, b):
    return jnp.dot(a, b, preferred_element_type=F32)


def _dot_nt(a, b):
    return lax.dot_general(a, b, (((1,), (1,)), ((), ())), preferred_element_type=F32)


def _split(a):
    hi = a.astype(BF16)
    return hi, (a - hi.astype(F32)).astype(BF16)


def _dot3(a, b, nt=False):
    d = _dot_nt if nt else _dot
    ah, al = _split(a)
    bh, bl = _split(b)
    return d(ah, bh) + (d(ah, bl) + d(al, bh))


def _masked_softmax(s, mask):
    s = jnp.where(mask, s, NEG)
    m = jnp.max(s, axis=-1, keepdims=True)
    p = jnp.where(mask, jnp.exp(s - m), 0.0)
    return p / jnp.maximum(jnp.sum(p, axis=-1, keepdims=True), 1e-30)


def _alibi_slope(h):
    return 2.0 ** (-8.0 * (h + 1) / N_HEADS)


def _head_slopes(rows, rows_per_head, first_head):
    h = lax.broadcasted_iota(jnp.int32, (rows, 1), 0) // rows_per_head
    slopes = jnp.zeros((rows, 1), F32)
    for j in range(rows // rows_per_head):
        slopes = jnp.where(h == j, _alibi_slope(first_head + j), slopes)
    return slopes


def _pool_prompt_kernel(x_ref, halo_ref, g_ref, w_ref, sc_ref, o_ref, hl_ref, full_ref, *, tile):
    t = pl.program_id(1)
    g = g_ref[...]
    x = x_ref[0]
    h = _rmsnorm(x, g)
    hh = _rmsnorm(halo_ref[0], g)
    full_ref[0:POOL_HALO, :] = jnp.where(t > 0, hh, 0.0)
    full_ref[POOL_HALO:POOL_HALO + tile, :] = h
    row = t * tile + lax.broadcasted_iota(jnp.int32, (tile, 1), 0)
    group = x.shape[-1] // len(POOL_WINDOWS)
    parts = []
    for gi, w in enumerate(POOL_WINDOWS):
        cs = slice(gi * group, (gi + 1) * group)
        hg = h[:, cs]
        acc = hg
        for k in range(1, w):
            acc = acc + full_ref[POOL_HALO - k:POOL_HALO - k + tile, cs]
        cnt = jnp.minimum(row + 1, w).astype(F32)
        pooled = acc / cnt - hg
        parts.append(_dot(pooled.astype(BF16), w_ref[gi]))
    o_ref[0] = x + jnp.concatenate(parts, axis=-1) * sc_ref[...]

    @pl.when(t == pl.num_programs(1) - 1)
    def _():
        hl_ref[0] = h[tile - POOL_HALO:, :]


def _pool_prompt(x, g, w_pool, scale, tile=512):
    B, L, D = x.shape
    assert L % tile == 0 and tile % POOL_HALO == 0
    hb = tile // POOL_HALO
    return pl.pallas_call(
        functools.partial(_pool_prompt_kernel, tile=tile),
        grid=(B, L // tile),
        in_specs=[
            pl.BlockSpec((1, tile, D), lambda b, t: (b, t, 0)),
            pl.BlockSpec((1, POOL_HALO, D), lambda b, t: (b, jnp.maximum(t * hb - 1, 0), 0)),
            pl.BlockSpec((1, D), lambda b, t: (0, 0)),
            pl.BlockSpec(w_pool.shape, lambda b, t: (0, 0, 0)),
            pl.BlockSpec((1, D), lambda b, t: (0, 0)),
        ],
        out_specs=[
            pl.BlockSpec((1, tile, D), lambda b, t: (b, t, 0)),
            pl.BlockSpec((1, POOL_HALO, D), lambda b, t: (b, 0, 0)),
        ],
        out_shape=[jax.ShapeDtypeStruct((B, L, D), F32), jax.ShapeDtypeStruct((B, POOL_HALO, D), F32)],
        scratch_shapes=[pltpu.VMEM((tile + POOL_HALO, D), F32)],
        compiler_params=_cparams(("parallel", "arbitrary")),
        name="pool_prompt",
    )(x, x, g, w_pool, scale)


def _pool_sample_kernel(x_ref, st_ref, g_ref, w_ref, sc_ref, o_ref, h_ref):
    x = x_ref[...]
    h = _rmsnorm(x, g_ref[...])
    h_ref[...] = h
    P = st_ref.shape[0]
    group = x.shape[-1] // len(POOL_WINDOWS)
    parts = []
    for gi, w in enumerate(POOL_WINDOWS):
        cs = slice(gi * group, (gi + 1) * group)
        hg = h[:, cs]
        acc = hg
        for k in range(1, w):
            acc = acc + st_ref[P - k][:, cs]
        pooled = acc / float(w) - hg
        parts.append(_dot3(pooled, w_ref[gi]))
    o_ref[...] = x + jnp.concatenate(parts, axis=-1) * sc_ref[...]


def _pool_sample(x, state_t, g, w_pool, scale):
    B, D = x.shape
    assert state_t.shape[0] >= max(POOL_WINDOWS) - 1
    return pl.pallas_call(
        _pool_sample_kernel,
        out_shape=[jax.ShapeDtypeStruct((B, D), F32), jax.ShapeDtypeStruct((B, D), F32)],
        compiler_params=pltpu.CompilerParams(vmem_limit_bytes=VMEM_LIMIT),
        name="pool_sample",
    )(x, state_t, g, w_pool, scale)


def _ffn_kernel(*refs, n_experts, final_norm, precise):
    moe = n_experts > 1
    it = iter(refs)
    x_ref, g_ref = next(it), next(it)
    wr_ref = next(it) if moe else None
    wg_ref, wu_ref, wd_ref = next(it), next(it), next(it)
    gf_ref = next(it) if final_norm else None
    o_ref, h_scr, acc_scr = next(it), next(it), next(it)
    eacc_scr, gate_scr = (next(it), next(it)) if moe else (None, None)

    e, f = pl.program_id(1), pl.program_id(2)
    last_f = f == pl.num_programs(2) - 1

    @pl.when((e == 0) & (f == 0))
    def _():
        h = _rmsnorm(x_ref[...], g_ref[...])
        h_scr[...] = h.astype(h_scr.dtype)
        acc_scr[...] = jnp.zeros_like(acc_scr)
        if moe:
            logits = jnp.dot(h, wr_ref[...], preferred_element_type=F32, precision=lax.Precision.HIGHEST)
            lane, i1, i2, w1, w2 = _top2_gates(logits, n_experts)
            gate_scr[...] = jnp.where(lane == i1, w1, 0.0) + jnp.where(lane == i2, w2, 0.0)

    mm = _dot3 if precise else _dot
    hb = h_scr[...]
    a = mm(hb, wg_ref[0])
    u = mm(hb, wu_ref[0])
    act = (a * jax.nn.sigmoid(a)) * u
    y = mm(act.astype(hb.dtype), wd_ref[0])

    if moe:
        @pl.when(f == 0)
        def _():
            eacc_scr[...] = y

        @pl.when(f > 0)
        def _():
            eacc_scr[...] += y

        @pl.when(last_f)
        def _():
            gate = gate_scr[...]
            lane = lax.broadcasted_iota(jnp.int32, gate.shape, 1)
            ge = jnp.sum(jnp.where(lane == e, gate, 0.0), axis=-1, keepdims=True)
            acc_scr[...] += ge * eacc_scr[...]
    else:
        acc_scr[...] += y

    @pl.when((e == pl.num_programs(1) - 1) & last_f)
    def _():
        out = x_ref[...] + acc_scr[...]
        if final_norm:
            out = _rmsnorm(out, gf_ref[...])
        o_ref[...] = out


def _ffn(x, g, wg, wu, wd, w_router=None, g_final=None, *, tm, tf=512):
    precise = wg.dtype == F32
    N, D = x.shape
    E, _, F = wg.shape
    assert N % tm == 0 and F % tf == 0
    moe = w_router is not None
    assert moe == (E > 1)
    final_norm = g_final is not None
    const2 = lambda i, e, f: (0, 0)
    args, in_specs = [x, g], [pl.BlockSpec((tm, D), lambda i, e, f: (i, 0)), pl.BlockSpec((1, D), const2)]
    if moe:
        args.append(w_router)
        in_specs.append(pl.BlockSpec(w_router.shape, const2))
    args += [wg, wu, wd]
    in_specs += [
        pl.BlockSpec((1, D, tf), lambda i, e, f: (e, 0, f)),
        pl.BlockSpec((1, D, tf), lambda i, e, f: (e, 0, f)),
        pl.BlockSpec((1, tf, D), lambda i, e, f: (e, f, 0)),
    ]
    if final_norm:
        args.append(g_final)
        in_specs.append(pl.BlockSpec((1, D), const2))
    scratch = [pltpu.VMEM((tm, D), F32 if precise else BF16), pltpu.VMEM((tm, D), F32)]
    if moe:
        scratch += [pltpu.VMEM((tm, D), F32), pltpu.VMEM((tm, LANES), F32)]
    return pl.pallas_call(
        functools.partial(_ffn_kernel, n_experts=E, final_norm=final_norm, precise=precise),
        grid=(N // tm, E, F // tf),
        in_specs=in_specs,
        out_specs=pl.BlockSpec((tm, D), lambda i, e, f: (i, 0)),
        out_shape=jax.ShapeDtypeStruct((N, D), F32),
        scratch_shapes=scratch,
        compiler_params=_cparams(("parallel", "arbitrary", "arbitrary")),
        name="moe_ffn" if moe else "dense_ffn",
    )(*args)


ROUTE_I1, ROUTE_I2, ROUTE_W1, ROUTE_W2, ROUTE_R1, ROUTE_R2 = range(6)


def _top2_gates(logits, n_experts):
    lane = lax.broadcasted_iota(jnp.int32, logits.shape, 1)
    lg = jnp.where(lane < n_experts, logits, -jnp.inf)
    m1 = jnp.max(lg, axis=-1, keepdims=True)
    i1 = jnp.min(jnp.where(lg == m1, lane, LANES), axis=-1, keepdims=True)
    lg2 = jnp.where(lane == i1, -jnp.inf, lg)
    m2 = jnp.max(lg2, axis=-1, keepdims=True)
    i2 = jnp.min(jnp.where(lg2 == m2, lane, LANES), axis=-1, keepdims=True)
    e2 = jnp.exp(m2 - m1)
    den = 1.0 + e2
    return lane, i1, i2, 1.0 / den, e2 / den


def _moe_route_kernel(x_ref, g_ref, wr_ref, tri_ref, route_ref, cnt_ref, carry, *, n_experts):
    @pl.when(pl.program_id(0) == 0)
    def _():
        carry[...] = jnp.zeros_like(carry)

    h = _rmsnorm(x_ref[...], g_ref[...])
    logits = jnp.dot(h, wr_ref[...], preferred_element_type=F32, precision=lax.Precision.HIGHEST)
    lane, i1, i2, w1, w2 = _top2_gates(logits, n_experts)
    member = jnp.where((lane == i1) | (lane == i2), 1.0, 0.0)
    before = _dot(tri_ref[...], member.astype(BF16)) + carry[0:1, :]
    r1 = jnp.sum(jnp.where(lane == i1, before, 0.0), axis=-1, keepdims=True)
    r2 = jnp.sum(jnp.where(lane == i2, before, 0.0), axis=-1, keepdims=True)
    rec = jnp.zeros(logits.shape, F32)
    for col, val in ((ROUTE_I1, i1.astype(F32)), (ROUTE_I2, i2.astype(F32)), (ROUTE_W1, w1), (ROUTE_W2, w2),
                     (ROUTE_R1, r1), (ROUTE_R2, r2)):
        rec = jnp.where(lane == col, val, rec)
    route_ref[...] = rec
    carry[...] = carry[...] + jnp.sum(member, axis=0, keepdims=True)
    cnt_ref[...] = carry[...]


def _moe_route(x, g, w_router, n_experts, tm=512):
    N, D = x.shape
    assert N % tm == 0
    tri = (jnp.arange(tm)[:, None] > jnp.arange(tm)[None, :]).astype(BF16)
    return pl.pallas_call(
        functools.partial(_moe_route_kernel, n_experts=n_experts),
        grid=(N // tm,),
        in_specs=[pl.BlockSpec((tm, D), lambda i: (i, 0)), pl.BlockSpec((1, D), lambda i: (0, 0)),
                  pl.BlockSpec(w_router.shape, lambda i: (0, 0)), pl.BlockSpec((tm, tm), lambda i: (0, 0))],
        out_specs=[pl.BlockSpec((tm, LANES), lambda i: (i, 0)), pl.BlockSpec((8, LANES), lambda i: (0, 0))],
        out_shape=[jax.ShapeDtypeStruct((N, LANES), F32), jax.ShapeDtypeStruct((8, LANES), F32)],
        scratch_shapes=[pltpu.VMEM((8, LANES), F32)],
        compiler_params=_cparams(("arbitrary",)),
        name="moe_route",
    )(x, g, w_router, tri)


GATHER_UNROLL = 8
GATHER_PRIORITY = 1


def _row_gather(idx_of, src_hbm, dst_at, sem, rows, priority):
    assert rows % GATHER_UNROLL == 0

    def copy(r):
        return pltpu.make_async_copy(src_hbm.at[pl.ds(idx_of(r), 1), :], dst_at(r), sem)

    def start():
        def group(i, c):
            for j in range(GATHER_UNROLL):
                copy(i * GATHER_UNROLL + j).start(priority=priority(j))
            return c
        lax.fori_loop(0, rows // GATHER_UNROLL, group, 0)

    def wait():
        lax.fori_loop(0, rows, lambda r, c: (copy(r).wait(), c)[1], 0, unroll=GATHER_UNROLL)

    return start, wait, copy


def _moe_experts_kernel(te_ref, nu_ref, tos_ref, x_hbm, g_ref, wg_ref, wu_ref, wd_ref, o_ref,
                        xbuf, h_scr, acc_scr, sems, *, tile, nf):
    i, f = pl.program_id(0), pl.program_id(1)
    used = i < nu_ref[0]
    slot = i % 2
    per_step = tile // nf

    def gather(t, sl):
        return _row_gather(lambda r: tos_ref[t * tile + r], x_hbm,
                           lambda r: xbuf.at[sl, pl.ds(r, 1), :], sems.at[sl], tile, lambda j: GATHER_PRIORITY)

    def prefetch_share():
        row_copy = gather(i + 1, 1 - slot)[2]
        for j in range(per_step):
            row_copy(f * per_step + j).start(priority=GATHER_PRIORITY)

    @pl.when((f == 0) & (i == 0))
    def _():
        gather(0, 0)[0]()

    @pl.when(f == 0)
    def _():
        gather(i, slot)[1]()

    @pl.when((f == 0) & used)
    def _():
        h_scr[...] = _rmsnorm(xbuf[slot], g_ref[...]).astype(BF16)
        acc_scr[...] = jnp.zeros_like(acc_scr)

    @pl.when(used)
    def _():
        prefetch_share()
        hb = h_scr[...]
        a = _dot(hb, wg_ref[0])
        u = _dot(hb, wu_ref[0])
        act = (a * jax.nn.sigmoid(a)) * u
        acc_scr[...] += _dot(act.astype(BF16), wd_ref[0])

    @pl.when(jnp.logical_not(used))
    def _():
        prefetch_share()

    @pl.when(f == nf - 1)
    def _():
        o_ref[...] = jnp.where(used, acc_scr[...], 0.0)

    @pl.when((f == nf - 1) & (i == pl.num_programs(0) - 1))
    def _():
        gather(i + 1, 1 - slot)[1]()


def _moe_experts(x, g, wg, wu, wd, tile_expert, n_used, token_of_slot, *, tile, tf=512):
    N, D = x.shape
    E, _, F = wg.shape
    n_slots = token_of_slot.shape[0] - tile
    assert n_slots % tile == 0 and F % tf == 0
    n_tiles, nf = n_slots // tile, F // tf
    assert tile % nf == 0
    fidx = lambda i, f, nu: jnp.where(i < nu[0], f, nf - 1)
    return pl.pallas_call(
        functools.partial(_moe_experts_kernel, tile=tile, nf=nf),
        grid_spec=pltpu.PrefetchScalarGridSpec(
            num_scalar_prefetch=3,
            grid=(n_tiles, nf),
            in_specs=[pl.BlockSpec(memory_space=pl.ANY),
                      pl.BlockSpec((1, D), lambda i, f, te, nu, tos: (0, 0)),
                      pl.BlockSpec((1, D, tf), lambda i, f, te, nu, tos: (te[i], 0, fidx(i, f, nu))),
                      pl.BlockSpec((1, D, tf), lambda i, f, te, nu, tos: (te[i], 0, fidx(i, f, nu))),
                      pl.BlockSpec((1, tf, D), lambda i, f, te, nu, tos: (te[i], fidx(i, f, nu), 0))],
            out_specs=pl.BlockSpec((tile, D), lambda i, f, te, nu, tos: (i, 0)),
            scratch_shapes=[pltpu.VMEM((2, tile, D), F32), pltpu.VMEM((tile, D), BF16),
                            pltpu.VMEM((tile, D), F32), pltpu.SemaphoreType.DMA((2,))],
        ),
        out_shape=jax.ShapeDtypeStruct((n_slots, D), F32),
        compiler_params=_cparams(("arbitrary", "arbitrary")),
        name="moe_experts",
    )(tile_expert, n_used, token_of_slot, x, g, wg, wu, wd)


def _moe_combine_kernel(slot_ref, x_ref, route_ref, gf_ref, ys_hbm, o_ref, ybuf, sems, *, tile):
    i = pl.program_id(0)
    slot = i % 2

    def gather(t, sl):
        return _row_gather(lambda j: slot_ref[t * 2 * tile + j], ys_hbm,
                           lambda j: ybuf.at[sl, pl.ds(j, 1), :], sems.at[sl], 2 * tile, lambda j: j % 2)

    @pl.when(i == 0)
    def _():
        gather(0, 0)[0]()

    gather(i, slot)[1]()

    @pl.when(i + 1 < pl.num_programs(0))
    def _():
        gather(i + 1, 1 - slot)[0]()

    route = route_ref[...]
    w1, w2 = route[:, ROUTE_W1:ROUTE_W1 + 1], route[:, ROUTE_W2:ROUTE_W2 + 1]
    y = w1 * ybuf[slot, 0:tile, :] + w2 * ybuf[slot, tile:2 * tile, :]
    o_ref[...] = _rmsnorm(x_ref[...] + y, gf_ref[...])


def _moe_combine(x, route, g_final, ys, slots, *, tile=256):
    N, D = x.shape
    assert N % tile == 0
    return pl.pallas_call(
        functools.partial(_moe_combine_kernel, tile=tile),
        grid_spec=pltpu.PrefetchScalarGridSpec(
            num_scalar_prefetch=1,
            grid=(N // tile,),
            in_specs=[pl.BlockSpec((tile, D), lambda i, s: (i, 0)),
                      pl.BlockSpec((tile, LANES), lambda i, s: (i, 0)),
                      pl.BlockSpec((1, D), lambda i, s: (0, 0)),
                      pl.BlockSpec(memory_space=pl.ANY)],
            out_specs=pl.BlockSpec((tile, D), lambda i, s: (i, 0)),
            scratch_shapes=[pltpu.VMEM((2, 2 * tile, D), F32), pltpu.SemaphoreType.DMA((2,))],
        ),
        out_shape=jax.ShapeDtypeStruct((N, D), F32),
        compiler_params=_cparams(("arbitrary",)),
        name="moe_combine",
    )(slots, x, route, g_final, ys)


def _moe_prompt(x, g, w_router, wg, wu, wd, g_final, *, tile=896, ctile=256):
    N, D = x.shape
    E = wg.shape[0]
    route, counts = _moe_route(x, g, w_router, E)
    cnt = counts[0, :E].astype(jnp.int32)
    padded = (cnt + tile - 1) // tile * tile
    ends = jnp.cumsum(padded)
    off = ends - padded
    i1, i2 = route[:, ROUTE_I1].astype(jnp.int32), route[:, ROUTE_I2].astype(jnp.int32)
    slot1 = off[i1] + route[:, ROUTE_R1].astype(jnp.int32)
    slot2 = off[i2] + route[:, ROUTE_R2].astype(jnp.int32)
    n_slots = (TOP_K * N + E * (tile - 1)) // tile * tile
    rows = jnp.arange(N, dtype=jnp.int32)
    token_of_slot = jnp.zeros((n_slots + tile,), jnp.int32).at[jnp.concatenate([slot1, slot2])].set(
        jnp.concatenate([rows, rows]), unique_indices=True)
    n_used = ends[-1:] // tile
    tile_start = jnp.minimum(jnp.arange(n_slots // tile, dtype=jnp.int32), n_used[0] - 1) * tile
    tile_expert = jnp.sum(tile_start[:, None] >= ends[None, :], axis=1).astype(jnp.int32)
    ys = _moe_experts(x, g, wg, wu, wd, tile_expert, n_used.astype(jnp.int32), token_of_slot, tile=tile)
    slots = jnp.stack([slot1.reshape(-1, ctile), slot2.reshape(-1, ctile)], axis=1).reshape(-1)
    return _moe_combine(x, route, g_final, ys, slots, tile=ctile)


def _nsa_proj_kernel(x_ref, g_ref, wq_ref, wkv_ref, wgt_ref, wck_ref, wcv_ref,
                     q_ref, kvt_ref, ks_ref, vs_ref, kw_ref, vw_ref, gt_ref, kc_ref, vc_ref):
    hb = _rmsnorm(x_ref[...], g_ref[...]).astype(BF16)
    q_ref[...] = _dot(hb, wq_ref[...]).astype(BF16)
    kv = _dot(hb, wkv_ref[...])
    kvt_ref[0] = kv.T
    ks_ref[...] = kv[:, 2 * KV_W:3 * KV_W].astype(BF16)
    vs_ref[...] = kv[:, 3 * KV_W:4 * KV_W].astype(BF16)
    kw_ref[...] = kv[:, 4 * KV_W:5 * KV_W].astype(BF16)
    vw_ref[...] = kv[:, 5 * KV_W:].astype(BF16)
    gt_ref[...] = jax.nn.sigmoid(_dot(hb, wgt_ref[...]))
    tm = kv.shape[0]
    kc_ref[...] = (kv[:, :KV_W] * wck_ref[...]).reshape(tm // CMP_BLOCK, CMP_BLOCK, KV_W).sum(axis=1)
    vc_ref[...] = (kv[:, KV_W:2 * KV_W] * wcv_ref[...]).reshape(tm // CMP_BLOCK, CMP_BLOCK, KV_W).sum(axis=1)


def _nsa_proj(x, g, wq, wkv, wgt, wck, wcv, *, tm, seq):
    N, D = x.shape
    assert N % tm == 0 and tm % (8 * CMP_BLOCK) == 0 and seq % tm == 0 and N % seq == 0
    per_seq = seq // tm
    row = lambda w: pl.BlockSpec((tm, w), lambda i: (i, 0))
    full = lambda a: pl.BlockSpec(a.shape, lambda i: (0, 0))
    out_shape = [jax.ShapeDtypeStruct((N, N_HEADS * HEAD_DIM), BF16),
                 jax.ShapeDtypeStruct((N // seq, 6 * KV_W, seq), F32)]
    out_shape += [jax.ShapeDtypeStruct((N, KV_W), BF16)] * 4
    out_shape += [jax.ShapeDtypeStruct((N, LANES), F32)]
    out_shape += [jax.ShapeDtypeStruct((N // CMP_BLOCK, KV_W), F32)] * 2
    out_specs = [row(N_HEADS * HEAD_DIM),
                 pl.BlockSpec((1, 6 * KV_W, tm), lambda i: (i // per_seq, 0, i % per_seq))]
    out_specs += [row(KV_W)] * 4 + [row(LANES)]
    out_specs += [pl.BlockSpec((tm // CMP_BLOCK, KV_W), lambda i: (i, 0))] * 2
    return pl.pallas_call(
        _nsa_proj_kernel,
        grid=(N // tm,),
        in_specs=[row(D), full(g), full(wq), full(wkv), full(wgt), full(wck), full(wcv)],
        out_specs=out_specs,
        out_shape=out_shape,
        compiler_params=_cparams(("parallel",)),
        name="nsa_proj_prompt",
    )(x, g, wq, wkv, wgt, wck, wcv)


def _nsa_proj_sample_kernel(x_ref, g_ref, w_ref, z_ref):
    z_ref[...] = _dot3(_rmsnorm(x_ref[...], g_ref[...]), w_ref[...])


def _nsa_proj_sample(x, g, w_in, tn=384):
    B, D = x.shape
    cols = w_in.shape[1]
    assert cols % tn == 0
    return pl.pallas_call(
        _nsa_proj_sample_kernel,
        grid=(cols // tn,),
        in_specs=[pl.BlockSpec((B, D), lambda j: (0, 0)), pl.BlockSpec((1, D), lambda j: (0, 0)),
                  pl.BlockSpec((D, tn), lambda j: (0, j))],
        out_specs=pl.BlockSpec((B, tn), lambda j: (0, j)),
        out_shape=jax.ShapeDtypeStruct((B, cols), F32),
        compiler_params=_cparams(("parallel",)),
        name="nsa_proj_sample",
    )(x, g, w_in)


def _nsa_prompt_kernel(q_ref, gt_ref, kc_ref, vc_ref, ks_ref, vs_ref, kw_ref, vw_ref, x_ref, wo_ref,
                       o_ref, o_scr, sel_scr, *, tq, kchunk, seq):
    t0 = pl.program_id(1) * tq
    rows = GQA * tq
    scale = HEAD_DIM ** -0.5
    assert scale == 0.125
    n_blocks = seq // SLC_BLOCK
    top_k = min(SLC_TOPK, n_blocks)

    q = q_ref[0]
    gates = gt_ref[0]
    qpos_t = t0 + lax.broadcasted_iota(jnp.int32, (tq, 1), 0)
    qpos = jnp.concatenate([qpos_t] * GQA, axis=0)
    tq_f = qpos.astype(F32)

    lane = lax.broadcasted_iota(jnp.int32, (tq, LANES), 1)
    jblk = lane >> 1
    real = ((lane & 1) == 0) & (jblk < n_blocks)
    blk = qpos_t // SLC_BLOCK
    valid = real & (jblk <= blk)
    forced = (jblk == 0) | (jblk == blk) | (jblk == blk - 1)
    valid_f = jnp.where(valid, 1.0, 0.0)
    valid_b = valid_f.astype(BF16)
    bonus = jnp.where(forced, FORCE_BONUS, 0.0)
    floor = jnp.where(valid, 0.0, jnp.where(real, -1.0, -2.0))
    c_lane = lax.broadcasted_iota(jnp.int32, (1, LANES), 1)
    c_mid = c_lane.astype(F32) * CMP_BLOCK + (CMP_BLOCK - 1) / 2
    c_end = (c_lane + 1) * CMP_BLOCK - 1

    n_chunks = (t0 + tq + kchunk - 1) // kchunk
    wstart = pl.multiple_of(jnp.maximum(t0 - WINDOW, 0), tq)
    wlen = WINDOW + tq

    need_rank = (t0 + tq - 1) // SLC_BLOCK >= top_k
    wpos = wstart + lax.broadcasted_iota(jnp.int32, (1, wlen), 1)
    w_dist = jnp.where((wpos <= qpos_t) & (qpos_t - wpos < WINDOW), (wpos - qpos_t).astype(F32), NEG)

    def with_ones(v):
        return jnp.concatenate([v, jnp.ones_like(v)], axis=1)

    def biased(scores, dist, kv):
        return jnp.concatenate(
            [scores[g * tq:(g + 1) * tq] + _alibi_slope(kv * GQA + g) * dist for g in range(GQA)], axis=0)

    for kv in range(N_KV_HEADS):
        hs = slice(kv * HEAD_DIM, (kv + 1) * HEAD_DIM)
        qs = jnp.concatenate(
            [q[:, (kv * GQA + g) * HEAD_DIM:(kv * GQA + g + 1) * HEAD_DIM] for g in range(GQA)], axis=0) * scale
        slope = _head_slopes(rows, tq, kv * GQA)

        s_c = _dot_nt(qs, kc_ref[0][:, hs]) - slope * (tq_f - c_mid)
        p_c = _masked_softmax(s_c, c_end <= qpos)
        o_c = _dot(p_c.astype(BF16), vc_ref[0][:, hs])

        imp = p_c[0:tq]
        for g in range(1, GQA):
            imp = imp + p_c[g * tq:(g + 1) * tq]
        sel_scr[...] = valid_b

        @pl.when(need_rank)
        def _():
            pair = imp + pltpu.roll(imp, LANES - 1, 1)
            score = valid_f * (pair + bonus) + floor
            score_t = score.T[:2 * n_blocks]
            row_id = lax.broadcasted_iota(jnp.int32, score_t.shape, 0)
            rank_t = jnp.zeros(score_t.shape, jnp.int32)
            for k in range(n_blocks):
                other = score_t[2 * k:2 * k + 1, :]
                beats = (other > score_t) | ((other == score_t) & (row_id > 2 * k))
                rank_t = rank_t + beats.astype(jnp.int32)
            top_t = jnp.where(rank_t < top_k, 1.0, 0.0)
            top = jnp.concatenate([top_t, jnp.zeros((LANES - 2 * n_blocks, tq), F32)], axis=0).T
            sel_scr[...] = (top * valid_f).astype(BF16)

        sel = sel_scr[...]

        def chunk(c, carry):
            m, acc = carry
            k0 = pl.multiple_of(c * kchunk, kchunk)
            kb = ks_ref[0, pl.ds(k0, kchunk), hs]
            vb = with_ones(vs_ref[0, pl.ds(k0, kchunk), hs])
            pos = k0 + lax.broadcasted_iota(jnp.int32, (1, kchunk), 1)
            expand = (lax.broadcasted_iota(jnp.int32, (LANES, kchunk), 0)
                      == 2 * ((k0 + lax.broadcasted_iota(jnp.int32, (LANES, kchunk), 1)) // SLC_BLOCK))
            picked = _dot(sel, jnp.where(expand, 1.0, 0.0).astype(BF16))
            ok = (picked > 0.5) & (pos <= qpos_t)
            s = biased(_dot_nt(qs, kb), jnp.where(ok, (pos - qpos_t).astype(F32), NEG), kv)
            m_new = jnp.maximum(m, jnp.max(s, axis=-1, keepdims=True))
            p = jnp.exp(s - m_new)
            acc = jnp.exp(m - m_new) * acc + _dot(p.astype(BF16), vb)
            return m_new, acc

        init = (jnp.full((rows, 1), NEG, F32), jnp.zeros((rows, 2 * HEAD_DIM), F32))
        _, acc_s = lax.fori_loop(0, n_chunks, chunk, init)
        o_s = acc_s[:, :HEAD_DIM] / acc_s[:, HEAD_DIM:HEAD_DIM + 1]

        kb = kw_ref[0, pl.ds(wstart, wlen), hs]
        vb = with_ones(vw_ref[0, pl.ds(wstart, wlen), hs])
        s_w = biased(_dot_nt(qs, kb), w_dist, kv)
        p_w = jnp.exp(s_w - jnp.max(s_w, axis=-1, keepdims=True))
        pv = _dot(p_w.astype(BF16), vb)
        o_w = pv[:, :HEAD_DIM] / pv[:, HEAD_DIM:HEAD_DIM + 1]

        def gate_col(n):
            base = n * N_HEADS + kv * GQA
            return jnp.concatenate([gates[:, base + g:base + g + 1] for g in range(GQA)], axis=0)

        o = gate_col(0) * o_c + gate_col(1) * o_s + gate_col(2) * o_w
        for g in range(GQA):
            h = kv * GQA + g
            o_scr[:, h * HEAD_DIM:(h + 1) * HEAD_DIM] = o[g * tq:(g + 1) * tq]

    o_ref[0] = x_ref[0] + _dot(o_scr[...].astype(BF16), wo_ref[...])


def _nsa_prompt(q, gates, kc, vc, ks, vs, kw, vw, x, w_o, *, tq=128, kchunk=512):
    B, L, D = x.shape
    assert L % kchunk == 0 and kchunk % tq == 0 and L >= WINDOW + tq and L % SLC_BLOCK == 0
    assert 2 * (L // SLC_BLOCK) <= LANES and L // CMP_BLOCK <= LANES and kc.shape[1] == LANES
    tile = lambda w: pl.BlockSpec((1, tq, w), lambda b, t: (b, t, 0))
    whole = lambda a: pl.BlockSpec((1,) + a.shape[1:], lambda b, t: (b, 0, 0))
    return pl.pallas_call(
        functools.partial(_nsa_prompt_kernel, tq=tq, kchunk=kchunk, seq=L),
        grid=(B, L // tq),
        in_specs=[tile(N_HEADS * HEAD_DIM), tile(LANES), whole(kc), whole(vc), whole(ks), whole(vs),
                  whole(kw), whole(vw), tile(D), pl.BlockSpec(w_o.shape, lambda b, t: (0, 0))],
        out_specs=tile(D),
        out_shape=jax.ShapeDtypeStruct((B, L, D), F32),
        scratch_shapes=[pltpu.VMEM((tq, N_HEADS * HEAD_DIM), F32), pltpu.VMEM((tq, LANES), BF16)],
        compiler_params=_cparams(("parallel", "arbitrary")),
        name="nsa_attn_prompt",
    )(q, gates, kc, vc, ks, vs, kw, vw, x, w_o)


CMP_GROUP = 32
MXU_DEPTH = 256


def _cmp_sample_kernel(pt_ref, cache_ref, w_ref, s_ref, o_ref, buf, sems, *, rows):
    b, g = pl.program_id(0), pl.program_id(1)
    n_groups = pl.num_programs(1)
    step = b * n_groups + g
    slot = step % 2

    def page_copy(bb, gg, sl, i):
        page = pt_ref[bb, gg * CMP_GROUP + i]
        return pltpu.make_async_copy(cache_ref.at[page, pl.ds(0, rows), :], buf.at[sl, i], sems.at[sl])

    def start_group(bb, gg, sl):
        for i in range(CMP_GROUP):
            page_copy(bb, gg, sl, i).start()

    @pl.when(step == 0)
    def _():
        start_group(b, g, slot)

    @pl.when(step + 1 < pl.num_programs(0) * n_groups)
    def _():
        nxt = step + 1
        start_group(nxt // n_groups, nxt % n_groups, 1 - slot)

    for i in range(CMP_GROUP):
        page_copy(b, g, slot, i).wait()

    w = w_ref[...]
    acc = jnp.zeros((rows, LANES), F32)
    for j in range(CMP_GROUP // 2):
        p = jnp.concatenate([buf[slot, 2 * j] * w, buf[slot, 2 * j + 1] * w], axis=1)
        hi, lo = _split(p)
        r = _dot(jnp.concatenate([hi, lo], axis=0), s_ref[j])
        acc = acc + (r[:rows] + r[rows:])
    o_ref[0] = acc


def _cmp_sample(page_table, cache_t, w_rows):
    B, n_pages = page_table.shape
    _, _, page = cache_t.shape
    rows = w_rows.shape[0]
    per_page = page // CMP_BLOCK
    assert n_pages % CMP_GROUP == 0 and CMP_GROUP * per_page == LANES and 2 * page == MXU_DEPTH
    k = jnp.arange(2 * page)
    token = (2 * jnp.arange(CMP_GROUP // 2)[:, None] + k[None, :] // page) * per_page + (k[None, :] % page) // CMP_BLOCK
    block_sum = (token[:, :, None] == jnp.arange(LANES)[None, None, :]).astype(BF16)
    return pl.pallas_call(
        functools.partial(_cmp_sample_kernel, rows=rows),
        grid_spec=pltpu.PrefetchScalarGridSpec(
            num_scalar_prefetch=1,
            grid=(B, n_pages // CMP_GROUP),
            in_specs=[pl.BlockSpec(memory_space=pl.ANY),
                      pl.BlockSpec(w_rows.shape, lambda b, g, pt: (0, 0)),
                      pl.BlockSpec(block_sum.shape, lambda b, g, pt: (0, 0, 0))],
            out_specs=pl.BlockSpec((1, rows, LANES), lambda b, g, pt: (b, 0, g)),
            scratch_shapes=[pltpu.VMEM((2, CMP_GROUP, rows, page), F32), pltpu.SemaphoreType.DMA((2,))],
        ),
        out_shape=jax.ShapeDtypeStruct((B, rows, n_pages * per_page), F32),
        compiler_params=_cparams(("arbitrary", "arbitrary")),
        name="cmp_sample",
    )(page_table, cache_t, w_rows, block_sum)


def _select_sample_kernel(q_ref, kcv_ref, oc_ref, idx_ref, *, past):
    scale = HEAD_DIM ** -0.5
    q = q_ref[0]
    nc = kcv_ref.shape[2]
    n_past_blocks = past // SLC_BLOCK
    per_block = SLC_BLOCK // CMP_BLOCK
    row = lax.broadcasted_iota(jnp.int32, (N_HEADS, 1), 0)
    slope = _head_slopes(N_HEADS, 1, 0)
    tok = lax.broadcasted_iota(jnp.int32, (1, nc), 1)
    c_mid = tok.astype(F32) * CMP_BLOCK + (CMP_BLOCK - 1) / 2
    c_ok = (tok + 1) * CMP_BLOCK - 1 <= past
    bias = slope * (float(past) - c_mid)

    s_c = jnp.zeros((N_HEADS, nc), F32)
    for kv in range(N_KV_HEADS):
        s_kv = _dot3(q, kcv_ref[0, kv * HEAD_DIM:(kv + 1) * HEAD_DIM, :])
        s_c = jnp.where(row // GQA == kv, s_kv, s_c)
    p_c = _masked_softmax(s_c * scale - bias, c_ok)
    o_c = jnp.zeros((N_HEADS, HEAD_DIM), F32)
    for kv in range(N_KV_HEADS):
        o_kv = _dot3(p_c, kcv_ref[0, KV_W + kv * HEAD_DIM:KV_W + (kv + 1) * HEAD_DIM, :], nt=True)
        o_c = jnp.where(row // GQA == kv, o_kv, o_c)
    oc_ref[0] = o_c

    imp = p_c
    shift = 1
    while shift < GQA:
        imp = imp + pltpu.roll(imp, shift, 0)
        shift *= 2
    assert per_block == 2
    pair = imp + pltpu.roll(imp, nc - 1, 1)
    jblk = tok // per_block
    real = tok % per_block == 0
    forced = (jblk == 0) | (jblk == n_past_blocks - 1)
    score = jnp.where(real, pair + jnp.where(forced, FORCE_BONUS, 0.0), -2.0)
    own = jnp.float32(FORCE_BONUS)
    rank = (own > score).astype(jnp.int32)
    for k in range(n_past_blocks):
        col = score[:, per_block * k:per_block * k + 1]
        beats = (col > score) | ((col == score) & (jblk > k))
        rank = rank + beats.astype(jnp.int32)
    own_rank = jnp.sum(jnp.where(real & (score >= own), 1, 0), axis=-1, keepdims=True)
    out_lane = lax.broadcasted_iota(jnp.int32, (N_HEADS, LANES), 1)
    idx = jnp.zeros((N_HEADS, LANES), jnp.int32)
    for r in range(SLC_TOPK):
        hit = jnp.sum(jnp.where(real & (rank == r), jblk, 0), axis=-1, keepdims=True)
        hit = hit + jnp.where(own_rank == r, n_past_blocks, 0)
        idx = jnp.where(out_lane == r, hit, idx)
    idx_ref[0] = idx


def _select_sample(q16, kcv, past):
    B = q16.shape[0]
    nc = kcv.shape[2]
    assert past % SLC_BLOCK == 0 and nc == past // CMP_BLOCK and nc % LANES == 0
    assert past // SLC_BLOCK + 1 > SLC_TOPK
    return pl.pallas_call(
        functools.partial(_select_sample_kernel, past=past),
        grid=(B,),
        in_specs=[pl.BlockSpec((1,) + q16.shape[1:], lambda b: (b, 0, 0)),
                  pl.BlockSpec((1,) + kcv.shape[1:], lambda b: (b, 0, 0))],
        out_specs=[pl.BlockSpec((1, N_HEADS, HEAD_DIM), lambda b: (b, 0, 0)),
                   pl.BlockSpec((1, N_HEADS, LANES), lambda b: (b, 0, 0))],
        out_shape=[jax.ShapeDtypeStruct((B, N_HEADS, HEAD_DIM), F32),
                   jax.ShapeDtypeStruct((B, N_HEADS, LANES), jnp.int32)],
        compiler_params=_cparams(("parallel",)),
        name="select_sample",
    )(q16, kcv)


def _attend_sample_kernel(pt_ref, idx_ref, q_ref, oc_ref, gt_ref, new_ref, win_ref, cache_ref,
                          o_ref, kbuf, vbuf, sems, *, past):
    b = pl.program_id(0)
    scale = HEAD_DIM ** -0.5
    n_past_blocks = past // SLC_BLOCK
    page_rows = cache_ref.shape[2]
    blocks_per_page = page_rows // SLC_BLOCK
    nkeys = SLC_TOPK * page_rows

    def block_copies(kv, n):
        blk = idx_ref[(b * N_KV_HEADS + kv) * SLC_TOPK + n]
        in_past = blk < n_past_blocks
        page = pt_ref[b, jnp.minimum(blk, n_past_blocks - 1) // blocks_per_page]
        dst = pl.ds(n * page_rows, page_rows)
        ck = pltpu.make_async_copy(cache_ref.at[page, pl.ds((2 * N_KV_HEADS + kv) * HEAD_DIM, HEAD_DIM), :],
                                   kbuf.at[kv, :, dst], sems.at[0, kv, n])
        cv = pltpu.make_async_copy(cache_ref.at[page, pl.ds((3 * N_KV_HEADS + kv) * HEAD_DIM, HEAD_DIM), :],
                                   vbuf.at[kv, :, dst], sems.at[1, kv, n])
        return blk, in_past, ck, cv

    for kv in range(N_KV_HEADS):
        for n in range(SLC_TOPK):
            _, in_past, ck, cv = block_copies(kv, n)

            @pl.when(in_past)
            def _():
                ck.start()
                cv.start()

            @pl.when(jnp.logical_not(in_past))
            def _():
                kbuf[kv, :, n * page_rows:(n + 1) * page_rows] = jnp.zeros((HEAD_DIM, page_rows), F32)
                vbuf[kv, :, n * page_rows:(n + 1) * page_rows] = jnp.zeros((HEAD_DIM, page_rows), F32)

    q = q_ref[0]
    row = lax.broadcasted_iota(jnp.int32, (N_HEADS, 1), 0)
    slope = _head_slopes(N_HEADS, 1, 0)
    key_lane = lax.broadcasted_iota(jnp.int32, (1, nkeys), 1)

    wb = win_ref.shape[2]
    w_lane = lax.broadcasted_iota(jnp.int32, (1, wb), 1)
    wpos = past - wb + w_lane
    w_ok = (past - wpos < WINDOW) & (wpos >= 0)
    w_bias = slope * (past - wpos).astype(F32)
    s_w = jnp.zeros((N_HEADS, wb), F32)
    s_n = jnp.zeros((N_HEADS, 1), F32)
    for kv in range(N_KV_HEADS):
        mine = row // GQA == kv
        s_kv = _dot3(q, win_ref[0, kv * HEAD_DIM:(kv + 1) * HEAD_DIM, :])
        s_w = jnp.where(mine, s_kv, s_w)
        s_n = jnp.where(mine, jnp.sum(q * new_ref[0, 4, kv:kv + 1, :], axis=-1, keepdims=True), s_n)
    s_w = jnp.where(w_ok, s_w * scale - w_bias, NEG)
    s_n = s_n * scale
    m_w = jnp.maximum(jnp.max(s_w, axis=-1, keepdims=True), s_n)
    p_w = jnp.where(w_ok, jnp.exp(s_w - m_w), 0.0)
    p_n = jnp.exp(s_n - m_w)
    l_w = jnp.maximum(jnp.sum(p_w, axis=-1, keepdims=True) + p_n, 1e-30)
    p_w = p_w / l_w
    p_n = p_n / l_w
    o_w = jnp.zeros((N_HEADS, HEAD_DIM), F32)
    for kv in range(N_KV_HEADS):
        o_kv = (_dot3(p_w, win_ref[0, KV_W + kv * HEAD_DIM:KV_W + (kv + 1) * HEAD_DIM, :], nt=True)
                + p_n * new_ref[0, 5, kv:kv + 1, :])
        o_w = jnp.where(row // GQA == kv, o_kv, o_w)

    for kv in range(N_KV_HEADS):
        for n in range(SLC_TOPK):
            _, in_past, ck, cv = block_copies(kv, n)

            @pl.when(in_past)
            def _():
                ck.wait()
                cv.wait()

    s_s = jnp.zeros((N_HEADS, nkeys), F32)
    pos = jnp.zeros((N_HEADS, nkeys), jnp.int32)
    live = jnp.zeros((N_HEADS, nkeys), jnp.int32)
    own = jnp.zeros((N_HEADS, 1), jnp.int32)
    s_n = jnp.zeros((N_HEADS, 1), F32)
    in_slab = key_lane % page_rows
    for kv in range(N_KV_HEADS):
        mine = row // GQA == kv
        s_kv = _dot3(q, kbuf[kv])
        s_s = jnp.where(mine, s_kv, s_s)
        s_n = jnp.where(mine, jnp.sum(q * new_ref[0, 2, kv:kv + 1, :], axis=-1, keepdims=True), s_n)
        base_kv = jnp.zeros((1, nkeys), jnp.int32)
        half_kv = jnp.zeros((1, nkeys), jnp.int32)
        own_kv = jnp.int32(0)
        for n in range(SLC_TOPK):
            blk = idx_ref[(b * N_KV_HEADS + kv) * SLC_TOPK + n]
            here = key_lane // page_rows == n
            base_kv = jnp.where(here, (blk // blocks_per_page) * page_rows, base_kv)
            half_kv = jnp.where(here, blk % blocks_per_page, half_kv)
            own_kv = own_kv + (blk >= n_past_blocks).astype(jnp.int32)
        pos = jnp.where(mine, base_kv + in_slab, pos)
        live = jnp.where(mine, (in_slab // SLC_BLOCK == half_kv).astype(jnp.int32), live)
        own = jnp.where(mine, own_kv, own)
    s_ok = (live > 0) & (pos < past)
    n_ok = own > 0
    s_s = jnp.where(s_ok, s_s * scale - slope * (past - pos).astype(F32), NEG)
    s_n = jnp.where(n_ok, s_n * scale, NEG)
    m_s = jnp.maximum(jnp.max(s_s, axis=-1, keepdims=True), s_n)
    p_s = jnp.where(s_ok, jnp.exp(s_s - m_s), 0.0)
    p_n = jnp.where(n_ok, jnp.exp(s_n - m_s), 0.0)
    l_s = jnp.maximum(jnp.sum(p_s, axis=-1, keepdims=True) + p_n, 1e-30)
    p_s = p_s / l_s
    p_n = p_n / l_s
    o_s = jnp.zeros((N_HEADS, HEAD_DIM), F32)
    for kv in range(N_KV_HEADS):
        o_kv = _dot3(p_s, vbuf[kv], nt=True) + p_n * new_ref[0, 3, kv:kv + 1, :]
        o_s = jnp.where(row // GQA == kv, o_kv, o_s)

    gt = jax.nn.sigmoid(gt_ref[0])
    o_ref[0] = gt[:, 0:1] * oc_ref[0] + gt[:, 1:2] * o_s + gt[:, 2:3] * o_w


def _attend_sample(page_table, idx, q16, o_c, gates, kv_new, win_t, cache_t, past):
    B = q16.shape[0]
    page_rows = cache_t.shape[2]
    assert win_t.shape[2] <= WINDOW and page_rows % SLC_BLOCK == 0
    blk3 = lambda a: pl.BlockSpec((1,) + a.shape[1:], lambda b, pt, ix: (b,) + (0,) * (a.ndim - 1))
    return pl.pallas_call(
        functools.partial(_attend_sample_kernel, past=past),
        grid_spec=pltpu.PrefetchScalarGridSpec(
            num_scalar_prefetch=2,
            grid=(B,),
            in_specs=[blk3(q16), blk3(o_c), blk3(gates), blk3(kv_new), blk3(win_t),
                      pl.BlockSpec(memory_space=pl.ANY)],
            out_specs=pl.BlockSpec((1, N_HEADS, HEAD_DIM), lambda b, pt, ix: (b, 0, 0)),
            scratch_shapes=[pltpu.VMEM((N_KV_HEADS, HEAD_DIM, SLC_TOPK * page_rows), F32),
                            pltpu.VMEM((N_KV_HEADS, HEAD_DIM, SLC_TOPK * page_rows), F32),
                            pltpu.SemaphoreType.DMA((2, N_KV_HEADS, SLC_TOPK))],
        ),
        out_shape=jax.ShapeDtypeStruct((B, N_HEADS, HEAD_DIM), F32),
        compiler_params=_cparams(("arbitrary",)),
        name="attend_sample",
    )(page_table, idx, q16, o_c, gates, kv_new, win_t, cache_t)


def _linear_res_kernel(x_ref, a_ref, w_ref, o_ref):
    o_ref[...] = x_ref[...] + _dot3(a_ref[...], w_ref[...])


def _linear_res(x, a, w):
    return pl.pallas_call(
        _linear_res_kernel,
        out_shape=jax.ShapeDtypeStruct(x.shape, F32),
        compiler_params=pltpu.CompilerParams(vmem_limit_bytes=VMEM_LIMIT),
        name="out_proj_sample",
    )(x, a, w)


def _expand_cmp_weights(w_cmp, rows):
    w = jnp.repeat(w_cmp, HEAD_DIM, axis=1)
    return jnp.tile(w, (rows // CMP_BLOCK, 1))


def _row(v):
    return v.reshape(1, -1)


def _prompt_mixers(x, norm_mix, w_pool, pool_scale, w_in, w_cmp_k, w_cmp_v, w_out, norm_ffn0, wfg, wfu, wfd):
    B, L, D = x.shape
    q_w = N_HEADS * HEAD_DIM
    x, h_last = _pool_prompt(x, _row(norm_mix[0]), w_pool.astype(BF16), _row(pool_scale))
    new_pool = h_last[None, :, POOL_HALO - (max(POOL_WINDOWS) - 1):, :]
    x = _ffn(x.reshape(B * L, D), _row(norm_ffn0), wfg.astype(BF16), wfu.astype(BF16), wfd.astype(BF16), tm=1024)

    tm = 512
    wq = w_in[:, :q_w].astype(BF16)
    wkv = w_in[:, q_w:q_w + 6 * KV_W].astype(BF16)
    wgt = jnp.pad(w_in[:, q_w + 6 * KV_W:], ((0, 0), (0, LANES - N_BRANCH * N_HEADS))).astype(BF16)
    q, kv_t, ks, vs, kw, vw, gates, kc, vc = _nsa_proj(
        x, _row(norm_mix[1]), wq, wkv, wgt, _expand_cmp_weights(w_cmp_k, tm), _expand_cmp_weights(w_cmp_v, tm),
        tm=tm, seq=L)
    kv_t = kv_t.reshape(B, 6, N_KV_HEADS, HEAD_DIM, L)
    new_kv = kv_t[:, :4].transpose(0, 4, 1, 2, 3)[None]
    new_win = kv_t[:, 4:, :, :, L - min(WINDOW, L):].transpose(0, 4, 1, 2, 3)[None]
    nc = L // CMP_BLOCK
    pad_c = lambda a: jnp.pad(a.reshape(B, nc, KV_W), ((0, 0), (0, LANES - nc), (0, 0))).astype(BF16)
    per_seq = lambda a: a.reshape(B, L, a.shape[-1])
    x = _nsa_prompt(per_seq(q), per_seq(gates), pad_c(kc), pad_c(vc), per_seq(ks), per_seq(vs),
                    per_seq(kw), per_seq(vw), per_seq(x), w_out.astype(BF16))
    return x.reshape(B * L, D), new_pool, new_kv, new_win


def _sample_mixers(x, state_pool, cache, state_win, page_table, norm_mix, w_pool, pool_scale, w_in, w_cmp_k,
                   w_cmp_v, w_out, norm_ffn0, wfg, wfu, wfd):
    SB, D = x.shape
    n_phys, page = cache.shape[:2]
    past = page_table.shape[1] * page
    q_w = N_HEADS * HEAD_DIM
    x, h = _pool_sample(x, state_pool.transpose(1, 0, 2), _row(norm_mix[0]), w_pool, _row(pool_scale))
    new_pool = jnp.concatenate([state_pool[:, 1:], h[:, None]], axis=1)[None]
    x = _ffn(x, _row(norm_ffn0), wfg, wfu, wfd, tm=SB)

    cols = w_in.shape[1]
    z = _nsa_proj_sample(x, _row(norm_mix[1]), jnp.pad(w_in, ((0, 0), (0, -cols % LANES))))
    kv_new = z[:, q_w:q_w + 6 * KV_W].reshape(SB, 6, N_KV_HEADS, HEAD_DIM)
    new_kv = kv_new[None, :, None, :4]
    keep = min(WINDOW, state_win.shape[1] + 1)
    new_win = jnp.concatenate([state_win, kv_new[:, None, 4:]], axis=1)[None, :, -keep:]

    cache_t = cache.transpose(0, 2, 3, 4, 1).reshape(n_phys, 4 * KV_W, page)
    win_t = state_win.transpose(0, 2, 3, 4, 1).reshape(SB, 2 * KV_W, state_win.shape[1])
    w_rows = jnp.concatenate([_expand_cmp_weights(w_cmp_k, page).T, _expand_cmp_weights(w_cmp_v, page).T], axis=0)
    kcv = _cmp_sample(page_table, cache_t, w_rows)
    q16 = z[:, :q_w].reshape(SB, N_HEADS, HEAD_DIM)
    o_c, idx = _select_sample(q16, kcv, past)
    idx = idx[:, GQA - 1::GQA, :SLC_TOPK].reshape(-1)
    gate_logits = z[:, q_w + 6 * KV_W:cols].reshape(SB, N_BRANCH, N_HEADS).transpose(0, 2, 1)
    gate_logits = jnp.pad(gate_logits, ((0, 0), (0, 0), (0, LANES - N_BRANCH)))
    o = _attend_sample(page_table, idx, q16, o_c, gate_logits, kv_new, win_t, cache_t, past)
    return _linear_res(x, o.reshape(SB, q_w), w_out), new_pool, new_kv, new_win


def kernel(x_prompt, x_sample, state_pool, cache_kv, state_win, page_table, norm_mix, w_pool, pool_scale,
           w_nsa_in, w_cmp_k, w_cmp_v, w_nsa_out, norm_ffn, w_ffn_gate, w_ffn_up, w_ffn_down, w_router,
           w_moe_gate, w_moe_up, w_moe_down, norm_final):
    assert x_sample.shape[1] == 1 and norm_mix.shape[0] == 2 and x_prompt.shape[-1] == N_HEADS * HEAD_DIM
    w_rt = jnp.pad(w_router[0], ((0, 0), (0, LANES - w_router.shape[-1])))
    moe_w = (w_moe_gate[0].astype(BF16), w_moe_up[0].astype(BF16), w_moe_down[0].astype(BF16))

    xp, new_pool_prompt, new_kv_prompt, new_win_prompt = _prompt_mixers(
        x_prompt, norm_mix, w_pool[0], pool_scale[0], w_nsa_in[0], w_cmp_k[0], w_cmp_v[0], w_nsa_out[0],
        norm_ffn[0], w_ffn_gate, w_ffn_up, w_ffn_down)
    xs, new_pool_sample, new_kv_sample, new_win_sample = _sample_mixers(
        x_sample[:, 0], state_pool[0], cache_kv[0], state_win[0], page_table, norm_mix, w_pool[0], pool_scale[0],
        w_nsa_in[0], w_cmp_k[0], w_cmp_v[0], w_nsa_out[0], norm_ffn[0], w_ffn_gate, w_ffn_up, w_ffn_down)

    yp = _moe_prompt(xp, _row(norm_ffn[1]), w_rt, *moe_w, _row(norm_final))
    ys = _ffn(xs, _row(norm_ffn[1]), *moe_w, w_rt, _row(norm_final), tm=xs.shape[0])
    return (yp.reshape(x_prompt.shape), ys.reshape(x_sample.shape), new_pool_prompt, new_pool_sample,
            new_kv_prompt, new_kv_sample, new_win_prompt, new_win_sample)
```

```python
import functools

import jax
import jax.numpy as jnp
from jax import lax
from jax.experimental import pallas as pl
from jax.experimental.pallas import tpu as pltpu

F32 = jnp.float32
BF16 = jnp.bfloat16

EPS = 1e-6
NEG = -1e30
POOL_WINDOWS = (2, 4, 8, 16)
POOL_HALO = 16
N_HEADS = 16
N_KV_HEADS = 4
GQA = N_HEADS // N_KV_HEADS
HEAD_DIM = 64
KV_W = N_KV_HEADS * HEAD_DIM
CMP_BLOCK = 32
SLC_BLOCK = 64
SLC_TOPK = 16
WINDOW = 512
N_BRANCH = 3
FORCE_BONUS = 1000.0
TOP_K = 2
LANES = 128
VMEM_LIMIT = 56 * 1024 * 1024


def _cparams(sem):
    return pltpu.CompilerParams(dimension_semantics=sem, vmem_limit_bytes=VMEM_LIMIT)


def _rmsnorm(x, g):
    ms = jnp.mean(x * x, axis=-1, keepdims=True)
    return (x * lax.rsqrt(ms + EPS)) * g


def _dot(a, b):
    return jnp.dot(a, b, preferred_element_type=F32)


def _dot_nt(a, b):
    return lax.dot_general(a, b, (((1,), (1,)), ((), ())), preferred_element_type=F32)


def _split(a):
    hi = a.astype(BF16)
    return hi, (a - hi.astype(F32)).astype(BF16)


def _dot3(a, b, nt=False):
    d = _dot_nt if nt else _dot
    ah, al = _split(a)
    bh, bl = _split(b)
    return d(ah, bh) + (d(ah, bl) + d(al, bh))


def _masked_softmax(s, mask):
    s = jnp.where(mask, s, NEG)
    m = jnp.max(s, axis=-1, keepdims=True)
    p = jnp.where(mask, jnp.exp(s - m), 0.0)
    return p / jnp.maximum(jnp.sum(p, axis=-1, keepdims=True), 1e-30)


def _alibi_slope(h):
    return 2.0 ** (-8.0 * (h + 1) / N_HEADS)


def _head_slopes(rows, rows_per_head, first_head):
    h = lax.broadcasted_iota(jnp.int32, (rows, 1), 0) // rows_per_head
    slopes = jnp.zeros((rows, 1), F32)
    for j in range(rows // rows_per_head):
        slopes = jnp.where(h == j, _alibi_slope(first_head + j), slopes)
    return slopes


def _pool_prompt_kernel(x_ref, halo_ref, g_ref, w_ref, sc_ref, o_ref, hl_ref, full_ref, *, tile):
    t = pl.program_id(1)
    g = g_ref[...]
    x = x_ref[0]
    h = _rmsnorm(x, g)
    hh = _rmsnorm(halo_ref[0], g)
    full_ref[0:POOL_HALO, :] = jnp.where(t > 0, hh, 0.0)
    full_ref[POOL_HALO:POOL_HALO + tile, :] = h
    row = t * tile + lax.broadcasted_iota(jnp.int32, (tile, 1), 0)
    group = x.shape[-1] // len(POOL_WINDOWS)
    parts = []
    for gi, w in enumerate(POOL_WINDOWS):
        cs = slice(gi * group, (gi + 1) * group)
        hg = h[:, cs]
        acc = hg
        for k in range(1, w):
            acc = acc + full_ref[POOL_HALO - k:POOL_HALO - k + tile, cs]
        cnt = jnp.minimum(row + 1, w).astype(F32)
        pooled = acc / cnt - hg
        parts.append(_dot(pooled.astype(BF16), w_ref[gi]))
    o_ref[0] = x + jnp.concatenate(parts, axis=-1) * sc_ref[...]

    @pl.when(t == pl.num_programs(1) - 1)
    def _():
        hl_ref[0] = h[tile - POOL_HALO:, :]


def _pool_prompt(x, g, w_pool, scale, tile=512):
    B, L, D = x.shape
    assert L % tile == 0 and tile % POOL_HALO == 0
    hb = tile // POOL_HALO
    return pl.pallas_call(
        functools.partial(_pool_prompt_kernel, tile=tile),
        grid=(B, L // tile),
        in_specs=[
            pl.BlockSpec((1, tile, D), lambda b, t: (b, t, 0)),
            pl.BlockSpec((1, POOL_HALO, D), lambda b, t: (b, jnp.maximum(t * hb - 1, 0), 0)),
            pl.BlockSpec((1, D), lambda b, t: (0, 0)),
            pl.BlockSpec(w_pool.shape, lambda b, t: (0, 0, 0)),
            pl.BlockSpec((1, D), lambda b, t: (0, 0)),
        ],
        out_specs=[
            pl.BlockSpec((1, tile, D), lambda b, t: (b, t, 0)),
            pl.BlockSpec((1, POOL_HALO, D), lambda b, t: (b, 0, 0)),
        ],
        out_shape=[jax.ShapeDtypeStruct((B, L, D), F32), jax.ShapeDtypeStruct((B, POOL_HALO, D), F32)],
        scratch_shapes=[pltpu.VMEM((tile + POOL_HALO, D), F32)],
        compiler_params=_cparams(("parallel", "arbitrary")),
        name="pool_prompt",
    )(x, x, g, w_pool, scale)


def _pool_sample_kernel(x_ref, st_ref, g_ref, w_ref, sc_ref, o_ref, h_ref):
    x = x_ref[...]
    h = _rmsnorm(x, g_ref[...])
    h_ref[...] = h
    P = st_ref.shape[0]
    group = x.shape[-1] // len(POOL_WINDOWS)
    parts = []
    for gi, w in enumerate(POOL_WINDOWS):
        cs = slice(gi * group, (gi + 1) * group)
        hg = h[:, cs]
        acc = hg
        for k in range(1, w):
            acc = acc + st_ref[P - k][:, cs]
        pooled = acc / float(w) - hg
        parts.append(_dot3(pooled, w_ref[gi]))
    o_ref[...] = x + jnp.concatenate(parts, axis=-1) * sc_ref[...]


def _pool_sample(x, state_t, g, w_pool, scale):
    B, D = x.shape
    assert state_t.shape[0] >= max(POOL_WINDOWS) - 1
    return pl.pallas_call(
        _pool_sample_kernel,
        out_shape=[jax.ShapeDtypeStruct((B, D), F32), jax.ShapeDtypeStruct((B, D), F32)],
        compiler_params=pltpu.CompilerParams(vmem_limit_bytes=VMEM_LIMIT),
        name="pool_sample",
    )(x, state_t, g, w_pool, scale)


def _ffn_kernel(*refs, n_experts, final_norm, precise):
    moe = n_experts > 1
    it = iter(refs)
    x_ref, g_ref = next(it), next(it)
    wr_ref = next(it) if moe else None
    wg_ref, wu_ref, wd_ref = next(it), next(it), next(it)
    gf_ref = next(it) if final_norm else None
    o_ref, h_scr, acc_scr = next(it), next(it), next(it)
    eacc_scr, gate_scr = (next(it), next(it)) if moe else (None, None)

    e, f = pl.program_id(1), pl.program_id(2)
    last_f = f == pl.num_programs(2) - 1

    @pl.when((e == 0) & (f == 0))
    def _():
        h = _rmsnorm(x_ref[...], g_ref[...])
        h_scr[...] = h.astype(h_scr.dtype)
        acc_scr[...] = jnp.zeros_like(acc_scr)
        if moe:
            logits = jnp.dot(h, wr_ref[...], preferred_element_type=F32, precision=lax.Precision.HIGHEST)
            lane, i1, i2, w1, w2 = _top2_gates(logits, n_experts)
            gate_scr[...] = jnp.where(lane == i1, w1, 0.0) + jnp.where(lane == i2, w2, 0.0)

    mm = _dot3 if precise else _dot
    hb = h_scr[...]
    a = mm(hb, wg_ref[0])
    u = mm(hb, wu_ref[0])
    act = (a * jax.nn.sigmoid(a)) * u
    y = mm(act.astype(hb.dtype), wd_ref[0])

    if moe:
        @pl.when(f == 0)
        def _():
            eacc_scr[...] = y

        @pl.when(f > 0)
        def _():
            eacc_scr[...] += y

        @pl.when(last_f)
        def _():
            gate = gate_scr[...]
            lane = lax.broadcasted_iota(jnp.int32, gate.shape, 1)
            ge = jnp.sum(jnp.where(lane == e, gate, 0.0), axis=-1, keepdims=True)
            acc_scr[...] += ge * eacc_scr[...]
    else:
        acc_scr[...] += y

    @pl.when((e == pl.num_programs(1) - 1) & last_f)
    def _():
        out = x_ref[...] + acc_scr[...]
        if final_norm:
            out = _rmsnorm(out, gf_ref[...])
        o_ref[...] = out


def _ffn(x, g, wg, wu, wd, w_router=None, g_final=None, *, tm, tf=512):
    precise = wg.dtype == F32
    N, D = x.shape
    E, _, F = wg.shape
    assert N % tm == 0 and F % tf == 0
    moe = w_router is not None
    assert moe == (E > 1)
    final_norm = g_final is not None
    const2 = lambda i, e, f: (0, 0)
    args, in_specs = [x, g], [pl.BlockSpec((tm, D), lambda i, e, f: (i, 0)), pl.BlockSpec((1, D), const2)]
    if moe:
        args.append(w_router)
        in_specs.append(pl.BlockSpec(w_router.shape, const2))
    args += [wg, wu, wd]
    in_specs += [
        pl.BlockSpec((1, D, tf), lambda i, e, f: (e, 0, f)),
        pl.BlockSpec((1, D, tf), lambda i, e, f: (e, 0, f)),
        pl.BlockSpec((1, tf, D), lambda i, e, f: (e, f, 0)),
    ]
    if final_norm:
        args.append(g_final)
        in_specs.append(pl.BlockSpec((1, D), const2))
    scratch = [pltpu.VMEM((tm, D), F32 if precise else BF16), pltpu.VMEM((tm, D), F32)]
    if moe:
        scratch += [pltpu.VMEM((tm, D), F32), pltpu.VMEM((tm, LANES), F32)]
    return pl.pallas_call(
        functools.partial(_ffn_kernel, n_experts=E, final_norm=final_norm, precise=precise),
        grid=(N // tm, E, F // tf),
        in_specs=in_specs,
        out_specs=pl.BlockSpec((tm, D), lambda i, e, f: (i, 0)),
        out_shape=jax.ShapeDtypeStruct((N, D), F32),
        scratch_shapes=scratch,
        compiler_params=_cparams(("parallel", "arbitrary", "arbitrary")),
        name="moe_ffn" if moe else "dense_ffn",
    )(*args)


ROUTE_I1, ROUTE_I2, ROUTE_W1, ROUTE_W2, ROUTE_R1, ROUTE_R2 = range(6)


def _top2_gates(logits, n_experts):
    lane = lax.broadcasted_iota(jnp.int32, logits.shape, 1)
    lg = jnp.where(lane < n_experts, logits, -jnp.inf)
    m1 = jnp.max(lg, axis=-1, keepdims=True)
    i1 = jnp.min(jnp.where(lg == m1, lane, LANES), axis=-1, keepdims=True)
    lg2 = jnp.where(lane == i1, -jnp.inf, lg)
    m2 = jnp.max(lg2, axis=-1, keepdims=True)
    i2 = jnp.min(jnp.where(lg2 == m2, lane, LANES), axis=-1, keepdims=True)
    e2 = jnp.exp(m2 - m1)
    den = 1.0 + e2
    return lane, i1, i2, 1.0 / den, e2 / den


def _moe_route_kernel(x_ref, g_ref, wr_ref, tri_ref, route_ref, cnt_ref, carry, *, n_experts):
    @pl.when(pl.program_id(0) == 0)
    def _():
        carry[...] = jnp.zeros_like(carry)

    h = _rmsnorm(x_ref[...], g_ref[...])
    logits = jnp.dot(h, wr_ref[...], preferred_element_type=F32, precision=lax.Precision.HIGHEST)
    lane, i1, i2, w1, w2 = _top2_gates(logits, n_experts)
    member = jnp.where((lane == i1) | (lane == i2), 1.0, 0.0)
    before = _dot(tri_ref[...], member.astype(BF16)) + carry[0:1, :]
    r1 = jnp.sum(jnp.where(lane == i1, before, 0.0), axis=-1, keepdims=True)
    r2 = jnp.sum(jnp.where(lane == i2, before, 0.0), axis=-1, keepdims=True)
    rec = jnp.zeros(logits.shape, F32)
    for col, val in ((ROUTE_I1, i1.astype(F32)), (ROUTE_I2, i2.astype(F32)), (ROUTE_W1, w1), (ROUTE_W2, w2),
                     (ROUTE_R1, r1), (ROUTE_R2, r2)):
        rec = jnp.where(lane == col, val, rec)
    route_ref[...] = rec
    carry[...] = carry[...] + jnp.sum(member, axis=0, keepdims=True)
    cnt_ref[...] = carry[...]


def _moe_route(x, g, w_router, n_experts, tm=512):
    N, D = x.shape
    assert N % tm == 0
    tri = (jnp.arange(tm)[:, None] > jnp.arange(tm)[None, :]).astype(BF16)
    return pl.pallas_call(
        functools.partial(_moe_route_kernel, n_experts=n_experts),
        grid=(N // tm,),
        in_specs=[pl.BlockSpec((tm, D), lambda i: (i, 0)), pl.BlockSpec((1, D), lambda i: (0, 0)),
                  pl.BlockSpec(w_router.shape, lambda i: (0, 0)), pl.BlockSpec((tm, tm), lambda i: (0, 0))],
        out_specs=[pl.BlockSpec((tm, LANES), lambda i: (i, 0)), pl.BlockSpec((8, LANES), lambda i: (0, 0))],
        out_shape=[jax.ShapeDtypeStruct((N, LANES), F32), jax.ShapeDtypeStruct((8, LANES), F32)],
        scratch_shapes=[pltpu.VMEM((8, LANES), F32)],
        compiler_params=_cparams(("arbitrary",)),
        name="moe_route",
    )(x, g, w_router, tri)


GATHER_UNROLL = 8
GATHER_PRIORITY = 1


def _row_gather(idx_of, src_hbm, dst_at, sem, rows, priority):
    assert rows % GATHER_UNROLL == 0

    def copy(r):
        return pltpu.make_async_copy(src_hbm.at[pl.ds(idx_of(r), 1), :], dst_at(r), sem)

    def start():
        def group(i, c):
            for j in range(GATHER_UNROLL):
                copy(i * GATHER_UNROLL + j).start(priority=priority(j))
            return c
        lax.fori_loop(0, rows // GATHER_UNROLL, group, 0)

    def wait():
        lax.fori_loop(0, rows, lambda r, c: (copy(r).wait(), c)[1], 0, unroll=GATHER_UNROLL)

    return start, wait, copy


def _moe_experts_kernel(te_ref, nu_ref, tos_ref, x_hbm, g_ref, wg_ref, wu_ref, wd_ref, o_ref,
                        xbuf, h_scr, acc_scr, sems, *, tile, nf):
    i, f = pl.program_id(0), pl.program_id(1)
    used = i < nu_ref[0]
    slot = i % 2
    per_step = tile // nf

    def gather(t, sl):
        return _row_gather(lambda r: tos_ref[t * tile + r], x_hbm,
                           lambda r: xbuf.at[sl, pl.ds(r, 1), :], sems.at[sl], tile, lambda j: GATHER_PRIORITY)

    def prefetch_share():
        row_copy = gather(i + 1, 1 - slot)[2]
        for j in range(per_step):
            row_copy(f * per_step + j).start(priority=GATHER_PRIORITY)

    @pl.when((f == 0) & (i == 0))
    def _():
        gather(0, 0)[0]()

    @pl.when(f == 0)
    def _():
        gather(i, slot)[1]()

    @pl.when((f == 0) & used)
    def _():
        h_scr[...] = _rmsnorm(xbuf[slot], g_ref[...]).astype(BF16)
        acc_scr[...] = jnp.zeros_like(acc_scr)

    @pl.when(used)
    def _():
        prefetch_share()
        hb = h_scr[...]
        a = _dot(hb, wg_ref[0])
        u = _dot(hb, wu_ref[0])
        act = (a * jax.nn.sigmoid(a)) * u
        acc_scr[...] += _dot(act.astype(BF16), wd_ref[0])

    @pl.when(jnp.logical_not(used))
    def _():
        prefetch_share()

    @pl.when(f == nf - 1)
    def _():
        o_ref[...] = jnp.where(used, acc_scr[...], 0.0)

    @pl.when((f == nf - 1) & (i == pl.num_programs(0) - 1))
    def _():
        gather(i + 1, 1 - slot)[1]()


def _moe_experts(x, g, wg, wu, wd, tile_expert, n_used, token_of_slot, *, tile, tf=512):
    N, D = x.shape
    E, _, F = wg.shape
    n_slots = token_of_slot.shape[0] - tile
    assert n_slots % tile == 0 and F % tf == 0
    n_tiles, nf = n_slots // tile, F // tf
    assert tile % nf == 0
    fidx = lambda i, f, nu: jnp.where(i < nu[0], f, nf - 1)
    return pl.pallas_call(
        functools.partial(_moe_experts_kernel, tile=tile, nf=nf),
        grid_spec=pltpu.PrefetchScalarGridSpec(
            num_scalar_prefetch=3,
            grid=(n_tiles, nf),
            in_specs=[pl.BlockSpec(memory_space=pl.ANY),
                      pl.BlockSpec((1, D), lambda i, f, te, nu, tos: (0, 0)),
                      pl.BlockSpec((1, D, tf), lambda i, f, te, nu, tos: (te[i], 0, fidx(i, f, nu))),
                      pl.BlockSpec((1, D, tf), lambda i, f, te, nu, tos: (te[i], 0, fidx(i, f, nu))),
                      pl.BlockSpec((1, tf, D), lambda i, f, te, nu, tos: (te[i], fidx(i, f, nu), 0))],
            out_specs=pl.BlockSpec((tile, D), lambda i, f, te, nu, tos: (i, 0)),
            scratch_shapes=[pltpu.VMEM((2, tile, D), F32), pltpu.VMEM((tile, D), BF16),
                            pltpu.VMEM((tile, D), F32), pltpu.SemaphoreType.DMA((2,))],
        ),
        out_shape=jax.ShapeDtypeStruct((n_slots, D), F32),
        compiler_params=_cparams(("arbitrary", "arbitrary")),
        name="moe_experts",
    )(tile_expert, n_used, token_of_slot, x, g, wg, wu, wd)


def _moe_combine_kernel(slot_ref, x_ref, route_ref, gf_ref, ys_hbm, o_ref, ybuf, sems, *, tile):
    i = pl.program_id(0)
    slot = i % 2

    def gather(t, sl):
        return _row_gather(lambda j: slot_ref[t * 2 * tile + j], ys_hbm,
                           lambda j: ybuf.at[sl, pl.ds(j, 1), :], sems.at[sl], 2 * tile, lambda j: j % 2)

    @pl.when(i == 0)
    def _():
        gather(0, 0)[0]()

    gather(i, slot)[1]()

    @pl.when(i + 1 < pl.num_programs(0))
    def _():
        gather(i + 1, 1 - slot)[0]()

    route = route_ref[...]
    w1, w2 = route[:, ROUTE_W1:ROUTE_W1 + 1], route[:, ROUTE_W2:ROUTE_W2 + 1]
    y = w1 * ybuf[slot, 0:tile, :] + w2 * ybuf[slot, tile:2 * tile, :]
    o_ref[...] = _rmsnorm(x_ref[...] + y, gf_ref[...])


def _moe_combine(x, route, g_final, ys, slots, *, tile=256):
    N, D = x.shape
    assert N % tile == 0
    return pl.pallas_call(
        functools.partial(_moe_combine_kernel, tile=tile),
        grid_spec=pltpu.PrefetchScalarGridSpec(
            num_scalar_prefetch=1,
            grid=(N // tile,),
            in_specs=[pl.BlockSpec((tile, D), lambda i, s: (i, 0)),
                      pl.BlockSpec((tile, LANES), lambda i, s: (i, 0)),
                      pl.BlockSpec((1, D), lambda i, s: (0, 0)),
                      pl.BlockSpec(memory_space=pl.ANY)],
            out_specs=pl.BlockSpec((tile, D), lambda i, s: (i, 0)),
            scratch_shapes=[pltpu.VMEM((2, 2 * tile, D), F32), pltpu.SemaphoreType.DMA((2,))],
        ),
        out_shape=jax.ShapeDtypeStruct((N, D), F32),
        compiler_params=_cparams(("arbitrary",)),
        name="moe_combine",
    )(slots, x, route, g_final, ys)


def _moe_prompt(x, g, w_router, wg, wu, wd, g_final, *, tile=896, ctile=256):
    N, D = x.shape
    E = wg.shape[0]
    route, counts = _moe_route(x, g, w_router, E)
    cnt = counts[0, :E].astype(jnp.int32)
    padded = (cnt + tile - 1) // tile * tile
    ends = jnp.cumsum(padded)
    off = ends - padded
    i1, i2 = route[:, ROUTE_I1].astype(jnp.int32), route[:, ROUTE_I2].astype(jnp.int32)
    slot1 = off[i1] + route[:, ROUTE_R1].astype(jnp.int32)
    slot2 = off[i2] + route[:, ROUTE_R2].astype(jnp.int32)
    n_slots = (TOP_K * N + E * (tile - 1)) // tile * tile
    rows = jnp.arange(N, dtype=jnp.int32)
    token_of_slot = jnp.zeros((n_slots + tile,), jnp.int32).at[jnp.concatenate([slot1, slot2])].set(
        jnp.concatenate([rows, rows]), unique_indices=True)
    n_used = ends[-1:] // tile
    tile_start = jnp.minimum(jnp.arange(n_slots // tile, dtype=jnp.int32), n_used[0] - 1) * tile
    tile_expert = jnp.sum(tile_start[:, None] >= ends[None, :], axis=1).astype(jnp.int32)
    ys = _moe_experts(x, g, wg, wu, wd, tile_expert, n_used.astype(jnp.int32), token_of_slot, tile=tile)
    slots = jnp.stack([slot1.reshape(-1, ctile), slot2.reshape(-1, ctile)], axis=1).reshape(-1)
    return _moe_combine(x, route, g_final, ys, slots, tile=ctile)


def _nsa_proj_kernel(x_ref, g_ref, wq_ref, wkv_ref, wgt_ref, wck_ref, wcv_ref,
                     q_ref, kvt_ref, ks_ref, vs_ref, kw_ref, vw_ref, gt_ref, kc_ref, vc_ref):
    hb = _rmsnorm(x_ref[...], g_ref[...]).astype(BF16)
    q_ref[...] = _dot(hb, wq_ref[...]).astype(BF16)
    kv = _dot(hb, wkv_ref[...])
    kvt_ref[0] = kv.T
    ks_ref[...] = kv[:, 2 * KV_W:3 * KV_W].astype(BF16)
    vs_ref[...] = kv[:, 3 * KV_W:4 * KV_W].astype(BF16)
    kw_ref[...] = kv[:, 4 * KV_W:5 * KV_W].astype(BF16)
    vw_ref[...] = kv[:, 5 * KV_W:].astype(BF16)
    gt_ref[...] = jax.nn.sigmoid(_dot(hb, wgt_ref[...]))
    tm = kv.shape[0]
    kc_ref[...] = (kv[:, :KV_W] * wck_ref[...]).reshape(tm // CMP_BLOCK, CMP_BLOCK, KV_W).sum(axis=1)
    vc_ref[...] = (kv[:, KV_W:2 * KV_W] * wcv_ref[...]).reshape(tm // CMP_BLOCK, CMP_BLOCK, KV_W).sum(axis=1)


def _nsa_proj(x, g, wq, wkv, wgt, wck, wcv, *, tm, seq):
    N, D = x.shape
    assert N % tm == 0 and tm % (8 * CMP_BLOCK) == 0 and seq % tm == 0 and N % seq == 0
    per_seq = seq // tm
    row = lambda w: pl.BlockSpec((tm, w), lambda i: (i, 0))
    full = lambda a: pl.BlockSpec(a.shape, lambda i: (0, 0))
    out_shape = [jax.ShapeDtypeStruct((N, N_HEADS * HEAD_DIM), BF16),
                 jax.ShapeDtypeStruct((N // seq, 6 * KV_W, seq), F32)]
    out_shape += [jax.ShapeDtypeStruct((N, KV_W), BF16)] * 4
    out_shape += [jax.ShapeDtypeStruct((N, LANES), F32)]
    out_shape += [jax.ShapeDtypeStruct((N // CMP_BLOCK, KV_W), F32)] * 2
    out_specs = [row(N_HEADS * HEAD_DIM),
                 pl.BlockSpec((1, 6 * KV_W, tm), lambda i: (i // per_seq, 0, i % per_seq))]
    out_specs += [row(KV_W)] * 4 + [row(LANES)]
    out_specs += [pl.BlockSpec((tm // CMP_BLOCK, KV_W), lambda i: (i, 0))] * 2
    return pl.pallas_call(
        _nsa_proj_kernel,
        grid=(N // tm,),
        in_specs=[row(D), full(g), full(wq), full(wkv), full(wgt), full(wck), full(wcv)],
        out_specs=out_specs,
        out_shape=out_shape,
        compiler_params=_cparams(("parallel",)),
        name="nsa_proj_prompt",
    )(x, g, wq, wkv, wgt, wck, wcv)


def _nsa_proj_sample_kernel(x_ref, g_ref, w_ref, z_ref):
    z_ref[...] = _dot3(_rmsnorm(x_ref[...], g_ref[...]), w_ref[...])


def _nsa_proj_sample(x, g, w_in, tn=384):
    B, D = x.shape
    cols = w_in.shape[1]
    assert cols % tn == 0
    return pl.pallas_call(
        _nsa_proj_sample_kernel,
        grid=(cols // tn,),
        in_specs=[pl.BlockSpec((B, D), lambda j: (0, 0)), pl.BlockSpec((1, D), lambda j: (0, 0)),
                  pl.BlockSpec((D, tn), lambda j: (0, j))],
        out_specs=pl.BlockSpec((B, tn), lambda j: (0, j)),
        out_shape=jax.ShapeDtypeStruct((B, cols), F32),
        compiler_params=_cparams(("parallel",)),
        name="nsa_proj_sample",
    )(x, g, w_in)


def _nsa_prompt_kernel(q_ref, gt_ref, kc_ref, vc_ref, ks_ref, vs_ref, kw_ref, vw_ref, x_ref, wo_ref,
                       o_ref, o_scr, sel_scr, *, tq, kchunk, seq):
    t0 = pl.program_id(1) * tq
    rows = GQA * tq
    scale = HEAD_DIM ** -0.5
    assert scale == 0.125
    n_blocks = seq // SLC_BLOCK
    top_k = min(SLC_TOPK, n_blocks)

    q = q_ref[0]
    gates = gt_ref[0]
    qpos_t = t0 + lax.broadcasted_iota(jnp.int32, (tq, 1), 0)
    qpos = jnp.concatenate([qpos_t] * GQA, axis=0)
    tq_f = qpos.astype(F32)

    lane = lax.broadcasted_iota(jnp.int32, (tq, LANES), 1)
    jblk = lane >> 1
    real = ((lane & 1) == 0) & (jblk < n_blocks)
    blk = qpos_t // SLC_BLOCK
    valid = real & (jblk <= blk)
    forced = (jblk == 0) | (jblk == blk) | (jblk == blk - 1)
    valid_f = jnp.where(valid, 1.0, 0.0)
    valid_b = valid_f.astype(BF16)
    bonus = jnp.where(forced, FORCE_BONUS, 0.0)
    floor = jnp.where(valid, 0.0, jnp.where(real, -1.0, -2.0))
    c_lane = lax.broadcasted_iota(jnp.int32, (1, LANES), 1)
    c_mid = c_lane.astype(F32) * CMP_BLOCK + (CMP_BLOCK - 1) / 2
    c_end = (c_lane + 1) * CMP_BLOCK - 1

    n_chunks = (t0 + tq + kchunk - 1) // kchunk
    wstart = pl.multiple_of(jnp.maximum(t0 - WINDOW, 0), tq)
    wlen = WINDOW + tq

    need_rank = (t0 + tq - 1) // SLC_BLOCK >= top_k
    wpos = wstart + lax.broadcasted_iota(jnp.int32, (1, wlen), 1)
    w_dist = jnp.where((wpos <= qpos_t) & (qpos_t - wpos < WINDOW), (wpos - qpos_t).astype(F32), NEG)

    def with_ones(v):
        return jnp.concatenate([v, jnp.ones_like(v)], axis=1)

    def biased(scores, dist, kv):
        return jnp.concatenate(
            [scores[g * tq:(g + 1) * tq] + _alibi_slope(kv * GQA + g) * dist for g in range(GQA)], axis=0)

    for kv in range(N_KV_HEADS):
        hs = slice(kv * HEAD_DIM, (kv + 1) * HEAD_DIM)
        qs = jnp.concatenate(
            [q[:, (kv * GQA + g) * HEAD_DIM:(kv * GQA + g + 1) * HEAD_DIM] for g in range(GQA)], axis=0) * scale
        slope = _head_slopes(rows, tq, kv * GQA)

        s_c = _dot_nt(qs, kc_ref[0][:, hs]) - slope * (tq_f - c_mid)
        p_c = _masked_softmax(s_c, c_end <= qpos)
        o_c = _dot(p_c.astype(BF16), vc_ref[0][:, hs])

        imp = p_c[0:tq]
        for g in range(1, GQA):
            imp = imp + p_c[g * tq:(g + 1) * tq]
        sel_scr[...] = valid_b

        @pl.when(need_rank)
        def _():
            pair = imp + pltpu.roll(imp, LANES - 1, 1)
            score = valid_f * (pair + bonus) + floor
            score_t = score.T[:2 * n_blocks]
            row_id = lax.broadcasted_iota(jnp.int32, score_t.shape, 0)
            rank_t = jnp.zeros(score_t.shape, jnp.int32)
            for k in range(n_blocks):
                other = score_t[2 * k:2 * k + 1, :]
                beats = (other > score_t) | ((other == score_t) & (row_id > 2 * k))
                rank_t = rank_t + beats.astype(jnp.int32)
            top_t = jnp.where(rank_t < top_k, 1.0, 0.0)
            top = jnp.concatenate([top_t, jnp.zeros((LANES - 2 * n_blocks, tq), F32)], axis=0).T
            sel_scr[...] = (top * valid_f).astype(BF16)

        sel = sel_scr[...]

        def chunk(c, carry):
            m, acc = carry
            k0 = pl.multiple_of(c * kchunk, kchunk)
            kb = ks_ref[0, pl.ds(k0, kchunk), hs]
            vb = with_ones(vs_ref[0, pl.ds(k0, kchunk), hs])
            pos = k0 + lax.broadcasted_iota(jnp.int32, (1, kchunk), 1)
            expand = (lax.broadcasted_iota(jnp.int32, (LANES, kchunk), 0)
                      == 2 * ((k0 + lax.broadcasted_iota(jnp.int32, (LANES, kchunk), 1)) // SLC_BLOCK))
            picked = _dot(sel, jnp.where(expand, 1.0, 0.0).astype(BF16))
            ok = (picked > 0.5) & (pos <= qpos_t)
            s = biased(_dot_nt(qs, kb), jnp.where(ok, (pos - qpos_t).astype(F32), NEG), kv)
            m_new = jnp.maximum(m, jnp.max(s, axis=-1, keepdims=True))
            p = jnp.exp(s - m_new)
            acc = jnp.exp(m - m_new) * acc + _dot(p.astype(BF16), vb)
            return m_new, acc

        init = (jnp.full((rows, 1), NEG, F32), jnp.zeros((rows, 2 * HEAD_DIM), F32))
        _, acc_s = lax.fori_loop(0, n_chunks, chunk, init)
        o_s = acc_s[:, :HEAD_DIM] / acc_s[:, HEAD_DIM:HEAD_DIM + 1]

        kb = kw_ref[0, pl.ds(wstart, wlen), hs]
        vb = with_ones(vw_ref[0, pl.ds(wstart, wlen), hs])
        s_w = biased(_dot_nt(qs, kb), w_dist, kv)
        p_w = jnp.exp(s_w - jnp.max(s_w, axis=-1, keepdims=True))
        pv = _dot(p_w.astype(BF16), vb)
        o_w = pv[:, :HEAD_DIM] / pv[:, HEAD_DIM:HEAD_DIM + 1]

        def gate_col(n):
            base = n * N_HEADS + kv * GQA
            return jnp.concatenate([gates[:, base + g:base + g + 1] for g in range(GQA)], axis=0)

        o = gate_col(0) * o_c + gate_col(1) * o_s + gate_col(2) * o_w
        for g in range(GQA):
            h = kv * GQA + g
            o_scr[:, h * HEAD_DIM:(h + 1) * HEAD_DIM] = o[g * tq:(g + 1) * tq]

    o_ref[0] = x_ref[0] + _dot(o_scr[...].astype(BF16), wo_ref[...])


def _nsa_prompt(q, gates, kc, vc, ks, vs, kw, vw, x, w_o, *, tq=256, kchunk=512):
    B, L, D = x.shape
    assert L % kchunk == 0 and kchunk % tq == 0 and L >= WINDOW + tq and L % SLC_BLOCK == 0
    assert 2 * (L // SLC_BLOCK) <= LANES and L // CMP_BLOCK <= LANES and kc.shape[1] == LANES
    tile = lambda w: pl.BlockSpec((1, tq, w), lambda b, t: (b, t, 0))
    whole = lambda a: pl.BlockSpec((1,) + a.shape[1:], lambda b, t: (b, 0, 0))
    return pl.pallas_call(
        functools.partial(_nsa_prompt_kernel, tq=tq, kchunk=kchunk, seq=L),
        grid=(B, L // tq),
        in_specs=[tile(N_HEADS * HEAD_DIM), tile(LANES), whole(kc), whole(vc), whole(ks), whole(vs),
                  whole(kw), whole(vw), tile(D), pl.BlockSpec(w_o.shape, lambda b, t: (0, 0))],
        out_specs=tile(D),
        out_shape=jax.ShapeDtypeStruct((B, L, D), F32),
        scratch_shapes=[pltpu.VMEM((tq, N_HEADS * HEAD_DIM), F32), pltpu.VMEM((tq, LANES), BF16)],
        compiler_params=_cparams(("parallel", "arbitrary")),
        name="nsa_attn_prompt",
    )(q, gates, kc, vc, ks, vs, kw, vw, x, w_o)


CMP_GROUP = 32
MXU_DEPTH = 256


def _cmp_sample_kernel(pt_ref, cache_ref, w_ref, s_ref, o_ref, buf, sems, *, rows):
    b, g = pl.program_id(0), pl.program_id(1)
    n_groups = pl.num_programs(1)
    step = b * n_groups + g
    slot = step % 2

    def page_copy(bb, gg, sl, i):
        page = pt_ref[bb, gg * CMP_GROUP + i]
        return pltpu.make_async_copy(cache_ref.at[page, pl.ds(0, rows), :], buf.at[sl, i], sems.at[sl])

    def start_group(bb, gg, sl):
        for i in range(CMP_GROUP):
            page_copy(bb, gg, sl, i).start()

    @pl.when(step == 0)
    def _():
        start_group(b, g, slot)

    @pl.when(step + 1 < pl.num_programs(0) * n_groups)
    def _():
        nxt = step + 1
        start_group(nxt // n_groups, nxt % n_groups, 1 - slot)

    for i in range(CMP_GROUP):
        page_copy(b, g, slot, i).wait()

    w = w_ref[...]
    acc = jnp.zeros((rows, LANES), F32)
    for j in range(CMP_GROUP // 2):
        p = jnp.concatenate([buf[slot, 2 * j] * w, buf[slot, 2 * j + 1] * w], axis=1)
        hi, lo = _split(p)
        r = _dot(jnp.concatenate([hi, lo], axis=0), s_ref[j])
        acc = acc + (r[:rows] + r[rows:])
    o_ref[0] = acc


def _cmp_sample(page_table, cache_t, w_rows):
    B, n_pages = page_table.shape
    _, _, page = cache_t.shape
    rows = w_rows.shape[0]
    per_page = page // CMP_BLOCK
    assert n_pages % CMP_GROUP == 0 and CMP_GROUP * per_page == LANES and 2 * page == MXU_DEPTH
    k = jnp.arange(2 * page)
    token = (2 * jnp.arange(CMP_GROUP // 2)[:, None] + k[None, :] // page) * per_page + (k[None, :] % page) // CMP_BLOCK
    block_sum = (token[:, :, None] == jnp.arange(LANES)[None, None, :]).astype(BF16)
    return pl.pallas_call(
        functools.partial(_cmp_sample_kernel, rows=rows),
        grid_spec=pltpu.PrefetchScalarGridSpec(
            num_scalar_prefetch=1,
            grid=(B, n_pages // CMP_GROUP),
            in_specs=[pl.BlockSpec(memory_space=pl.ANY),
                      pl.BlockSpec(w_rows.shape, lambda b, g, pt: (0, 0)),
                      pl.BlockSpec(block_sum.shape, lambda b, g, pt: (0, 0, 0))],
            out_specs=pl.BlockSpec((1, rows, LANES), lambda b, g, pt: (b, 0, g)),
            scratch_shapes=[pltpu.VMEM((2, CMP_GROUP, rows, page), F32), pltpu.SemaphoreType.DMA((2,))],
        ),
        out_shape=jax.ShapeDtypeStruct((B, rows, n_pages * per_page), F32),
        compiler_params=_cparams(("arbitrary", "arbitrary")),
        name="cmp_sample",
    )(page_table, cache_t, w_rows, block_sum)


def _select_sample_kernel(q_ref, kcv_ref, oc_ref, idx_ref, *, past):
    scale = HEAD_DIM ** -0.5
    q = q_ref[0]
    nc = kcv_ref.shape[2]
    n_past_blocks = past // SLC_BLOCK
    per_block = SLC_BLOCK // CMP_BLOCK
    row = lax.broadcasted_iota(jnp.int32, (N_HEADS, 1), 0)
    slope = _head_slopes(N_HEADS, 1, 0)
    tok = lax.broadcasted_iota(jnp.int32, (1, nc), 1)
    c_mid = tok.astype(F32) * CMP_BLOCK + (CMP_BLOCK - 1) / 2
    c_ok = (tok + 1) * CMP_BLOCK - 1 <= past
    bias = slope * (float(past) - c_mid)

    s_c = jnp.zeros((N_HEADS, nc), F32)
    for kv in range(N_KV_HEADS):
        s_kv = _dot3(q, kcv_ref[0, kv * HEAD_DIM:(kv + 1) * HEAD_DIM, :])
        s_c = jnp.where(row // GQA == kv, s_kv, s_c)
    p_c = _masked_softmax(s_c * scale - bias, c_ok)
    o_c = jnp.zeros((N_HEADS, HEAD_DIM), F32)
    for kv in range(N_KV_HEADS):
        o_kv = _dot3(p_c, kcv_ref[0, KV_W + kv * HEAD_DIM:KV_W + (kv + 1) * HEAD_DIM, :], nt=True)
        o_c = jnp.where(row // GQA == kv, o_kv, o_c)
    oc_ref[0] = o_c

    imp = p_c
    shift = 1
    while shift < GQA:
        imp = imp + pltpu.roll(imp, shift, 0)
        shift *= 2
    assert per_block == 2
    pair = imp + pltpu.roll(imp, nc - 1, 1)
    jblk = tok // per_block
    real = tok % per_block == 0
    forced = (jblk == 0) | (jblk == n_past_blocks - 1)
    score = jnp.where(real, pair + jnp.where(forced, FORCE_BONUS, 0.0), -2.0)
    own = jnp.float32(FORCE_BONUS)
    rank = (own > score).astype(jnp.int32)
    for k in range(n_past_blocks):
        col = score[:, per_block * k:per_block * k + 1]
        beats = (col > score) | ((col == score) & (jblk > k))
        rank = rank + beats.astype(jnp.int32)
    own_rank = jnp.sum(jnp.where(real & (score >= own), 1, 0), axis=-1, keepdims=True)
    out_lane = lax.broadcasted_iota(jnp.int32, (N_HEADS, LANES), 1)
    idx = jnp.zeros((N_HEADS, LANES), jnp.int32)
    for r in range(SLC_TOPK):
        hit = jnp.sum(jnp.where(real & (rank == r), jblk, 0), axis=-1, keepdims=True)
        hit = hit + jnp.where(own_rank == r, n_past_blocks, 0)
        idx = jnp.where(out_lane == r, hit, idx)
    idx_ref[0] = idx


def _select_sample(q16, kcv, past):
    B = q16.shape[0]
    nc = kcv.shape[2]
    assert past % SLC_BLOCK == 0 and nc == past // CMP_BLOCK and nc % LANES == 0
    assert past // SLC_BLOCK + 1 > SLC_TOPK
    return pl.pallas_call(
        functools.partial(_select_sample_kernel, past=past),
        grid=(B,),
        in_specs=[pl.BlockSpec((1,) + q16.shape[1:], lambda b: (b, 0, 0)),
                  pl.BlockSpec((1,) + kcv.shape[1:], lambda b: (b, 0, 0))],
        out_specs=[pl.BlockSpec((1, N_HEADS, HEAD_DIM), lambda b: (b, 0, 0)),
                   pl.BlockSpec((1, N_HEADS, LANES), lambda b: (b, 0, 0))],
        out_shape=[jax.ShapeDtypeStruct((B, N_HEADS, HEAD_DIM), F32),
                   jax.ShapeDtypeStruct((B, N_HEADS, LANES), jnp.int32)],
        compiler_params=_cparams(("parallel",)),
        name="select_sample",
    )(q16, kcv)


def _attend_sample_kernel(pt_ref, idx_ref, q_ref, oc_ref, gt_ref, new_ref, win_ref, cache_ref,
                          o_ref, kbuf, vbuf, sems, *, past):
    b = pl.program_id(0)
    scale = HEAD_DIM ** -0.5
    n_past_blocks = past // SLC_BLOCK
    page_rows = cache_ref.shape[2]
    blocks_per_page = page_rows // SLC_BLOCK
    nkeys = SLC_TOPK * page_rows

    def block_copies(kv, n):
        blk = idx_ref[(b * N_KV_HEADS + kv) * SLC_TOPK + n]
        in_past = blk < n_past_blocks
        page = pt_ref[b, jnp.minimum(blk, n_past_blocks - 1) // blocks_per_page]
        dst = pl.ds(n * page_rows, page_rows)
        ck = pltpu.make_async_copy(cache_ref.at[page, pl.ds((2 * N_KV_HEADS + kv) * HEAD_DIM, HEAD_DIM), :],
                                   kbuf.at[kv, :, dst], sems.at[0, kv, n])
        cv = pltpu.make_async_copy(cache_ref.at[page, pl.ds((3 * N_KV_HEADS + kv) * HEAD_DIM, HEAD_DIM), :],
                                   vbuf.at[kv, :, dst], sems.at[1, kv, n])
        return blk, in_past, ck, cv

    for kv in range(N_KV_HEADS):
        for n in range(SLC_TOPK):
            _, in_past, ck, cv = block_copies(kv, n)

            @pl.when(in_past)
            def _():
                ck.start()
                cv.start()

            @pl.when(jnp.logical_not(in_past))
            def _():
                kbuf[kv, :, n * page_rows:(n + 1) * page_rows] = jnp.zeros((HEAD_DIM, page_rows), F32)
                vbuf[kv, :, n * page_rows:(n + 1) * page_rows] = jnp.zeros((HEAD_DIM, page_rows), F32)

    q = q_ref[0]
    row = lax.broadcasted_iota(jnp.int32, (N_HEADS, 1), 0)
    slope = _head_slopes(N_HEADS, 1, 0)
    key_lane = lax.broadcasted_iota(jnp.int32, (1, nkeys), 1)

    wb = win_ref.shape[2]
    w_lane = lax.broadcasted_iota(jnp.int32, (1, wb), 1)
    wpos = past - wb + w_lane
    w_ok = (past - wpos < WINDOW) & (wpos >= 0)
    w_bias = slope * (past - wpos).astype(F32)
    s_w = jnp.zeros((N_HEADS, wb), F32)
    s_n = jnp.zeros((N_HEADS, 1), F32)
    for kv in range(N_KV_HEADS):
        mine = row // GQA == kv
        s_kv = _dot3(q, win_ref[0, kv * HEAD_DIM:(kv + 1) * HEAD_DIM, :])
        s_w = jnp.where(mine, s_kv, s_w)
        s_n = jnp.where(mine, jnp.sum(q * new_ref[0, 4, kv:kv + 1, :], axis=-1, keepdims=True), s_n)
    s_w = jnp.where(w_ok, s_w * scale - w_bias, NEG)
    s_n = s_n * scale
    m_w = jnp.maximum(jnp.max(s_w, axis=-1, keepdims=True), s_n)
    p_w = jnp.where(w_ok, jnp.exp(s_w - m_w), 0.0)
    p_n = jnp.exp(s_n - m_w)
    l_w = jnp.maximum(jnp.sum(p_w, axis=-1, keepdims=True) + p_n, 1e-30)
    p_w = p_w / l_w
    p_n = p_n / l_w
    o_w = jnp.zeros((N_HEADS, HEAD_DIM), F32)
    for kv in range(N_KV_HEADS):
        o_kv = (_dot3(p_w, win_ref[0, KV_W + kv * HEAD_DIM:KV_W + (kv + 1) * HEAD_DIM, :], nt=True)
                + p_n * new_ref[0, 5, kv:kv + 1, :])
        o_w = jnp.where(row // GQA == kv, o_kv, o_w)

    for kv in range(N_KV_HEADS):
        for n in range(SLC_TOPK):
            _, in_past, ck, cv = block_copies(kv, n)

            @pl.when(in_past)
            def _():
                ck.wait()
                cv.wait()

    s_s = jnp.zeros((N_HEADS, nkeys), F32)
    pos = jnp.zeros((N_HEADS, nkeys), jnp.int32)
    live = jnp.zeros((N_HEADS, nkeys), jnp.int32)
    own = jnp.zeros((N_HEADS, 1), jnp.int32)
    s_n = jnp.zeros((N_HEADS, 1), F32)
    in_slab = key_lane % page_rows
    for kv in range(N_KV_HEADS):
        mine = row // GQA == kv
        s_kv = _dot3(q, kbuf[kv])
        s_s = jnp.where(mine, s_kv, s_s)
        s_n = jnp.where(mine, jnp.sum(q * new_ref[0, 2, kv:kv + 1, :], axis=-1, keepdims=True), s_n)
        base_kv = jnp.zeros((1, nkeys), jnp.int32)
        half_kv = jnp.zeros((1, nkeys), jnp.int32)
        own_kv = jnp.int32(0)
        for n in range(SLC_TOPK):
            blk = idx_ref[(b * N_KV_HEADS + kv) * SLC_TOPK + n]
            here = key_lane // page_rows == n
            base_kv = jnp.where(here, (blk // blocks_per_page) * page_rows, base_kv)
            half_kv = jnp.where(here, blk % blocks_per_page, half_kv)
            own_kv = own_kv + (blk >= n_past_blocks).astype(jnp.int32)
        pos = jnp.where(mine, base_kv + in_slab, pos)
        live = jnp.where(mine, (in_slab // SLC_BLOCK == half_kv).astype(jnp.int32), live)
        own = jnp.where(mine, own_kv, own)
    s_ok = (live > 0) & (pos < past)
    n_ok = own > 0
    s_s = jnp.where(s_ok, s_s * scale - slope * (past - pos).astype(F32), NEG)
    s_n = jnp.where(n_ok, s_n * scale, NEG)
    m_s = jnp.maximum(jnp.max(s_s, axis=-1, keepdims=True), s_n)
    p_s = jnp.where(s_ok, jnp.exp(s_s - m_s), 0.0)
    p_n = jnp.where(n_ok, jnp.exp(s_n - m_s), 0.0)
    l_s = jnp.maximum(jnp.sum(p_s, axis=-1, keepdims=True) + p_n, 1e-30)
    p_s = p_s / l_s
    p_n = p_n / l_s
    o_s = jnp.zeros((N_HEADS, HEAD_DIM), F32)
    for kv in range(N_KV_HEADS):
        o_kv = _dot3(p_s, vbuf[kv], nt=True) + p_n * new_ref[0, 3, kv:kv + 1, :]
        o_s = jnp.where(row // GQA == kv, o_kv, o_s)

    gt = jax.nn.sigmoid(gt_ref[0])
    o_ref[0] = gt[:, 0:1] * oc_ref[0] + gt[:, 1:2] * o_s + gt[:, 2:3] * o_w


def _attend_sample(page_table, idx, q16, o_c, gates, kv_new, win_t, cache_t, past):
    B = q16.shape[0]
    page_rows = cache_t.shape[2]
    assert win_t.shape[2] <= WINDOW and page_rows % SLC_BLOCK == 0
    blk3 = lambda a: pl.BlockSpec((1,) + a.shape[1:], lambda b, pt, ix: (b,) + (0,) * (a.ndim - 1))
    return pl.pallas_call(
        functools.partial(_attend_sample_kernel, past=past),
        grid_spec=pltpu.PrefetchScalarGridSpec(
            num_scalar_prefetch=2,
            grid=(B,),
            in_specs=[blk3(q16), blk3(o_c), blk3(gates), blk3(kv_new), blk3(win_t),
                      pl.BlockSpec(memory_space=pl.ANY)],
            out_specs=pl.BlockSpec((1, N_HEADS, HEAD_DIM), lambda b, pt, ix: (b, 0, 0)),
            scratch_shapes=[pltpu.VMEM((N_KV_HEADS, HEAD_DIM, SLC_TOPK * page_rows), F32),
                            pltpu.VMEM((N_KV_HEADS, HEAD_DIM, SLC_TOPK * page_rows), F32),
                            pltpu.SemaphoreType.DMA((2, N_KV_HEADS, SLC_TOPK))],
        ),
        out_shape=jax.ShapeDtypeStruct((B, N_HEADS, HEAD_DIM), F32),
        compiler_params=_cparams(("arbitrary",)),
        name="attend_sample",
    )(page_table, idx, q16, o_c, gates, kv_new, win_t, cache_t)


def _linear_res_kernel(x_ref, a_ref, w_ref, o_ref):
    o_ref[...] = x_ref[...] + _dot3(a_ref[...], w_ref[...])


def _linear_res(x, a, w):
    return pl.pallas_call(
        _linear_res_kernel,
        out_shape=jax.ShapeDtypeStruct(x.shape, F32),
        compiler_params=pltpu.CompilerParams(vmem_limit_bytes=VMEM_LIMIT),
        name="out_proj_sample",
    )(x, a, w)


def _expand_cmp_weights(w_cmp, rows):
    w = jnp.repeat(w_cmp, HEAD_DIM, axis=1)
    return jnp.tile(w, (rows // CMP_BLOCK, 1))


def _row(v):
    return v.reshape(1, -1)


def _prompt_mixers(x, norm_mix, w_pool, pool_scale, w_in, w_cmp_k, w_cmp_v, w_out, norm_ffn0, wfg, wfu, wfd):
    B, L, D = x.shape
    q_w = N_HEADS * HEAD_DIM
    x, h_last = _pool_prompt(x, _row(norm_mix[0]), w_pool.astype(BF16), _row(pool_scale))
    new_pool = h_last[None, :, POOL_HALO - (max(POOL_WINDOWS) - 1):, :]
    x = _ffn(x.reshape(B * L, D), _row(norm_ffn0), wfg.astype(BF16), wfu.astype(BF16), wfd.astype(BF16), tm=1024)

    tm = 512
    wq = w_in[:, :q_w].astype(BF16)
    wkv = w_in[:, q_w:q_w + 6 * KV_W].astype(BF16)
    wgt = jnp.pad(w_in[:, q_w + 6 * KV_W:], ((0, 0), (0, LANES - N_BRANCH * N_HEADS))).astype(BF16)
    q, kv_t, ks, vs, kw, vw, gates, kc, vc = _nsa_proj(
        x, _row(norm_mix[1]), wq, wkv, wgt, _expand_cmp_weights(w_cmp_k, tm), _expand_cmp_weights(w_cmp_v, tm),
        tm=tm, seq=L)
    kv_t = kv_t.reshape(B, 6, N_KV_HEADS, HEAD_DIM, L)
    new_kv = kv_t[:, :4].transpose(0, 4, 1, 2, 3)[None]
    new_win = kv_t[:, 4:, :, :, L - min(WINDOW, L):].transpose(0, 4, 1, 2, 3)[None]
    nc = L // CMP_BLOCK
    pad_c = lambda a: jnp.pad(a.reshape(B, nc, KV_W), ((0, 0), (0, LANES - nc), (0, 0))).astype(BF16)
    per_seq = lambda a: a.reshape(B, L, a.shape[-1])
    x = _nsa_prompt(per_seq(q), per_seq(gates), pad_c(kc), pad_c(vc), per_seq(ks), per_seq(vs),
                    per_seq(kw), per_seq(vw), per_seq(x), w_out.astype(BF16))
    return x.reshape(B * L, D), new_pool, new_kv, new_win


def _sample_mixers(x, state_pool, cache, state_win, page_table, norm_mix, w_pool, pool_scale, w_in, w_cmp_k,
                   w_cmp_v, w_out, norm_ffn0, wfg, wfu, wfd):
    SB, D = x.shape
    n_phys, page = cache.shape[:2]
    past = page_table.shape[1] * page
    q_w = N_HEADS * HEAD_DIM
    x, h = _pool_sample(x, state_pool.transpose(1, 0, 2), _row(norm_mix[0]), w_pool, _row(pool_scale))
    new_pool = jnp.concatenate([state_pool[:, 1:], h[:, None]], axis=1)[None]
    x = _ffn(x, _row(norm_ffn0), wfg, wfu, wfd, tm=SB)

    cols = w_in.shape[1]
    z = _nsa_proj_sample(x, _row(norm_mix[1]), jnp.pad(w_in, ((0, 0), (0, -cols % LANES))))
    kv_new = z[:, q_w:q_w + 6 * KV_W].reshape(SB, 6, N_KV_HEADS, HEAD_DIM)
    new_kv = kv_new[None, :, None, :4]
    keep = min(WINDOW, state_win.shape[1] + 1)
    new_win = jnp.concatenate([state_win, kv_new[:, None, 4:]], axis=1)[None, :, -keep:]

    cache_t = cache.transpose(0, 2, 3, 4, 1).reshape(n_phys, 4 * KV_W, page)
    win_t = state_win.transpose(0, 2, 3, 4, 1).reshape(SB, 2 * KV_W, state_win.shape[1])
    w_rows = jnp.concatenate([_expand_cmp_weights(w_cmp_k, page).T, _expand_cmp_weights(w_cmp_v, page).T], axis=0)
    kcv = _cmp_sample(page_table, cache_t, w_rows)
    q16 = z[:, :q_w].reshape(SB, N_HEADS, HEAD_DIM)
    o_c, idx = _select_sample(q16, kcv, past)
    idx = idx[:, GQA - 1::GQA, :SLC_TOPK].reshape(-1)
    gate_logits = z[:, q_w + 6 * KV_W:cols].reshape(SB, N_BRANCH, N_HEADS).transpose(0, 2, 1)
    gate_logits = jnp.pad(gate_logits, ((0, 0), (0, 0), (0, LANES - N_BRANCH)))
    o = _attend_sample(page_table, idx, q16, o_c, gate_logits, kv_new, win_t, cache_t, past)
    return _linear_res(x, o.reshape(SB, q_w), w_out), new_pool, new_kv, new_win


def kernel(x_prompt, x_sample, state_pool, cache_kv, state_win, page_table, norm_mix, w_pool, pool_scale,
           w_nsa_in, w_cmp_k, w_cmp_v, w_nsa_out, norm_ffn, w_ffn_gate, w_ffn_up, w_ffn_down, w_router,
           w_moe_gate, w_moe_up, w_moe_down, norm_final):
    assert x_sample.shape[1] == 1 and norm_mix.shape[0] == 2 and x_prompt.shape[-1] == N_HEADS * HEAD_DIM
    w_rt = jnp.pad(w_router[0], ((0, 0), (0, LANES - w_router.shape[-1])))
    moe_w = (w_moe_gate[0].astype(BF16), w_moe_up[0].astype(BF16), w_moe_down[0].astype(BF16))

    xp, new_pool_prompt, new_kv_prompt, new_win_prompt = _prompt_mixers(
        x_prompt, norm_mix, w_pool[0], pool_scale[0], w_nsa_in[0], w_cmp_k[0], w_cmp_v[0], w_nsa_out[0],
        norm_ffn[0], w_ffn_gate, w_ffn_up, w_ffn_down)
    xs, new_pool_sample, new_kv_sample, new_win_sample = _sample_mixers(
        x_sample[:, 0], state_pool[0], cache_kv[0], state_win[0], page_table, norm_mix, w_pool[0], pool_scale[0],
        w_nsa_in[0], w_cmp_k[0], w_cmp_v[0], w_nsa_out[0], norm_ffn[0], w_ffn_gate, w_ffn_up, w_ffn_down)

    yp = _moe_prompt(xp, _row(norm_ffn[1]), w_rt, *moe_w, _row(norm_final))
    ys = _ffn(xs, _row(norm_ffn[1]), *moe_w, w_rt, _row(norm_final), tm=xs.shape[0])
    return (yp.reshape(x_prompt.shape), ys.reshape(x_sample.shape), new_pool_prompt, new_pool_sample,
            new_kv_prompt, new_kv_sample, new_win_prompt, new_win_sample)
```

```python
import functools

import jax
import jax.numpy as jnp
from jax import lax
from jax.experimental import pallas as pl
from jax.experimental.pallas import tpu as pltpu

F32 = jnp.float32
BF16 = jnp.bfloat16

EPS = 1e-6
NEG = -1e30
POOL_WINDOWS = (2, 4, 8, 16)
POOL_HALO = 16
N_HEADS = 16
N_KV_HEADS = 4
GQA = N_HEADS // N_KV_HEADS
HEAD_DIM = 64
KV_W = N_KV_HEADS * HEAD_DIM
CMP_BLOCK = 32
SLC_BLOCK = 64
SLC_TOPK = 16
WINDOW = 512
N_BRANCH = 3
FORCE_BONUS = 1000.0
TOP_K = 2
LANES = 128
VMEM_LIMIT = 56 * 1024 * 1024

POOL_ROW_TILE = 512
FFN_ROW_TILE = 1024
FFN_COL_TILE = 512
PROJ_ROW_TILE = 512
ATTN_Q_TILE = 256
ATTN_KEY_CHUNK = 512
ROUTE_ROW_TILE = 512
EXPERT_ROW_TILE = 896
COMBINE_ROW_TILE = 256
SAMPLE_PROJ_COL_TILE = 384


def _cparams(sem):
    return pltpu.CompilerParams(dimension_semantics=sem, vmem_limit_bytes=VMEM_LIMIT)


def _rmsnorm(x, g):
    ms = jnp.mean(x * x, axis=-1, keepdims=True)
    return (x * lax.rsqrt(ms + EPS)) * g


def _dot(a, b):
    return jnp.dot(a, b, preferred_element_type=F32)


def _dot_nt(a, b):
    return lax.dot_general(a, b, (((1,), (1,)), ((), ())), preferred_element_type=F32)


def _split(a):
    hi = a.astype(BF16)
    return hi, (a - hi.astype(F32)).astype(BF16)


def _dot3(a, b, nt=False):
    d = _dot_nt if nt else _dot
    ah, al = _split(a)
    bh, bl = _split(b)
    return d(ah, bh) + (d(ah, bl) + d(al, bh))


def _masked_softmax(s, mask):
    s = jnp.where(mask, s, NEG)
    m = jnp.max(s, axis=-1, keepdims=True)
    p = jnp.where(mask, jnp.exp(s - m), 0.0)
    return p / jnp.maximum(jnp.sum(p, axis=-1, keepdims=True), 1e-30)


def _alibi_slope(h):
    return 2.0 ** (-8.0 * (h + 1) / N_HEADS)


def _head_slopes(rows, rows_per_head, first_head):
    h = lax.broadcasted_iota(jnp.int32, (rows, 1), 0) // rows_per_head
    slopes = jnp.zeros((rows, 1), F32)
    for j in range(rows // rows_per_head):
        slopes = jnp.where(h == j, _alibi_slope(first_head + j), slopes)
    return slopes


def _pool_prompt_kernel(x_ref, halo_ref, g_ref, w_ref, sc_ref, o_ref, hl_ref, full_ref, *, tile):
    t = pl.program_id(1)
    g = g_ref[...]
    x = x_ref[0]
    h = _rmsnorm(x, g)
    hh = _rmsnorm(halo_ref[0], g)
    full_ref[0:POOL_HALO, :] = jnp.where(t > 0, hh, 0.0)
    full_ref[POOL_HALO:POOL_HALO + tile, :] = h
    row = t * tile + lax.broadcasted_iota(jnp.int32, (tile, 1), 0)
    group = x.shape[-1] // len(POOL_WINDOWS)
    parts = []
    for gi, w in enumerate(POOL_WINDOWS):
        cs = slice(gi * group, (gi + 1) * group)
        hg = h[:, cs]
        acc = hg
        for k in range(1, w):
            acc = acc + full_ref[POOL_HALO - k:POOL_HALO - k + tile, cs]
        cnt = jnp.minimum(row + 1, w).astype(F32)
        pooled = acc / cnt - hg
        parts.append(_dot(pooled.astype(BF16), w_ref[gi]))
    o_ref[0] = x + jnp.concatenate(parts, axis=-1) * sc_ref[...]

    @pl.when(t == pl.num_programs(1) - 1)
    def _():
        hl_ref[0] = h[tile - POOL_HALO:, :]


def _pool_prompt(x, g, w_pool, scale, tile=POOL_ROW_TILE):
    B, L, D = x.shape
    assert L % tile == 0 and tile % POOL_HALO == 0
    hb = tile // POOL_HALO
    return pl.pallas_call(
        functools.partial(_pool_prompt_kernel, tile=tile),
        grid=(B, L // tile),
        in_specs=[
            pl.BlockSpec((1, tile, D), lambda b, t: (b, t, 0)),
            pl.BlockSpec((1, POOL_HALO, D), lambda b, t: (b, jnp.maximum(t * hb - 1, 0), 0)),
            pl.BlockSpec((1, D), lambda b, t: (0, 0)),
            pl.BlockSpec(w_pool.shape, lambda b, t: (0, 0, 0)),
            pl.BlockSpec((1, D), lambda b, t: (0, 0)),
        ],
        out_specs=[
            pl.BlockSpec((1, tile, D), lambda b, t: (b, t, 0)),
            pl.BlockSpec((1, POOL_HALO, D), lambda b, t: (b, 0, 0)),
        ],
        out_shape=[jax.ShapeDtypeStruct((B, L, D), F32), jax.ShapeDtypeStruct((B, POOL_HALO, D), F32)],
        scratch_shapes=[pltpu.VMEM((tile + POOL_HALO, D), F32)],
        compiler_params=_cparams(("parallel", "arbitrary")),
        name="pool_prompt",
    )(x, x, g, w_pool, scale)


def _pool_sample_kernel(x_ref, st_ref, g_ref, w_ref, sc_ref, o_ref, h_ref):
    x = x_ref[...]
    h = _rmsnorm(x, g_ref[...])
    h_ref[...] = h
    P = st_ref.shape[0]
    group = x.shape[-1] // len(POOL_WINDOWS)
    parts = []
    for gi, w in enumerate(POOL_WINDOWS):
        cs = slice(gi * group, (gi + 1) * group)
        hg = h[:, cs]
        acc = hg
        for k in range(1, w):
            acc = acc + st_ref[P - k][:, cs]
        pooled = acc / float(w) - hg
        parts.append(_dot3(pooled, w_ref[gi]))
    o_ref[...] = x + jnp.concatenate(parts, axis=-1) * sc_ref[...]


def _pool_sample(x, state_t, g, w_pool, scale):
    B, D = x.shape
    assert state_t.shape[0] >= max(POOL_WINDOWS) - 1
    return pl.pallas_call(
        _pool_sample_kernel,
        out_shape=[jax.ShapeDtypeStruct((B, D), F32), jax.ShapeDtypeStruct((B, D), F32)],
        compiler_params=pltpu.CompilerParams(vmem_limit_bytes=VMEM_LIMIT),
        name="pool_sample",
    )(x, state_t, g, w_pool, scale)


def _ffn_kernel(*refs, n_experts, final_norm, precise):
    moe = n_experts > 1
    it = iter(refs)
    x_ref, g_ref = next(it), next(it)
    wr_ref = next(it) if moe else None
    wg_ref, wu_ref, wd_ref = next(it), next(it), next(it)
    gf_ref = next(it) if final_norm else None
    o_ref, h_scr, acc_scr = next(it), next(it), next(it)
    eacc_scr, gate_scr = (next(it), next(it)) if moe else (None, None)

    e, f = pl.program_id(1), pl.program_id(2)
    last_f = f == pl.num_programs(2) - 1

    @pl.when((e == 0) & (f == 0))
    def _():
        h = _rmsnorm(x_ref[...], g_ref[...])
        h_scr[...] = h.astype(h_scr.dtype)
        acc_scr[...] = jnp.zeros_like(acc_scr)
        if moe:
            logits = _dot3(h, wr_ref[...])
            lane, i1, i2, w1, w2 = _top2_gates(logits, n_experts)
            gate_scr[...] = jnp.where(lane == i1, w1, 0.0) + jnp.where(lane == i2, w2, 0.0)

    mm = _dot3 if precise else _dot
    hb = h_scr[...]
    a = mm(hb, wg_ref[0])
    u = mm(hb, wu_ref[0])
    act = (a * jax.nn.sigmoid(a)) * u
    y = mm(act.astype(hb.dtype), wd_ref[0])

    if moe:
        @pl.when(f == 0)
        def _():
            eacc_scr[...] = y

        @pl.when(f > 0)
        def _():
            eacc_scr[...] += y

        @pl.when(last_f)
        def _():
            gate = gate_scr[...]
            lane = lax.broadcasted_iota(jnp.int32, gate.shape, 1)
            ge = jnp.sum(jnp.where(lane == e, gate, 0.0), axis=-1, keepdims=True)
            acc_scr[...] += ge * eacc_scr[...]
    else:
        acc_scr[...] += y

    @pl.when((e == pl.num_programs(1) - 1) & last_f)
    def _():
        out = x_ref[...] + acc_scr[...]
        if final_norm:
            out = _rmsnorm(out, gf_ref[...])
        o_ref[...] = out


def _ffn(x, g, wg, wu, wd, w_router=None, g_final=None, *, tm, tf=FFN_COL_TILE):
    precise = wg.dtype == F32
    N, D = x.shape
    E, _, F = wg.shape
    assert N % tm == 0 and F % tf == 0
    moe = w_router is not None
    assert moe == (E > 1)
    final_norm = g_final is not None
    const2 = lambda i, e, f: (0, 0)
    args, in_specs = [x, g], [pl.BlockSpec((tm, D), lambda i, e, f: (i, 0)), pl.BlockSpec((1, D), const2)]
    if moe:
        args.append(w_router)
        in_specs.append(pl.BlockSpec(w_router.shape, const2))
    args += [wg, wu, wd]
    in_specs += [
        pl.BlockSpec((1, D, tf), lambda i, e, f: (e, 0, f)),
        pl.BlockSpec((1, D, tf), lambda i, e, f: (e, 0, f)),
        pl.BlockSpec((1, tf, D), lambda i, e, f: (e, f, 0)),
    ]
    if final_norm:
        args.append(g_final)
        in_specs.append(pl.BlockSpec((1, D), const2))
    scratch = [pltpu.VMEM((tm, D), F32 if precise else BF16), pltpu.VMEM((tm, D), F32)]
    if moe:
        scratch += [pltpu.VMEM((tm, D), F32), pltpu.VMEM((tm, LANES), F32)]
    return pl.pallas_call(
        functools.partial(_ffn_kernel, n_experts=E, final_norm=final_norm, precise=precise),
        grid=(N // tm, E, F // tf),
        in_specs=in_specs,
        out_specs=pl.BlockSpec((tm, D), lambda i, e, f: (i, 0)),
        out_shape=jax.ShapeDtypeStruct((N, D), F32),
        scratch_shapes=scratch,
        compiler_params=_cparams(("parallel", "arbitrary", "arbitrary")),
        name="moe_ffn" if moe else "dense_ffn",
    )(*args)


ROUTE_I1, ROUTE_I2, ROUTE_W1, ROUTE_W2, ROUTE_R1, ROUTE_R2 = range(6)


def _top2_gates(logits, n_experts):
    lane = lax.broadcasted_iota(jnp.int32, logits.shape, 1)
    lg = jnp.where(lane < n_experts, logits, -jnp.inf)
    m1 = jnp.max(lg, axis=-1, keepdims=True)
    i1 = jnp.min(jnp.where(lg == m1, lane, LANES), axis=-1, keepdims=True)
    lg2 = jnp.where(lane == i1, -jnp.inf, lg)
    m2 = jnp.max(lg2, axis=-1, keepdims=True)
    i2 = jnp.min(jnp.where(lg2 == m2, lane, LANES), axis=-1, keepdims=True)
    e2 = jnp.exp(m2 - m1)
    den = 1.0 + e2
    return lane, i1, i2, 1.0 / den, e2 / den


def _moe_route_kernel(x_ref, g_ref, wr_ref, tri_ref, route_ref, cnt_ref, carry, *, n_experts):
    @pl.when(pl.program_id(0) == 0)
    def _():
        carry[...] = jnp.zeros_like(carry)

    h = _rmsnorm(x_ref[...], g_ref[...])
    logits = _dot3(h, wr_ref[...])
    lane, i1, i2, w1, w2 = _top2_gates(logits, n_experts)
    member = jnp.where((lane == i1) | (lane == i2), 1.0, 0.0)
    before = _dot(tri_ref[...], member.astype(BF16)) + carry[0:1, :]
    r1 = jnp.sum(jnp.where(lane == i1, before, 0.0), axis=-1, keepdims=True)
    r2 = jnp.sum(jnp.where(lane == i2, before, 0.0), axis=-1, keepdims=True)
    rec = jnp.zeros(logits.shape, F32)
    for col, val in ((ROUTE_I1, i1.astype(F32)), (ROUTE_I2, i2.astype(F32)), (ROUTE_W1, w1), (ROUTE_W2, w2),
                     (ROUTE_R1, r1), (ROUTE_R2, r2)):
        rec = jnp.where(lane == col, val, rec)
    route_ref[...] = rec
    carry[...] = carry[...] + jnp.sum(member, axis=0, keepdims=True)
    cnt_ref[...] = carry[...]


def _moe_route(x, g, w_router, n_experts, tm=ROUTE_ROW_TILE):
    N, D = x.shape
    assert N % tm == 0
    tri = (jnp.arange(tm)[:, None] > jnp.arange(tm)[None, :]).astype(BF16)
    return pl.pallas_call(
        functools.partial(_moe_route_kernel, n_experts=n_experts),
        grid=(N // tm,),
        in_specs=[pl.BlockSpec((tm, D), lambda i: (i, 0)), pl.BlockSpec((1, D), lambda i: (0, 0)),
                  pl.BlockSpec(w_router.shape, lambda i: (0, 0)), pl.BlockSpec((tm, tm), lambda i: (0, 0))],
        out_specs=[pl.BlockSpec((tm, LANES), lambda i: (i, 0)), pl.BlockSpec((8, LANES), lambda i: (0, 0))],
        out_shape=[jax.ShapeDtypeStruct((N, LANES), F32), jax.ShapeDtypeStruct((8, LANES), F32)],
        scratch_shapes=[pltpu.VMEM((8, LANES), F32)],
        compiler_params=_cparams(("arbitrary",)),
        name="moe_route",
    )(x, g, w_router, tri)


GATHER_UNROLL = 8
GATHER_PRIORITY = 1


def _row_gather(idx_of, src_hbm, dst_at, sem, rows, priority):
    assert rows % GATHER_UNROLL == 0

    def copy(r):
        return pltpu.make_async_copy(src_hbm.at[pl.ds(idx_of(r), 1), :], dst_at(r), sem)

    def start():
        def group(i, c):
            for j in range(GATHER_UNROLL):
                copy(i * GATHER_UNROLL + j).start(priority=priority(j))
            return c
        lax.fori_loop(0, rows // GATHER_UNROLL, group, 0)

    def wait():
        lax.fori_loop(0, rows, lambda r, c: (copy(r).wait(), c)[1], 0, unroll=GATHER_UNROLL)

    return start, wait, copy


def _moe_experts_kernel(te_ref, nu_ref, tos_ref, x_hbm, g_ref, wg_ref, wu_ref, wd_ref, o_ref,
                        xbuf, h_scr, acc_scr, sems, *, tile, nf):
    i, f = pl.program_id(0), pl.program_id(1)
    used = i < nu_ref[0]
    slot = i % 2
    per_step = tile // nf

    def gather(t, sl):
        return _row_gather(lambda r: tos_ref[t * tile + r], x_hbm,
                           lambda r: xbuf.at[sl, pl.ds(r, 1), :], sems.at[sl], tile, lambda j: GATHER_PRIORITY)

    def prefetch_share():
        row_copy = gather(i + 1, 1 - slot)[2]
        for j in range(per_step):
            row_copy(f * per_step + j).start(priority=GATHER_PRIORITY)

    @pl.when((f == 0) & (i == 0))
    def _():
        gather(0, 0)[0]()

    @pl.when(f == 0)
    def _():
        gather(i, slot)[1]()

    @pl.when((f == 0) & used)
    def _():
        h_scr[...] = _rmsnorm(xbuf[slot], g_ref[...]).astype(BF16)
        acc_scr[...] = jnp.zeros_like(acc_scr)

    @pl.when(used)
    def _():
        prefetch_share()
        hb = h_scr[...]
        a = _dot(hb, wg_ref[0])
        u = _dot(hb, wu_ref[0])
        act = (a * jax.nn.sigmoid(a)) * u
        acc_scr[...] += _dot(act.astype(BF16), wd_ref[0])

    @pl.when(jnp.logical_not(used))
    def _():
        prefetch_share()

    @pl.when(f == nf - 1)
    def _():
        o_ref[...] = jnp.where(used, acc_scr[...], 0.0)

    @pl.when((f == nf - 1) & (i == pl.num_programs(0) - 1))
    def _():
        gather(i + 1, 1 - slot)[1]()


def _moe_experts(x, g, wg, wu, wd, tile_expert, n_used, token_of_slot, *, tile, tf=FFN_COL_TILE):
    N, D = x.shape
    E, _, F = wg.shape
    n_slots = token_of_slot.shape[0] - tile
    assert n_slots % tile == 0 and F % tf == 0
    n_tiles, nf = n_slots // tile, F // tf
    assert tile % nf == 0
    fidx = lambda i, f, nu: jnp.where(i < nu[0], f, nf - 1)
    return pl.pallas_call(
        functools.partial(_moe_experts_kernel, tile=tile, nf=nf),
        grid_spec=pltpu.PrefetchScalarGridSpec(
            num_scalar_prefetch=3,
            grid=(n_tiles, nf),
            in_specs=[pl.BlockSpec(memory_space=pl.ANY),
                      pl.BlockSpec((1, D), lambda i, f, te, nu, tos: (0, 0)),
                      pl.BlockSpec((1, D, tf), lambda i, f, te, nu, tos: (te[i], 0, fidx(i, f, nu))),
                      pl.BlockSpec((1, D, tf), lambda i, f, te, nu, tos: (te[i], 0, fidx(i, f, nu))),
                      pl.BlockSpec((1, tf, D), lambda i, f, te, nu, tos: (te[i], fidx(i, f, nu), 0))],
            out_specs=pl.BlockSpec((tile, D), lambda i, f, te, nu, tos: (i, 0)),
            scratch_shapes=[pltpu.VMEM((2, tile, D), F32), pltpu.VMEM((tile, D), BF16),
                            pltpu.VMEM((tile, D), F32), pltpu.SemaphoreType.DMA((2,))],
        ),
        out_shape=jax.ShapeDtypeStruct((n_slots, D), F32),
        compiler_params=_cparams(("arbitrary", "arbitrary")),
        name="moe_experts",
    )(tile_expert, n_used, token_of_slot, x, g, wg, wu, wd)


def _moe_combine_kernel(slot_ref, x_ref, route_ref, gf_ref, ys_hbm, o_ref, ybuf, sems, *, tile):
    i = pl.program_id(0)
    slot = i % 2

    def gather(t, sl):
        return _row_gather(lambda j: slot_ref[t * 2 * tile + j], ys_hbm,
                           lambda j: ybuf.at[sl, pl.ds(j, 1), :], sems.at[sl], 2 * tile, lambda j: j % 2)

    @pl.when(i == 0)
    def _():
        gather(0, 0)[0]()

    gather(i, slot)[1]()

    @pl.when(i + 1 < pl.num_programs(0))
    def _():
        gather(i + 1, 1 - slot)[0]()

    route = route_ref[...]
    w1, w2 = route[:, ROUTE_W1:ROUTE_W1 + 1], route[:, ROUTE_W2:ROUTE_W2 + 1]
    y = w1 * ybuf[slot, 0:tile, :] + w2 * ybuf[slot, tile:2 * tile, :]
    o_ref[...] = _rmsnorm(x_ref[...] + y, gf_ref[...])


def _moe_combine(x, route, g_final, ys, slots, *, tile):
    N, D = x.shape
    assert N % tile == 0
    return pl.pallas_call(
        functools.partial(_moe_combine_kernel, tile=tile),
        grid_spec=pltpu.PrefetchScalarGridSpec(
            num_scalar_prefetch=1,
            grid=(N // tile,),
            in_specs=[pl.BlockSpec((tile, D), lambda i, s: (i, 0)),
                      pl.BlockSpec((tile, LANES), lambda i, s: (i, 0)),
                      pl.BlockSpec((1, D), lambda i, s: (0, 0)),
                      pl.BlockSpec(memory_space=pl.ANY)],
            out_specs=pl.BlockSpec((tile, D), lambda i, s: (i, 0)),
            scratch_shapes=[pltpu.VMEM((2, 2 * tile, D), F32), pltpu.SemaphoreType.DMA((2,))],
        ),
        out_shape=jax.ShapeDtypeStruct((N, D), F32),
        compiler_params=_cparams(("arbitrary",)),
        name="moe_combine",
    )(slots, x, route, g_final, ys)


def _moe_prompt(x, g, w_router, wg, wu, wd, g_final, *, tile=EXPERT_ROW_TILE, ctile=COMBINE_ROW_TILE):
    N, D = x.shape
    E = wg.shape[0]
    route, counts = _moe_route(x, g, w_router, E)
    cnt = counts[0, :E].astype(jnp.int32)
    padded = (cnt + tile - 1) // tile * tile
    ends = jnp.cumsum(padded)
    off = ends - padded
    i1, i2 = route[:, ROUTE_I1].astype(jnp.int32), route[:, ROUTE_I2].astype(jnp.int32)
    slot1 = off[i1] + route[:, ROUTE_R1].astype(jnp.int32)
    slot2 = off[i2] + route[:, ROUTE_R2].astype(jnp.int32)
    n_slots = (TOP_K * N + E * (tile - 1)) // tile * tile
    rows = jnp.arange(N, dtype=jnp.int32)
    token_of_slot = jnp.zeros((n_slots + tile,), jnp.int32).at[jnp.concatenate([slot1, slot2])].set(
        jnp.concatenate([rows, rows]), unique_indices=True)
    n_used = ends[-1:] // tile
    tile_start = jnp.minimum(jnp.arange(n_slots // tile, dtype=jnp.int32), n_used[0] - 1) * tile
    tile_expert = jnp.sum(tile_start[:, None] >= ends[None, :], axis=1).astype(jnp.int32)
    ys = _moe_experts(x, g, wg, wu, wd, tile_expert, n_used.astype(jnp.int32), token_of_slot, tile=tile)
    slots = jnp.stack([slot1.reshape(-1, ctile), slot2.reshape(-1, ctile)], axis=1).reshape(-1)
    return _moe_combine(x, route, g_final, ys, slots, tile=ctile)


def _nsa_proj_kernel(x_ref, g_ref, wq_ref, wkv_ref, wgt_ref, wck_ref, wcv_ref,
                     q_ref, kvt_ref, ks_ref, vs_ref, kw_ref, vw_ref, gt_ref, kc_ref, vc_ref):
    hb = _rmsnorm(x_ref[...], g_ref[...]).astype(BF16)
    q_ref[...] = _dot(hb, wq_ref[...]).astype(BF16)
    kv = _dot(hb, wkv_ref[...])
    kvt_ref[0] = kv.T
    ks_ref[...] = kv[:, 2 * KV_W:3 * KV_W].astype(BF16)
    vs_ref[...] = kv[:, 3 * KV_W:4 * KV_W].astype(BF16)
    kw_ref[...] = kv[:, 4 * KV_W:5 * KV_W].astype(BF16)
    vw_ref[...] = kv[:, 5 * KV_W:].astype(BF16)
    gt_ref[...] = jax.nn.sigmoid(_dot(hb, wgt_ref[...]))
    tm = kv.shape[0]
    kc_ref[...] = (kv[:, :KV_W] * wck_ref[...]).reshape(tm // CMP_BLOCK, CMP_BLOCK, KV_W).sum(axis=1)
    vc_ref[...] = (kv[:, KV_W:2 * KV_W] * wcv_ref[...]).reshape(tm // CMP_BLOCK, CMP_BLOCK, KV_W).sum(axis=1)


def _nsa_proj(x, g, wq, wkv, wgt, wck, wcv, *, tm, seq):
    N, D = x.shape
    assert N % tm == 0 and tm % (8 * CMP_BLOCK) == 0 and seq % tm == 0 and N % seq == 0
    per_seq = seq // tm
    row = lambda w: pl.BlockSpec((tm, w), lambda i: (i, 0))
    full = lambda a: pl.BlockSpec(a.shape, lambda i: (0, 0))
    out_shape = [jax.ShapeDtypeStruct((N, N_HEADS * HEAD_DIM), BF16),
                 jax.ShapeDtypeStruct((N // seq, 6 * KV_W, seq), F32)]
    out_shape += [jax.ShapeDtypeStruct((N, KV_W), BF16)] * 4
    out_shape += [jax.ShapeDtypeStruct((N, LANES), F32)]
    out_shape += [jax.ShapeDtypeStruct((N // CMP_BLOCK, KV_W), F32)] * 2
    out_specs = [row(N_HEADS * HEAD_DIM),
                 pl.BlockSpec((1, 6 * KV_W, tm), lambda i: (i // per_seq, 0, i % per_seq))]
    out_specs += [row(KV_W)] * 4 + [row(LANES)]
    out_specs += [pl.BlockSpec((tm // CMP_BLOCK, KV_W), lambda i: (i, 0))] * 2
    return pl.pallas_call(
        _nsa_proj_kernel,
        grid=(N // tm,),
        in_specs=[row(D), full(g), full(wq), full(wkv), full(wgt), full(wck), full(wcv)],
        out_specs=out_specs,
        out_shape=out_shape,
        compiler_params=_cparams(("parallel",)),
        name="nsa_proj_prompt",
    )(x, g, wq, wkv, wgt, wck, wcv)


def _nsa_proj_sample_kernel(x_ref, g_ref, w_ref, z_ref):
    z_ref[...] = _dot3(_rmsnorm(x_ref[...], g_ref[...]), w_ref[...])


def _nsa_proj_sample(x, g, w_in, tn=SAMPLE_PROJ_COL_TILE):
    B, D = x.shape
    cols = w_in.shape[1]
    assert cols % tn == 0
    return pl.pallas_call(
        _nsa_proj_sample_kernel,
        grid=(cols // tn,),
        in_specs=[pl.BlockSpec((B, D), lambda j: (0, 0)), pl.BlockSpec((1, D), lambda j: (0, 0)),
                  pl.BlockSpec((D, tn), lambda j: (0, j))],
        out_specs=pl.BlockSpec((B, tn), lambda j: (0, j)),
        out_shape=jax.ShapeDtypeStruct((B, cols), F32),
        compiler_params=_cparams(("parallel",)),
        name="nsa_proj_sample",
    )(x, g, w_in)


def _nsa_prompt_kernel(q_ref, gt_ref, kc_ref, vc_ref, ks_ref, vs_ref, kw_ref, vw_ref, x_ref, wo_ref,
                       o_ref, o_scr, sel_scr, *, tq, kchunk, seq):
    t0 = pl.program_id(1) * tq
    rows = GQA * tq
    scale = HEAD_DIM ** -0.5
    assert scale == 0.125
    n_blocks = seq // SLC_BLOCK
    top_k = min(SLC_TOPK, n_blocks)

    q = q_ref[0]
    gates = gt_ref[0]
    qpos_t = t0 + lax.broadcasted_iota(jnp.int32, (tq, 1), 0)
    qpos = jnp.concatenate([qpos_t] * GQA, axis=0)
    tq_f = qpos.astype(F32)

    lane = lax.broadcasted_iota(jnp.int32, (tq, LANES), 1)
    jblk = lane >> 1
    real = ((lane & 1) == 0) & (jblk < n_blocks)
    blk = qpos_t // SLC_BLOCK
    valid = real & (jblk <= blk)
    forced = (jblk == 0) | (jblk == blk) | (jblk == blk - 1)
    valid_f = jnp.where(valid, 1.0, 0.0)
    valid_b = valid_f.astype(BF16)
    bonus = jnp.where(forced, FORCE_BONUS, 0.0)
    floor = jnp.where(valid, 0.0, jnp.where(real, -1.0, -2.0))
    c_lane = lax.broadcasted_iota(jnp.int32, (1, LANES), 1)
    c_mid = c_lane.astype(F32) * CMP_BLOCK + (CMP_BLOCK - 1) / 2
    c_end = (c_lane + 1) * CMP_BLOCK - 1

    n_chunks = (t0 + tq + kchunk - 1) // kchunk
    wstart = pl.multiple_of(jnp.maximum(t0 - WINDOW, 0), tq)
    wlen = WINDOW + tq

    need_rank = (t0 + tq - 1) // SLC_BLOCK >= top_k
    wpos = wstart + lax.broadcasted_iota(jnp.int32, (1, wlen), 1)
    w_dist = jnp.where((wpos <= qpos_t) & (qpos_t - wpos < WINDOW), (wpos - qpos_t).astype(F32), NEG)

    def with_ones(v):
        return jnp.concatenate([v, jnp.ones_like(v)], axis=1)

    def biased(scores, dist, kv):
        return jnp.concatenate(
            [scores[g * tq:(g + 1) * tq] + _alibi_slope(kv * GQA + g) * dist for g in range(GQA)], axis=0)

    for kv in range(N_KV_HEADS):
        hs = slice(kv * HEAD_DIM, (kv + 1) * HEAD_DIM)
        qs = jnp.concatenate(
            [q[:, (kv * GQA + g) * HEAD_DIM:(kv * GQA + g + 1) * HEAD_DIM] for g in range(GQA)], axis=0) * scale
        slope = _head_slopes(rows, tq, kv * GQA)

        s_c = _dot_nt(qs, kc_ref[0][:, hs]) - slope * (tq_f - c_mid)
        p_c = _masked_softmax(s_c, c_end <= qpos)
        o_c = _dot(p_c.astype(BF16), vc_ref[0][:, hs])

        imp = p_c[0:tq]
        for g in range(1, GQA):
            imp = imp + p_c[g * tq:(g + 1) * tq]
        sel_scr[...] = valid_b

        @pl.when(need_rank)
        def _():
            pair = imp + pltpu.roll(imp, LANES - 1, 1)
            score = valid_f * (pair + bonus) + floor
            score_t = score.T[:2 * n_blocks]
            row_id = lax.broadcasted_iota(jnp.int32, score_t.shape, 0)
            rank_t = jnp.zeros(score_t.shape, jnp.int32)
            for k in range(n_blocks):
                other = score_t[2 * k:2 * k + 1, :]
                beats = (other > score_t) | ((other == score_t) & (row_id > 2 * k))
                rank_t = rank_t + beats.astype(jnp.int32)
            top_t = jnp.where(rank_t < top_k, 1.0, 0.0)
            top = jnp.concatenate([top_t, jnp.zeros((LANES - 2 * n_blocks, tq), F32)], axis=0).T
            sel_scr[...] = (top * valid_f).astype(BF16)

        sel = sel_scr[...]

        def chunk(c, carry):
            m, acc = carry
            k0 = pl.multiple_of(c * kchunk, kchunk)
            kb = ks_ref[0, pl.ds(k0, kchunk), hs]
            vb = with_ones(vs_ref[0, pl.ds(k0, kchunk), hs])
            pos = k0 + lax.broadcasted_iota(jnp.int32, (1, kchunk), 1)
            expand = (lax.broadcasted_iota(jnp.int32, (LANES, kchunk), 0)
                      == 2 * ((k0 + lax.broadcasted_iota(jnp.int32, (LANES, kchunk), 1)) // SLC_BLOCK))
            picked = _dot(sel, jnp.where(expand, 1.0, 0.0).astype(BF16))
            ok = (picked > 0.5) & (pos <= qpos_t)
            s = biased(_dot_nt(qs, kb), jnp.where(ok, (pos - qpos_t).astype(F32), NEG), kv)
            m_new = jnp.maximum(m, jnp.max(s, axis=-1, keepdims=True))
            p = jnp.exp(s - m_new)
            acc = jnp.exp(m - m_new) * acc + _dot(p.astype(BF16), vb)
            return m_new, acc

        init = (jnp.full((rows, 1), NEG, F32), jnp.zeros((rows, 2 * HEAD_DIM), F32))
        _, acc_s = lax.fori_loop(0, n_chunks, chunk, init)
        o_s = acc_s[:, :HEAD_DIM] / acc_s[:, HEAD_DIM:HEAD_DIM + 1]

        kb = kw_ref[0, pl.ds(wstart, wlen), hs]
        vb = with_ones(vw_ref[0, pl.ds(wstart, wlen), hs])
        s_w = biased(_dot_nt(qs, kb), w_dist, kv)
        p_w = jnp.exp(s_w - jnp.max(s_w, axis=-1, keepdims=True))
        pv = _dot(p_w.astype(BF16), vb)
        o_w = pv[:, :HEAD_DIM] / pv[:, HEAD_DIM:HEAD_DIM + 1]

        def gate_col(n):
            base = n * N_HEADS + kv * GQA
            return jnp.concatenate([gates[:, base + g:base + g + 1] for g in range(GQA)], axis=0)

        o = gate_col(0) * o_c + gate_col(1) * o_s + gate_col(2) * o_w
        for g in range(GQA):
            h = kv * GQA + g
            o_scr[:, h * HEAD_DIM:(h + 1) * HEAD_DIM] = o[g * tq:(g + 1) * tq]

    o_ref[0] = x_ref[0] + _dot(o_scr[...].astype(BF16), wo_ref[...])


def _nsa_prompt(q, gates, kc, vc, ks, vs, kw, vw, x, w_o, *, tq=ATTN_Q_TILE, kchunk=ATTN_KEY_CHUNK):
    B, L, D = x.shape
    assert L % kchunk == 0 and kchunk % tq == 0 and L >= WINDOW + tq and L % SLC_BLOCK == 0
    assert 2 * (L // SLC_BLOCK) <= LANES and L // CMP_BLOCK <= LANES and kc.shape[1] == LANES
    tile = lambda w: pl.BlockSpec((1, tq, w), lambda b, t: (b, t, 0))
    whole = lambda a: pl.BlockSpec((1,) + a.shape[1:], lambda b, t: (b, 0, 0))
    return pl.pallas_call(
        functools.partial(_nsa_prompt_kernel, tq=tq, kchunk=kchunk, seq=L),
        grid=(B, L // tq),
        in_specs=[tile(N_HEADS * HEAD_DIM), tile(LANES), whole(kc), whole(vc), whole(ks), whole(vs),
                  whole(kw), whole(vw), tile(D), pl.BlockSpec(w_o.shape, lambda b, t: (0, 0))],
        out_specs=tile(D),
        out_shape=jax.ShapeDtypeStruct((B, L, D), F32),
        scratch_shapes=[pltpu.VMEM((tq, N_HEADS * HEAD_DIM), F32), pltpu.VMEM((tq, LANES), BF16)],
        compiler_params=_cparams(("parallel", "arbitrary")),
        name="nsa_attn_prompt",
    )(q, gates, kc, vc, ks, vs, kw, vw, x, w_o)


CMP_GROUP = 32
MXU_DEPTH = 256


def _cmp_sample_kernel(pt_ref, cache_ref, w_ref, s_ref, o_ref, buf, sems, *, rows):
    b, g = pl.program_id(0), pl.program_id(1)
    n_groups = pl.num_programs(1)
    step = b * n_groups + g
    slot = step % 2

    def page_copy(bb, gg, sl, i):
        page = pt_ref[bb, gg * CMP_GROUP + i]
        return pltpu.make_async_copy(cache_ref.at[page, pl.ds(0, rows), :], buf.at[sl, i], sems.at[sl])

    def start_group(bb, gg, sl):
        for i in range(CMP_GROUP):
            page_copy(bb, gg, sl, i).start()

    @pl.when(step == 0)
    def _():
        start_group(b, g, slot)

    @pl.when(step + 1 < pl.num_programs(0) * n_groups)
    def _():
        nxt = step + 1
        start_group(nxt // n_groups, nxt % n_groups, 1 - slot)

    for i in range(CMP_GROUP):
        page_copy(b, g, slot, i).wait()

    w = w_ref[...]
    acc = jnp.zeros((rows, LANES), F32)
    for j in range(CMP_GROUP // 2):
        p = jnp.concatenate([buf[slot, 2 * j] * w, buf[slot, 2 * j + 1] * w], axis=1)
        hi, lo = _split(p)
        r = _dot(jnp.concatenate([hi, lo], axis=0), s_ref[j])
        acc = acc + (r[:rows] + r[rows:])
    o_ref[0] = acc


def _cmp_sample(page_table, cache_t, w_rows):
    B, n_pages = page_table.shape
    _, _, page = cache_t.shape
    rows = w_rows.shape[0]
    per_page = page // CMP_BLOCK
    assert n_pages % CMP_GROUP == 0 and CMP_GROUP * per_page == LANES and 2 * page == MXU_DEPTH
    k = jnp.arange(2 * page)
    token = (2 * jnp.arange(CMP_GROUP // 2)[:, None] + k[None, :] // page) * per_page + (k[None, :] % page) // CMP_BLOCK
    block_sum = (token[:, :, None] == jnp.arange(LANES)[None, None, :]).astype(BF16)
    return pl.pallas_call(
        functools.partial(_cmp_sample_kernel, rows=rows),
        grid_spec=pltpu.PrefetchScalarGridSpec(
            num_scalar_prefetch=1,
            grid=(B, n_pages // CMP_GROUP),
            in_specs=[pl.BlockSpec(memory_space=pl.ANY),
                      pl.BlockSpec(w_rows.shape, lambda b, g, pt: (0, 0)),
                      pl.BlockSpec(block_sum.shape, lambda b, g, pt: (0, 0, 0))],
            out_specs=pl.BlockSpec((1, rows, LANES), lambda b, g, pt: (b, 0, g)),
            scratch_shapes=[pltpu.VMEM((2, CMP_GROUP, rows, page), F32), pltpu.SemaphoreType.DMA((2,))],
        ),
        out_shape=jax.ShapeDtypeStruct((B, rows, n_pages * per_page), F32),
        compiler_params=_cparams(("arbitrary", "arbitrary")),
        name="cmp_sample",
    )(page_table, cache_t, w_rows, block_sum)


def _select_sample_kernel(q_ref, kcv_ref, oc_ref, idx_ref, *, past):
    scale = HEAD_DIM ** -0.5
    q = q_ref[0]
    nc = kcv_ref.shape[2]
    n_past_blocks = past // SLC_BLOCK
    per_block = SLC_BLOCK // CMP_BLOCK
    row = lax.broadcasted_iota(jnp.int32, (N_HEADS, 1), 0)
    slope = _head_slopes(N_HEADS, 1, 0)
    tok = lax.broadcasted_iota(jnp.int32, (1, nc), 1)
    c_mid = tok.astype(F32) * CMP_BLOCK + (CMP_BLOCK - 1) / 2
    c_ok = (tok + 1) * CMP_BLOCK - 1 <= past
    bias = slope * (float(past) - c_mid)

    s_c = jnp.zeros((N_HEADS, nc), F32)
    for kv in range(N_KV_HEADS):
        s_kv = _dot3(q, kcv_ref[0, kv * HEAD_DIM:(kv + 1) * HEAD_DIM, :])
        s_c = jnp.where(row // GQA == kv, s_kv, s_c)
    p_c = _masked_softmax(s_c * scale - bias, c_ok)
    o_c = jnp.zeros((N_HEADS, HEAD_DIM), F32)
    for kv in range(N_KV_HEADS):
        o_kv = _dot3(p_c, kcv_ref[0, KV_W + kv * HEAD_DIM:KV_W + (kv + 1) * HEAD_DIM, :], nt=True)
        o_c = jnp.where(row // GQA == kv, o_kv, o_c)
    oc_ref[0] = o_c

    imp = p_c
    shift = 1
    while shift < GQA:
        imp = imp + pltpu.roll(imp, shift, 0)
        shift *= 2
    assert per_block == 2
    pair = imp + pltpu.roll(imp, nc - 1, 1)
    jblk = tok // per_block
    real = tok % per_block == 0
    forced = (jblk == 0) | (jblk == n_past_blocks - 1)
    score = jnp.where(real, pair + jnp.where(forced, FORCE_BONUS, 0.0), -2.0)
    own = jnp.float32(FORCE_BONUS)
    rank = (own > score).astype(jnp.int32)
    for k in range(n_past_blocks):
        col = score[:, per_block * k:per_block * k + 1]
        beats = (col > score) | ((col == score) & (jblk > k))
        rank = rank + beats.astype(jnp.int32)
    own_rank = jnp.sum(jnp.where(real & (score >= own), 1, 0), axis=-1, keepdims=True)
    out_lane = lax.broadcasted_iota(jnp.int32, (N_HEADS, LANES), 1)
    idx = jnp.zeros((N_HEADS, LANES), jnp.int32)
    for r in range(SLC_TOPK):
        hit = jnp.sum(jnp.where(real & (rank == r), jblk, 0), axis=-1, keepdims=True)
        hit = hit + jnp.where(own_rank == r, n_past_blocks, 0)
        idx = jnp.where(out_lane == r, hit, idx)
    idx_ref[0] = idx


def _select_sample(q16, kcv, past):
    B = q16.shape[0]
    nc = kcv.shape[2]
    assert past % SLC_BLOCK == 0 and nc == past // CMP_BLOCK and nc % LANES == 0
    assert past // SLC_BLOCK + 1 > SLC_TOPK
    return pl.pallas_call(
        functools.partial(_select_sample_kernel, past=past),
        grid=(B,),
        in_specs=[pl.BlockSpec((1,) + q16.shape[1:], lambda b: (b, 0, 0)),
                  pl.BlockSpec((1,) + kcv.shape[1:], lambda b: (b, 0, 0))],
        out_specs=[pl.BlockSpec((1, N_HEADS, HEAD_DIM), lambda b: (b, 0, 0)),
                   pl.BlockSpec((1, N_HEADS, LANES), lambda b: (b, 0, 0))],
        out_shape=[jax.ShapeDtypeStruct((B, N_HEADS, HEAD_DIM), F32),
                   jax.ShapeDtypeStruct((B, N_HEADS, LANES), jnp.int32)],
        compiler_params=_cparams(("parallel",)),
        name="select_sample",
    )(q16, kcv)


def _attend_sample_kernel(pt_ref, idx_ref, q_ref, oc_ref, gt_ref, new_ref, win_ref, cache_ref,
                          o_ref, kbuf, vbuf, sems, *, past):
    b = pl.program_id(0)
    scale = HEAD_DIM ** -0.5
    n_past_blocks = past // SLC_BLOCK
    page_rows = cache_ref.shape[2]
    blocks_per_page = page_rows // SLC_BLOCK
    nkeys = SLC_TOPK * page_rows

    def block_copies(kv, n):
        blk = idx_ref[(b * N_KV_HEADS + kv) * SLC_TOPK + n]
        in_past = blk < n_past_blocks
        page = pt_ref[b, jnp.minimum(blk, n_past_blocks - 1) // blocks_per_page]
        dst = pl.ds(n * page_rows, page_rows)
        ck = pltpu.make_async_copy(cache_ref.at[page, pl.ds((2 * N_KV_HEADS + kv) * HEAD_DIM, HEAD_DIM), :],
                                   kbuf.at[kv, :, dst], sems.at[0, kv, n])
        cv = pltpu.make_async_copy(cache_ref.at[page, pl.ds((3 * N_KV_HEADS + kv) * HEAD_DIM, HEAD_DIM), :],
                                   vbuf.at[kv, :, dst], sems.at[1, kv, n])
        return blk, in_past, ck, cv

    for kv in range(N_KV_HEADS):
        for n in range(SLC_TOPK):
            _, in_past, ck, cv = block_copies(kv, n)

            @pl.when(in_past)
            def _():
                ck.start()
                cv.start()

            @pl.when(jnp.logical_not(in_past))
            def _():
                kbuf[kv, :, n * page_rows:(n + 1) * page_rows] = jnp.zeros((HEAD_DIM, page_rows), F32)
                vbuf[kv, :, n * page_rows:(n + 1) * page_rows] = jnp.zeros((HEAD_DIM, page_rows), F32)

    q = q_ref[0]
    row = lax.broadcasted_iota(jnp.int32, (N_HEADS, 1), 0)
    slope = _head_slopes(N_HEADS, 1, 0)
    key_lane = lax.broadcasted_iota(jnp.int32, (1, nkeys), 1)

    wb = win_ref.shape[2]
    w_lane = lax.broadcasted_iota(jnp.int32, (1, wb), 1)
    wpos = past - wb + w_lane
    w_ok = (past - wpos < WINDOW) & (wpos >= 0)
    w_bias = slope * (past - wpos).astype(F32)
    s_w = jnp.zeros((N_HEADS, wb), F32)
    s_n = jnp.zeros((N_HEADS, 1), F32)
    for kv in range(N_KV_HEADS):
        mine = row // GQA == kv
        s_kv = _dot3(q, win_ref[0, kv * HEAD_DIM:(kv + 1) * HEAD_DIM, :])
        s_w = jnp.where(mine, s_kv, s_w)
        s_n = jnp.where(mine, jnp.sum(q * new_ref[0, 4, kv:kv + 1, :], axis=-1, keepdims=True), s_n)
    s_w = jnp.where(w_ok, s_w * scale - w_bias, NEG)
    s_n = s_n * scale
    m_w = jnp.maximum(jnp.max(s_w, axis=-1, keepdims=True), s_n)
    p_w = jnp.where(w_ok, jnp.exp(s_w - m_w), 0.0)
    p_n = jnp.exp(s_n - m_w)
    l_w = jnp.maximum(jnp.sum(p_w, axis=-1, keepdims=True) + p_n, 1e-30)
    p_w = p_w / l_w
    p_n = p_n / l_w
    o_w = jnp.zeros((N_HEADS, HEAD_DIM), F32)
    for kv in range(N_KV_HEADS):
        o_kv = (_dot3(p_w, win_ref[0, KV_W + kv * HEAD_DIM:KV_W + (kv + 1) * HEAD_DIM, :], nt=True)
                + p_n * new_ref[0, 5, kv:kv + 1, :])
        o_w = jnp.where(row // GQA == kv, o_kv, o_w)

    for kv in range(N_KV_HEADS):
        for n in range(SLC_TOPK):
            _, in_past, ck, cv = block_copies(kv, n)

            @pl.when(in_past)
            def _():
                ck.wait()
                cv.wait()

    s_s = jnp.zeros((N_HEADS, nkeys), F32)
    pos = jnp.zeros((N_HEADS, nkeys), jnp.int32)
    live = jnp.zeros((N_HEADS, nkeys), jnp.int32)
    own = jnp.zeros((N_HEADS, 1), jnp.int32)
    s_n = jnp.zeros((N_HEADS, 1), F32)
    in_slab = key_lane % page_rows
    for kv in range(N_KV_HEADS):
        mine = row // GQA == kv
        s_kv = _dot3(q, kbuf[kv])
        s_s = jnp.where(mine, s_kv, s_s)
        s_n = jnp.where(mine, jnp.sum(q * new_ref[0, 2, kv:kv + 1, :], axis=-1, keepdims=True), s_n)
        base_kv = jnp.zeros((1, nkeys), jnp.int32)
        half_kv = jnp.zeros((1, nkeys), jnp.int32)
        own_kv = jnp.int32(0)
        for n in range(SLC_TOPK):
            blk = idx_ref[(b * N_KV_HEADS + kv) * SLC_TOPK + n]
            here = key_lane // page_rows == n
            base_kv = jnp.where(here, (blk // blocks_per_page) * page_rows, base_kv)
            half_kv = jnp.where(here, blk % blocks_per_page, half_kv)
            own_kv = own_kv + (blk >= n_past_blocks).astype(jnp.int32)
        pos = jnp.where(mine, base_kv + in_slab, pos)
        live = jnp.where(mine, (in_slab // SLC_BLOCK == half_kv).astype(jnp.int32), live)
        own = jnp.where(mine, own_kv, own)
    s_ok = (live > 0) & (pos < past)
    n_ok = own > 0
    s_s = jnp.where(s_ok, s_s * scale - slope * (past - pos).astype(F32), NEG)
    s_n = jnp.where(n_ok, s_n * scale, NEG)
    m_s = jnp.maximum(jnp.max(s_s, axis=-1, keepdims=True), s_n)
    p_s = jnp.where(s_ok, jnp.exp(s_s - m_s), 0.0)
    p_n = jnp.where(n_ok, jnp.exp(s_n - m_s), 0.0)
    l_s = jnp.maximum(jnp.sum(p_s, axis=-1, keepdims=True) + p_n, 1e-30)
    p_s = p_s / l_s
    p_n = p_n / l_s
    o_s = jnp.zeros((N_HEADS, HEAD_DIM), F32)
    for kv in range(N_KV_HEADS):
        o_kv = _dot3(p_s, vbuf[kv], nt=True) + p_n * new_ref[0, 3, kv:kv + 1, :]
        o_s = jnp.where(row // GQA == kv, o_kv, o_s)

    gt = jax.nn.sigmoid(gt_ref[0])
    o_ref[0] = gt[:, 0:1] * oc_ref[0] + gt[:, 1:2] * o_s + gt[:, 2:3] * o_w


def _attend_sample(page_table, idx, q16, o_c, gates, kv_new, win_t, cache_t, past):
    B = q16.shape[0]
    page_rows = cache_t.shape[2]
    assert win_t.shape[2] <= WINDOW and page_rows % SLC_BLOCK == 0
    blk3 = lambda a: pl.BlockSpec((1,) + a.shape[1:], lambda b, pt, ix: (b,) + (0,) * (a.ndim - 1))
    return pl.pallas_call(
        functools.partial(_attend_sample_kernel, past=past),
        grid_spec=pltpu.PrefetchScalarGridSpec(
            num_scalar_prefetch=2,
            grid=(B,),
            in_specs=[blk3(q16), blk3(o_c), blk3(gates), blk3(kv_new), blk3(win_t),
                      pl.BlockSpec(memory_space=pl.ANY)],
            out_specs=pl.BlockSpec((1, N_HEADS, HEAD_DIM), lambda b, pt, ix: (b, 0, 0)),
            scratch_shapes=[pltpu.VMEM((N_KV_HEADS, HEAD_DIM, SLC_TOPK * page_rows), F32),
                            pltpu.VMEM((N_KV_HEADS, HEAD_DIM, SLC_TOPK * page_rows), F32),
                            pltpu.SemaphoreType.DMA((2, N_KV_HEADS, SLC_TOPK))],
        ),
        out_shape=jax.ShapeDtypeStruct((B, N_HEADS, HEAD_DIM), F32),
        compiler_params=_cparams(("arbitrary",)),
        name="attend_sample",
    )(page_table, idx, q16, o_c, gates, kv_new, win_t, cache_t)


def _linear_res_kernel(x_ref, a_ref, w_ref, o_ref):
    o_ref[...] = x_ref[...] + _dot3(a_ref[...], w_ref[...])


def _linear_res(x, a, w):
    return pl.pallas_call(
        _linear_res_kernel,
        out_shape=jax.ShapeDtypeStruct(x.shape, F32),
        compiler_params=pltpu.CompilerParams(vmem_limit_bytes=VMEM_LIMIT),
        name="out_proj_sample",
    )(x, a, w)


def _expand_cmp_weights(w_cmp, rows):
    w = jnp.repeat(w_cmp, HEAD_DIM, axis=1)
    return jnp.tile(w, (rows // CMP_BLOCK, 1))


def _row(v):
    return v.reshape(1, -1)


def _prompt_mixers(x, norm_mix, w_pool, pool_scale, w_in, w_cmp_k, w_cmp_v, w_out, norm_ffn0, wfg, wfu, wfd):
    B, L, D = x.shape
    q_w = N_HEADS * HEAD_DIM
    x, h_last = _pool_prompt(x, _row(norm_mix[0]), w_pool.astype(BF16), _row(pool_scale))
    new_pool = h_last[None, :, POOL_HALO - (max(POOL_WINDOWS) - 1):, :]
    x = _ffn(x.reshape(B * L, D), _row(norm_ffn0), wfg.astype(BF16), wfu.astype(BF16), wfd.astype(BF16),
             tm=FFN_ROW_TILE)

    tm = PROJ_ROW_TILE
    wq = w_in[:, :q_w].astype(BF16)
    wkv = w_in[:, q_w:q_w + 6 * KV_W].astype(BF16)
    wgt = jnp.pad(w_in[:, q_w + 6 * KV_W:], ((0, 0), (0, LANES - N_BRANCH * N_HEADS))).astype(BF16)
    q, kv_t, ks, vs, kw, vw, gates, kc, vc = _nsa_proj(
        x, _row(norm_mix[1]), wq, wkv, wgt, _expand_cmp_weights(w_cmp_k, tm), _expand_cmp_weights(w_cmp_v, tm),
        tm=tm, seq=L)
    kv_t = kv_t.reshape(B, 6, N_KV_HEADS, HEAD_DIM, L)
    new_kv = kv_t[:, :4].transpose(0, 4, 1, 2, 3)[None]
    new_win = kv_t[:, 4:, :, :, L - min(WINDOW, L):].transpose(0, 4, 1, 2, 3)[None]
    nc = L // CMP_BLOCK
    pad_c = lambda a: jnp.pad(a.reshape(B, nc, KV_W), ((0, 0), (0, LANES - nc), (0, 0))).astype(BF16)
    per_seq = lambda a: a.reshape(B, L, a.shape[-1])
    x = _nsa_prompt(per_seq(q), per_seq(gates), pad_c(kc), pad_c(vc), per_seq(ks), per_seq(vs),
                    per_seq(kw), per_seq(vw), per_seq(x), w_out.astype(BF16))
    return x.reshape(B * L, D), new_pool, new_kv, new_win


def _sample_mixers(x, state_pool, cache, state_win, page_table, norm_mix, w_pool, pool_scale, w_in, w_cmp_k,
                   w_cmp_v, w_out, norm_ffn0, wfg, wfu, wfd):
    SB, D = x.shape
    n_phys, page = cache.shape[:2]
    past = page_table.shape[1] * page
    q_w = N_HEADS * HEAD_DIM
    x, h = _pool_sample(x, state_pool.transpose(1, 0, 2), _row(norm_mix[0]), w_pool, _row(pool_scale))
    new_pool = jnp.concatenate([state_pool[:, 1:], h[:, None]], axis=1)[None]
    x = _ffn(x, _row(norm_ffn0), wfg, wfu, wfd, tm=SB)

    cols = w_in.shape[1]
    z = _nsa_proj_sample(x, _row(norm_mix[1]), jnp.pad(w_in, ((0, 0), (0, -cols % LANES))))
    kv_new = z[:, q_w:q_w + 6 * KV_W].reshape(SB, 6, N_KV_HEADS, HEAD_DIM)
    new_kv = kv_new[None, :, None, :4]
    keep = min(WINDOW, state_win.shape[1] + 1)
    new_win = jnp.concatenate([state_win, kv_new[:, None, 4:]], axis=1)[None, :, -keep:]

    cache_t = cache.transpose(0, 2, 3, 4, 1).reshape(n_phys, 4 * KV_W, page)
    win_t = state_win.transpose(0, 2, 3, 4, 1).reshape(SB, 2 * KV_W, state_win.shape[1])
    w_rows = jnp.concatenate([_expand_cmp_weights(w_cmp_k, page).T, _expand_cmp_weights(w_cmp_v, page).T], axis=0)
    kcv = _cmp_sample(page_table, cache_t, w_rows)
    q16 = z[:, :q_w].reshape(SB, N_HEADS, HEAD_DIM)
    o_c, idx = _select_sample(q16, kcv, past)
    idx = idx[:, GQA - 1::GQA, :SLC_TOPK].reshape(-1)
    gate_logits = z[:, q_w + 6 * KV_W:cols].reshape(SB, N_BRANCH, N_HEADS).transpose(0, 2, 1)
    gate_logits = jnp.pad(gate_logits, ((0, 0), (0, 0), (0, LANES - N_BRANCH)))
    o = _attend_sample(page_table, idx, q16, o_c, gate_logits, kv_new, win_t, cache_t, past)
    return _linear_res(x, o.reshape(SB, q_w), w_out), new_pool, new_kv, new_win


def kernel(x_prompt, x_sample, state_pool, cache_kv, state_win, page_table, norm_mix, w_pool, pool_scale,
           w_nsa_in, w_cmp_k, w_cmp_v, w_nsa_out, norm_ffn, w_ffn_gate, w_ffn_up, w_ffn_down, w_router,
           w_moe_gate, w_moe_up, w_moe_down, norm_final):
    assert x_sample.shape[1] == 1 and norm_mix.shape[0] == 2 and x_prompt.shape[-1] == N_HEADS * HEAD_DIM
    w_rt = jnp.pad(w_router[0], ((0, 0), (0, LANES - w_router.shape[-1])))
    moe_w = (w_moe_gate[0].astype(BF16), w_moe_up[0].astype(BF16), w_moe_down[0].astype(BF16))

    xp, new_pool_prompt, new_kv_prompt, new_win_prompt = _prompt_mixers(
        x_prompt, norm_mix, w_pool[0], pool_scale[0], w_nsa_in[0], w_cmp_k[0], w_cmp_v[0], w_nsa_out[0],
        norm_ffn[0], w_ffn_gate, w_ffn_up, w_ffn_down)
    xs, new_pool_sample, new_kv_sample, new_win_sample = _sample_mixers(
        x_sample[:, 0], state_pool[0], cache_kv[0], state_win[0], page_table, norm_mix, w_pool[0], pool_scale[0],
        w_nsa_in[0], w_cmp_k[0], w_cmp_v[0], w_nsa_out[0], norm_ffn[0], w_ffn_gate, w_ffn_up, w_ffn_down)

    yp = _moe_prompt(xp, _row(norm_ffn[1]), w_rt, *moe_w, _row(norm_final))
    ys = _ffn(xs, _row(norm_ffn[1]), *moe_w, w_rt, _row(norm_final), tm=xs.shape[0])
    return (yp.reshape(x_prompt.shape), ys.reshape(x_sample.shape), new_pool_prompt, new_pool_sample,
            new_kv_prompt, new_kv_sample, new_win_prompt, new_win_sample)
```

```python
import functools

import jax
import jax.numpy as jnp
from jax import lax
from jax.experimental import pallas as pl
from jax.experimental.pallas import tpu as pltpu

F32 = jnp.float32
BF16 = jnp.bfloat16

EPS = 1e-6
NEG = -1e30
POOL_WINDOWS = (2, 4, 8, 16)
POOL_HALO = 16
N_HEADS = 16
N_KV_HEADS = 4
GQA = N_HEADS // N_KV_HEADS
HEAD_DIM = 64
KV_W = N_KV_HEADS * HEAD_DIM
CMP_BLOCK = 32
SLC_BLOCK = 64
SLC_TOPK = 16
WINDOW = 512
N_BRANCH = 3
FORCE_BONUS = 1000.0
TOP_K = 2
LANES = 128
VMEM_LIMIT = 56 * 1024 * 1024

POOL_ROW_TILE = 512
FFN_ROW_TILE = 1024
FFN_COL_TILE = 512
PROJ_ROW_TILE = 512
ATTN_Q_TILE = 256
ATTN_KEY_CHUNK = 512
ROUTE_ROW_TILE = 512
EXPERT_ROW_TILE = 768
COMBINE_ROW_TILE = 256
SAMPLE_PROJ_COL_TILE = 384


def _cparams(sem):
    return pltpu.CompilerParams(dimension_semantics=sem, vmem_limit_bytes=VMEM_LIMIT)


def _rmsnorm(x, g):
    ms = jnp.mean(x * x, axis=-1, keepdims=True)
    return (x * lax.rsqrt(ms + EPS)) * g


def _dot(a, b):
    return jnp.dot(a, b, preferred_element_type=F32)


def _dot_nt(a, b):
    return lax.dot_general(a, b, (((1,), (1,)), ((), ())), preferred_element_type=F32)


def _split(a):
    hi = a.astype(BF16)
    return hi, (a - hi.astype(F32)).astype(BF16)


def _dot3(a, b, nt=False):
    d = _dot_nt if nt else _dot
    ah, al = _split(a)
    bh, bl = _split(b)
    return d(ah, bh) + (d(ah, bl) + d(al, bh))


def _masked_softmax(s, mask):
    s = jnp.where(mask, s, NEG)
    m = jnp.max(s, axis=-1, keepdims=True)
    p = jnp.where(mask, jnp.exp(s - m), 0.0)
    return p / jnp.maximum(jnp.sum(p, axis=-1, keepdims=True), 1e-30)


def _alibi_slope(h):
    return 2.0 ** (-8.0 * (h + 1) / N_HEADS)


def _head_slopes(rows, rows_per_head, first_head):
    h = lax.broadcasted_iota(jnp.int32, (rows, 1), 0) // rows_per_head
    slopes = jnp.zeros((rows, 1), F32)
    for j in range(rows // rows_per_head):
        slopes = jnp.where(h == j, _alibi_slope(first_head + j), slopes)
    return slopes


def _pool_prompt_kernel(x_ref, halo_ref, g_ref, w_ref, sc_ref, o_ref, hl_ref, full_ref, *, tile):
    t = pl.program_id(1)
    g = g_ref[...]
    x = x_ref[0]
    h = _rmsnorm(x, g)
    hh = _rmsnorm(halo_ref[0], g)
    full_ref[0:POOL_HALO, :] = jnp.where(t > 0, hh, 0.0)
    full_ref[POOL_HALO:POOL_HALO + tile, :] = h
    row = t * tile + lax.broadcasted_iota(jnp.int32, (tile, 1), 0)
    group = x.shape[-1] // len(POOL_WINDOWS)
    parts = []
    for gi, w in enumerate(POOL_WINDOWS):
        cs = slice(gi * group, (gi + 1) * group)
        hg = h[:, cs]
        acc = hg
        for k in range(1, w):
            acc = acc + full_ref[POOL_HALO - k:POOL_HALO - k + tile, cs]
        cnt = jnp.minimum(row + 1, w).astype(F32)
        pooled = acc / cnt - hg
        parts.append(_dot(pooled.astype(BF16), w_ref[gi]))
    o_ref[0] = x + jnp.concatenate(parts, axis=-1) * sc_ref[...]

    @pl.when(t == pl.num_programs(1) - 1)
    def _():
        hl_ref[0] = h[tile - POOL_HALO:, :]


def _pool_prompt(x, g, w_pool, scale, tile=POOL_ROW_TILE):
    B, L, D = x.shape
    assert L % tile == 0 and tile % POOL_HALO == 0
    hb = tile // POOL_HALO
    return pl.pallas_call(
        functools.partial(_pool_prompt_kernel, tile=tile),
        grid=(B, L // tile),
        in_specs=[
            pl.BlockSpec((1, tile, D), lambda b, t: (b, t, 0)),
            pl.BlockSpec((1, POOL_HALO, D), lambda b, t: (b, jnp.maximum(t * hb - 1, 0), 0)),
            pl.BlockSpec((1, D), lambda b, t: (0, 0)),
            pl.BlockSpec(w_pool.shape, lambda b, t: (0, 0, 0)),
            pl.BlockSpec((1, D), lambda b, t: (0, 0)),
        ],
        out_specs=[
            pl.BlockSpec((1, tile, D), lambda b, t: (b, t, 0)),
            pl.BlockSpec((1, POOL_HALO, D), lambda b, t: (b, 0, 0)),
        ],
        out_shape=[jax.ShapeDtypeStruct((B, L, D), F32), jax.ShapeDtypeStruct((B, POOL_HALO, D), F32)],
        scratch_shapes=[pltpu.VMEM((tile + POOL_HALO, D), F32)],
        compiler_params=_cparams(("parallel", "arbitrary")),
        name="pool_prompt",
    )(x, x, g, w_pool, scale)


def _pool_sample_kernel(x_ref, st_ref, g_ref, w_ref, sc_ref, o_ref, h_ref):
    x = x_ref[...]
    h = _rmsnorm(x, g_ref[...])
    h_ref[...] = h
    P = st_ref.shape[0]
    group = x.shape[-1] // len(POOL_WINDOWS)
    parts = []
    for gi, w in enumerate(POOL_WINDOWS):
        cs = slice(gi * group, (gi + 1) * group)
        hg = h[:, cs]
        acc = hg
        for k in range(1, w):
            acc = acc + st_ref[P - k][:, cs]
        pooled = acc / float(w) - hg
        parts.append(_dot3(pooled, w_ref[gi]))
    o_ref[...] = x + jnp.concatenate(parts, axis=-1) * sc_ref[...]


def _pool_sample(x, state_t, g, w_pool, scale):
    B, D = x.shape
    assert state_t.shape[0] >= max(POOL_WINDOWS) - 1
    return pl.pallas_call(
        _pool_sample_kernel,
        out_shape=[jax.ShapeDtypeStruct((B, D), F32), jax.ShapeDtypeStruct((B, D), F32)],
        compiler_params=pltpu.CompilerParams(vmem_limit_bytes=VMEM_LIMIT),
        name="pool_sample",
    )(x, state_t, g, w_pool, scale)


def _ffn_kernel(*refs, n_experts, final_norm, precise):
    moe = n_experts > 1
    it = iter(refs)
    x_ref, g_ref = next(it), next(it)
    wr_ref = next(it) if moe else None
    wg_ref, wu_ref, wd_ref = next(it), next(it), next(it)
    gf_ref = next(it) if final_norm else None
    o_ref, h_scr, acc_scr = next(it), next(it), next(it)
    eacc_scr, gate_scr = (next(it), next(it)) if moe else (None, None)

    e, f = pl.program_id(1), pl.program_id(2)
    last_f = f == pl.num_programs(2) - 1

    @pl.when((e == 0) & (f == 0))
    def _():
        h = _rmsnorm(x_ref[...], g_ref[...])
        h_scr[...] = h.astype(h_scr.dtype)
        acc_scr[...] = jnp.zeros_like(acc_scr)
        if moe:
            logits = _dot3(h, wr_ref[...])
            lane, i1, i2, w1, w2 = _top2_gates(logits, n_experts)
            gate_scr[...] = jnp.where(lane == i1, w1, 0.0) + jnp.where(lane == i2, w2, 0.0)

    mm = _dot3 if precise else _dot
    hb = h_scr[...]
    a = mm(hb, wg_ref[0])
    u = mm(hb, wu_ref[0])
    act = (a * jax.nn.sigmoid(a)) * u
    y = mm(act.astype(hb.dtype), wd_ref[0])

    if moe:
        @pl.when(f == 0)
        def _():
            eacc_scr[...] = y

        @pl.when(f > 0)
        def _():
            eacc_scr[...] += y

        @pl.when(last_f)
        def _():
            gate = gate_scr[...]
            lane = lax.broadcasted_iota(jnp.int32, gate.shape, 1)
            ge = jnp.sum(jnp.where(lane == e, gate, 0.0), axis=-1, keepdims=True)
            acc_scr[...] += ge * eacc_scr[...]
    else:
        acc_scr[...] += y

    @pl.when((e == pl.num_programs(1) - 1) & last_f)
    def _():
        out = x_ref[...] + acc_scr[...]
        if final_norm:
            out = _rmsnorm(out, gf_ref[...])
        o_ref[...] = out


def _ffn(x, g, wg, wu, wd, w_router=None, g_final=None, *, tm, tf=FFN_COL_TILE):
    precise = wg.dtype == F32
    N, D = x.shape
    E, _, F = wg.shape
    assert N % tm == 0 and F % tf == 0
    moe = w_router is not None
    assert moe == (E > 1)
    final_norm = g_final is not None
    const2 = lambda i, e, f: (0, 0)
    args, in_specs = [x, g], [pl.BlockSpec((tm, D), lambda i, e, f: (i, 0)), pl.BlockSpec((1, D), const2)]
    if moe:
        args.append(w_router)
        in_specs.append(pl.BlockSpec(w_router.shape, const2))
    args += [wg, wu, wd]
    in_specs += [
        pl.BlockSpec((1, D, tf), lambda i, e, f: (e, 0, f)),
        pl.BlockSpec((1, D, tf), lambda i, e, f: (e, 0, f)),
        pl.BlockSpec((1, tf, D), lambda i, e, f: (e, f, 0)),
    ]
    if final_norm:
        args.append(g_final)
        in_specs.append(pl.BlockSpec((1, D), const2))
    scratch = [pltpu.VMEM((tm, D), F32 if precise else BF16), pltpu.VMEM((tm, D), F32)]
    if moe:
        scratch += [pltpu.VMEM((tm, D), F32), pltpu.VMEM((tm, LANES), F32)]
    return pl.pallas_call(
        functools.partial(_ffn_kernel, n_experts=E, final_norm=final_norm, precise=precise),
        grid=(N // tm, E, F // tf),
        in_specs=in_specs,
        out_specs=pl.BlockSpec((tm, D), lambda i, e, f: (i, 0)),
        out_shape=jax.ShapeDtypeStruct((N, D), F32),
        scratch_shapes=scratch,
        compiler_params=_cparams(("parallel", "arbitrary", "arbitrary")),
        name="moe_ffn" if moe else "dense_ffn",
    )(*args)


ROUTE_I1, ROUTE_I2, ROUTE_W1, ROUTE_W2, ROUTE_R1, ROUTE_R2 = range(6)


def _top2_gates(logits, n_experts):
    lane = lax.broadcasted_iota(jnp.int32, logits.shape, 1)
    lg = jnp.where(lane < n_experts, logits, -jnp.inf)
    m1 = jnp.max(lg, axis=-1, keepdims=True)
    i1 = jnp.min(jnp.where(lg == m1, lane, LANES), axis=-1, keepdims=True)
    lg2 = jnp.where(lane == i1, -jnp.inf, lg)
    m2 = jnp.max(lg2, axis=-1, keepdims=True)
    i2 = jnp.min(jnp.where(lg2 == m2, lane, LANES), axis=-1, keepdims=True)
    e2 = jnp.exp(m2 - m1)
    den = 1.0 + e2
    return lane, i1, i2, 1.0 / den, e2 / den


def _moe_route_kernel(x_ref, g_ref, wr_ref, tri_ref, route_ref, cnt_ref, carry, *, n_experts):
    @pl.when(pl.program_id(0) == 0)
    def _():
        carry[...] = jnp.zeros_like(carry)

    h = _rmsnorm(x_ref[...], g_ref[...])
    logits = _dot3(h, wr_ref[...])
    lane, i1, i2, w1, w2 = _top2_gates(logits, n_experts)
    member = jnp.where((lane == i1) | (lane == i2), 1.0, 0.0)
    before = _dot(tri_ref[...], member.astype(BF16)) + carry[0:1, :]
    r1 = jnp.sum(jnp.where(lane == i1, before, 0.0), axis=-1, keepdims=True)
    r2 = jnp.sum(jnp.where(lane == i2, before, 0.0), axis=-1, keepdims=True)
    rec = jnp.zeros(logits.shape, F32)
    for col, val in ((ROUTE_I1, i1.astype(F32)), (ROUTE_I2, i2.astype(F32)), (ROUTE_W1, w1), (ROUTE_W2, w2),
                     (ROUTE_R1, r1), (ROUTE_R2, r2)):
        rec = jnp.where(lane == col, val, rec)
    route_ref[...] = rec
    carry[...] = carry[...] + jnp.sum(member, axis=0, keepdims=True)
    cnt_ref[...] = carry[...]


def _moe_route(x, g, w_router, n_experts, tm=ROUTE_ROW_TILE):
    N, D = x.shape
    assert N % tm == 0
    tri = (jnp.arange(tm)[:, None] > jnp.arange(tm)[None, :]).astype(BF16)
    return pl.pallas_call(
        functools.partial(_moe_route_kernel, n_experts=n_experts),
        grid=(N // tm,),
        in_specs=[pl.BlockSpec((tm, D), lambda i: (i, 0)), pl.BlockSpec((1, D), lambda i: (0, 0)),
                  pl.BlockSpec(w_router.shape, lambda i: (0, 0)), pl.BlockSpec((tm, tm), lambda i: (0, 0))],
        out_specs=[pl.BlockSpec((tm, LANES), lambda i: (i, 0)), pl.BlockSpec((8, LANES), lambda i: (0, 0))],
        out_shape=[jax.ShapeDtypeStruct((N, LANES), F32), jax.ShapeDtypeStruct((8, LANES), F32)],
        scratch_shapes=[pltpu.VMEM((8, LANES), F32)],
        compiler_params=_cparams(("arbitrary",)),
        name="moe_route",
    )(x, g, w_router, tri)


GATHER_UNROLL = 8
GATHER_PRIORITY = 1


def _row_gather(idx_of, src_hbm, dst_at, sem, rows, priority):
    assert rows % GATHER_UNROLL == 0

    def copy(r):
        return pltpu.make_async_copy(src_hbm.at[pl.ds(idx_of(r), 1), :], dst_at(r), sem)

    def start():
        def group(i, c):
            for j in range(GATHER_UNROLL):
                copy(i * GATHER_UNROLL + j).start(priority=priority(j))
            return c
        lax.fori_loop(0, rows // GATHER_UNROLL, group, 0)

    def wait():
        lax.fori_loop(0, rows, lambda r, c: (copy(r).wait(), c)[1], 0, unroll=GATHER_UNROLL)

    return start, wait, copy


def _moe_experts_kernel(te_ref, nu_ref, tos_ref, x_hbm, g_ref, wg_ref, wu_ref, wd_ref, o_ref,
                        xbuf, h_scr, acc_scr, sems, *, tile, nf):
    i, f = pl.program_id(0), pl.program_id(1)
    used = i < nu_ref[0]
    slot = i % 2
    per_step = tile // (nf - 1)

    def gather(t, sl):
        return _row_gather(lambda r: tos_ref[t * tile + r], x_hbm,
                           lambda r: xbuf.at[sl, pl.ds(r, 1), :], sems.at[sl], tile, lambda j: GATHER_PRIORITY)

    def prefetch_share():
        row_copy = gather(i + 1, 1 - slot)[2]
        for j in range(per_step):
            row_copy(f * per_step + j).start(priority=GATHER_PRIORITY)

    @pl.when((f == 0) & (i == 0))
    def _():
        gather(0, 0)[0]()

    @pl.when(f == 0)
    def _():
        gather(i, slot)[1]()

    @pl.when((f == 0) & used)
    def _():
        h_scr[...] = _rmsnorm(xbuf[slot], g_ref[...]).astype(BF16)
        acc_scr[...] = jnp.zeros_like(acc_scr)

    @pl.when(f < nf - 1)
    def _():
        prefetch_share()

    @pl.when(used)
    def _():
        hb = h_scr[...]
        a = _dot(hb, wg_ref[0])
        u = _dot(hb, wu_ref[0])
        act = (a * jax.nn.sigmoid(a)) * u
        acc_scr[...] += _dot(act.astype(BF16), wd_ref[0])

    @pl.when(f == nf - 1)
    def _():
        o_ref[...] = jnp.where(used, acc_scr[...], 0.0)

    @pl.when((f == nf - 1) & (i == pl.num_programs(0) - 1))
    def _():
        gather(i + 1, 1 - slot)[1]()


def _moe_experts(x, g, wg, wu, wd, tile_expert, n_used, token_of_slot, *, tile, tf=FFN_COL_TILE):
    N, D = x.shape
    E, _, F = wg.shape
    n_slots = token_of_slot.shape[0] - tile
    assert n_slots % tile == 0 and F % tf == 0
    n_tiles, nf = n_slots // tile, F // tf
    assert nf > 1 and tile % (nf - 1) == 0
    fidx = lambda i, f, nu: jnp.where(i < nu[0], f, nf - 1)
    return pl.pallas_call(
        functools.partial(_moe_experts_kernel, tile=tile, nf=nf),
        grid_spec=pltpu.PrefetchScalarGridSpec(
            num_scalar_prefetch=3,
            grid=(n_tiles, nf),
            in_specs=[pl.BlockSpec(memory_space=pl.ANY),
                      pl.BlockSpec((1, D), lambda i, f, te, nu, tos: (0, 0)),
                      pl.BlockSpec((1, D, tf), lambda i, f, te, nu, tos: (te[i], 0, fidx(i, f, nu))),
                      pl.BlockSpec((1, D, tf), lambda i, f, te, nu, tos: (te[i], 0, fidx(i, f, nu))),
                      pl.BlockSpec((1, tf, D), lambda i, f, te, nu, tos: (te[i], fidx(i, f, nu), 0))],
            out_specs=pl.BlockSpec((tile, D), lambda i, f, te, nu, tos: (i, 0)),
            scratch_shapes=[pltpu.VMEM((2, tile, D), F32), pltpu.VMEM((tile, D), BF16),
                            pltpu.VMEM((tile, D), F32), pltpu.SemaphoreType.DMA((2,))],
        ),
        out_shape=jax.ShapeDtypeStruct((n_slots, D), F32),
        compiler_params=_cparams(("arbitrary", "arbitrary")),
        name="moe_experts",
    )(tile_expert, n_used, token_of_slot, x, g, wg, wu, wd)


def _moe_combine_kernel(slot_ref, x_ref, route_ref, gf_ref, ys_hbm, o_ref, ybuf, sems, *, tile):
    i = pl.program_id(0)
    slot = i % 2

    def gather(t, sl):
        return _row_gather(lambda j: slot_ref[t * 2 * tile + j], ys_hbm,
                           lambda j: ybuf.at[sl, pl.ds(j, 1), :], sems.at[sl], 2 * tile, lambda j: j % 2)

    @pl.when(i == 0)
    def _():
        gather(0, 0)[0]()

    gather(i, slot)[1]()

    @pl.when(i + 1 < pl.num_programs(0))
    def _():
        gather(i + 1, 1 - slot)[0]()

    route = route_ref[...]
    w1, w2 = route[:, ROUTE_W1:ROUTE_W1 + 1], route[:, ROUTE_W2:ROUTE_W2 + 1]
    y = w1 * ybuf[slot, 0:tile, :] + w2 * ybuf[slot, tile:2 * tile, :]
    o_ref[...] = _rmsnorm(x_ref[...] + y, gf_ref[...])


def _moe_combine(x, route, g_final, ys, slots, *, tile):
    N, D = x.shape
    assert N % tile == 0
    return pl.pallas_call(
        functools.partial(_moe_combine_kernel, tile=tile),
        grid_spec=pltpu.PrefetchScalarGridSpec(
            num_scalar_prefetch=1,
            grid=(N // tile,),
            in_specs=[pl.BlockSpec((tile, D), lambda i, s: (i, 0)),
                      pl.BlockSpec((tile, LANES), lambda i, s: (i, 0)),
                      pl.BlockSpec((1, D), lambda i, s: (0, 0)),
                      pl.BlockSpec(memory_space=pl.ANY)],
            out_specs=pl.BlockSpec((tile, D), lambda i, s: (i, 0)),
            scratch_shapes=[pltpu.VMEM((2, 2 * tile, D), F32), pltpu.SemaphoreType.DMA((2,))],
        ),
        out_shape=jax.ShapeDtypeStruct((N, D), F32),
        compiler_params=_cparams(("arbitrary",)),
        name="moe_combine",
    )(slots, x, route, g_final, ys)


def _moe_prompt(x, g, w_router, wg, wu, wd, g_final, *, tile=EXPERT_ROW_TILE, ctile=COMBINE_ROW_TILE):
    N, D = x.shape
    E = wg.shape[0]
    route, counts = _moe_route(x, g, w_router, E)
    cnt = counts[0, :E].astype(jnp.int32)
    padded = (cnt + tile - 1) // tile * tile
    ends = jnp.cumsum(padded)
    off = ends - padded
    i1, i2 = route[:, ROUTE_I1].astype(jnp.int32), route[:, ROUTE_I2].astype(jnp.int32)
    slot1 = off[i1] + route[:, ROUTE_R1].astype(jnp.int32)
    slot2 = off[i2] + route[:, ROUTE_R2].astype(jnp.int32)
    n_slots = (TOP_K * N + E * (tile - 1)) // tile * tile
    rows = jnp.arange(N, dtype=jnp.int32)
    token_of_slot = jnp.zeros((n_slots + tile,), jnp.int32).at[jnp.concatenate([slot1, slot2])].set(
        jnp.concatenate([rows, rows]), unique_indices=True)
    n_used = ends[-1:] // tile
    tile_start = jnp.minimum(jnp.arange(n_slots // tile, dtype=jnp.int32), n_used[0] - 1) * tile
    tile_expert = jnp.sum(tile_start[:, None] >= ends[None, :], axis=1).astype(jnp.int32)
    ys = _moe_experts(x, g, wg, wu, wd, tile_expert, n_used.astype(jnp.int32), token_of_slot, tile=tile)
    slots = jnp.stack([slot1.reshape(-1, ctile), slot2.reshape(-1, ctile)], axis=1).reshape(-1)
    return _moe_combine(x, route, g_final, ys, slots, tile=ctile)


def _nsa_proj_kernel(x_ref, g_ref, wq_ref, wkv_ref, wgt_ref, wck_ref, wcv_ref,
                     q_ref, kvt_ref, ks_ref, vs_ref, kw_ref, vw_ref, gt_ref, kc_ref, vc_ref):
    hb = _rmsnorm(x_ref[...], g_ref[...]).astype(BF16)
    q_ref[...] = _dot(hb, wq_ref[...]).astype(BF16)
    kv = _dot(hb, wkv_ref[...])
    kvt_ref[0] = kv.T
    ks_ref[...] = kv[:, 2 * KV_W:3 * KV_W].astype(BF16)
    vs_ref[...] = kv[:, 3 * KV_W:4 * KV_W].astype(BF16)
    kw_ref[...] = kv[:, 4 * KV_W:5 * KV_W].astype(BF16)
    vw_ref[...] = kv[:, 5 * KV_W:].astype(BF16)
    gt_ref[...] = jax.nn.sigmoid(_dot(hb, wgt_ref[...]))
    tm = kv.shape[0]
    kc_ref[...] = (kv[:, :KV_W] * wck_ref[...]).reshape(tm // CMP_BLOCK, CMP_BLOCK, KV_W).sum(axis=1)
    vc_ref[...] = (kv[:, KV_W:2 * KV_W] * wcv_ref[...]).reshape(tm // CMP_BLOCK, CMP_BLOCK, KV_W).sum(axis=1)


def _nsa_proj(x, g, wq, wkv, wgt, wck, wcv, *, tm, seq):
    N, D = x.shape
    assert N % tm == 0 and tm % (8 * CMP_BLOCK) == 0 and seq % tm == 0 and N % seq == 0
    per_seq = seq // tm
    row = lambda w: pl.BlockSpec((tm, w), lambda i: (i, 0))
    full = lambda a: pl.BlockSpec(a.shape, lambda i: (0, 0))
    out_shape = [jax.ShapeDtypeStruct((N, N_HEADS * HEAD_DIM), BF16),
                 jax.ShapeDtypeStruct((N // seq, 6 * KV_W, seq), F32)]
    out_shape += [jax.ShapeDtypeStruct((N, KV_W), BF16)] * 4
    out_shape += [jax.ShapeDtypeStruct((N, LANES), F32)]
    out_shape += [jax.ShapeDtypeStruct((N // CMP_BLOCK, KV_W), F32)] * 2
    out_specs = [row(N_HEADS * HEAD_DIM),
                 pl.BlockSpec((1, 6 * KV_W, tm), lambda i: (i // per_seq, 0, i % per_seq))]
    out_specs += [row(KV_W)] * 4 + [row(LANES)]
    out_specs += [pl.BlockSpec((tm // CMP_BLOCK, KV_W), lambda i: (i, 0))] * 2
    return pl.pallas_call(
        _nsa_proj_kernel,
        grid=(N // tm,),
        in_specs=[row(D), full(g), full(wq), full(wkv), full(wgt), full(wck), full(wcv)],
        out_specs=out_specs,
        out_shape=out_shape,
        compiler_params=_cparams(("parallel",)),
        name="nsa_proj_prompt",
    )(x, g, wq, wkv, wgt, wck, wcv)


def _nsa_proj_sample_kernel(x_ref, g_ref, w_ref, z_ref):
    z_ref[...] = _dot3(_rmsnorm(x_ref[...], g_ref[...]), w_ref[...])


def _nsa_proj_sample(x, g, w_in, tn=SAMPLE_PROJ_COL_TILE):
    B, D = x.shape
    cols = w_in.shape[1]
    assert cols % tn == 0
    return pl.pallas_call(
        _nsa_proj_sample_kernel,
        grid=(cols // tn,),
        in_specs=[pl.BlockSpec((B, D), lambda j: (0, 0)), pl.BlockSpec((1, D), lambda j: (0, 0)),
                  pl.BlockSpec((D, tn), lambda j: (0, j))],
        out_specs=pl.BlockSpec((B, tn), lambda j: (0, j)),
        out_shape=jax.ShapeDtypeStruct((B, cols), F32),
        compiler_params=_cparams(("parallel",)),
        name="nsa_proj_sample",
    )(x, g, w_in)


def _nsa_prompt_kernel(q_ref, gt_ref, kc_ref, vc_ref, ks_ref, vs_ref, kw_ref, vw_ref, x_ref, wo_ref,
                       o_ref, o_scr, sel_scr, *, tq, kchunk, seq):
    t0 = pl.program_id(1) * tq
    rows = GQA * tq
    scale = HEAD_DIM ** -0.5
    assert scale == 0.125
    n_blocks = seq // SLC_BLOCK
    top_k = min(SLC_TOPK, n_blocks)

    q = q_ref[0]
    gates = gt_ref[0]
    qpos_t = t0 + lax.broadcasted_iota(jnp.int32, (tq, 1), 0)
    qpos = jnp.concatenate([qpos_t] * GQA, axis=0)
    tq_f = qpos.astype(F32)

    lane = lax.broadcasted_iota(jnp.int32, (tq, LANES), 1)
    jblk = lane >> 1
    real = ((lane & 1) == 0) & (jblk < n_blocks)
    blk = qpos_t // SLC_BLOCK
    valid = real & (jblk <= blk)
    forced = (jblk == 0) | (jblk == blk) | (jblk == blk - 1)
    valid_f = jnp.where(valid, 1.0, 0.0)
    valid_b = valid_f.astype(BF16)
    bonus = jnp.where(forced, FORCE_BONUS, 0.0)
    floor = jnp.where(valid, 0.0, jnp.where(real, -1.0, -2.0))
    c_lane = lax.broadcasted_iota(jnp.int32, (1, LANES), 1)
    c_mid = c_lane.astype(F32) * CMP_BLOCK + (CMP_BLOCK - 1) / 2
    c_end = (c_lane + 1) * CMP_BLOCK - 1

    n_chunks = (t0 + tq + kchunk - 1) // kchunk
    wstart = pl.multiple_of(jnp.maximum(t0 - WINDOW, 0), tq)
    wlen = WINDOW + tq

    need_rank = (t0 + tq - 1) // SLC_BLOCK >= top_k
    wpos = wstart + lax.broadcasted_iota(jnp.int32, (1, wlen), 1)
    w_dist = jnp.where((wpos <= qpos_t) & (qpos_t - wpos < WINDOW), (wpos - qpos_t).astype(F32), NEG)

    def with_ones(v):
        return jnp.concatenate([v, jnp.ones_like(v)], axis=1)

    def biased(scores, dist, kv):
        return jnp.concatenate(
            [scores[g * tq:(g + 1) * tq] + _alibi_slope(kv * GQA + g) * dist for g in range(GQA)], axis=0)

    for kv in range(N_KV_HEADS):
        hs = slice(kv * HEAD_DIM, (kv + 1) * HEAD_DIM)
        qs = jnp.concatenate(
            [q[:, (kv * GQA + g) * HEAD_DIM:(kv * GQA + g + 1) * HEAD_DIM] for g in range(GQA)], axis=0) * scale
        slope = _head_slopes(rows, tq, kv * GQA)

        s_c = _dot_nt(qs, kc_ref[0][:, hs]) - slope * (tq_f - c_mid)
        p_c = _masked_softmax(s_c, c_end <= qpos)
        o_c = _dot(p_c.astype(BF16), vc_ref[0][:, hs])

        imp = p_c[0:tq]
        for g in range(1, GQA):
            imp = imp + p_c[g * tq:(g + 1) * tq]
        sel_scr[...] = valid_b

        @pl.when(need_rank)
        def _():
            pair = imp + pltpu.roll(imp, LANES - 1, 1)
            score = valid_f * (pair + bonus) + floor
            score_t = score.T[:2 * n_blocks]
            row_id = lax.broadcasted_iota(jnp.int32, score_t.shape, 0)
            rank_t = jnp.zeros(score_t.shape, jnp.int32)
            for k in range(n_blocks):
                other = score_t[2 * k:2 * k + 1, :]
                beats = (other > score_t) | ((other == score_t) & (row_id > 2 * k))
                rank_t = rank_t + beats.astype(jnp.int32)
            top_t = jnp.where(rank_t < top_k, 1.0, 0.0)
            top = jnp.concatenate([top_t, jnp.zeros((LANES - 2 * n_blocks, tq), F32)], axis=0).T
            sel_scr[...] = (top * valid_f).astype(BF16)

        sel = sel_scr[...]

        def chunk(c, carry):
            m, acc = carry
            k0 = pl.multiple_of(c * kchunk, kchunk)
            kb = ks_ref[0, pl.ds(k0, kchunk), hs]
            vb = with_ones(vs_ref[0, pl.ds(k0, kchunk), hs])
            pos = k0 + lax.broadcasted_iota(jnp.int32, (1, kchunk), 1)
            expand = (lax.broadcasted_iota(jnp.int32, (LANES, kchunk), 0)
                      == 2 * ((k0 + lax.broadcasted_iota(jnp.int32, (LANES, kchunk), 1)) // SLC_BLOCK))
            picked = _dot(sel, jnp.where(expand, 1.0, 0.0).astype(BF16))
            ok = (picked > 0.5) & (pos <= qpos_t)
            s = biased(_dot_nt(qs, kb), jnp.where(ok, (pos - qpos_t).astype(F32), NEG), kv)
            m_new = jnp.maximum(m, jnp.max(s, axis=-1, keepdims=True))
            p = jnp.exp(s - m_new)
            acc = jnp.exp(m - m_new) * acc + _dot(p.astype(BF16), vb)
            return m_new, acc

        init = (jnp.full((rows, 1), NEG, F32), jnp.zeros((rows, 2 * HEAD_DIM), F32))
        _, acc_s = lax.fori_loop(0, n_chunks, chunk, init)

        kb = kw_ref[0, pl.ds(wstart, wlen), hs]
        vb = with_ones(vw_ref[0, pl.ds(wstart, wlen), hs])
        s_w = biased(_dot_nt(qs, kb), w_dist, kv)
        p_w = jnp.exp(s_w - jnp.max(s_w, axis=-1, keepdims=True))
        acc_w = _dot(p_w.astype(BF16), vb)

        def gate_col(n):
            base = n * N_HEADS + kv * GQA
            return jnp.concatenate([gates[:, base + g:base + g + 1] for g in range(GQA)], axis=0)

        g_s = gate_col(1) / acc_s[:, HEAD_DIM:HEAD_DIM + 1]
        g_w = gate_col(2) / acc_w[:, HEAD_DIM:HEAD_DIM + 1]
        o = gate_col(0) * o_c + g_s * acc_s[:, :HEAD_DIM] + g_w * acc_w[:, :HEAD_DIM]
        for g in range(GQA):
            h = kv * GQA + g
            o_scr[:, h * HEAD_DIM:(h + 1) * HEAD_DIM] = o[g * tq:(g + 1) * tq]

    o_ref[0] = x_ref[0] + _dot(o_scr[...].astype(BF16), wo_ref[...])


def _nsa_prompt(q, gates, kc, vc, ks, vs, kw, vw, x, w_o, *, tq=ATTN_Q_TILE, kchunk=ATTN_KEY_CHUNK):
    B, L, D = x.shape
    assert L % kchunk == 0 and kchunk % tq == 0 and L >= WINDOW + tq and L % SLC_BLOCK == 0
    assert 2 * (L // SLC_BLOCK) <= LANES and L // CMP_BLOCK <= LANES and kc.shape[1] == LANES
    tile = lambda w: pl.BlockSpec((1, tq, w), lambda b, t: (b, t, 0))
    whole = lambda a: pl.BlockSpec((1,) + a.shape[1:], lambda b, t: (b, 0, 0))
    return pl.pallas_call(
        functools.partial(_nsa_prompt_kernel, tq=tq, kchunk=kchunk, seq=L),
        grid=(B, L // tq),
        in_specs=[tile(N_HEADS * HEAD_DIM), tile(LANES), whole(kc), whole(vc), whole(ks), whole(vs),
                  whole(kw), whole(vw), tile(D), pl.BlockSpec(w_o.shape, lambda b, t: (0, 0))],
        out_specs=tile(D),
        out_shape=jax.ShapeDtypeStruct((B, L, D), F32),
        scratch_shapes=[pltpu.VMEM((tq, N_HEADS * HEAD_DIM), F32), pltpu.VMEM((tq, LANES), BF16)],
        compiler_params=_cparams(("parallel", "arbitrary")),
        name="nsa_attn_prompt",
    )(q, gates, kc, vc, ks, vs, kw, vw, x, w_o)


CMP_GROUP = 32
MXU_DEPTH = 256


def _cmp_sample_kernel(pt_ref, cache_ref, w_ref, s_ref, o_ref, buf, sems, *, rows):
    b, g = pl.program_id(0), pl.program_id(1)
    n_groups = pl.num_programs(1)
    step = b * n_groups + g
    slot = step % 2

    def page_copy(bb, gg, sl, i):
        page = pt_ref[bb, gg * CMP_GROUP + i]
        return pltpu.make_async_copy(cache_ref.at[page, pl.ds(0, rows), :], buf.at[sl, i], sems.at[sl])

    def start_group(bb, gg, sl):
        for i in range(CMP_GROUP):
            page_copy(bb, gg, sl, i).start()

    @pl.when(step == 0)
    def _():
        start_group(b, g, slot)

    @pl.when(step + 1 < pl.num_programs(0) * n_groups)
    def _():
        nxt = step + 1
        start_group(nxt // n_groups, nxt % n_groups, 1 - slot)

    for i in range(CMP_GROUP):
        page_copy(b, g, slot, i).wait()

    w = w_ref[...]
    acc = jnp.zeros((rows, LANES), F32)
    for j in range(CMP_GROUP // 2):
        p = jnp.concatenate([buf[slot, 2 * j] * w, buf[slot, 2 * j + 1] * w], axis=1)
        hi, lo = _split(p)
        r = _dot(jnp.concatenate([hi, lo], axis=0), s_ref[j])
        acc = acc + (r[:rows] + r[rows:])
    o_ref[0] = acc


def _cmp_sample(page_table, cache_t, w_rows):
    B, n_pages = page_table.shape
    _, _, page = cache_t.shape
    rows = w_rows.shape[0]
    per_page = page // CMP_BLOCK
    assert n_pages % CMP_GROUP == 0 and CMP_GROUP * per_page == LANES and 2 * page == MXU_DEPTH
    k = jnp.arange(2 * page)
    token = (2 * jnp.arange(CMP_GROUP // 2)[:, None] + k[None, :] // page) * per_page + (k[None, :] % page) // CMP_BLOCK
    block_sum = (token[:, :, None] == jnp.arange(LANES)[None, None, :]).astype(BF16)
    return pl.pallas_call(
        functools.partial(_cmp_sample_kernel, rows=rows),
        grid_spec=pltpu.PrefetchScalarGridSpec(
            num_scalar_prefetch=1,
            grid=(B, n_pages // CMP_GROUP),
            in_specs=[pl.BlockSpec(memory_space=pl.ANY),
                      pl.BlockSpec(w_rows.shape, lambda b, g, pt: (0, 0)),
                      pl.BlockSpec(block_sum.shape, lambda b, g, pt: (0, 0, 0))],
            out_specs=pl.BlockSpec((1, rows, LANES), lambda b, g, pt: (b, 0, g)),
            scratch_shapes=[pltpu.VMEM((2, CMP_GROUP, rows, page), F32), pltpu.SemaphoreType.DMA((2,))],
        ),
        out_shape=jax.ShapeDtypeStruct((B, rows, n_pages * per_page), F32),
        compiler_params=_cparams(("arbitrary", "arbitrary")),
        name="cmp_sample",
    )(page_table, cache_t, w_rows, block_sum)


def _select_sample_kernel(q_ref, kcv_ref, oc_ref, idx_ref, *, past):
    scale = HEAD_DIM ** -0.5
    q = q_ref[0]
    nc = kcv_ref.shape[2]
    n_past_blocks = past // SLC_BLOCK
    per_block = SLC_BLOCK // CMP_BLOCK
    row = lax.broadcasted_iota(jnp.int32, (N_HEADS, 1), 0)
    slope = _head_slopes(N_HEADS, 1, 0)
    tok = lax.broadcasted_iota(jnp.int32, (1, nc), 1)
    c_mid = tok.astype(F32) * CMP_BLOCK + (CMP_BLOCK - 1) / 2
    c_ok = (tok + 1) * CMP_BLOCK - 1 <= past
    bias = slope * (float(past) - c_mid)

    s_c = jnp.zeros((N_HEADS, nc), F32)
    for kv in range(N_KV_HEADS):
        s_kv = _dot3(q, kcv_ref[0, kv * HEAD_DIM:(kv + 1) * HEAD_DIM, :])
        s_c = jnp.where(row // GQA == kv, s_kv, s_c)
    p_c = _masked_softmax(s_c * scale - bias, c_ok)
    o_c = jnp.zeros((N_HEADS, HEAD_DIM), F32)
    for kv in range(N_KV_HEADS):
        o_kv = _dot3(p_c, kcv_ref[0, KV_W + kv * HEAD_DIM:KV_W + (kv + 1) * HEAD_DIM, :], nt=True)
        o_c = jnp.where(row // GQA == kv, o_kv, o_c)
    oc_ref[0] = o_c

    imp = p_c
    shift = 1
    while shift < GQA:
        imp = imp + pltpu.roll(imp, shift, 0)
        shift *= 2
    assert per_block == 2
    pair = imp + pltpu.roll(imp, nc - 1, 1)
    jblk = tok // per_block
    real = tok % per_block == 0
    forced = (jblk == 0) | (jblk == n_past_blocks - 1)
    score = jnp.where(real, pair + jnp.where(forced, FORCE_BONUS, 0.0), -2.0)
    own = jnp.float32(FORCE_BONUS)
    rank = (own > score).astype(jnp.int32)
    for k in range(n_past_blocks):
        col = score[:, per_block * k:per_block * k + 1]
        beats = (col > score) | ((col == score) & (jblk > k))
        rank = rank + beats.astype(jnp.int32)
    own_rank = jnp.sum(jnp.where(real & (score >= own), 1, 0), axis=-1, keepdims=True)
    out_lane = lax.broadcasted_iota(jnp.int32, (N_HEADS, LANES), 1)
    idx = jnp.zeros((N_HEADS, LANES), jnp.int32)
    for r in range(SLC_TOPK):
        hit = jnp.sum(jnp.where(real & (rank == r), jblk, 0), axis=-1, keepdims=True)
        hit = hit + jnp.where(own_rank == r, n_past_blocks, 0)
        idx = jnp.where(out_lane == r, hit, idx)
    idx_ref[0] = idx


def _select_sample(q16, kcv, past):
    B = q16.shape[0]
    nc = kcv.shape[2]
    assert past % SLC_BLOCK == 0 and nc == past // CMP_BLOCK and nc % LANES == 0
    assert past // SLC_BLOCK + 1 > SLC_TOPK
    return pl.pallas_call(
        functools.partial(_select_sample_kernel, past=past),
        grid=(B,),
        in_specs=[pl.BlockSpec((1,) + q16.shape[1:], lambda b: (b, 0, 0)),
                  pl.BlockSpec((1,) + kcv.shape[1:], lambda b: (b, 0, 0))],
        out_specs=[pl.BlockSpec((1, N_HEADS, HEAD_DIM), lambda b: (b, 0, 0)),
                   pl.BlockSpec((1, N_HEADS, LANES), lambda b: (b, 0, 0))],
        out_shape=[jax.ShapeDtypeStruct((B, N_HEADS, HEAD_DIM), F32),
                   jax.ShapeDtypeStruct((B, N_HEADS, LANES), jnp.int32)],
        compiler_params=_cparams(("parallel",)),
        name="select_sample",
    )(q16, kcv)


def _attend_sample_kernel(pt_ref, idx_ref, q_ref, oc_ref, gt_ref, new_ref, win_ref, cache_ref,
                          o_ref, kbuf, vbuf, sems, *, past):
    b = pl.program_id(0)
    scale = HEAD_DIM ** -0.5
    n_past_blocks = past // SLC_BLOCK
    page_rows = cache_ref.shape[2]
    blocks_per_page = page_rows // SLC_BLOCK
    nkeys = SLC_TOPK * page_rows

    def block_copies(kv, n):
        blk = idx_ref[(b * N_KV_HEADS + kv) * SLC_TOPK + n]
        in_past = blk < n_past_blocks
        page = pt_ref[b, jnp.minimum(blk, n_past_blocks - 1) // blocks_per_page]
        dst = pl.ds(n * page_rows, page_rows)
        ck = pltpu.make_async_copy(cache_ref.at[page, pl.ds((2 * N_KV_HEADS + kv) * HEAD_DIM, HEAD_DIM), :],
                                   kbuf.at[kv, :, dst], sems.at[0, kv, n])
        cv = pltpu.make_async_copy(cache_ref.at[page, pl.ds((3 * N_KV_HEADS + kv) * HEAD_DIM, HEAD_DIM), :],
                                   vbuf.at[kv, :, dst], sems.at[1, kv, n])
        return blk, in_past, ck, cv

    for kv in range(N_KV_HEADS):
        for n in range(SLC_TOPK):
            _, in_past, ck, cv = block_copies(kv, n)

            @pl.when(in_past)
            def _():
                ck.start()
                cv.start()

            @pl.when(jnp.logical_not(in_past))
            def _():
                kbuf[kv, :, n * page_rows:(n + 1) * page_rows] = jnp.zeros((HEAD_DIM, page_rows), F32)
                vbuf[kv, :, n * page_rows:(n + 1) * page_rows] = jnp.zeros((HEAD_DIM, page_rows), F32)

    q = q_ref[0]
    row = lax.broadcasted_iota(jnp.int32, (N_HEADS, 1), 0)
    slope = _head_slopes(N_HEADS, 1, 0)
    key_lane = lax.broadcasted_iota(jnp.int32, (1, nkeys), 1)

    wb = win_ref.shape[2]
    w_lane = lax.broadcasted_iota(jnp.int32, (1, wb), 1)
    wpos = past - wb + w_lane
    w_ok = (past - wpos < WINDOW) & (wpos >= 0)
    w_bias = slope * (past - wpos).astype(F32)
    s_w = jnp.zeros((N_HEADS, wb), F32)
    s_n = jnp.zeros((N_HEADS, 1), F32)
    for kv in range(N_KV_HEADS):
        mine = row // GQA == kv
        s_kv = _dot3(q, win_ref[0, kv * HEAD_DIM:(kv + 1) * HEAD_DIM, :])
        s_w = jnp.where(mine, s_kv, s_w)
        s_n = jnp.where(mine, jnp.sum(q * new_ref[0, 4, kv:kv + 1, :], axis=-1, keepdims=True), s_n)
    s_w = jnp.where(w_ok, s_w * scale - w_bias, NEG)
    s_n = s_n * scale
    m_w = jnp.maximum(jnp.max(s_w, axis=-1, keepdims=True), s_n)
    p_w = jnp.where(w_ok, jnp.exp(s_w - m_w), 0.0)
    p_n = jnp.exp(s_n - m_w)
    l_w = jnp.maximum(jnp.sum(p_w, axis=-1, keepdims=True) + p_n, 1e-30)
    p_w = p_w / l_w
    p_n = p_n / l_w
    o_w = jnp.zeros((N_HEADS, HEAD_DIM), F32)
    for kv in range(N_KV_HEADS):
        o_kv = (_dot3(p_w, win_ref[0, KV_W + kv * HEAD_DIM:KV_W + (kv + 1) * HEAD_DIM, :], nt=True)
                + p_n * new_ref[0, 5, kv:kv + 1, :])
        o_w = jnp.where(row // GQA == kv, o_kv, o_w)

    for kv in range(N_KV_HEADS):
        for n in range(SLC_TOPK):
            _, in_past, ck, cv = block_copies(kv, n)

            @pl.when(in_past)
            def _():
                ck.wait()
                cv.wait()

    s_s = jnp.zeros((N_HEADS, nkeys), F32)
    pos = jnp.zeros((N_HEADS, nkeys), jnp.int32)
    live = jnp.zeros((N_HEADS, nkeys), jnp.int32)
    own = jnp.zeros((N_HEADS, 1), jnp.int32)
    s_n = jnp.zeros((N_HEADS, 1), F32)
    in_slab = key_lane % page_rows
    for kv in range(N_KV_HEADS):
        mine = row // GQA == kv
        s_kv = _dot3(q, kbuf[kv])
        s_s = jnp.where(mine, s_kv, s_s)
        s_n = jnp.where(mine, jnp.sum(q * new_ref[0, 2, kv:kv + 1, :], axis=-1, keepdims=True), s_n)
        base_kv = jnp.zeros((1, nkeys), jnp.int32)
        half_kv = jnp.zeros((1, nkeys), jnp.int32)
        own_kv = jnp.int32(0)
        for n in range(SLC_TOPK):
            blk = idx_ref[(b * N_KV_HEADS + kv) * SLC_TOPK + n]
            here = key_lane // page_rows == n
            base_kv = jnp.where(here, (blk // blocks_per_page) * page_rows, base_kv)
            half_kv = jnp.where(here, blk % blocks_per_page, half_kv)
            own_kv = own_kv + (blk >= n_past_blocks).astype(jnp.int32)
        pos = jnp.where(mine, base_kv + in_slab, pos)
        live = jnp.where(mine, (in_slab // SLC_BLOCK == half_kv).astype(jnp.int32), live)
        own = jnp.where(mine, own_kv, own)
    s_ok = (live > 0) & (pos < past)
    n_ok = own > 0
    s_s = jnp.where(s_ok, s_s * scale - slope * (past - pos).astype(F32), NEG)
    s_n = jnp.where(n_ok, s_n * scale, NEG)
    m_s = jnp.maximum(jnp.max(s_s, axis=-1, keepdims=True), s_n)
    p_s = jnp.where(s_ok, jnp.exp(s_s - m_s), 0.0)
    p_n = jnp.where(n_ok, jnp.exp(s_n - m_s), 0.0)
    l_s = jnp.maximum(jnp.sum(p_s, axis=-1, keepdims=True) + p_n, 1e-30)
    p_s = p_s / l_s
    p_n = p_n / l_s
    o_s = jnp.zeros((N_HEADS, HEAD_DIM), F32)
    for kv in range(N_KV_HEADS):
        o_kv = _dot3(p_s, vbuf[kv], nt=True) + p_n * new_ref[0, 3, kv:kv + 1, :]
        o_s = jnp.where(row // GQA == kv, o_kv, o_s)

    gt = jax.nn.sigmoid(gt_ref[0])
    o_ref[0] = gt[:, 0:1] * oc_ref[0] + gt[:, 1:2] * o_s + gt[:, 2:3] * o_w


def _attend_sample(page_table, idx, q16, o_c, gates, kv_new, win_t, cache_t, past):
    B = q16.shape[0]
    page_rows = cache_t.shape[2]
    assert win_t.shape[2] <= WINDOW and page_rows % SLC_BLOCK == 0
    blk3 = lambda a: pl.BlockSpec((1,) + a.shape[1:], lambda b, pt, ix: (b,) + (0,) * (a.ndim - 1))
    return pl.pallas_call(
        functools.partial(_attend_sample_kernel, past=past),
        grid_spec=pltpu.PrefetchScalarGridSpec(
            num_scalar_prefetch=2,
            grid=(B,),
            in_specs=[blk3(q16), blk3(o_c), blk3(gates), blk3(kv_new), blk3(win_t),
                      pl.BlockSpec(memory_space=pl.ANY)],
            out_specs=pl.BlockSpec((1, N_HEADS, HEAD_DIM), lambda b, pt, ix: (b, 0, 0)),
            scratch_shapes=[pltpu.VMEM((N_KV_HEADS, HEAD_DIM, SLC_TOPK * page_rows), F32),
                            pltpu.VMEM((N_KV_HEADS, HEAD_DIM, SLC_TOPK * page_rows), F32),
                            pltpu.SemaphoreType.DMA((2, N_KV_HEADS, SLC_TOPK))],
        ),
        out_shape=jax.ShapeDtypeStruct((B, N_HEADS, HEAD_DIM), F32),
        compiler_params=_cparams(("arbitrary",)),
        name="attend_sample",
    )(page_table, idx, q16, o_c, gates, kv_new, win_t, cache_t)


def _linear_res_kernel(x_ref, a_ref, w_ref, o_ref):
    o_ref[...] = x_ref[...] + _dot3(a_ref[...], w_ref[...])


def _linear_res(x, a, w):
    return pl.pallas_call(
        _linear_res_kernel,
        out_shape=jax.ShapeDtypeStruct(x.shape, F32),
        compiler_params=pltpu.CompilerParams(vmem_limit_bytes=VMEM_LIMIT),
        name="out_proj_sample",
    )(x, a, w)


def _expand_cmp_weights(w_cmp, rows):
    w = jnp.repeat(w_cmp, HEAD_DIM, axis=1)
    return jnp.tile(w, (rows // CMP_BLOCK, 1))


def _row(v):
    return v.reshape(1, -1)


def _prompt_mixers(x, norm_mix, w_pool, pool_scale, w_in, w_cmp_k, w_cmp_v, w_out, norm_ffn0, wfg, wfu, wfd):
    B, L, D = x.shape
    q_w = N_HEADS * HEAD_DIM
    x, h_last = _pool_prompt(x, _row(norm_mix[0]), w_pool.astype(BF16), _row(pool_scale))
    new_pool = h_last[None, :, POOL_HALO - (max(POOL_WINDOWS) - 1):, :]
    x = _ffn(x.reshape(B * L, D), _row(norm_ffn0), wfg.astype(BF16), wfu.astype(BF16), wfd.astype(BF16),
             tm=FFN_ROW_TILE)

    tm = PROJ_ROW_TILE
    wq = w_in[:, :q_w].astype(BF16)
    wkv = w_in[:, q_w:q_w + 6 * KV_W].astype(BF16)
    wgt = jnp.pad(w_in[:, q_w + 6 * KV_W:], ((0, 0), (0, LANES - N_BRANCH * N_HEADS))).astype(BF16)
    q, kv_t, ks, vs, kw, vw, gates, kc, vc = _nsa_proj(
        x, _row(norm_mix[1]), wq, wkv, wgt, _expand_cmp_weights(w_cmp_k, tm), _expand_cmp_weights(w_cmp_v, tm),
        tm=tm, seq=L)
    kv_t = kv_t.reshape(B, 6, N_KV_HEADS, HEAD_DIM, L)
    new_kv = kv_t[:, :4].transpose(0, 4, 1, 2, 3)[None]
    new_win = kv_t[:, 4:, :, :, L - min(WINDOW, L):].transpose(0, 4, 1, 2, 3)[None]
    nc = L // CMP_BLOCK
    pad_c = lambda a: jnp.pad(a.reshape(B, nc, KV_W), ((0, 0), (0, LANES - nc), (0, 0))).astype(BF16)
    per_seq = lambda a: a.reshape(B, L, a.shape[-1])
    x = _nsa_prompt(per_seq(q), per_seq(gates), pad_c(kc), pad_c(vc), per_seq(ks), per_seq(vs),
                    per_seq(kw), per_seq(vw), per_seq(x), w_out.astype(BF16))
    return x.reshape(B * L, D), new_pool, new_kv, new_win


def _sample_mixers(x, state_pool, cache, state_win, page_table, norm_mix, w_pool, pool_scale, w_in, w_cmp_k,
                   w_cmp_v, w_out, norm_ffn0, wfg, wfu, wfd):
    SB, D = x.shape
    n_phys, page = cache.shape[:2]
    past = page_table.shape[1] * page
    q_w = N_HEADS * HEAD_DIM
    x, h = _pool_sample(x, state_pool.transpose(1, 0, 2), _row(norm_mix[0]), w_pool, _row(pool_scale))
    new_pool = jnp.concatenate([state_pool[:, 1:], h[:, None]], axis=1)[None]
    x = _ffn(x, _row(norm_ffn0), wfg, wfu, wfd, tm=SB)

    cols = w_in.shape[1]
    z = _nsa_proj_sample(x, _row(norm_mix[1]), jnp.pad(w_in, ((0, 0), (0, -cols % LANES))))
    kv_new = z[:, q_w:q_w + 6 * KV_W].reshape(SB, 6, N_KV_HEADS, HEAD_DIM)
    new_kv = kv_new[None, :, None, :4]
    keep = min(WINDOW, state_win.shape[1] + 1)
    new_win = jnp.concatenate([state_win, kv_new[:, None, 4:]], axis=1)[None, :, -keep:]

    cache_t = cache.transpose(0, 2, 3, 4, 1).reshape(n_phys, 4 * KV_W, page)
    win_t = state_win.transpose(0, 2, 3, 4, 1).reshape(SB, 2 * KV_W, state_win.shape[1])
    w_rows = jnp.concatenate([_expand_cmp_weights(w_cmp_k, page).T, _expand_cmp_weights(w_cmp_v, page).T], axis=0)
    kcv = _cmp_sample(page_table, cache_t, w_rows)
    q16 = z[:, :q_w].reshape(SB, N_HEADS, HEAD_DIM)
    o_c, idx = _select_sample(q16, kcv, past)
    idx = idx[:, GQA - 1::GQA, :SLC_TOPK].reshape(-1)
    gate_logits = z[:, q_w + 6 * KV_W:cols].reshape(SB, N_BRANCH, N_HEADS).transpose(0, 2, 1)
    gate_logits = jnp.pad(gate_logits, ((0, 0), (0, 0), (0, LANES - N_BRANCH)))
    o = _attend_sample(page_table, idx, q16, o_c, gate_logits, kv_new, win_t, cache_t, past)
    return _linear_res(x, o.reshape(SB, q_w), w_out), new_pool, new_kv, new_win


def kernel(x_prompt, x_sample, state_pool, cache_kv, state_win, page_table, norm_mix, w_pool, pool_scale,
           w_nsa_in, w_cmp_k, w_cmp_v, w_nsa_out, norm_ffn, w_ffn_gate, w_ffn_up, w_ffn_down, w_router,
           w_moe_gate, w_moe_up, w_moe_down, norm_final):
    assert x_sample.shape[1] == 1 and norm_mix.shape[0] == 2 and x_prompt.shape[-1] == N_HEADS * HEAD_DIM
    w_rt = jnp.pad(w_router[0], ((0, 0), (0, LANES - w_router.shape[-1])))
    moe_w = (w_moe_gate[0].astype(BF16), w_moe_up[0].astype(BF16), w_moe_down[0].astype(BF16))

    xp, new_pool_prompt, new_kv_prompt, new_win_prompt = _prompt_mixers(
        x_prompt, norm_mix, w_pool[0], pool_scale[0], w_nsa_in[0], w_cmp_k[0], w_cmp_v[0], w_nsa_out[0],
        norm_ffn[0], w_ffn_gate, w_ffn_up, w_ffn_down)
    xs, new_pool_sample, new_kv_sample, new_win_sample = _sample_mixers(
        x_sample[:, 0], state_pool[0], cache_kv[0], state_win[0], page_table, norm_mix, w_pool[0], pool_scale[0],
        w_nsa_in[0], w_cmp_k[0], w_cmp_v[0], w_nsa_out[0], norm_ffn[0], w_ffn_gate, w_ffn_up, w_ffn_down)

    yp = _moe_prompt(xp, _row(norm_ffn[1]), w_rt, *moe_w, _row(norm_final))
    ys = _ffn(xs, _row(norm_ffn[1]), *moe_w, w_rt, _row(norm_final), tm=xs.shape[0])
    return (yp.reshape(x_prompt.shape), ys.reshape(x_sample.shape), new_pool_prompt, new_pool_sample,
            new_kv_prompt, new_kv_sample, new_win_prompt, new_win_sample)
```

```python
import functools

import jax
import jax.numpy as jnp
from jax import lax
from jax.experimental import pallas as pl
from jax.experimental.pallas import tpu as pltpu

F32 = jnp.float32
BF16 = jnp.bfloat16

EPS = 1e-6
NEG = -1e30
POOL_WINDOWS = (2, 4, 8, 16)
POOL_HALO = 16
N_HEADS = 16
N_KV_HEADS = 4
GQA = N_HEADS // N_KV_HEADS
HEAD_DIM = 64
KV_W = N_KV_HEADS * HEAD_DIM
CMP_BLOCK = 32
SLC_BLOCK = 64
SLC_TOPK = 16
WINDOW = 512
N_BRANCH = 3
FORCE_BONUS = 1000.0
TOP_K = 2
LANES = 128
VMEM_LIMIT = 56 * 1024 * 1024

POOL_ROW_TILE = 512
FFN_ROW_TILE = 1024
FFN_COL_TILE = 512
PROJ_ROW_TILE = 512
ATTN_Q_TILE = 256
ATTN_KEY_CHUNK = 512
ROUTE_ROW_TILE = 512
EXPERT_ROW_TILE = 896
COMBINE_ROW_TILE = 256
SAMPLE_PROJ_COL_TILE = 384


def _cparams(sem):
    return pltpu.CompilerParams(dimension_semantics=sem, vmem_limit_bytes=VMEM_LIMIT)


def _rmsnorm(x, g):
    ms = jnp.mean(x * x, axis=-1, keepdims=True)
    return (x * lax.rsqrt(ms + EPS)) * g


def _dot(a, b):
    return jnp.dot(a, b, preferred_element_type=F32)


def _dot_nt(a, b):
    return lax.dot_general(a, b, (((1,), (1,)), ((), ())), preferred_element_type=F32)


def _split(a):
    hi = a.astype(BF16)
    return hi, (a - hi.astype(F32)).astype(BF16)


def _dot3(a, b, nt=False):
    d = _dot_nt if nt else _dot
    ah, al = _split(a)
    bh, bl = _split(b)
    return d(ah, bh) + (d(ah, bl) + d(al, bh))


def _masked_softmax(s, mask):
    s = jnp.where(mask, s, NEG)
    m = jnp.max(s, axis=-1, keepdims=True)
    p = jnp.where(mask, jnp.exp(s - m), 0.0)
    return p / jnp.maximum(jnp.sum(p, axis=-1, keepdims=True), 1e-30)


def _alibi_slope(h):
    return 2.0 ** (-8.0 * (h + 1) / N_HEADS)


def _head_slopes(rows, rows_per_head, first_head):
    h = lax.broadcasted_iota(jnp.int32, (rows, 1), 0) // rows_per_head
    slopes = jnp.zeros((rows, 1), F32)
    for j in range(rows // rows_per_head):
        slopes = jnp.where(h == j, _alibi_slope(first_head + j), slopes)
    return slopes


def _pool_prompt_kernel(x_ref, halo_ref, g_ref, w_ref, sc_ref, o_ref, hl_ref, full_ref, *, tile):
    t = pl.program_id(1)
    g = g_ref[...]
    x = x_ref[0]
    h = _rmsnorm(x, g)
    hh = _rmsnorm(halo_ref[0], g)
    full_ref[0:POOL_HALO, :] = jnp.where(t > 0, hh, 0.0)
    full_ref[POOL_HALO:POOL_HALO + tile, :] = h
    row = t * tile + lax.broadcasted_iota(jnp.int32, (tile, 1), 0)
    group = x.shape[-1] // len(POOL_WINDOWS)
    parts = []
    for gi, w in enumerate(POOL_WINDOWS):
        cs = slice(gi * group, (gi + 1) * group)
        hg = h[:, cs]
        acc = hg
        for k in range(1, w):
            acc = acc + full_ref[POOL_HALO - k:POOL_HALO - k + tile, cs]
        cnt = jnp.minimum(row + 1, w).astype(F32)
        pooled = acc / cnt - hg
        parts.append(_dot(pooled.astype(BF16), w_ref[gi]))
    o_ref[0] = x + jnp.concatenate(parts, axis=-1) * sc_ref[...]

    @pl.when(t == pl.num_programs(1) - 1)
    def _():
        hl_ref[0] = h[tile - POOL_HALO:, :]


def _pool_prompt(x, g, w_pool, scale, tile=POOL_ROW_TILE):
    B, L, D = x.shape
    assert L % tile == 0 and tile % POOL_HALO == 0
    hb = tile // POOL_HALO
    return pl.pallas_call(
        functools.partial(_pool_prompt_kernel, tile=tile),
        grid=(B, L // tile),
        in_specs=[
            pl.BlockSpec((1, tile, D), lambda b, t: (b, t, 0)),
            pl.BlockSpec((1, POOL_HALO, D), lambda b, t: (b, jnp.maximum(t * hb - 1, 0), 0)),
            pl.BlockSpec((1, D), lambda b, t: (0, 0)),
            pl.BlockSpec(w_pool.shape, lambda b, t: (0, 0, 0)),
            pl.BlockSpec((1, D), lambda b, t: (0, 0)),
        ],
        out_specs=[
            pl.BlockSpec((1, tile, D), lambda b, t: (b, t, 0)),
            pl.BlockSpec((1, POOL_HALO, D), lambda b, t: (b, 0, 0)),
        ],
        out_shape=[jax.ShapeDtypeStruct((B, L, D), F32), jax.ShapeDtypeStruct((B, POOL_HALO, D), F32)],
        scratch_shapes=[pltpu.VMEM((tile + POOL_HALO, D), F32)],
        compiler_params=_cparams(("parallel", "arbitrary")),
        name="pool_prompt",
    )(x, x, g, w_pool, scale)


def _pool_sample_kernel(x_ref, st_ref, g_ref, w_ref, sc_ref, o_ref, h_ref):
    x = x_ref[...]
    h = _rmsnorm(x, g_ref[...])
    h_ref[...] = h
    P = st_ref.shape[0]
    group = x.shape[-1] // len(POOL_WINDOWS)
    parts = []
    for gi, w in enumerate(POOL_WINDOWS):
        cs = slice(gi * group, (gi + 1) * group)
        hg = h[:, cs]
        acc = hg
        for k in range(1, w):
            acc = acc + st_ref[P - k][:, cs]
        pooled = acc / float(w) - hg
        parts.append(_dot3(pooled, w_ref[gi]))
    o_ref[...] = x + jnp.concatenate(parts, axis=-1) * sc_ref[...]


def _pool_sample(x, state_t, g, w_pool, scale):
    B, D = x.shape
    assert state_t.shape[0] >= max(POOL_WINDOWS) - 1
    return pl.pallas_call(
        _pool_sample_kernel,
        out_shape=[jax.ShapeDtypeStruct((B, D), F32), jax.ShapeDtypeStruct((B, D), F32)],
        compiler_params=pltpu.CompilerParams(vmem_limit_bytes=VMEM_LIMIT),
        name="pool_sample",
    )(x, state_t, g, w_pool, scale)


def _ffn_kernel(*refs, n_experts, final_norm, precise):
    moe = n_experts > 1
    it = iter(refs)
    x_ref, g_ref = next(it), next(it)
    wr_ref = next(it) if moe else None
    wg_ref, wu_ref, wd_ref = next(it), next(it), next(it)
    gf_ref = next(it) if final_norm else None
    o_ref, h_scr, acc_scr = next(it), next(it), next(it)
    eacc_scr, gate_scr = (next(it), next(it)) if moe else (None, None)

    e, f = pl.program_id(1), pl.program_id(2)
    last_f = f == pl.num_programs(2) - 1

    @pl.when((e == 0) & (f == 0))
    def _():
        h = _rmsnorm(x_ref[...], g_ref[...])
        h_scr[...] = h.astype(h_scr.dtype)
        acc_scr[...] = jnp.zeros_like(acc_scr)
        if moe:
            logits = _dot3(h, wr_ref[...])
            lane, i1, i2, w1, w2 = _top2_gates(logits, n_experts)
            gate_scr[...] = jnp.where(lane == i1, w1, 0.0) + jnp.where(lane == i2, w2, 0.0)

    mm = _dot3 if precise else _dot
    hb = h_scr[...]
    a = mm(hb, wg_ref[0])
    u = mm(hb, wu_ref[0])
    act = (a * jax.nn.sigmoid(a)) * u
    y = mm(act.astype(hb.dtype), wd_ref[0])

    if moe:
        @pl.when(f == 0)
        def _():
            eacc_scr[...] = y

        @pl.when(f > 0)
        def _():
            eacc_scr[...] += y

        @pl.when(last_f)
        def _():
            gate = gate_scr[...]
            lane = lax.broadcasted_iota(jnp.int32, gate.shape, 1)
            ge = jnp.sum(jnp.where(lane == e, gate, 0.0), axis=-1, keepdims=True)
            acc_scr[...] += ge * eacc_scr[...]
    else:
        acc_scr[...] += y

    @pl.when((e == pl.num_programs(1) - 1) & last_f)
    def _():
        out = x_ref[...] + acc_scr[...]
        if final_norm:
            out = _rmsnorm(out, gf_ref[...])
        o_ref[...] = out


def _ffn(x, g, wg, wu, wd, w_router=None, g_final=None, *, tm, tf=FFN_COL_TILE):
    precise = wg.dtype == F32
    N, D = x.shape
    E, _, F = wg.shape
    assert N % tm == 0 and F % tf == 0
    moe = w_router is not None
    assert moe == (E > 1)
    final_norm = g_final is not None
    const2 = lambda i, e, f: (0, 0)
    args, in_specs = [x, g], [pl.BlockSpec((tm, D), lambda i, e, f: (i, 0)), pl.BlockSpec((1, D), const2)]
    if moe:
        args.append(w_router)
        in_specs.append(pl.BlockSpec(w_router.shape, const2))
    args += [wg, wu, wd]
    in_specs += [
        pl.BlockSpec((1, D, tf), lambda i, e, f: (e, 0, f)),
        pl.BlockSpec((1, D, tf), lambda i, e, f: (e, 0, f)),
        pl.BlockSpec((1, tf, D), lambda i, e, f: (e, f, 0)),
    ]
    if final_norm:
        args.append(g_final)
        in_specs.append(pl.BlockSpec((1, D), const2))
    scratch = [pltpu.VMEM((tm, D), F32 if precise else BF16), pltpu.VMEM((tm, D), F32)]
    if moe:
        scratch += [pltpu.VMEM((tm, D), F32), pltpu.VMEM((tm, LANES), F32)]
    return pl.pallas_call(
        functools.partial(_ffn_kernel, n_experts=E, final_norm=final_norm, precise=precise),
        grid=(N // tm, E, F // tf),
        in_specs=in_specs,
        out_specs=pl.BlockSpec((tm, D), lambda i, e, f: (i, 0)),
        out_shape=jax.ShapeDtypeStruct((N, D), F32),
        scratch_shapes=scratch,
        compiler_params=_cparams(("parallel", "arbitrary", "arbitrary")),
        name="moe_ffn" if moe else "dense_ffn",
    )(*args)


ROUTE_I1, ROUTE_I2, ROUTE_W1, ROUTE_W2, ROUTE_R1, ROUTE_R2 = range(6)


def _top2_gates(logits, n_experts):
    lane = lax.broadcasted_iota(jnp.int32, logits.shape, 1)
    lg = jnp.where(lane < n_experts, logits, -jnp.inf)
    m1 = jnp.max(lg, axis=-1, keepdims=True)
    i1 = jnp.min(jnp.where(lg == m1, lane, LANES), axis=-1, keepdims=True)
    lg2 = jnp.where(lane == i1, -jnp.inf, lg)
    m2 = jnp.max(lg2, axis=-1, keepdims=True)
    i2 = jnp.min(jnp.where(lg2 == m2, lane, LANES), axis=-1, keepdims=True)
    e2 = jnp.exp(m2 - m1)
    den = 1.0 + e2
    return lane, i1, i2, 1.0 / den, e2 / den


def _moe_route_kernel(x_ref, g_ref, wr_ref, tri_ref, route_ref, cnt_ref, carry, *, n_experts):
    @pl.when(pl.program_id(0) == 0)
    def _():
        carry[...] = jnp.zeros_like(carry)

    h = _rmsnorm(x_ref[...], g_ref[...])
    logits = _dot3(h, wr_ref[...])
    lane, i1, i2, w1, w2 = _top2_gates(logits, n_experts)
    member = jnp.where((lane == i1) | (lane == i2), 1.0, 0.0)
    before = _dot(tri_ref[...], member.astype(BF16)) + carry[0:1, :]
    r1 = jnp.sum(jnp.where(lane == i1, before, 0.0), axis=-1, keepdims=True)
    r2 = jnp.sum(jnp.where(lane == i2, before, 0.0), axis=-1, keepdims=True)
    rec = jnp.zeros(logits.shape, F32)
    for col, val in ((ROUTE_I1, i1.astype(F32)), (ROUTE_I2, i2.astype(F32)), (ROUTE_W1, w1), (ROUTE_W2, w2),
                     (ROUTE_R1, r1), (ROUTE_R2, r2)):
        rec = jnp.where(lane == col, val, rec)
    route_ref[...] = rec
    carry[...] = carry[...] + jnp.sum(member, axis=0, keepdims=True)
    cnt_ref[...] = carry[...]


def _moe_route(x, g, w_router, n_experts, tm=ROUTE_ROW_TILE):
    N, D = x.shape
    assert N % tm == 0
    tri = (jnp.arange(tm)[:, None] > jnp.arange(tm)[None, :]).astype(BF16)
    return pl.pallas_call(
        functools.partial(_moe_route_kernel, n_experts=n_experts),
        grid=(N // tm,),
        in_specs=[pl.BlockSpec((tm, D), lambda i: (i, 0)), pl.BlockSpec((1, D), lambda i: (0, 0)),
                  pl.BlockSpec(w_router.shape, lambda i: (0, 0)), pl.BlockSpec((tm, tm), lambda i: (0, 0))],
        out_specs=[pl.BlockSpec((tm, LANES), lambda i: (i, 0)), pl.BlockSpec((8, LANES), lambda i: (0, 0))],
        out_shape=[jax.ShapeDtypeStruct((N, LANES), F32), jax.ShapeDtypeStruct((8, LANES), F32)],
        scratch_shapes=[pltpu.VMEM((8, LANES), F32)],
        compiler_params=_cparams(("arbitrary",)),
        name="moe_route",
    )(x, g, w_router, tri)


GATHER_UNROLL = 8


def _row_gather(idx_of, src_hbm, dst_at, sem, rows, priority):
    assert rows % GATHER_UNROLL == 0

    def copy(r):
        return pltpu.make_async_copy(src_hbm.at[pl.ds(idx_of(r), 1), :], dst_at(r), sem)

    def start():
        def group(i, c):
            for j in range(GATHER_UNROLL):
                copy(i * GATHER_UNROLL + j).start(priority=priority(j))
            return c
        lax.fori_loop(0, rows // GATHER_UNROLL, group, 0)

    def wait():
        lax.fori_loop(0, rows, lambda r, c: (copy(r).wait(), c)[1], 0, unroll=GATHER_UNROLL)

    return start, wait, copy


def _moe_experts_kernel(te_ref, nu_ref, tos_ref, x_hbm, g_ref, wg_ref, wu_ref, wd_ref, o_ref,
                        xbuf, h_scr, acc_scr, sems, *, tile, nf):
    i, f = pl.program_id(0), pl.program_id(1)
    used = i < nu_ref[0]
    slot = i % 2
    per_step = tile // nf

    def gather(t, sl):
        return _row_gather(lambda r: tos_ref[t * tile + r], x_hbm,
                           lambda r: xbuf.at[sl, pl.ds(r, 1), :], sems.at[sl], tile, lambda j: j % 2)

    def prefetch_share():
        row_copy = gather(i + 1, 1 - slot)[2]
        for j in range(per_step):
            row_copy(f * per_step + j).start(priority=j % 2)

    @pl.when((f == 0) & (i == 0))
    def _():
        gather(0, 0)[0]()

    @pl.when(f == 0)
    def _():
        gather(i, slot)[1]()

    @pl.when((f == 0) & used)
    def _():
        h_scr[...] = _rmsnorm(xbuf[slot], g_ref[...]).astype(BF16)
        acc_scr[...] = jnp.zeros_like(acc_scr)

    @pl.when(used)
    def _():
        prefetch_share()
        hb = h_scr[...]
        a = _dot(hb, wg_ref[0])
        u = _dot(hb, wu_ref[0])
        act = (a * jax.nn.sigmoid(a)) * u
        acc_scr[...] += _dot(act.astype(BF16), wd_ref[0])

    @pl.when(jnp.logical_not(used))
    def _():
        prefetch_share()

    @pl.when(f == nf - 1)
    def _():
        o_ref[...] = jnp.where(used, acc_scr[...], 0.0)

    @pl.when((f == nf - 1) & (i == pl.num_programs(0) - 1))
    def _():
        gather(i + 1, 1 - slot)[1]()


def _moe_experts(x, g, wg, wu, wd, tile_expert, n_used, token_of_slot, *, tile, tf=FFN_COL_TILE):
    N, D = x.shape
    E, _, F = wg.shape
    n_slots = token_of_slot.shape[0] - tile
    assert n_slots % tile == 0 and F % tf == 0
    n_tiles, nf = n_slots // tile, F // tf
    assert tile % nf == 0
    fidx = lambda i, f, nu: jnp.where(i < nu[0], f, nf - 1)
    return pl.pallas_call(
        functools.partial(_moe_experts_kernel, tile=tile, nf=nf),
        grid_spec=pltpu.PrefetchScalarGridSpec(
            num_scalar_prefetch=3,
            grid=(n_tiles, nf),
            in_specs=[pl.BlockSpec(memory_space=pl.ANY),
                      pl.BlockSpec((1, D), lambda i, f, te, nu, tos: (0, 0)),
                      pl.BlockSpec((1, D, tf), lambda i, f, te, nu, tos: (te[i], 0, fidx(i, f, nu))),
                      pl.BlockSpec((1, D, tf), lambda i, f, te, nu, tos: (te[i], 0, fidx(i, f, nu))),
                      pl.BlockSpec((1, tf, D), lambda i, f, te, nu, tos: (te[i], fidx(i, f, nu), 0))],
            out_specs=pl.BlockSpec((tile, D), lambda i, f, te, nu, tos: (i, 0)),
            scratch_shapes=[pltpu.VMEM((2, tile, D), F32), pltpu.VMEM((tile, D), BF16),
                            pltpu.VMEM((tile, D), F32), pltpu.SemaphoreType.DMA((2,))],
        ),
        out_shape=jax.ShapeDtypeStruct((n_slots, D), F32),
        compiler_params=_cparams(("arbitrary", "arbitrary")),
        name="moe_experts",
    )(tile_expert, n_used, token_of_slot, x, g, wg, wu, wd)


def _moe_combine_kernel(slot_ref, x_ref, route_ref, gf_ref, ys_hbm, o_ref, ybuf, sems, *, tile):
    i = pl.program_id(0)
    slot = i % 2

    def gather(t, sl):
        return _row_gather(lambda j: slot_ref[t * 2 * tile + j], ys_hbm,
                           lambda j: ybuf.at[sl, pl.ds(j, 1), :], sems.at[sl], 2 * tile, lambda j: j % 2)

    @pl.when(i == 0)
    def _():
        gather(0, 0)[0]()

    gather(i, slot)[1]()

    @pl.when(i + 1 < pl.num_programs(0))
    def _():
        gather(i + 1, 1 - slot)[0]()

    route = route_ref[...]
    w1, w2 = route[:, ROUTE_W1:ROUTE_W1 + 1], route[:, ROUTE_W2:ROUTE_W2 + 1]
    y = w1 * ybuf[slot, 0:tile, :] + w2 * ybuf[slot, tile:2 * tile, :]
    o_ref[...] = _rmsnorm(x_ref[...] + y, gf_ref[...])


def _moe_combine(x, route, g_final, ys, slots, *, tile):
    N, D = x.shape
    assert N % tile == 0
    return pl.pallas_call(
        functools.partial(_moe_combine_kernel, tile=tile),
        grid_spec=pltpu.PrefetchScalarGridSpec(
            num_scalar_prefetch=1,
            grid=(N // tile,),
            in_specs=[pl.BlockSpec((tile, D), lambda i, s: (i, 0)),
                      pl.BlockSpec((tile, LANES), lambda i, s: (i, 0)),
                      pl.BlockSpec((1, D), lambda i, s: (0, 0)),
                      pl.BlockSpec(memory_space=pl.ANY)],
            out_specs=pl.BlockSpec((tile, D), lambda i, s: (i, 0)),
            scratch_shapes=[pltpu.VMEM((2, 2 * tile, D), F32), pltpu.SemaphoreType.DMA((2,))],
        ),
        out_shape=jax.ShapeDtypeStruct((N, D), F32),
        compiler_params=_cparams(("arbitrary",)),
        name="moe_combine",
    )(slots, x, route, g_final, ys)


def _moe_prompt(x, g, w_router, wg, wu, wd, g_final, *, tile=EXPERT_ROW_TILE, ctile=COMBINE_ROW_TILE):
    N, D = x.shape
    E = wg.shape[0]
    route, counts = _moe_route(x, g, w_router, E)
    cnt = counts[0, :E].astype(jnp.int32)
    padded = (cnt + tile - 1) // tile * tile
    ends = jnp.cumsum(padded)
    off = ends - padded
    i1, i2 = route[:, ROUTE_I1].astype(jnp.int32), route[:, ROUTE_I2].astype(jnp.int32)
    slot1 = off[i1] + route[:, ROUTE_R1].astype(jnp.int32)
    slot2 = off[i2] + route[:, ROUTE_R2].astype(jnp.int32)
    n_slots = (TOP_K * N + E * (tile - 1)) // tile * tile
    rows = jnp.arange(N, dtype=jnp.int32)
    token_of_slot = jnp.zeros((n_slots + tile,), jnp.int32).at[jnp.concatenate([slot1, slot2])].set(
        jnp.concatenate([rows, rows]), unique_indices=True)
    n_used = ends[-1:] // tile
    tile_start = jnp.minimum(jnp.arange(n_slots // tile, dtype=jnp.int32), n_used[0] - 1) * tile
    tile_expert = jnp.sum(tile_start[:, None] >= ends[None, :], axis=1).astype(jnp.int32)
    ys = _moe_experts(x, g, wg, wu, wd, tile_expert, n_used.astype(jnp.int32), token_of_slot, tile=tile)
    slots = jnp.stack([slot1.reshape(-1, ctile), slot2.reshape(-1, ctile)], axis=1).reshape(-1)
    return _moe_combine(x, route, g_final, ys, slots, tile=ctile)


def _nsa_proj_kernel(x_ref, g_ref, wq_ref, wkv_ref, wgt_ref, wck_ref, wcv_ref,
                     q_ref, kvt_ref, ks_ref, vs_ref, kw_ref, vw_ref, gt_ref, kc_ref, vc_ref):
    hb = _rmsnorm(x_ref[...], g_ref[...]).astype(BF16)
    q_ref[...] = _dot(hb, wq_ref[...]).astype(BF16)
    kv = _dot(hb, wkv_ref[...])
    kvt_ref[0] = kv.T
    ks_ref[...] = kv[:, 2 * KV_W:3 * KV_W].astype(BF16)
    vs_ref[...] = kv[:, 3 * KV_W:4 * KV_W].astype(BF16)
    kw_ref[...] = kv[:, 4 * KV_W:5 * KV_W].astype(BF16)
    vw_ref[...] = kv[:, 5 * KV_W:].astype(BF16)
    gt_ref[...] = jax.nn.sigmoid(_dot(hb, wgt_ref[...]))
    tm = kv.shape[0]
    kc_ref[...] = (kv[:, :KV_W] * wck_ref[...]).reshape(tm // CMP_BLOCK, CMP_BLOCK, KV_W).sum(axis=1)
    vc_ref[...] = (kv[:, KV_W:2 * KV_W] * wcv_ref[...]).reshape(tm // CMP_BLOCK, CMP_BLOCK, KV_W).sum(axis=1)


def _nsa_proj(x, g, wq, wkv, wgt, wck, wcv, *, tm, seq):
    N, D = x.shape
    assert N % tm == 0 and tm % (8 * CMP_BLOCK) == 0 and seq % tm == 0 and N % seq == 0
    per_seq = seq // tm
    row = lambda w: pl.BlockSpec((tm, w), lambda i: (i, 0))
    full = lambda a: pl.BlockSpec(a.shape, lambda i: (0, 0))
    out_shape = [jax.ShapeDtypeStruct((N, N_HEADS * HEAD_DIM), BF16),
                 jax.ShapeDtypeStruct((N // seq, 6 * KV_W, seq), F32)]
    out_shape += [jax.ShapeDtypeStruct((N, KV_W), BF16)] * 4
    out_shape += [jax.ShapeDtypeStruct((N, LANES), F32)]
    out_shape += [jax.ShapeDtypeStruct((N // CMP_BLOCK, KV_W), F32)] * 2
    out_specs = [row(N_HEADS * HEAD_DIM),
                 pl.BlockSpec((1, 6 * KV_W, tm), lambda i: (i // per_seq, 0, i % per_seq))]
    out_specs += [row(KV_W)] * 4 + [row(LANES)]
    out_specs += [pl.BlockSpec((tm // CMP_BLOCK, KV_W), lambda i: (i, 0))] * 2
    return pl.pallas_call(
        _nsa_proj_kernel,
        grid=(N // tm,),
        in_specs=[row(D), full(g), full(wq), full(wkv), full(wgt), full(wck), full(wcv)],
        out_specs=out_specs,
        out_shape=out_shape,
        compiler_params=_cparams(("parallel",)),
        name="nsa_proj_prompt",
    )(x, g, wq, wkv, wgt, wck, wcv)


def _nsa_proj_sample_kernel(x_ref, g_ref, w_ref, z_ref):
    z_ref[...] = _dot3(_rmsnorm(x_ref[...], g_ref[...]), w_ref[...])


def _nsa_proj_sample(x, g, w_in, tn=SAMPLE_PROJ_COL_TILE):
    B, D = x.shape
    cols = w_in.shape[1]
    assert cols % tn == 0
    return pl.pallas_call(
        _nsa_proj_sample_kernel,
        grid=(cols // tn,),
        in_specs=[pl.BlockSpec((B, D), lambda j: (0, 0)), pl.BlockSpec((1, D), lambda j: (0, 0)),
                  pl.BlockSpec((D, tn), lambda j: (0, j))],
        out_specs=pl.BlockSpec((B, tn), lambda j: (0, j)),
        out_shape=jax.ShapeDtypeStruct((B, cols), F32),
        compiler_params=_cparams(("parallel",)),
        name="nsa_proj_sample",
    )(x, g, w_in)


def _nsa_prompt_kernel(q_ref, gt_ref, kc_ref, vc_ref, ks_ref, vs_ref, kw_ref, vw_ref, x_ref, wo_ref,
                       o_ref, o_scr, sel_scr, *, tq, kchunk, seq):
    t0 = pl.program_id(1) * tq
    rows = GQA * tq
    scale = HEAD_DIM ** -0.5
    assert scale == 0.125
    n_blocks = seq // SLC_BLOCK
    top_k = min(SLC_TOPK, n_blocks)

    q = q_ref[0]
    gates = gt_ref[0]
    qpos_t = t0 + lax.broadcasted_iota(jnp.int32, (tq, 1), 0)
    qpos = jnp.concatenate([qpos_t] * GQA, axis=0)
    tq_f = qpos.astype(F32)

    lane = lax.broadcasted_iota(jnp.int32, (tq, LANES), 1)
    jblk = lane >> 1
    real = ((lane & 1) == 0) & (jblk < n_blocks)
    blk = qpos_t // SLC_BLOCK
    valid = real & (jblk <= blk)
    forced = (jblk == 0) | (jblk == blk) | (jblk == blk - 1)
    valid_f = jnp.where(valid, 1.0, 0.0)
    valid_b = valid_f.astype(BF16)
    bonus = jnp.where(forced, FORCE_BONUS, 0.0)
    floor = jnp.where(valid, 0.0, jnp.where(real, -1.0, -2.0))
    c_lane = lax.broadcasted_iota(jnp.int32, (1, LANES), 1)
    c_mid = c_lane.astype(F32) * CMP_BLOCK + (CMP_BLOCK - 1) / 2
    c_end = (c_lane + 1) * CMP_BLOCK - 1

    n_chunks = (t0 + tq + kchunk - 1) // kchunk
    wstart = pl.multiple_of(jnp.maximum(t0 - WINDOW, 0), tq)
    wlen = WINDOW + tq

    need_rank = (t0 + tq - 1) // SLC_BLOCK >= top_k
    wpos = wstart + lax.broadcasted_iota(jnp.int32, (1, wlen), 1)
    w_dist = jnp.where((wpos <= qpos_t) & (qpos_t - wpos < WINDOW), (wpos - qpos_t).astype(F32), NEG)

    def with_ones(v):
        return jnp.concatenate([v, jnp.ones_like(v)], axis=1)

    def biased(scores, dist, kv):
        return jnp.concatenate(
            [scores[g * tq:(g + 1) * tq] + _alibi_slope(kv * GQA + g) * dist for g in range(GQA)], axis=0)

    for kv in range(N_KV_HEADS):
        hs = slice(kv * HEAD_DIM, (kv + 1) * HEAD_DIM)
        qs = jnp.concatenate(
            [q[:, (kv * GQA + g) * HEAD_DIM:(kv * GQA + g + 1) * HEAD_DIM] for g in range(GQA)], axis=0) * scale
        slope = _head_slopes(rows, tq, kv * GQA)

        s_c = _dot_nt(qs, kc_ref[0][:, hs]) - slope * (tq_f - c_mid)
        p_c = _masked_softmax(s_c, c_end <= qpos)
        o_c = _dot(p_c.astype(BF16), vc_ref[0][:, hs])

        imp = p_c[0:tq]
        for g in range(1, GQA):
            imp = imp + p_c[g * tq:(g + 1) * tq]
        sel_scr[...] = valid_b

        @pl.when(need_rank)
        def _():
            pair = imp + pltpu.roll(imp, LANES - 1, 1)
            score = valid_f * (pair + bonus) + floor
            score_t = score.T[:2 * n_blocks]
            row_id = lax.broadcasted_iota(jnp.int32, score_t.shape, 0)
            rank_t = jnp.zeros(score_t.shape, jnp.int32)
            for k in range(n_blocks):
                other = score_t[2 * k:2 * k + 1, :]
                beats = (other > score_t) | ((other == score_t) & (row_id > 2 * k))
                rank_t = rank_t + beats.astype(jnp.int32)
            top_t = jnp.where(rank_t < top_k, 1.0, 0.0)
            top = jnp.concatenate([top_t, jnp.zeros((LANES - 2 * n_blocks, tq), F32)], axis=0).T
            sel_scr[...] = (top * valid_f).astype(BF16)

        sel = sel_scr[...]

        def chunk(c, carry):
            m, acc = carry
            k0 = pl.multiple_of(c * kchunk, kchunk)
            kb = ks_ref[0, pl.ds(k0, kchunk), hs]
            vb = with_ones(vs_ref[0, pl.ds(k0, kchunk), hs])
            pos = k0 + lax.broadcasted_iota(jnp.int32, (1, kchunk), 1)
            expand = (lax.broadcasted_iota(jnp.int32, (LANES, kchunk), 0)
                      == 2 * ((k0 + lax.broadcasted_iota(jnp.int32, (LANES, kchunk), 1)) // SLC_BLOCK))
            picked = _dot(sel, jnp.where(expand, 1.0, 0.0).astype(BF16))
            ok = (picked > 0.5) & (pos <= qpos_t)
            s = biased(_dot_nt(qs, kb), jnp.where(ok, (pos - qpos_t).astype(F32), NEG), kv)
            m_new = jnp.maximum(m, jnp.max(s, axis=-1, keepdims=True))
            p = jnp.exp(s - m_new)
            acc = jnp.exp(m - m_new) * acc + _dot(p.astype(BF16), vb)
            return m_new, acc

        init = (jnp.full((rows, 1), NEG, F32), jnp.zeros((rows, 2 * HEAD_DIM), F32))
        _, acc_s = lax.fori_loop(0, n_chunks, chunk, init)
        o_s = acc_s[:, :HEAD_DIM] / acc_s[:, HEAD_DIM:HEAD_DIM + 1]

        kb = kw_ref[0, pl.ds(wstart, wlen), hs]
        vb = with_ones(vw_ref[0, pl.ds(wstart, wlen), hs])
        s_w = biased(_dot_nt(qs, kb), w_dist, kv)
        p_w = jnp.exp(s_w - jnp.max(s_w, axis=-1, keepdims=True))
        pv = _dot(p_w.astype(BF16), vb)
        o_w = pv[:, :HEAD_DIM] / pv[:, HEAD_DIM:HEAD_DIM + 1]

        def gate_col(n):
            base = n * N_HEADS + kv * GQA
            return jnp.concatenate([gates[:, base + g:base + g + 1] for g in range(GQA)], axis=0)

        o = gate_col(0) * o_c + gate_col(1) * o_s + gate_col(2) * o_w
        for g in range(GQA):
            h = kv * GQA + g
            o_scr[:, h * HEAD_DIM:(h + 1) * HEAD_DIM] = o[g * tq:(g + 1) * tq]

    o_ref[0] = x_ref[0] + _dot(o_scr[...].astype(BF16), wo_ref[...])


def _nsa_prompt(q, gates, kc, vc, ks, vs, kw, vw, x, w_o, *, tq=ATTN_Q_TILE, kchunk=ATTN_KEY_CHUNK):
    B, L, D = x.shape
    assert L % kchunk == 0 and kchunk % tq == 0 and L >= WINDOW + tq and L % SLC_BLOCK == 0
    assert 2 * (L // SLC_BLOCK) <= LANES and L // CMP_BLOCK <= LANES and kc.shape[1] == LANES
    tile = lambda w: pl.BlockSpec((1, tq, w), lambda b, t: (b, t, 0))
    whole = lambda a: pl.BlockSpec((1,) + a.shape[1:], lambda b, t: (b, 0, 0))
    return pl.pallas_call(
        functools.partial(_nsa_prompt_kernel, tq=tq, kchunk=kchunk, seq=L),
        grid=(B, L // tq),
        in_specs=[tile(N_HEADS * HEAD_DIM), tile(LANES), whole(kc), whole(vc), whole(ks), whole(vs),
                  whole(kw), whole(vw), tile(D), pl.BlockSpec(w_o.shape, lambda b, t: (0, 0))],
        out_specs=tile(D),
        out_shape=jax.ShapeDtypeStruct((B, L, D), F32),
        scratch_shapes=[pltpu.VMEM((tq, N_HEADS * HEAD_DIM), F32), pltpu.VMEM((tq, LANES), BF16)],
        compiler_params=_cparams(("parallel", "arbitrary")),
        name="nsa_attn_prompt",
    )(q, gates, kc, vc, ks, vs, kw, vw, x, w_o)


CMP_GROUP = 32
MXU_DEPTH = 256


def _cmp_sample_kernel(pt_ref, cache_ref, w_ref, s_ref, o_ref, buf, sems, *, rows):
    b, g = pl.program_id(0), pl.program_id(1)
    n_groups = pl.num_programs(1)
    step = b * n_groups + g
    slot = step % 2

    def page_copy(bb, gg, sl, i):
        page = pt_ref[bb, gg * CMP_GROUP + i]
        return pltpu.make_async_copy(cache_ref.at[page, pl.ds(0, rows), :], buf.at[sl, i], sems.at[sl])

    def start_group(bb, gg, sl):
        for i in range(CMP_GROUP):
            page_copy(bb, gg, sl, i).start()

    @pl.when(step == 0)
    def _():
        start_group(b, g, slot)

    @pl.when(step + 1 < pl.num_programs(0) * n_groups)
    def _():
        nxt = step + 1
        start_group(nxt // n_groups, nxt % n_groups, 1 - slot)

    for i in range(CMP_GROUP):
        page_copy(b, g, slot, i).wait()

    w = w_ref[...]
    acc = jnp.zeros((rows, LANES), F32)
    for j in range(CMP_GROUP // 2):
        p = jnp.concatenate([buf[slot, 2 * j] * w, buf[slot, 2 * j + 1] * w], axis=1)
        hi, lo = _split(p)
        r = _dot(jnp.concatenate([hi, lo], axis=0), s_ref[j])
        acc = acc + (r[:rows] + r[rows:])
    o_ref[0] = acc


def _cmp_sample(page_table, cache_t, w_rows):
    B, n_pages = page_table.shape
    _, _, page = cache_t.shape
    rows = w_rows.shape[0]
    per_page = page // CMP_BLOCK
    assert n_pages % CMP_GROUP == 0 and CMP_GROUP * per_page == LANES and 2 * page == MXU_DEPTH
    k = jnp.arange(2 * page)
    token = (2 * jnp.arange(CMP_GROUP // 2)[:, None] + k[None, :] // page) * per_page + (k[None, :] % page) // CMP_BLOCK
    block_sum = (token[:, :, None] == jnp.arange(LANES)[None, None, :]).astype(BF16)
    return pl.pallas_call(
        functools.partial(_cmp_sample_kernel, rows=rows),
        grid_spec=pltpu.PrefetchScalarGridSpec(
            num_scalar_prefetch=1,
            grid=(B, n_pages // CMP_GROUP),
            in_specs=[pl.BlockSpec(memory_space=pl.ANY),
                      pl.BlockSpec(w_rows.shape, lambda b, g, pt: (0, 0)),
                      pl.BlockSpec(block_sum.shape, lambda b, g, pt: (0, 0, 0))],
            out_specs=pl.BlockSpec((1, rows, LANES), lambda b, g, pt: (b, 0, g)),
            scratch_shapes=[pltpu.VMEM((2, CMP_GROUP, rows, page), F32), pltpu.SemaphoreType.DMA((2,))],
        ),
        out_shape=jax.ShapeDtypeStruct((B, rows, n_pages * per_page), F32),
        compiler_params=_cparams(("arbitrary", "arbitrary")),
        name="cmp_sample",
    )(page_table, cache_t, w_rows, block_sum)


def _select_sample_kernel(q_ref, kcv_ref, oc_ref, idx_ref, *, past):
    scale = HEAD_DIM ** -0.5
    q = q_ref[0]
    nc = kcv_ref.shape[2]
    n_past_blocks = past // SLC_BLOCK
    per_block = SLC_BLOCK // CMP_BLOCK
    row = lax.broadcasted_iota(jnp.int32, (N_HEADS, 1), 0)
    slope = _head_slopes(N_HEADS, 1, 0)
    tok = lax.broadcasted_iota(jnp.int32, (1, nc), 1)
    c_mid = tok.astype(F32) * CMP_BLOCK + (CMP_BLOCK - 1) / 2
    c_ok = (tok + 1) * CMP_BLOCK - 1 <= past
    bias = slope * (float(past) - c_mid)

    s_c = jnp.zeros((N_HEADS, nc), F32)
    for kv in range(N_KV_HEADS):
        s_kv = _dot3(q, kcv_ref[0, kv * HEAD_DIM:(kv + 1) * HEAD_DIM, :])
        s_c = jnp.where(row // GQA == kv, s_kv, s_c)
    p_c = _masked_softmax(s_c * scale - bias, c_ok)
    o_c = jnp.zeros((N_HEADS, HEAD_DIM), F32)
    for kv in range(N_KV_HEADS):
        o_kv = _dot3(p_c, kcv_ref[0, KV_W + kv * HEAD_DIM:KV_W + (kv + 1) * HEAD_DIM, :], nt=True)
        o_c = jnp.where(row // GQA == kv, o_kv, o_c)
    oc_ref[0] = o_c

    imp = p_c
    shift = 1
    while shift < GQA:
        imp = imp + pltpu.roll(imp, shift, 0)
        shift *= 2
    assert per_block == 2
    pair = imp + pltpu.roll(imp, nc - 1, 1)
    jblk = tok // per_block
    real = tok % per_block == 0
    forced = (jblk == 0) | (jblk == n_past_blocks - 1)
    score = jnp.where(real, pair + jnp.where(forced, FORCE_BONUS, 0.0), -2.0)
    own = jnp.float32(FORCE_BONUS)
    rank = (own > score).astype(jnp.int32)
    for k in range(n_past_blocks):
        col = score[:, per_block * k:per_block * k + 1]
        beats = (col > score) | ((col == score) & (jblk > k))
        rank = rank + beats.astype(jnp.int32)
    own_rank = jnp.sum(jnp.where(real & (score >= own), 1, 0), axis=-1, keepdims=True)
    out_lane = lax.broadcasted_iota(jnp.int32, (N_HEADS, LANES), 1)
    idx = jnp.zeros((N_HEADS, LANES), jnp.int32)
    for r in range(SLC_TOPK):
        hit = jnp.sum(jnp.where(real & (rank == r), jblk, 0), axis=-1, keepdims=True)
        hit = hit + jnp.where(own_rank == r, n_past_blocks, 0)
        idx = jnp.where(out_lane == r, hit, idx)
    idx_ref[0] = idx


def _select_sample(q16, kcv, past):
    B = q16.shape[0]
    nc = kcv.shape[2]
    assert past % SLC_BLOCK == 0 and nc == past // CMP_BLOCK and nc % LANES == 0
    assert past // SLC_BLOCK + 1 > SLC_TOPK
    return pl.pallas_call(
        functools.partial(_select_sample_kernel, past=past),
        grid=(B,),
        in_specs=[pl.BlockSpec((1,) + q16.shape[1:], lambda b: (b, 0, 0)),
                  pl.BlockSpec((1,) + kcv.shape[1:], lambda b: (b, 0, 0))],
        out_specs=[pl.BlockSpec((1, N_HEADS, HEAD_DIM), lambda b: (b, 0, 0)),
                   pl.BlockSpec((1, N_HEADS, LANES), lambda b: (b, 0, 0))],
        out_shape=[jax.ShapeDtypeStruct((B, N_HEADS, HEAD_DIM), F32),
                   jax.ShapeDtypeStruct((B, N_HEADS, LANES), jnp.int32)],
        compiler_params=_cparams(("parallel",)),
        name="select_sample",
    )(q16, kcv)


def _attend_sample_kernel(pt_ref, idx_ref, q_ref, oc_ref, gt_ref, new_ref, win_ref, cache_ref,
                          o_ref, kbuf, vbuf, sems, *, past):
    b = pl.program_id(0)
    scale = HEAD_DIM ** -0.5
    n_past_blocks = past // SLC_BLOCK
    page_rows = cache_ref.shape[2]
    blocks_per_page = page_rows // SLC_BLOCK
    nkeys = SLC_TOPK * page_rows

    def block_copies(kv, n):
        blk = idx_ref[(b * N_KV_HEADS + kv) * SLC_TOPK + n]
        in_past = blk < n_past_blocks
        page = pt_ref[b, jnp.minimum(blk, n_past_blocks - 1) // blocks_per_page]
        dst = pl.ds(n * page_rows, page_rows)
        ck = pltpu.make_async_copy(cache_ref.at[page, pl.ds((2 * N_KV_HEADS + kv) * HEAD_DIM, HEAD_DIM), :],
                                   kbuf.at[kv, :, dst], sems.at[0, kv, n])
        cv = pltpu.make_async_copy(cache_ref.at[page, pl.ds((3 * N_KV_HEADS + kv) * HEAD_DIM, HEAD_DIM), :],
                                   vbuf.at[kv, :, dst], sems.at[1, kv, n])
        return blk, in_past, ck, cv

    for kv in range(N_KV_HEADS):
        for n in range(SLC_TOPK):
            _, in_past, ck, cv = block_copies(kv, n)

            @pl.when(in_past)
            def _():
                ck.start()
                cv.start()

            @pl.when(jnp.logical_not(in_past))
            def _():
                kbuf[kv, :, n * page_rows:(n + 1) * page_rows] = jnp.zeros((HEAD_DIM, page_rows), F32)
                vbuf[kv, :, n * page_rows:(n + 1) * page_rows] = jnp.zeros((HEAD_DIM, page_rows), F32)

    q = q_ref[0]
    row = lax.broadcasted_iota(jnp.int32, (N_HEADS, 1), 0)
    slope = _head_slopes(N_HEADS, 1, 0)
    key_lane = lax.broadcasted_iota(jnp.int32, (1, nkeys), 1)

    wb = win_ref.shape[2]
    w_lane = lax.broadcasted_iota(jnp.int32, (1, wb), 1)
    wpos = past - wb + w_lane
    w_ok = (past - wpos < WINDOW) & (wpos >= 0)
    w_bias = slope * (past - wpos).astype(F32)
    s_w = jnp.zeros((N_HEADS, wb), F32)
    s_n = jnp.zeros((N_HEADS, 1), F32)
    for kv in range(N_KV_HEADS):
        mine = row // GQA == kv
        s_kv = _dot3(q, win_ref[0, kv * HEAD_DIM:(kv + 1) * HEAD_DIM, :])
        s_w = jnp.where(mine, s_kv, s_w)
        s_n = jnp.where(mine, jnp.sum(q * new_ref[0, 4, kv:kv + 1, :], axis=-1, keepdims=True), s_n)
    s_w = jnp.where(w_ok, s_w * scale - w_bias, NEG)
    s_n = s_n * scale
    m_w = jnp.maximum(jnp.max(s_w, axis=-1, keepdims=True), s_n)
    p_w = jnp.where(w_ok, jnp.exp(s_w - m_w), 0.0)
    p_n = jnp.exp(s_n - m_w)
    l_w = jnp.maximum(jnp.sum(p_w, axis=-1, keepdims=True) + p_n, 1e-30)
    p_w = p_w / l_w
    p_n = p_n / l_w
    o_w = jnp.zeros((N_HEADS, HEAD_DIM), F32)
    for kv in range(N_KV_HEADS):
        o_kv = (_dot3(p_w, win_ref[0, KV_W + kv * HEAD_DIM:KV_W + (kv + 1) * HEAD_DIM, :], nt=True)
                + p_n * new_ref[0, 5, kv:kv + 1, :])
        o_w = jnp.where(row // GQA == kv, o_kv, o_w)

    for kv in range(N_KV_HEADS):
        for n in range(SLC_TOPK):
            _, in_past, ck, cv = block_copies(kv, n)

            @pl.when(in_past)
            def _():
                ck.wait()
                cv.wait()

    s_s = jnp.zeros((N_HEADS, nkeys), F32)
    pos = jnp.zeros((N_HEADS, nkeys), jnp.int32)
    live = jnp.zeros((N_HEADS, nkeys), jnp.int32)
    own = jnp.zeros((N_HEADS, 1), jnp.int32)
    s_n = jnp.zeros((N_HEADS, 1), F32)
    in_slab = key_lane % page_rows
    for kv in range(N_KV_HEADS):
        mine = row // GQA == kv
        s_kv = _dot3(q, kbuf[kv])
        s_s = jnp.where(mine, s_kv, s_s)
        s_n = jnp.where(mine, jnp.sum(q * new_ref[0, 2, kv:kv + 1, :], axis=-1, keepdims=True), s_n)
        base_kv = jnp.zeros((1, nkeys), jnp.int32)
        half_kv = jnp.zeros((1, nkeys), jnp.int32)
        own_kv = jnp.int32(0)
        for n in range(SLC_TOPK):
            blk = idx_ref[(b * N_KV_HEADS + kv) * SLC_TOPK + n]
            here = key_lane // page_rows == n
            base_kv = jnp.where(here, (blk // blocks_per_page) * page_rows, base_kv)
            half_kv = jnp.where(here, blk % blocks_per_page, half_kv)
            own_kv = own_kv + (blk >= n_past_blocks).astype(jnp.int32)
        pos = jnp.where(mine, base_kv + in_slab, pos)
        live = jnp.where(mine, (in_slab // SLC_BLOCK == half_kv).astype(jnp.int32), live)
        own = jnp.where(mine, own_kv, own)
    s_ok = (live > 0) & (pos < past)
    n_ok = own > 0
    s_s = jnp.where(s_ok, s_s * scale - slope * (past - pos).astype(F32), NEG)
    s_n = jnp.where(n_ok, s_n * scale, NEG)
    m_s = jnp.maximum(jnp.max(s_s, axis=-1, keepdims=True), s_n)
    p_s = jnp.where(s_ok, jnp.exp(s_s - m_s), 0.0)
    p_n = jnp.where(n_ok, jnp.exp(s_n - m_s), 0.0)
    l_s = jnp.maximum(jnp.sum(p_s, axis=-1, keepdims=True) + p_n, 1e-30)
    p_s = p_s / l_s
    p_n = p_n / l_s
    o_s = jnp.zeros((N_HEADS, HEAD_DIM), F32)
    for kv in range(N_KV_HEADS):
        o_kv = _dot3(p_s, vbuf[kv], nt=True) + p_n * new_ref[0, 3, kv:kv + 1, :]
        o_s = jnp.where(row // GQA == kv, o_kv, o_s)

    gt = jax.nn.sigmoid(gt_ref[0])
    o_ref[0] = gt[:, 0:1] * oc_ref[0] + gt[:, 1:2] * o_s + gt[:, 2:3] * o_w


def _attend_sample(page_table, idx, q16, o_c, gates, kv_new, win_t, cache_t, past):
    B = q16.shape[0]
    page_rows = cache_t.shape[2]
    assert win_t.shape[2] <= WINDOW and page_rows % SLC_BLOCK == 0
    blk3 = lambda a: pl.BlockSpec((1,) + a.shape[1:], lambda b, pt, ix: (b,) + (0,) * (a.ndim - 1))
    return pl.pallas_call(
        functools.partial(_attend_sample_kernel, past=past),
        grid_spec=pltpu.PrefetchScalarGridSpec(
            num_scalar_prefetch=2,
            grid=(B,),
            in_specs=[blk3(q16), blk3(o_c), blk3(gates), blk3(kv_new), blk3(win_t),
                      pl.BlockSpec(memory_space=pl.ANY)],
            out_specs=pl.BlockSpec((1, N_HEADS, HEAD_DIM), lambda b, pt, ix: (b, 0, 0)),
            scratch_shapes=[pltpu.VMEM((N_KV_HEADS, HEAD_DIM, SLC_TOPK * page_rows), F32),
                            pltpu.VMEM((N_KV_HEADS, HEAD_DIM, SLC_TOPK * page_rows), F32),
                            pltpu.SemaphoreType.DMA((2, N_KV_HEADS, SLC_TOPK))],
        ),
        out_shape=jax.ShapeDtypeStruct((B, N_HEADS, HEAD_DIM), F32),
        compiler_params=_cparams(("arbitrary",)),
        name="attend_sample",
    )(page_table, idx, q16, o_c, gates, kv_new, win_t, cache_t)


def _linear_res_kernel(x_ref, a_ref, w_ref, o_ref):
    o_ref[...] = x_ref[...] + _dot3(a_ref[...], w_ref[...])


def _linear_res(x, a, w):
    return pl.pallas_call(
        _linear_res_kernel,
        out_shape=jax.ShapeDtypeStruct(x.shape, F32),
        compiler_params=pltpu.CompilerParams(vmem_limit_bytes=VMEM_LIMIT),
        name="out_proj_sample",
    )(x, a, w)


def _expand_cmp_weights(w_cmp, rows):
    w = jnp.repeat(w_cmp, HEAD_DIM, axis=1)
    return jnp.tile(w, (rows // CMP_BLOCK, 1))


def _row(v):
    return v.reshape(1, -1)


def _prompt_mixers(x, norm_mix, w_pool, pool_scale, w_in, w_cmp_k, w_cmp_v, w_out, norm_ffn0, wfg, wfu, wfd):
    B, L, D = x.shape
    q_w = N_HEADS * HEAD_DIM
    x, h_last = _pool_prompt(x, _row(norm_mix[0]), w_pool.astype(BF16), _row(pool_scale))
    new_pool = h_last[None, :, POOL_HALO - (max(POOL_WINDOWS) - 1):, :]
    x = _ffn(x.reshape(B * L, D), _row(norm_ffn0), wfg.astype(BF16), wfu.astype(BF16), wfd.astype(BF16),
             tm=FFN_ROW_TILE)

    tm = PROJ_ROW_TILE
    wq = w_in[:, :q_w].astype(BF16)
    wkv = w_in[:, q_w:q_w + 6 * KV_W].astype(BF16)
    wgt = jnp.pad(w_in[:, q_w + 6 * KV_W:], ((0, 0), (0, LANES - N_BRANCH * N_HEADS))).astype(BF16)
    q, kv_t, ks, vs, kw, vw, gates, kc, vc = _nsa_proj(
        x, _row(norm_mix[1]), wq, wkv, wgt, _expand_cmp_weights(w_cmp_k, tm), _expand_cmp_weights(w_cmp_v, tm),
        tm=tm, seq=L)
    kv_t = kv_t.reshape(B, 6, N_KV_HEADS, HEAD_DIM, L)
    new_kv = kv_t[:, :4].transpose(0, 4, 1, 2, 3)[None]
    new_win = kv_t[:, 4:, :, :, L - min(WINDOW, L):].transpose(0, 4, 1, 2, 3)[None]
    nc = L // CMP_BLOCK
    pad_c = lambda a: jnp.pad(a.reshape(B, nc, KV_W), ((0, 0), (0, LANES - nc), (0, 0))).astype(BF16)
    per_seq = lambda a: a.reshape(B, L, a.shape[-1])
    x = _nsa_prompt(per_seq(q), per_seq(gates), pad_c(kc), pad_c(vc), per_seq(ks), per_seq(vs),
                    per_seq(kw), per_seq(vw), per_seq(x), w_out.astype(BF16))
    return x.reshape(B * L, D), new_pool, new_kv, new_win


def _sample_mixers(x, state_pool, cache, state_win, page_table, norm_mix, w_pool, pool_scale, w_in, w_cmp_k,
                   w_cmp_v, w_out, norm_ffn0, wfg, wfu, wfd):
    SB, D = x.shape
    n_phys, page = cache.shape[:2]
    past = page_table.shape[1] * page
    q_w = N_HEADS * HEAD_DIM
    x, h = _pool_sample(x, state_pool.transpose(1, 0, 2), _row(norm_mix[0]), w_pool, _row(pool_scale))
    new_pool = jnp.concatenate([state_pool[:, 1:], h[:, None]], axis=1)[None]
    x = _ffn(x, _row(norm_ffn0), wfg, wfu, wfd, tm=SB)

    cols = w_in.shape[1]
    z = _nsa_proj_sample(x, _row(norm_mix[1]), jnp.pad(w_in, ((0, 0), (0, -cols % LANES))))
    kv_new = z[:, q_w:q_w + 6 * KV_W].reshape(SB, 6, N_KV_HEADS, HEAD_DIM)
    new_kv = kv_new[None, :, None, :4]
    keep = min(WINDOW, state_win.shape[1] + 1)
    new_win = jnp.concatenate([state_win, kv_new[:, None, 4:]], axis=1)[None, :, -keep:]

    cache_t = cache.transpose(0, 2, 3, 4, 1).reshape(n_phys, 4 * KV_W, page)
    win_t = state_win.transpose(0, 2, 3, 4, 1).reshape(SB, 2 * KV_W, state_win.shape[1])
    w_rows = jnp.concatenate([_expand_cmp_weights(w_cmp_k, page).T, _expand_cmp_weights(w_cmp_v, page).T], axis=0)
    kcv = _cmp_sample(page_table, cache_t, w_rows)
    q16 = z[:, :q_w].reshape(SB, N_HEADS, HEAD_DIM)
    o_c, idx = _select_sample(q16, kcv, past)
    idx = idx[:, GQA - 1::GQA, :SLC_TOPK].reshape(-1)
    gate_logits = z[:, q_w + 6 * KV_W:cols].reshape(SB, N_BRANCH, N_HEADS).transpose(0, 2, 1)
    gate_logits = jnp.pad(gate_logits, ((0, 0), (0, 0), (0, LANES - N_BRANCH)))
    o = _attend_sample(page_table, idx, q16, o_c, gate_logits, kv_new, win_t, cache_t, past)
    return _linear_res(x, o.reshape(SB, q_w), w_out), new_pool, new_kv, new_win


def kernel(x_prompt, x_sample, state_pool, cache_kv, state_win, page_table, norm_mix, w_pool, pool_scale,
           w_nsa_in, w_cmp_k, w_cmp_v, w_nsa_out, norm_ffn, w_ffn_gate, w_ffn_up, w_ffn_down, w_router,
           w_moe_gate, w_moe_up, w_moe_down, norm_final):
    assert x_sample.shape[1] == 1 and norm_mix.shape[0] == 2 and x_prompt.shape[-1] == N_HEADS * HEAD_DIM
    w_rt = jnp.pad(w_router[0], ((0, 0), (0, LANES - w_router.shape[-1])))
    moe_w = (w_moe_gate[0].astype(BF16), w_moe_up[0].astype(BF16), w_moe_down[0].astype(BF16))

    xp, new_pool_prompt, new_kv_prompt, new_win_prompt = _prompt_mixers(
        x_prompt, norm_mix, w_pool[0], pool_scale[0], w_nsa_in[0], w_cmp_k[0], w_cmp_v[0], w_nsa_out[0],
        norm_ffn[0], w_ffn_gate, w_ffn_up, w_ffn_down)
    xs, new_pool_sample, new_kv_sample, new_win_sample = _sample_mixers(
        x_sample[:, 0], state_pool[0], cache_kv[0], state_win[0], page_table, norm_mix, w_pool[0], pool_scale[0],
        w_nsa_in[0], w_cmp_k[0], w_cmp_v[0], w_nsa_out[0], norm_ffn[0], w_ffn_gate, w_ffn_up, w_ffn_down)

    yp = _moe_prompt(xp, _row(norm_ffn[1]), w_rt, *moe_w, _row(norm_final))
    ys = _ffn(xs, _row(norm_ffn[1]), *moe_w, w_rt, _row(norm_final), tm=xs.shape[0])
    return (yp.reshape(x_prompt.shape), ys.reshape(x_sample.shape), new_pool_prompt, new_pool_sample,
            new_kv_prompt, new_kv_sample, new_win_prompt, new_win_sample)
```

```python
import functools

import jax
import jax.numpy as jnp
from jax import lax
from jax.experimental import pallas as pl
from jax.experimental.pallas import tpu as pltpu

F32 = jnp.float32
BF16 = jnp.bfloat16

EPS = 1e-6
NEG = -1e30
POOL_WINDOWS = (2, 4, 8, 16)
POOL_HALO = 16
N_HEADS = 16
N_KV_HEADS = 4
GQA = N_HEADS // N_KV_HEADS
HEAD_DIM = 64
KV_W = N_KV_HEADS * HEAD_DIM
CMP_BLOCK = 32
SLC_BLOCK = 64
SLC_TOPK = 16
WINDOW = 512
N_BRANCH = 3
FORCE_BONUS = 1000.0
TOP_K = 2
LANES = 128
VMEM_LIMIT = 56 * 1024 * 1024

POOL_ROW_TILE = 512
FFN_ROW_TILE = 1024
FFN_COL_TILE = 512
PROJ_ROW_TILE = 512
ATTN_Q_TILE = 256
ATTN_KEY_CHUNK = 512
ROUTE_ROW_TILE = 512
EXPERT_ROW_TILE = 448
COMBINE_ROW_TILE = 256
SAMPLE_PROJ_COL_TILE = 384


def _cparams(sem):
    return pltpu.CompilerParams(dimension_semantics=sem, vmem_limit_bytes=VMEM_LIMIT)


def _rmsnorm(x, g):
    ms = jnp.mean(x * x, axis=-1, keepdims=True)
    return (x * lax.rsqrt(ms + EPS)) * g


def _dot(a, b):
    return jnp.dot(a, b, preferred_element_type=F32)


def _dot_nt(a, b):
    return lax.dot_general(a, b, (((1,), (1,)), ((), ())), preferred_element_type=F32)


def _split(a):
    hi = a.astype(BF16)
    return hi, (a - hi.astype(F32)).astype(BF16)


def _dot3(a, b, nt=False):
    d = _dot_nt if nt else _dot
    ah, al = _split(a)
    bh, bl = _split(b)
    return d(ah, bh) + (d(ah, bl) + d(al, bh))


def _masked_softmax(s, mask):
    s = jnp.where(mask, s, NEG)
    m = jnp.max(s, axis=-1, keepdims=True)
    p = jnp.where(mask, jnp.exp(s - m), 0.0)
    return p / jnp.maximum(jnp.sum(p, axis=-1, keepdims=True), 1e-30)


def _alibi_slope(h):
    return 2.0 ** (-8.0 * (h + 1) / N_HEADS)


def _head_slopes(rows, rows_per_head, first_head):
    h = lax.broadcasted_iota(jnp.int32, (rows, 1), 0) // rows_per_head
    slopes = jnp.zeros((rows, 1), F32)
    for j in range(rows // rows_per_head):
        slopes = jnp.where(h == j, _alibi_slope(first_head + j), slopes)
    return slopes


def _pool_prompt_kernel(x_ref, halo_ref, g_ref, w_ref, sc_ref, o_ref, hl_ref, full_ref, *, tile):
    t = pl.program_id(1)
    g = g_ref[...]
    x = x_ref[0]
    h = _rmsnorm(x, g)
    hh = _rmsnorm(halo_ref[0], g)
    full_ref[0:POOL_HALO, :] = jnp.where(t > 0, hh, 0.0)
    full_ref[POOL_HALO:POOL_HALO + tile, :] = h
    row = t * tile + lax.broadcasted_iota(jnp.int32, (tile, 1), 0)
    group = x.shape[-1] // len(POOL_WINDOWS)
    parts = []
    for gi, w in enumerate(POOL_WINDOWS):
        cs = slice(gi * group, (gi + 1) * group)
        hg = h[:, cs]
        acc = hg
        for k in range(1, w):
            acc = acc + full_ref[POOL_HALO - k:POOL_HALO - k + tile, cs]
        cnt = jnp.minimum(row + 1, w).astype(F32)
        pooled = acc / cnt - hg
        parts.append(_dot(pooled.astype(BF16), w_ref[gi]))
    o_ref[0] = x + jnp.concatenate(parts, axis=-1) * sc_ref[...]

    @pl.when(t == pl.num_programs(1) - 1)
    def _():
        hl_ref[0] = h[tile - POOL_HALO:, :]


def _pool_prompt(x, g, w_pool, scale, tile=POOL_ROW_TILE):
    B, L, D = x.shape
    assert L % tile == 0 and tile % POOL_HALO == 0
    hb = tile // POOL_HALO
    return pl.pallas_call(
        functools.partial(_pool_prompt_kernel, tile=tile),
        grid=(B, L // tile),
        in_specs=[
            pl.BlockSpec((1, tile, D), lambda b, t: (b, t, 0)),
            pl.BlockSpec((1, POOL_HALO, D), lambda b, t: (b, jnp.maximum(t * hb - 1, 0), 0)),
            pl.BlockSpec((1, D), lambda b, t: (0, 0)),
            pl.BlockSpec(w_pool.shape, lambda b, t: (0, 0, 0)),
            pl.BlockSpec((1, D), lambda b, t: (0, 0)),
        ],
        out_specs=[
            pl.BlockSpec((1, tile, D), lambda b, t: (b, t, 0)),
            pl.BlockSpec((1, POOL_HALO, D), lambda b, t: (b, 0, 0)),
        ],
        out_shape=[jax.ShapeDtypeStruct((B, L, D), F32), jax.ShapeDtypeStruct((B, POOL_HALO, D), F32)],
        scratch_shapes=[pltpu.VMEM((tile + POOL_HALO, D), F32)],
        compiler_params=_cparams(("parallel", "arbitrary")),
        name="pool_prompt",
    )(x, x, g, w_pool, scale)


def _pool_sample_kernel(x_ref, st_ref, g_ref, w_ref, sc_ref, o_ref, h_ref):
    x = x_ref[...]
    h = _rmsnorm(x, g_ref[...])
    h_ref[...] = h
    P = st_ref.shape[0]
    group = x.shape[-1] // len(POOL_WINDOWS)
    parts = []
    for gi, w in enumerate(POOL_WINDOWS):
        cs = slice(gi * group, (gi + 1) * group)
        hg = h[:, cs]
        acc = hg
        for k in range(1, w):
            acc = acc + st_ref[P - k][:, cs]
        pooled = acc / float(w) - hg
        parts.append(_dot3(pooled, w_ref[gi]))
    o_ref[...] = x + jnp.concatenate(parts, axis=-1) * sc_ref[...]


def _pool_sample(x, state_t, g, w_pool, scale):
    B, D = x.shape
    assert state_t.shape[0] >= max(POOL_WINDOWS) - 1
    return pl.pallas_call(
        _pool_sample_kernel,
        out_shape=[jax.ShapeDtypeStruct((B, D), F32), jax.ShapeDtypeStruct((B, D), F32)],
        compiler_params=pltpu.CompilerParams(vmem_limit_bytes=VMEM_LIMIT),
        name="pool_sample",
    )(x, state_t, g, w_pool, scale)


def _ffn_kernel(*refs, n_experts, final_norm, precise):
    moe = n_experts > 1
    it = iter(refs)
    x_ref, g_ref = next(it), next(it)
    wr_ref = next(it) if moe else None
    wg_ref, wu_ref, wd_ref = next(it), next(it), next(it)
    gf_ref = next(it) if final_norm else None
    o_ref, h_scr, acc_scr = next(it), next(it), next(it)
    eacc_scr, gate_scr = (next(it), next(it)) if moe else (None, None)

    e, f = pl.program_id(1), pl.program_id(2)
    last_f = f == pl.num_programs(2) - 1

    @pl.when((e == 0) & (f == 0))
    def _():
        h = _rmsnorm(x_ref[...], g_ref[...])
        h_scr[...] = h.astype(h_scr.dtype)
        acc_scr[...] = jnp.zeros_like(acc_scr)
        if moe:
            logits = _dot3(h, wr_ref[...])
            lane, i1, i2, w1, w2 = _top2_gates(logits, n_experts)
            gate_scr[...] = jnp.where(lane == i1, w1, 0.0) + jnp.where(lane == i2, w2, 0.0)

    mm = _dot3 if precise else _dot
    hb = h_scr[...]
    a = mm(hb, wg_ref[0])
    u = mm(hb, wu_ref[0])
    act = (a * jax.nn.sigmoid(a)) * u
    y = mm(act.astype(hb.dtype), wd_ref[0])

    if moe:
        @pl.when(f == 0)
        def _():
            eacc_scr[...] = y

        @pl.when(f > 0)
        def _():
            eacc_scr[...] += y

        @pl.when(last_f)
        def _():
            gate = gate_scr[...]
            lane = lax.broadcasted_iota(jnp.int32, gate.shape, 1)
            ge = jnp.sum(jnp.where(lane == e, gate, 0.0), axis=-1, keepdims=True)
            acc_scr[...] += ge * eacc_scr[...]
    else:
        acc_scr[...] += y

    @pl.when((e == pl.num_programs(1) - 1) & last_f)
    def _():
        out = x_ref[...] + acc_scr[...]
        if final_norm:
            out = _rmsnorm(out, gf_ref[...])
        o_ref[...] = out


def _ffn(x, g, wg, wu, wd, w_router=None, g_final=None, *, tm, tf=FFN_COL_TILE):
    precise = wg.dtype == F32
    N, D = x.shape
    E, _, F = wg.shape
    assert N % tm == 0 and F % tf == 0
    moe = w_router is not None
    assert moe == (E > 1)
    final_norm = g_final is not None
    const2 = lambda i, e, f: (0, 0)
    args, in_specs = [x, g], [pl.BlockSpec((tm, D), lambda i, e, f: (i, 0)), pl.BlockSpec((1, D), const2)]
    if moe:
        args.append(w_router)
        in_specs.append(pl.BlockSpec(w_router.shape, const2))
    args += [wg, wu, wd]
    in_specs += [
        pl.BlockSpec((1, D, tf), lambda i, e, f: (e, 0, f)),
        pl.BlockSpec((1, D, tf), lambda i, e, f: (e, 0, f)),
        pl.BlockSpec((1, tf, D), lambda i, e, f: (e, f, 0)),
    ]
    if final_norm:
        args.append(g_final)
        in_specs.append(pl.BlockSpec((1, D), const2))
    scratch = [pltpu.VMEM((tm, D), F32 if precise else BF16), pltpu.VMEM((tm, D), F32)]
    if moe:
        scratch += [pltpu.VMEM((tm, D), F32), pltpu.VMEM((tm, LANES), F32)]
    return pl.pallas_call(
        functools.partial(_ffn_kernel, n_experts=E, final_norm=final_norm, precise=precise),
        grid=(N // tm, E, F // tf),
        in_specs=in_specs,
        out_specs=pl.BlockSpec((tm, D), lambda i, e, f: (i, 0)),
        out_shape=jax.ShapeDtypeStruct((N, D), F32),
        scratch_shapes=scratch,
        compiler_params=_cparams(("parallel", "arbitrary", "arbitrary")),
        name="moe_ffn" if moe else "dense_ffn",
    )(*args)


ROUTE_I1, ROUTE_I2, ROUTE_W1, ROUTE_W2, ROUTE_R1, ROUTE_R2 = range(6)


def _top2_gates(logits, n_experts):
    lane = lax.broadcasted_iota(jnp.int32, logits.shape, 1)
    lg = jnp.where(lane < n_experts, logits, -jnp.inf)
    m1 = jnp.max(lg, axis=-1, keepdims=True)
    i1 = jnp.min(jnp.where(lg == m1, lane, LANES), axis=-1, keepdims=True)
    lg2 = jnp.where(lane == i1, -jnp.inf, lg)
    m2 = jnp.max(lg2, axis=-1, keepdims=True)
    i2 = jnp.min(jnp.where(lg2 == m2, lane, LANES), axis=-1, keepdims=True)
    e2 = jnp.exp(m2 - m1)
    den = 1.0 + e2
    return lane, i1, i2, 1.0 / den, e2 / den


def _moe_route_kernel(x_ref, g_ref, wr_ref, tri_ref, route_ref, cnt_ref, carry, *, n_experts):
    @pl.when(pl.program_id(0) == 0)
    def _():
        carry[...] = jnp.zeros_like(carry)

    h = _rmsnorm(x_ref[...], g_ref[...])
    logits = _dot3(h, wr_ref[...])
    lane, i1, i2, w1, w2 = _top2_gates(logits, n_experts)
    member = jnp.where((lane == i1) | (lane == i2), 1.0, 0.0)
    before = _dot(tri_ref[...], member.astype(BF16)) + carry[0:1, :]
    r1 = jnp.sum(jnp.where(lane == i1, before, 0.0), axis=-1, keepdims=True)
    r2 = jnp.sum(jnp.where(lane == i2, before, 0.0), axis=-1, keepdims=True)
    rec = jnp.zeros(logits.shape, F32)
    for col, val in ((ROUTE_I1, i1.astype(F32)), (ROUTE_I2, i2.astype(F32)), (ROUTE_W1, w1), (ROUTE_W2, w2),
                     (ROUTE_R1, r1), (ROUTE_R2, r2)):
        rec = jnp.where(lane == col, val, rec)
    route_ref[...] = rec
    carry[...] = carry[...] + jnp.sum(member, axis=0, keepdims=True)
    cnt_ref[...] = carry[...]


def _moe_route(x, g, w_router, n_experts, tm=ROUTE_ROW_TILE):
    N, D = x.shape
    assert N % tm == 0
    tri = (jnp.arange(tm)[:, None] > jnp.arange(tm)[None, :]).astype(BF16)
    return pl.pallas_call(
        functools.partial(_moe_route_kernel, n_experts=n_experts),
        grid=(N // tm,),
        in_specs=[pl.BlockSpec((tm, D), lambda i: (i, 0)), pl.BlockSpec((1, D), lambda i: (0, 0)),
                  pl.BlockSpec(w_router.shape, lambda i: (0, 0)), pl.BlockSpec((tm, tm), lambda i: (0, 0))],
        out_specs=[pl.BlockSpec((tm, LANES), lambda i: (i, 0)), pl.BlockSpec((8, LANES), lambda i: (0, 0))],
        out_shape=[jax.ShapeDtypeStruct((N, LANES), F32), jax.ShapeDtypeStruct((8, LANES), F32)],
        scratch_shapes=[pltpu.VMEM((8, LANES), F32)],
        compiler_params=_cparams(("arbitrary",)),
        name="moe_route",
    )(x, g, w_router, tri)


GATHER_UNROLL = 8
GATHER_PRIORITY = 1


def _row_gather(idx_of, src_hbm, dst_at, sem, rows, priority):
    assert rows % GATHER_UNROLL == 0

    def copy(r):
        return pltpu.make_async_copy(src_hbm.at[pl.ds(idx_of(r), 1), :], dst_at(r), sem)

    def start():
        def group(i, c):
            for j in range(GATHER_UNROLL):
                copy(i * GATHER_UNROLL + j).start(priority=priority(j))
            return c
        lax.fori_loop(0, rows // GATHER_UNROLL, group, 0)

    def wait():
        lax.fori_loop(0, rows, lambda r, c: (copy(r).wait(), c)[1], 0, unroll=GATHER_UNROLL)

    return start, wait, copy


def _moe_experts_kernel(te_ref, nu_ref, tos_ref, x_hbm, g_ref, wg_ref, wu_ref, wd_ref, o_ref,
                        xbuf, h_scr, acc_scr, sems, *, tile, nf):
    i, f = pl.program_id(0), pl.program_id(1)
    used = i < nu_ref[0]
    slot = i % 2
    per_step = tile // nf

    def gather(t, sl):
        return _row_gather(lambda r: tos_ref[t * tile + r], x_hbm,
                           lambda r: xbuf.at[sl, pl.ds(r, 1), :], sems.at[sl], tile, lambda j: GATHER_PRIORITY)

    def prefetch_share():
        row_copy = gather(i + 1, 1 - slot)[2]
        for j in range(per_step):
            row_copy(f * per_step + j).start(priority=GATHER_PRIORITY)

    @pl.when((f == 0) & (i == 0))
    def _():
        gather(0, 0)[0]()

    @pl.when(f == 0)
    def _():
        gather(i, slot)[1]()

    @pl.when((f == 0) & used)
    def _():
        h_scr[...] = _rmsnorm(xbuf[slot], g_ref[...]).astype(BF16)
        acc_scr[...] = jnp.zeros_like(acc_scr)

    @pl.when(used)
    def _():
        prefetch_share()
        hb = h_scr[...]
        a = _dot(hb, wg_ref[0])
        u = _dot(hb, wu_ref[0])
        act = (a * jax.nn.sigmoid(a)) * u
        acc_scr[...] += _dot(act.astype(BF16), wd_ref[0])

    @pl.when(jnp.logical_not(used))
    def _():
        prefetch_share()

    @pl.when(f == nf - 1)
    def _():
        o_ref[...] = jnp.where(used, acc_scr[...], 0.0)

    @pl.when((f == nf - 1) & (i == pl.num_programs(0) - 1))
    def _():
        gather(i + 1, 1 - slot)[1]()


def _moe_experts(x, g, wg, wu, wd, tile_expert, n_used, token_of_slot, *, tile, tf=FFN_COL_TILE):
    N, D = x.shape
    E, _, F = wg.shape
    n_slots = token_of_slot.shape[0] - tile
    assert n_slots % tile == 0 and F % tf == 0
    n_tiles, nf = n_slots // tile, F // tf
    assert tile % nf == 0
    fidx = lambda i, f, nu: jnp.where(i < nu[0], f, nf - 1)
    return pl.pallas_call(
        functools.partial(_moe_experts_kernel, tile=tile, nf=nf),
        grid_spec=pltpu.PrefetchScalarGridSpec(
            num_scalar_prefetch=3,
            grid=(n_tiles, nf),
            in_specs=[pl.BlockSpec(memory_space=pl.ANY),
                      pl.BlockSpec((1, D), lambda i, f, te, nu, tos: (0, 0)),
                      pl.BlockSpec((1, D, tf), lambda i, f, te, nu, tos: (te[i], 0, fidx(i, f, nu))),
                      pl.BlockSpec((1, D, tf), lambda i, f, te, nu, tos: (te[i], 0, fidx(i, f, nu))),
                      pl.BlockSpec((1, tf, D), lambda i, f, te, nu, tos: (te[i], fidx(i, f, nu), 0))],
            out_specs=pl.BlockSpec((tile, D), lambda i, f, te, nu, tos: (i, 0)),
            scratch_shapes=[pltpu.VMEM((2, tile, D), F32), pltpu.VMEM((tile, D), BF16),
                            pltpu.VMEM((tile, D), F32), pltpu.SemaphoreType.DMA((2,))],
        ),
        out_shape=jax.ShapeDtypeStruct((n_slots, D), F32),
        compiler_params=_cparams(("arbitrary", "arbitrary")),
        name="moe_experts",
    )(tile_expert, n_used, token_of_slot, x, g, wg, wu, wd)


def _moe_combine_kernel(slot_ref, x_ref, route_ref, gf_ref, ys_hbm, o_ref, ybuf, sems, *, tile):
    i = pl.program_id(0)
    slot = i % 2

    def gather(t, sl):
        return _row_gather(lambda j: slot_ref[t * 2 * tile + j], ys_hbm,
                           lambda j: ybuf.at[sl, pl.ds(j, 1), :], sems.at[sl], 2 * tile, lambda j: j % 2)

    @pl.when(i == 0)
    def _():
        gather(0, 0)[0]()

    gather(i, slot)[1]()

    @pl.when(i + 1 < pl.num_programs(0))
    def _():
        gather(i + 1, 1 - slot)[0]()

    route = route_ref[...]
    w1, w2 = route[:, ROUTE_W1:ROUTE_W1 + 1], route[:, ROUTE_W2:ROUTE_W2 + 1]
    y = w1 * ybuf[slot, 0:tile, :] + w2 * ybuf[slot, tile:2 * tile, :]
    o_ref[...] = _rmsnorm(x_ref[...] + y, gf_ref[...])


def _moe_combine(x, route, g_final, ys, slots, *, tile):
    N, D = x.shape
    assert N % tile == 0
    return pl.pallas_call(
        functools.partial(_moe_combine_kernel, tile=tile),
        grid_spec=pltpu.PrefetchScalarGridSpec(
            num_scalar_prefetch=1,
            grid=(N // tile,),
            in_specs=[pl.BlockSpec((tile, D), lambda i, s: (i, 0)),
                      pl.BlockSpec((tile, LANES), lambda i, s: (i, 0)),
                      pl.BlockSpec((1, D), lambda i, s: (0, 0)),
                      pl.BlockSpec(memory_space=pl.ANY)],
            out_specs=pl.BlockSpec((tile, D), lambda i, s: (i, 0)),
            scratch_shapes=[pltpu.VMEM((2, 2 * tile, D), F32), pltpu.SemaphoreType.DMA((2,))],
        ),
        out_shape=jax.ShapeDtypeStruct((N, D), F32),
        compiler_params=_cparams(("arbitrary",)),
        name="moe_combine",
    )(slots, x, route, g_final, ys)


def _moe_prompt(x, g, w_router, wg, wu, wd, g_final, *, tile=EXPERT_ROW_TILE, ctile=COMBINE_ROW_TILE):
    N, D = x.shape
    E = wg.shape[0]
    route, counts = _moe_route(x, g, w_router, E)
    cnt = counts[0, :E].astype(jnp.int32)
    padded = (cnt + tile - 1) // tile * tile
    ends = jnp.cumsum(padded)
    off = ends - padded
    i1, i2 = route[:, ROUTE_I1].astype(jnp.int32), route[:, ROUTE_I2].astype(jnp.int32)
    slot1 = off[i1] + route[:, ROUTE_R1].astype(jnp.int32)
    slot2 = off[i2] + route[:, ROUTE_R2].astype(jnp.int32)
    n_slots = (TOP_K * N + E * (tile - 1)) // tile * tile
    rows = jnp.arange(N, dtype=jnp.int32)
    token_of_slot = jnp.zeros((n_slots + tile,), jnp.int32).at[jnp.concatenate([slot1, slot2])].set(
        jnp.concatenate([rows, rows]), unique_indices=True)
    n_used = ends[-1:] // tile
    tile_start = jnp.minimum(jnp.arange(n_slots // tile, dtype=jnp.int32), n_used[0] - 1) * tile
    tile_expert = jnp.sum(tile_start[:, None] >= ends[None, :], axis=1).astype(jnp.int32)
    ys = _moe_experts(x, g, wg, wu, wd, tile_expert, n_used.astype(jnp.int32), token_of_slot, tile=tile)
    slots = jnp.stack([slot1.reshape(-1, ctile), slot2.reshape(-1, ctile)], axis=1).reshape(-1)
    return _moe_combine(x, route, g_final, ys, slots, tile=ctile)


def _nsa_proj_kernel(x_ref, g_ref, wq_ref, wkv_ref, wgt_ref, wck_ref, wcv_ref,
                     q_ref, kvt_ref, ks_ref, vs_ref, kw_ref, vw_ref, gt_ref, kc_ref, vc_ref):
    hb = _rmsnorm(x_ref[...], g_ref[...]).astype(BF16)
    q_ref[...] = _dot(hb, wq_ref[...]).astype(BF16)
    kv = _dot(hb, wkv_ref[...])
    kvt_ref[0] = kv.T
    ks_ref[...] = kv[:, 2 * KV_W:3 * KV_W].astype(BF16)
    vs_ref[...] = kv[:, 3 * KV_W:4 * KV_W].astype(BF16)
    kw_ref[...] = kv[:, 4 * KV_W:5 * KV_W].astype(BF16)
    vw_ref[...] = kv[:, 5 * KV_W:].astype(BF16)
    gt_ref[...] = jax.nn.sigmoid(_dot(hb, wgt_ref[...]))
    tm = kv.shape[0]
    kc_ref[...] = (kv[:, :KV_W] * wck_ref[...]).reshape(tm // CMP_BLOCK, CMP_BLOCK, KV_W).sum(axis=1)
    vc_ref[...] = (kv[:, KV_W:2 * KV_W] * wcv_ref[...]).reshape(tm // CMP_BLOCK, CMP_BLOCK, KV_W).sum(axis=1)


def _nsa_proj(x, g, wq, wkv, wgt, wck, wcv, *, tm, seq):
    N, D = x.shape
    assert N % tm == 0 and tm % (8 * CMP_BLOCK) == 0 and seq % tm == 0 and N % seq == 0
    per_seq = seq // tm
    row = lambda w: pl.BlockSpec((tm, w), lambda i: (i, 0))
    full = lambda a: pl.BlockSpec(a.shape, lambda i: (0, 0))
    out_shape = [jax.ShapeDtypeStruct((N, N_HEADS * HEAD_DIM), BF16),
                 jax.ShapeDtypeStruct((N // seq, 6 * KV_W, seq), F32)]
    out_shape += [jax.ShapeDtypeStruct((N, KV_W), BF16)] * 4
    out_shape += [jax.ShapeDtypeStruct((N, LANES), F32)]
    out_shape += [jax.ShapeDtypeStruct((N // CMP_BLOCK, KV_W), F32)] * 2
    out_specs = [row(N_HEADS * HEAD_DIM),
                 pl.BlockSpec((1, 6 * KV_W, tm), lambda i: (i // per_seq, 0, i % per_seq))]
    out_specs += [row(KV_W)] * 4 + [row(LANES)]
    out_specs += [pl.BlockSpec((tm // CMP_BLOCK, KV_W), lambda i: (i, 0))] * 2
    return pl.pallas_call(
        _nsa_proj_kernel,
        grid=(N // tm,),
        in_specs=[row(D), full(g), full(wq), full(wkv), full(wgt), full(wck), full(wcv)],
        out_specs=out_specs,
        out_shape=out_shape,
        compiler_params=_cparams(("parallel",)),
        name="nsa_proj_prompt",
    )(x, g, wq, wkv, wgt, wck, wcv)


def _nsa_proj_sample_kernel(x_ref, g_ref, w_ref, z_ref):
    z_ref[...] = _dot3(_rmsnorm(x_ref[...], g_ref[...]), w_ref[...])


def _nsa_proj_sample(x, g, w_in, tn=SAMPLE_PROJ_COL_TILE):
    B, D = x.shape
    cols = w_in.shape[1]
    assert cols % tn == 0
    return pl.pallas_call(
        _nsa_proj_sample_kernel,
        grid=(cols // tn,),
        in_specs=[pl.BlockSpec((B, D), lambda j: (0, 0)), pl.BlockSpec((1, D), lambda j: (0, 0)),
                  pl.BlockSpec((D, tn), lambda j: (0, j))],
        out_specs=pl.BlockSpec((B, tn), lambda j: (0, j)),
        out_shape=jax.ShapeDtypeStruct((B, cols), F32),
        compiler_params=_cparams(("parallel",)),
        name="nsa_proj_sample",
    )(x, g, w_in)


def _nsa_prompt_kernel(q_ref, gt_ref, kc_ref, vc_ref, ks_ref, vs_ref, kw_ref, vw_ref, x_ref, wo_ref,
                       o_ref, o_scr, sel_scr, *, tq, kchunk, seq):
    t0 = pl.program_id(1) * tq
    rows = GQA * tq
    scale = HEAD_DIM ** -0.5
    assert scale == 0.125
    n_blocks = seq // SLC_BLOCK
    top_k = min(SLC_TOPK, n_blocks)

    q = q_ref[0]
    gates = gt_ref[0]
    qpos_t = t0 + lax.broadcasted_iota(jnp.int32, (tq, 1), 0)
    qpos = jnp.concatenate([qpos_t] * GQA, axis=0)
    tq_f = qpos.astype(F32)

    lane = lax.broadcasted_iota(jnp.int32, (tq, LANES), 1)
    jblk = lane >> 1
    real = ((lane & 1) == 0) & (jblk < n_blocks)
    blk = qpos_t // SLC_BLOCK
    valid = real & (jblk <= blk)
    forced = (jblk == 0) | (jblk == blk) | (jblk == blk - 1)
    valid_f = jnp.where(valid, 1.0, 0.0)
    valid_b = valid_f.astype(BF16)
    bonus = jnp.where(forced, FORCE_BONUS, 0.0)
    floor = jnp.where(valid, 0.0, jnp.where(real, -1.0, -2.0))
    c_lane = lax.broadcasted_iota(jnp.int32, (1, LANES), 1)
    c_mid = c_lane.astype(F32) * CMP_BLOCK + (CMP_BLOCK - 1) / 2
    c_end = (c_lane + 1) * CMP_BLOCK - 1

    n_chunks = (t0 + tq + kchunk - 1) // kchunk
    wstart = pl.multiple_of(jnp.maximum(t0 - WINDOW, 0), tq)
    wlen = WINDOW + tq

    need_rank = (t0 + tq - 1) // SLC_BLOCK >= top_k
    wpos = wstart + lax.broadcasted_iota(jnp.int32, (1, wlen), 1)
    w_dist = jnp.where((wpos <= qpos_t) & (qpos_t - wpos < WINDOW), (wpos - qpos_t).astype(F32), NEG)

    def with_ones(v):
        return jnp.concatenate([v, jnp.ones_like(v)], axis=1)

    def biased(scores, dist, kv):
        return jnp.concatenate(
            [scores[g * tq:(g + 1) * tq] + _alibi_slope(kv * GQA + g) * dist for g in range(GQA)], axis=0)

    for kv in range(N_KV_HEADS):
        hs = slice(kv * HEAD_DIM, (kv + 1) * HEAD_DIM)
        qs = jnp.concatenate(
            [q[:, (kv * GQA + g) * HEAD_DIM:(kv * GQA + g + 1) * HEAD_DIM] for g in range(GQA)], axis=0) * scale
        slope = _head_slopes(rows, tq, kv * GQA)

        s_c = _dot_nt(qs, kc_ref[0][:, hs]) - slope * (tq_f - c_mid)
        p_c = _masked_softmax(s_c, c_end <= qpos)
        o_c = _dot(p_c.astype(BF16), vc_ref[0][:, hs])

        imp = p_c[0:tq]
        for g in range(1, GQA):
            imp = imp + p_c[g * tq:(g + 1) * tq]
        sel_scr[...] = valid_b

        @pl.when(need_rank)
        def _():
            pair = imp + pltpu.roll(imp, LANES - 1, 1)
            score = valid_f * (pair + bonus) + floor
            score_t = score.T[:2 * n_blocks]
            row_id = lax.broadcasted_iota(jnp.int32, score_t.shape, 0)
            rank_t = jnp.zeros(score_t.shape, jnp.int32)
            for k in range(n_blocks):
                other = score_t[2 * k:2 * k + 1, :]
                beats = (other > score_t) | ((other == score_t) & (row_id > 2 * k))
                rank_t = rank_t + beats.astype(jnp.int32)
            top_t = jnp.where(rank_t < top_k, 1.0, 0.0)
            top = jnp.concatenate([top_t, jnp.zeros((LANES - 2 * n_blocks, tq), F32)], axis=0).T
            sel_scr[...] = (top * valid_f).astype(BF16)

        sel = sel_scr[...]

        def chunk(c, carry):
            m, acc = carry
            k0 = pl.multiple_of(c * kchunk, kchunk)
            kb = ks_ref[0, pl.ds(k0, kchunk), hs]
            vb = with_ones(vs_ref[0, pl.ds(k0, kchunk), hs])
            pos = k0 + lax.broadcasted_iota(jnp.int32, (1, kchunk), 1)
            expand = (lax.broadcasted_iota(jnp.int32, (LANES, kchunk), 0)
                      == 2 * ((k0 + lax.broadcasted_iota(jnp.int32, (LANES, kchunk), 1)) // SLC_BLOCK))
            picked = _dot(sel, jnp.where(expand, 1.0, 0.0).astype(BF16))
            ok = (picked > 0.5) & (pos <= qpos_t)
            s = biased(_dot_nt(qs, kb), jnp.where(ok, (pos - qpos_t).astype(F32), NEG), kv)
            m_new = jnp.maximum(m, jnp.max(s, axis=-1, keepdims=True))
            p = jnp.exp(s - m_new)
            acc = jnp.exp(m - m_new) * acc + _dot(p.astype(BF16), vb)
            return m_new, acc

        init = (jnp.full((rows, 1), NEG, F32), jnp.zeros((rows, 2 * HEAD_DIM), F32))
        _, acc_s = lax.fori_loop(0, n_chunks, chunk, init)
        o_s = acc_s[:, :HEAD_DIM] / acc_s[:, HEAD_DIM:HEAD_DIM + 1]

        kb = kw_ref[0, pl.ds(wstart, wlen), hs]
        vb = with_ones(vw_ref[0, pl.ds(wstart, wlen), hs])
        s_w = biased(_dot_nt(qs, kb), w_dist, kv)
        p_w = jnp.exp(s_w - jnp.max(s_w, axis=-1, keepdims=True))
        pv = _dot(p_w.astype(BF16), vb)
        o_w = pv[:, :HEAD_DIM] / pv[:, HEAD_DIM:HEAD_DIM + 1]

        def gate_col(n):
            base = n * N_HEADS + kv * GQA
            return jnp.concatenate([gates[:, base + g:base + g + 1] for g in range(GQA)], axis=0)

        o = gate_col(0) * o_c + gate_col(1) * o_s + gate_col(2) * o_w
        for g in range(GQA):
            h = kv * GQA + g
            o_scr[:, h * HEAD_DIM:(h + 1) * HEAD_DIM] = o[g * tq:(g + 1) * tq]

    o_ref[0] = x_ref[0] + _dot(o_scr[...].astype(BF16), wo_ref[...])


def _nsa_prompt(q, gates, kc, vc, ks, vs, kw, vw, x, w_o, *, tq=ATTN_Q_TILE, kchunk=ATTN_KEY_CHUNK):
    B, L, D = x.shape
    assert L % kchunk == 0 and kchunk % tq == 0 and L >= WINDOW + tq and L % SLC_BLOCK == 0
    assert 2 * (L // SLC_BLOCK) <= LANES and L // CMP_BLOCK <= LANES and kc.shape[1] == LANES
    tile = lambda w: pl.BlockSpec((1, tq, w), lambda b, t: (b, t, 0))
    whole = lambda a: pl.BlockSpec((1,) + a.shape[1:], lambda b, t: (b, 0, 0))
    return pl.pallas_call(
        functools.partial(_nsa_prompt_kernel, tq=tq, kchunk=kchunk, seq=L),
        grid=(B, L // tq),
        in_specs=[tile(N_HEADS * HEAD_DIM), tile(LANES), whole(kc), whole(vc), whole(ks), whole(vs),
                  whole(kw), whole(vw), tile(D), pl.BlockSpec(w_o.shape, lambda b, t: (0, 0))],
        out_specs=tile(D),
        out_shape=jax.ShapeDtypeStruct((B, L, D), F32),
        scratch_shapes=[pltpu.VMEM((tq, N_HEADS * HEAD_DIM), F32), pltpu.VMEM((tq, LANES), BF16)],
        compiler_params=_cparams(("parallel", "arbitrary")),
        name="nsa_attn_prompt",
    )(q, gates, kc, vc, ks, vs, kw, vw, x, w_o)


CMP_GROUP = 32
MXU_DEPTH = 256


def _cmp_sample_kernel(pt_ref, cache_ref, w_ref, s_ref, o_ref, buf, sems, *, rows):
    b, g = pl.program_id(0), pl.program_id(1)
    n_groups = pl.num_programs(1)
    step = b * n_groups + g
    slot = step % 2

    def page_copy(bb, gg, sl, i):
        page = pt_ref[bb, gg * CMP_GROUP + i]
        return pltpu.make_async_copy(cache_ref.at[page, pl.ds(0, rows), :], buf.at[sl, i], sems.at[sl])

    def start_group(bb, gg, sl):
        for i in range(CMP_GROUP):
            page_copy(bb, gg, sl, i).start()

    @pl.when(step == 0)
    def _():
        start_group(b, g, slot)

    @pl.when(step + 1 < pl.num_programs(0) * n_groups)
    def _():
        nxt = step + 1
        start_group(nxt // n_groups, nxt % n_groups, 1 - slot)

    for i in range(CMP_GROUP):
        page_copy(b, g, slot, i).wait()

    w = w_ref[...]
    acc = jnp.zeros((rows, LANES), F32)
    for j in range(CMP_GROUP // 2):
        p = jnp.concatenate([buf[slot, 2 * j] * w, buf[slot, 2 * j + 1] * w], axis=1)
        hi, lo = _split(p)
        r = _dot(jnp.concatenate([hi, lo], axis=0), s_ref[j])
        acc = acc + (r[:rows] + r[rows:])
    o_ref[0] = acc


def _cmp_sample(page_table, cache_t, w_rows):
    B, n_pages = page_table.shape
    _, _, page = cache_t.shape
    rows = w_rows.shape[0]
    per_page = page // CMP_BLOCK
    assert n_pages % CMP_GROUP == 0 and CMP_GROUP * per_page == LANES and 2 * page == MXU_DEPTH
    k = jnp.arange(2 * page)
    token = (2 * jnp.arange(CMP_GROUP // 2)[:, None] + k[None, :] // page) * per_page + (k[None, :] % page) // CMP_BLOCK
    block_sum = (token[:, :, None] == jnp.arange(LANES)[None, None, :]).astype(BF16)
    return pl.pallas_call(
        functools.partial(_cmp_sample_kernel, rows=rows),
        grid_spec=pltpu.PrefetchScalarGridSpec(
            num_scalar_prefetch=1,
            grid=(B, n_pages // CMP_GROUP),
            in_specs=[pl.BlockSpec(memory_space=pl.ANY),
                      pl.BlockSpec(w_rows.shape, lambda b, g, pt: (0, 0)),
                      pl.BlockSpec(block_sum.shape, lambda b, g, pt: (0, 0, 0))],
            out_specs=pl.BlockSpec((1, rows, LANES), lambda b, g, pt: (b, 0, g)),
            scratch_shapes=[pltpu.VMEM((2, CMP_GROUP, rows, page), F32), pltpu.SemaphoreType.DMA((2,))],
        ),
        out_shape=jax.ShapeDtypeStruct((B, rows, n_pages * per_page), F32),
        compiler_params=_cparams(("arbitrary", "arbitrary")),
        name="cmp_sample",
    )(page_table, cache_t, w_rows, block_sum)


def _select_sample_kernel(q_ref, kcv_ref, oc_ref, idx_ref, *, past):
    scale = HEAD_DIM ** -0.5
    q = q_ref[0]
    nc = kcv_ref.shape[2]
    n_past_blocks = past // SLC_BLOCK
    per_block = SLC_BLOCK // CMP_BLOCK
    row = lax.broadcasted_iota(jnp.int32, (N_HEADS, 1), 0)
    slope = _head_slopes(N_HEADS, 1, 0)
    tok = lax.broadcasted_iota(jnp.int32, (1, nc), 1)
    c_mid = tok.astype(F32) * CMP_BLOCK + (CMP_BLOCK - 1) / 2
    c_ok = (tok + 1) * CMP_BLOCK - 1 <= past
    bias = slope * (float(past) - c_mid)

    s_c = jnp.zeros((N_HEADS, nc), F32)
    for kv in range(N_KV_HEADS):
        s_kv = _dot3(q, kcv_ref[0, kv * HEAD_DIM:(kv + 1) * HEAD_DIM, :])
        s_c = jnp.where(row // GQA == kv, s_kv, s_c)
    p_c = _masked_softmax(s_c * scale - bias, c_ok)
    o_c = jnp.zeros((N_HEADS, HEAD_DIM), F32)
    for kv in range(N_KV_HEADS):
        o_kv = _dot3(p_c, kcv_ref[0, KV_W + kv * HEAD_DIM:KV_W + (kv + 1) * HEAD_DIM, :], nt=True)
        o_c = jnp.where(row // GQA == kv, o_kv, o_c)
    oc_ref[0] = o_c

    imp = p_c
    shift = 1
    while shift < GQA:
        imp = imp + pltpu.roll(imp, shift, 0)
        shift *= 2
    assert per_block == 2
    pair = imp + pltpu.roll(imp, nc - 1, 1)
    jblk = tok // per_block
    real = tok % per_block == 0
    forced = (jblk == 0) | (jblk == n_past_blocks - 1)
    score = jnp.where(real, pair + jnp.where(forced, FORCE_BONUS, 0.0), -2.0)
    own = jnp.float32(FORCE_BONUS)
    rank = (own > score).astype(jnp.int32)
    for k in range(n_past_blocks):
        col = score[:, per_block * k:per_block * k + 1]
        beats = (col > score) | ((col == score) & (jblk > k))
        rank = rank + beats.astype(jnp.int32)
    own_rank = jnp.sum(jnp.where(real & (score >= own), 1, 0), axis=-1, keepdims=True)
    out_lane = lax.broadcasted_iota(jnp.int32, (N_HEADS, LANES), 1)
    idx = jnp.zeros((N_HEADS, LANES), jnp.int32)
    for r in range(SLC_TOPK):
        hit = jnp.sum(jnp.where(real & (rank == r), jblk, 0), axis=-1, keepdims=True)
        hit = hit + jnp.where(own_rank == r, n_past_blocks, 0)
        idx = jnp.where(out_lane == r, hit, idx)
    idx_ref[0] = idx


def _select_sample(q16, kcv, past):
    B = q16.shape[0]
    nc = kcv.shape[2]
    assert past % SLC_BLOCK == 0 and nc == past // CMP_BLOCK and nc % LANES == 0
    assert past // SLC_BLOCK + 1 > SLC_TOPK
    return pl.pallas_call(
        functools.partial(_select_sample_kernel, past=past),
        grid=(B,),
        in_specs=[pl.BlockSpec((1,) + q16.shape[1:], lambda b: (b, 0, 0)),
                  pl.BlockSpec((1,) + kcv.shape[1:], lambda b: (b, 0, 0))],
        out_specs=[pl.BlockSpec((1, N_HEADS, HEAD_DIM), lambda b: (b, 0, 0)),
                   pl.BlockSpec((1, N_HEADS, LANES), lambda b: (b, 0, 0))],
        out_shape=[jax.ShapeDtypeStruct((B, N_HEADS, HEAD_DIM), F32),
                   jax.ShapeDtypeStruct((B, N_HEADS, LANES), jnp.int32)],
        compiler_params=_cparams(("parallel",)),
        name="select_sample",
    )(q16, kcv)


def _attend_sample_kernel(pt_ref, idx_ref, q_ref, oc_ref, gt_ref, new_ref, win_ref, cache_ref,
                          o_ref, kbuf, vbuf, sems, *, past):
    b = pl.program_id(0)
    scale = HEAD_DIM ** -0.5
    n_past_blocks = past // SLC_BLOCK
    page_rows = cache_ref.shape[2]
    blocks_per_page = page_rows // SLC_BLOCK
    nkeys = SLC_TOPK * page_rows

    def block_copies(kv, n):
        blk = idx_ref[(b * N_KV_HEADS + kv) * SLC_TOPK + n]
        in_past = blk < n_past_blocks
        page = pt_ref[b, jnp.minimum(blk, n_past_blocks - 1) // blocks_per_page]
        dst = pl.ds(n * page_rows, page_rows)
        ck = pltpu.make_async_copy(cache_ref.at[page, pl.ds((2 * N_KV_HEADS + kv) * HEAD_DIM, HEAD_DIM), :],
                                   kbuf.at[kv, :, dst], sems.at[0, kv, n])
        cv = pltpu.make_async_copy(cache_ref.at[page, pl.ds((3 * N_KV_HEADS + kv) * HEAD_DIM, HEAD_DIM), :],
                                   vbuf.at[kv, :, dst], sems.at[1, kv, n])
        return blk, in_past, ck, cv

    for kv in range(N_KV_HEADS):
        for n in range(SLC_TOPK):
            _, in_past, ck, cv = block_copies(kv, n)

            @pl.when(in_past)
            def _():
                ck.start()
                cv.start()

            @pl.when(jnp.logical_not(in_past))
            def _():
                kbuf[kv, :, n * page_rows:(n + 1) * page_rows] = jnp.zeros((HEAD_DIM, page_rows), F32)
                vbuf[kv, :, n * page_rows:(n + 1) * page_rows] = jnp.zeros((HEAD_DIM, page_rows), F32)

    q = q_ref[0]
    row = lax.broadcasted_iota(jnp.int32, (N_HEADS, 1), 0)
    slope = _head_slopes(N_HEADS, 1, 0)
    key_lane = lax.broadcasted_iota(jnp.int32, (1, nkeys), 1)

    wb = win_ref.shape[2]
    w_lane = lax.broadcasted_iota(jnp.int32, (1, wb), 1)
    wpos = past - wb + w_lane
    w_ok = (past - wpos < WINDOW) & (wpos >= 0)
    w_bias = slope * (past - wpos).astype(F32)
    s_w = jnp.zeros((N_HEADS, wb), F32)
    s_n = jnp.zeros((N_HEADS, 1), F32)
    for kv in range(N_KV_HEADS):
        mine = row // GQA == kv
        s_kv = _dot3(q, win_ref[0, kv * HEAD_DIM:(kv + 1) * HEAD_DIM, :])
        s_w = jnp.where(mine, s_kv, s_w)
        s_n = jnp.where(mine, jnp.sum(q * new_ref[0, 4, kv:kv + 1, :], axis=-1, keepdims=True), s_n)
    s_w = jnp.where(w_ok, s_w * scale - w_bias, NEG)
    s_n = s_n * scale
    m_w = jnp.maximum(jnp.max(s_w, axis=-1, keepdims=True), s_n)
    p_w = jnp.where(w_ok, jnp.exp(s_w - m_w), 0.0)
    p_n = jnp.exp(s_n - m_w)
    l_w = jnp.maximum(jnp.sum(p_w, axis=-1, keepdims=True) + p_n, 1e-30)
    p_w = p_w / l_w
    p_n = p_n / l_w
    o_w = jnp.zeros((N_HEADS, HEAD_DIM), F32)
    for kv in range(N_KV_HEADS):
        o_kv = (_dot3(p_w, win_ref[0, KV_W + kv * HEAD_DIM:KV_W + (kv + 1) * HEAD_DIM, :], nt=True)
                + p_n * new_ref[0, 5, kv:kv + 1, :])
        o_w = jnp.where(row // GQA == kv, o_kv, o_w)

    for kv in range(N_KV_HEADS):
        for n in range(SLC_TOPK):
            _, in_past, ck, cv = block_copies(kv, n)

            @pl.when(in_past)
            def _():
                ck.wait()
                cv.wait()

    s_s = jnp.zeros((N_HEADS, nkeys), F32)
    pos = jnp.zeros((N_HEADS, nkeys), jnp.int32)
    live = jnp.zeros((N_HEADS, nkeys), jnp.int32)
    own = jnp.zeros((N_HEADS, 1), jnp.int32)
    s_n = jnp.zeros((N_HEADS, 1), F32)
    in_slab = key_lane % page_rows
    for kv in range(N_KV_HEADS):
        mine = row // GQA == kv
        s_kv = _dot3(q, kbuf[kv])
        s_s = jnp.where(mine, s_kv, s_s)
        s_n = jnp.where(mine, jnp.sum(q * new_ref[0, 2, kv:kv + 1, :], axis=-1, keepdims=True), s_n)
        base_kv = jnp.zeros((1, nkeys), jnp.int32)
        half_kv = jnp.zeros((1, nkeys), jnp.int32)
        own_kv = jnp.int32(0)
        for n in range(SLC_TOPK):
            blk = idx_ref[(b * N_KV_HEADS + kv) * SLC_TOPK + n]
            here = key_lane // page_rows == n
            base_kv = jnp.where(here, (blk // blocks_per_page) * page_rows, base_kv)
            half_kv = jnp.where(here, blk % blocks_per_page, half_kv)
            own_kv = own_kv + (blk >= n_past_blocks).astype(jnp.int32)
        pos = jnp.where(mine, base_kv + in_slab, pos)
        live = jnp.where(mine, (in_slab // SLC_BLOCK == half_kv).astype(jnp.int32), live)
        own = jnp.where(mine, own_kv, own)
    s_ok = (live > 0) & (pos < past)
    n_ok = own > 0
    s_s = jnp.where(s_ok, s_s * scale - slope * (past - pos).astype(F32), NEG)
    s_n = jnp.where(n_ok, s_n * scale, NEG)
    m_s = jnp.maximum(jnp.max(s_s, axis=-1, keepdims=True), s_n)
    p_s = jnp.where(s_ok, jnp.exp(s_s - m_s), 0.0)
    p_n = jnp.where(n_ok, jnp.exp(s_n - m_s), 0.0)
    l_s = jnp.maximum(jnp.sum(p_s, axis=-1, keepdims=True) + p_n, 1e-30)
    p_s = p_s / l_s
    p_n = p_n / l_s
    o_s = jnp.zeros((N_HEADS, HEAD_DIM), F32)
    for kv in range(N_KV_HEADS):
        o_kv = _dot3(p_s, vbuf[kv], nt=True) + p_n * new_ref[0, 3, kv:kv + 1, :]
        o_s = jnp.where(row // GQA == kv, o_kv, o_s)

    gt = jax.nn.sigmoid(gt_ref[0])
    o_ref[0] = gt[:, 0:1] * oc_ref[0] + gt[:, 1:2] * o_s + gt[:, 2:3] * o_w


def _attend_sample(page_table, idx, q16, o_c, gates, kv_new, win_t, cache_t, past):
    B = q16.shape[0]
    page_rows = cache_t.shape[2]
    assert win_t.shape[2] <= WINDOW and page_rows % SLC_BLOCK == 0
    blk3 = lambda a: pl.BlockSpec((1,) + a.shape[1:], lambda b, pt, ix: (b,) + (0,) * (a.ndim - 1))
    return pl.pallas_call(
        functools.partial(_attend_sample_kernel, past=past),
        grid_spec=pltpu.PrefetchScalarGridSpec(
            num_scalar_prefetch=2,
            grid=(B,),
            in_specs=[blk3(q16), blk3(o_c), blk3(gates), blk3(kv_new), blk3(win_t),
                      pl.BlockSpec(memory_space=pl.ANY)],
            out_specs=pl.BlockSpec((1, N_HEADS, HEAD_DIM), lambda b, pt, ix: (b, 0, 0)),
            scratch_shapes=[pltpu.VMEM((N_KV_HEADS, HEAD_DIM, SLC_TOPK * page_rows), F32),
                            pltpu.VMEM((N_KV_HEADS, HEAD_DIM, SLC_TOPK * page_rows), F32),
                            pltpu.SemaphoreType.DMA((2, N_KV_HEADS, SLC_TOPK))],
        ),
        out_shape=jax.ShapeDtypeStruct((B, N_HEADS, HEAD_DIM), F32),
        compiler_params=_cparams(("arbitrary",)),
        name="attend_sample",
    )(page_table, idx, q16, o_c, gates, kv_new, win_t, cache_t)


def _linear_res_kernel(x_ref, a_ref, w_ref, o_ref):
    o_ref[...] = x_ref[...] + _dot3(a_ref[...], w_ref[...])


def _linear_res(x, a, w):
    return pl.pallas_call(
        _linear_res_kernel,
        out_shape=jax.ShapeDtypeStruct(x.shape, F32),
        compiler_params=pltpu.CompilerParams(vmem_limit_bytes=VMEM_LIMIT),
        name="out_proj_sample",
    )(x, a, w)


def _expand_cmp_weights(w_cmp, rows):
    w = jnp.repeat(w_cmp, HEAD_DIM, axis=1)
    return jnp.tile(w, (rows // CMP_BLOCK, 1))


def _row(v):
    return v.reshape(1, -1)


def _prompt_mixers(x, norm_mix, w_pool, pool_scale, w_in, w_cmp_k, w_cmp_v, w_out, norm_ffn0, wfg, wfu, wfd):
    B, L, D = x.shape
    q_w = N_HEADS * HEAD_DIM
    x, h_last = _pool_prompt(x, _row(norm_mix[0]), w_pool.astype(BF16), _row(pool_scale))
    new_pool = h_last[None, :, POOL_HALO - (max(POOL_WINDOWS) - 1):, :]
    x = _ffn(x.reshape(B * L, D), _row(norm_ffn0), wfg.astype(BF16), wfu.astype(BF16), wfd.astype(BF16),
             tm=FFN_ROW_TILE)

    tm = PROJ_ROW_TILE
    wq = w_in[:, :q_w].astype(BF16)
    wkv = w_in[:, q_w:q_w + 6 * KV_W].astype(BF16)
    wgt = jnp.pad(w_in[:, q_w + 6 * KV_W:], ((0, 0), (0, LANES - N_BRANCH * N_HEADS))).astype(BF16)
    q, kv_t, ks, vs, kw, vw, gates, kc, vc = _nsa_proj(
        x, _row(norm_mix[1]), wq, wkv, wgt, _expand_cmp_weights(w_cmp_k, tm), _expand_cmp_weights(w_cmp_v, tm),
        tm=tm, seq=L)
    kv_t = kv_t.reshape(B, 6, N_KV_HEADS, HEAD_DIM, L)
    new_kv = kv_t[:, :4].transpose(0, 4, 1, 2, 3)[None]
    new_win = kv_t[:, 4:, :, :, L - min(WINDOW, L):].transpose(0, 4, 1, 2, 3)[None]
    nc = L // CMP_BLOCK
    pad_c = lambda a: jnp.pad(a.reshape(B, nc, KV_W), ((0, 0), (0, LANES - nc), (0, 0))).astype(BF16)
    per_seq = lambda a: a.reshape(B, L, a.shape[-1])
    x = _nsa_prompt(per_seq(q), per_seq(gates), pad_c(kc), pad_c(vc), per_seq(ks), per_seq(vs),
                    per_seq(kw), per_seq(vw), per_seq(x), w_out.astype(BF16))
    return x.reshape(B * L, D), new_pool, new_kv, new_win


def _sample_mixers(x, state_pool, cache, state_win, page_table, norm_mix, w_pool, pool_scale, w_in, w_cmp_k,
                   w_cmp_v, w_out, norm_ffn0, wfg, wfu, wfd):
    SB, D = x.shape
    n_phys, page = cache.shape[:2]
    past = page_table.shape[1] * page
    q_w = N_HEADS * HEAD_DIM
    x, h = _pool_sample(x, state_pool.transpose(1, 0, 2), _row(norm_mix[0]), w_pool, _row(pool_scale))
    new_pool = jnp.concatenate([state_pool[:, 1:], h[:, None]], axis=1)[None]
    x = _ffn(x, _row(norm_ffn0), wfg, wfu, wfd, tm=SB)

    cols = w_in.shape[1]
    z = _nsa_proj_sample(x, _row(norm_mix[1]), jnp.pad(w_in, ((0, 0), (0, -cols % LANES))))
    kv_new = z[:, q_w:q_w + 6 * KV_W].reshape(SB, 6, N_KV_HEADS, HEAD_DIM)
    new_kv = kv_new[None, :, None, :4]
    keep = min(WINDOW, state_win.shape[1] + 1)
    new_win = jnp.concatenate([state_win, kv_new[:, None, 4:]], axis=1)[None, :, -keep:]

    cache_t = cache.transpose(0, 2, 3, 4, 1).reshape(n_phys, 4 * KV_W, page)
    win_t = state_win.transpose(0, 2, 3, 4, 1).reshape(SB, 2 * KV_W, state_win.shape[1])
    w_rows = jnp.concatenate([_expand_cmp_weights(w_cmp_k, page).T, _expand_cmp_weights(w_cmp_v, page).T], axis=0)
    kcv = _cmp_sample(page_table, cache_t, w_rows)
    q16 = z[:, :q_w].reshape(SB, N_HEADS, HEAD_DIM)
    o_c, idx = _select_sample(q16, kcv, past)
    idx = idx[:, GQA - 1::GQA, :SLC_TOPK].reshape(-1)
    gate_logits = z[:, q_w + 6 * KV_W:cols].reshape(SB, N_BRANCH, N_HEADS).transpose(0, 2, 1)
    gate_logits = jnp.pad(gate_logits, ((0, 0), (0, 0), (0, LANES - N_BRANCH)))
    o = _attend_sample(page_table, idx, q16, o_c, gate_logits, kv_new, win_t, cache_t, past)
    return _linear_res(x, o.reshape(SB, q_w), w_out), new_pool, new_kv, new_win


def kernel(x_prompt, x_sample, state_pool, cache_kv, state_win, page_table, norm_mix, w_pool, pool_scale,
           w_nsa_in, w_cmp_k, w_cmp_v, w_nsa_out, norm_ffn, w_ffn_gate, w_ffn_up, w_ffn_down, w_router,
           w_moe_gate, w_moe_up, w_moe_down, norm_final):
    assert x_sample.shape[1] == 1 and norm_mix.shape[0] == 2 and x_prompt.shape[-1] == N_HEADS * HEAD_DIM
    w_rt = jnp.pad(w_router[0], ((0, 0), (0, LANES - w_router.shape[-1])))
    moe_w = (w_moe_gate[0].astype(BF16), w_moe_up[0].astype(BF16), w_moe_down[0].astype(BF16))

    xp, new_pool_prompt, new_kv_prompt, new_win_prompt = _prompt_mixers(
        x_prompt, norm_mix, w_pool[0], pool_scale[0], w_nsa_in[0], w_cmp_k[0], w_cmp_v[0], w_nsa_out[0],
        norm_ffn[0], w_ffn_gate, w_ffn_up, w_ffn_down)
    xs, new_pool_sample, new_kv_sample, new_win_sample = _sample_mixers(
        x_sample[:, 0], state_pool[0], cache_kv[0], state_win[0], page_table, norm_mix, w_pool[0], pool_scale[0],
        w_nsa_in[0], w_cmp_k[0], w_cmp_v[0], w_nsa_out[0], norm_ffn[0], w_ffn_gate, w_ffn_up, w_ffn_down)

    yp = _moe_prompt(xp, _row(norm_ffn[1]), w_rt, *moe_w, _row(norm_final))
    ys = _ffn(xs, _row(norm_ffn[1]), *moe_w, w_rt, _row(norm_final), tm=xs.shape[0])
    return (yp.reshape(x_prompt.shape), ys.reshape(x_sample.shape), new_pool_prompt, new_pool_sample,
            new_kv_prompt, new_kv_sample, new_win_prompt, new_win_sample)
```

```python
import functools

import jax
import jax.numpy as jnp
from jax import lax
from jax.experimental import pallas as pl
from jax.experimental.pallas import tpu as pltpu

F32 = jnp.float32
BF16 = jnp.bfloat16

EPS = 1e-6
NEG = -1e30
POOL_WINDOWS = (2, 4, 8, 16)
POOL_HALO = 16
N_HEADS = 16
N_KV_HEADS = 4
GQA = N_HEADS // N_KV_HEADS
HEAD_DIM = 64
KV_W = N_KV_HEADS * HEAD_DIM
CMP_BLOCK = 32
SLC_BLOCK = 64
SLC_TOPK = 16
WINDOW = 512
N_BRANCH = 3
FORCE_BONUS = 1000.0
TOP_K = 2
LANES = 128
VMEM_LIMIT = 56 * 1024 * 1024

POOL_ROW_TILE = 512
FFN_ROW_TILE = 1024
FFN_COL_TILE = 512
PROJ_ROW_TILE = 512
ATTN_Q_TILE = 256
ATTN_KEY_CHUNK = 512
ROUTE_ROW_TILE = 512
EXPERT_ROW_TILE = 448
COMBINE_ROW_TILE = 512
SAMPLE_FFN_COL_TILE = 1792
SAMPLE_PROJ_COL_TILE = 384


def _cparams(sem):
    return pltpu.CompilerParams(dimension_semantics=sem, vmem_limit_bytes=VMEM_LIMIT)


def _rmsnorm(x, g):
    ms = jnp.mean(x * x, axis=-1, keepdims=True)
    return (x * lax.rsqrt(ms + EPS)) * g


def _dot(a, b):
    return jnp.dot(a, b, preferred_element_type=F32)


def _dot_nt(a, b):
    return lax.dot_general(a, b, (((1,), (1,)), ((), ())), preferred_element_type=F32)


def _split(a):
    hi = a.astype(BF16)
    return hi, (a - hi.astype(F32)).astype(BF16)


def _dot3(a, b, nt=False):
    d = _dot_nt if nt else _dot
    ah, al = _split(a)
    bh, bl = _split(b)
    return d(ah, bh) + (d(ah, bl) + d(al, bh))


def _masked_softmax(s, mask):
    s = jnp.where(mask, s, NEG)
    m = jnp.max(s, axis=-1, keepdims=True)
    p = jnp.where(mask, jnp.exp(s - m), 0.0)
    return p / jnp.maximum(jnp.sum(p, axis=-1, keepdims=True), 1e-30)


def _alibi_slope(h):
    return 2.0 ** (-8.0 * (h + 1) / N_HEADS)


def _head_slopes(rows, rows_per_head, first_head):
    h = lax.broadcasted_iota(jnp.int32, (rows, 1), 0) // rows_per_head
    slopes = jnp.zeros((rows, 1), F32)
    for j in range(rows // rows_per_head):
        slopes = jnp.where(h == j, _alibi_slope(first_head + j), slopes)
    return slopes


def _pool_prompt_kernel(x_ref, halo_ref, g_ref, w_ref, sc_ref, o_ref, hl_ref, full_ref, *, tile):
    t = pl.program_id(1)
    g = g_ref[...]
    x = x_ref[0]
    h = _rmsnorm(x, g)
    hh = _rmsnorm(halo_ref[0], g)
    full_ref[0:POOL_HALO, :] = jnp.where(t > 0, hh, 0.0)
    full_ref[POOL_HALO:POOL_HALO + tile, :] = h
    row = t * tile + lax.broadcasted_iota(jnp.int32, (tile, 1), 0)
    group = x.shape[-1] // len(POOL_WINDOWS)
    parts = []
    for gi, w in enumerate(POOL_WINDOWS):
        cs = slice(gi * group, (gi + 1) * group)
        hg = h[:, cs]
        acc = hg
        for k in range(1, w):
            acc = acc + full_ref[POOL_HALO - k:POOL_HALO - k + tile, cs]
        cnt = jnp.minimum(row + 1, w).astype(F32)
        pooled = acc / cnt - hg
        parts.append(_dot(pooled.astype(BF16), w_ref[gi]))
    o_ref[0] = x + jnp.concatenate(parts, axis=-1) * sc_ref[...]

    @pl.when(t == pl.num_programs(1) - 1)
    def _():
        hl_ref[0] = h[tile - POOL_HALO:, :]


def _pool_prompt(x, g, w_pool, scale, tile=POOL_ROW_TILE):
    B, L, D = x.shape
    assert L % tile == 0 and tile % POOL_HALO == 0
    hb = tile // POOL_HALO
    return pl.pallas_call(
        functools.partial(_pool_prompt_kernel, tile=tile),
        grid=(B, L // tile),
        in_specs=[
            pl.BlockSpec((1, tile, D), lambda b, t: (b, t, 0)),
            pl.BlockSpec((1, POOL_HALO, D), lambda b, t: (b, jnp.maximum(t * hb - 1, 0), 0)),
            pl.BlockSpec((1, D), lambda b, t: (0, 0)),
            pl.BlockSpec(w_pool.shape, lambda b, t: (0, 0, 0)),
            pl.BlockSpec((1, D), lambda b, t: (0, 0)),
        ],
        out_specs=[
            pl.BlockSpec((1, tile, D), lambda b, t: (b, t, 0)),
            pl.BlockSpec((1, POOL_HALO, D), lambda b, t: (b, 0, 0)),
        ],
        out_shape=[jax.ShapeDtypeStruct((B, L, D), F32), jax.ShapeDtypeStruct((B, POOL_HALO, D), F32)],
        scratch_shapes=[pltpu.VMEM((tile + POOL_HALO, D), F32)],
        compiler_params=_cparams(("parallel", "arbitrary")),
        name="pool_prompt",
    )(x, x, g, w_pool, scale)


def _pool_sample_kernel(x_ref, st_ref, g_ref, w_ref, sc_ref, o_ref, h_ref):
    x = x_ref[...]
    h = _rmsnorm(x, g_ref[...])
    h_ref[...] = h
    P = st_ref.shape[0]
    group = x.shape[-1] // len(POOL_WINDOWS)
    parts = []
    for gi, w in enumerate(POOL_WINDOWS):
        cs = slice(gi * group, (gi + 1) * group)
        hg = h[:, cs]
        acc = hg
        for k in range(1, w):
            acc = acc + st_ref[P - k][:, cs]
        pooled = acc / float(w) - hg
        parts.append(_dot3(pooled, w_ref[gi]))
    o_ref[...] = x + jnp.concatenate(parts, axis=-1) * sc_ref[...]


def _pool_sample(x, state_t, g, w_pool, scale):
    B, D = x.shape
    assert state_t.shape[0] >= max(POOL_WINDOWS) - 1
    return pl.pallas_call(
        _pool_sample_kernel,
        out_shape=[jax.ShapeDtypeStruct((B, D), F32), jax.ShapeDtypeStruct((B, D), F32)],
        compiler_params=pltpu.CompilerParams(vmem_limit_bytes=VMEM_LIMIT),
        name="pool_sample",
    )(x, state_t, g, w_pool, scale)


def _ffn_kernel(*refs, n_experts, final_norm, precise):
    moe = n_experts > 1
    it = iter(refs)
    x_ref, g_ref = next(it), next(it)
    wr_ref = next(it) if moe else None
    wg_ref, wu_ref, wd_ref = next(it), next(it), next(it)
    gf_ref = next(it) if final_norm else None
    o_ref, h_scr, acc_scr = next(it), next(it), next(it)
    eacc_scr, gate_scr = (next(it), next(it)) if moe else (None, None)

    e, f = pl.program_id(1), pl.program_id(2)
    last_f = f == pl.num_programs(2) - 1

    @pl.when((e == 0) & (f == 0))
    def _():
        h = _rmsnorm(x_ref[...], g_ref[...])
        h_scr[...] = h.astype(h_scr.dtype)
        acc_scr[...] = jnp.zeros_like(acc_scr)
        if moe:
            logits = _dot3(h, wr_ref[...])
            lane, i1, i2, w1, w2 = _top2_gates(logits, n_experts)
            gate_scr[...] = jnp.where(lane == i1, w1, 0.0) + jnp.where(lane == i2, w2, 0.0)

    mm = _dot3 if precise else _dot
    hb = h_scr[...]
    a = mm(hb, wg_ref[0])
    u = mm(hb, wu_ref[0])
    act = (a * jax.nn.sigmoid(a)) * u
    y = mm(act.astype(hb.dtype), wd_ref[0])

    if moe:
        @pl.when(f == 0)
        def _():
            eacc_scr[...] = y

        @pl.when(f > 0)
        def _():
            eacc_scr[...] += y

        @pl.when(last_f)
        def _():
            gate = gate_scr[...]
            lane = lax.broadcasted_iota(jnp.int32, gate.shape, 1)
            ge = jnp.sum(jnp.where(lane == e, gate, 0.0), axis=-1, keepdims=True)
            acc_scr[...] += ge * eacc_scr[...]
    else:
        acc_scr[...] += y

    @pl.when((e == pl.num_programs(1) - 1) & last_f)
    def _():
        out = x_ref[...] + acc_scr[...]
        if final_norm:
            out = _rmsnorm(out, gf_ref[...])
        o_ref[...] = out


def _ffn(x, g, wg, wu, wd, w_router=None, g_final=None, *, tm, tf=FFN_COL_TILE):
    precise = wg.dtype == F32
    N, D = x.shape
    E, _, F = wg.shape
    assert N % tm == 0 and F % tf == 0
    moe = w_router is not None
    assert moe == (E > 1)
    final_norm = g_final is not None
    const2 = lambda i, e, f: (0, 0)
    args, in_specs = [x, g], [pl.BlockSpec((tm, D), lambda i, e, f: (i, 0)), pl.BlockSpec((1, D), const2)]
    if moe:
        args.append(w_router)
        in_specs.append(pl.BlockSpec(w_router.shape, const2))
    args += [wg, wu, wd]
    in_specs += [
        pl.BlockSpec((1, D, tf), lambda i, e, f: (e, 0, f)),
        pl.BlockSpec((1, D, tf), lambda i, e, f: (e, 0, f)),
        pl.BlockSpec((1, tf, D), lambda i, e, f: (e, f, 0)),
    ]
    if final_norm:
        args.append(g_final)
        in_specs.append(pl.BlockSpec((1, D), const2))
    scratch = [pltpu.VMEM((tm, D), F32 if precise else BF16), pltpu.VMEM((tm, D), F32)]
    if moe:
        scratch += [pltpu.VMEM((tm, D), F32), pltpu.VMEM((tm, LANES), F32)]
    return pl.pallas_call(
        functools.partial(_ffn_kernel, n_experts=E, final_norm=final_norm, precise=precise),
        grid=(N // tm, E, F // tf),
        in_specs=in_specs,
        out_specs=pl.BlockSpec((tm, D), lambda i, e, f: (i, 0)),
        out_shape=jax.ShapeDtypeStruct((N, D), F32),
        scratch_shapes=scratch,
        compiler_params=_cparams(("parallel", "arbitrary", "arbitrary")),
        name="moe_ffn" if moe else "dense_ffn",
    )(*args)


ROUTE_I1, ROUTE_I2, ROUTE_W1, ROUTE_W2, ROUTE_R1, ROUTE_R2 = range(6)


def _top2_gates(logits, n_experts):
    lane = lax.broadcasted_iota(jnp.int32, logits.shape, 1)
    lg = jnp.where(lane < n_experts, logits, -jnp.inf)
    m1 = jnp.max(lg, axis=-1, keepdims=True)
    i1 = jnp.min(jnp.where(lg == m1, lane, LANES), axis=-1, keepdims=True)
    lg2 = jnp.where(lane == i1, -jnp.inf, lg)
    m2 = jnp.max(lg2, axis=-1, keepdims=True)
    i2 = jnp.min(jnp.where(lg2 == m2, lane, LANES), axis=-1, keepdims=True)
    e2 = jnp.exp(m2 - m1)
    den = 1.0 + e2
    return lane, i1, i2, 1.0 / den, e2 / den


def _moe_route_kernel(x_ref, g_ref, wr_ref, tri_ref, route_ref, cnt_ref, carry, *, n_experts):
    @pl.when(pl.program_id(0) == 0)
    def _():
        carry[...] = jnp.zeros_like(carry)

    h = _rmsnorm(x_ref[...], g_ref[...])
    logits = _dot3(h, wr_ref[...])
    lane, i1, i2, w1, w2 = _top2_gates(logits, n_experts)
    member = jnp.where((lane == i1) | (lane == i2), 1.0, 0.0)
    before = _dot(tri_ref[...], member.astype(BF16)) + carry[0:1, :]
    r1 = jnp.sum(jnp.where(lane == i1, before, 0.0), axis=-1, keepdims=True)
    r2 = jnp.sum(jnp.where(lane == i2, before, 0.0), axis=-1, keepdims=True)
    rec = jnp.zeros(logits.shape, F32)
    for col, val in ((ROUTE_I1, i1.astype(F32)), (ROUTE_I2, i2.astype(F32)), (ROUTE_W1, w1), (ROUTE_W2, w2),
                     (ROUTE_R1, r1), (ROUTE_R2, r2)):
        rec = jnp.where(lane == col, val, rec)
    route_ref[...] = rec
    carry[...] = carry[...] + jnp.sum(member, axis=0, keepdims=True)
    cnt_ref[...] = carry[...]


def _moe_route(x, g, w_router, n_experts, tm=ROUTE_ROW_TILE):
    N, D = x.shape
    assert N % tm == 0
    tri = (jnp.arange(tm)[:, None] > jnp.arange(tm)[None, :]).astype(BF16)
    return pl.pallas_call(
        functools.partial(_moe_route_kernel, n_experts=n_experts),
        grid=(N // tm,),
        in_specs=[pl.BlockSpec((tm, D), lambda i: (i, 0)), pl.BlockSpec((1, D), lambda i: (0, 0)),
                  pl.BlockSpec(w_router.shape, lambda i: (0, 0)), pl.BlockSpec((tm, tm), lambda i: (0, 0))],
        out_specs=[pl.BlockSpec((tm, LANES), lambda i: (i, 0)), pl.BlockSpec((8, LANES), lambda i: (0, 0))],
        out_shape=[jax.ShapeDtypeStruct((N, LANES), F32), jax.ShapeDtypeStruct((8, LANES), F32)],
        scratch_shapes=[pltpu.VMEM((8, LANES), F32)],
        compiler_params=_cparams(("arbitrary",)),
        name="moe_route",
    )(x, g, w_router, tri)


GATHER_UNROLL = 8
GATHER_PRIORITY = 1


def _row_gather(idx_of, src_hbm, dst_at, sem, rows, priority):
    assert rows % GATHER_UNROLL == 0

    def copy(r):
        return pltpu.make_async_copy(src_hbm.at[pl.ds(idx_of(r), 1), :], dst_at(r), sem)

    def start():
        def group(i, c):
            for j in range(GATHER_UNROLL):
                copy(i * GATHER_UNROLL + j).start(priority=priority(j))
            return c
        lax.fori_loop(0, rows // GATHER_UNROLL, group, 0)

    def wait():
        lax.fori_loop(0, rows, lambda r, c: (copy(r).wait(), c)[1], 0, unroll=GATHER_UNROLL)

    return start, wait, copy


def _moe_experts_kernel(te_ref, nu_ref, tos_ref, x_hbm, g_ref, wg_ref, wu_ref, wd_ref, o_ref,
                        xbuf, h_scr, acc_scr, sems, *, tile, nf):
    i, f = pl.program_id(0), pl.program_id(1)
    used = i < nu_ref[0]
    slot = i % 2
    per_step = tile // nf

    def gather(t, sl):
        return _row_gather(lambda r: tos_ref[t * tile + r], x_hbm,
                           lambda r: xbuf.at[sl, pl.ds(r, 1), :], sems.at[sl], tile, lambda j: GATHER_PRIORITY)

    def prefetch_share():
        row_copy = gather(i + 1, 1 - slot)[2]
        for j in range(per_step):
            row_copy(f * per_step + j).start(priority=GATHER_PRIORITY)

    @pl.when((f == 0) & (i == 0))
    def _():
        gather(0, 0)[0]()

    @pl.when(f == 0)
    def _():
        gather(i, slot)[1]()

    @pl.when((f == 0) & used)
    def _():
        h_scr[...] = _rmsnorm(xbuf[slot], g_ref[...]).astype(BF16)
        acc_scr[...] = jnp.zeros_like(acc_scr)

    @pl.when(used)
    def _():
        prefetch_share()
        hb = h_scr[...]
        a = _dot(hb, wg_ref[0])
        u = _dot(hb, wu_ref[0])
        act = (a * jax.nn.sigmoid(a)) * u
        acc_scr[...] += _dot(act.astype(BF16), wd_ref[0])

    @pl.when(jnp.logical_not(used))
    def _():
        prefetch_share()

    @pl.when(f == nf - 1)
    def _():
        o_ref[...] = jnp.where(used, acc_scr[...], 0.0)

    @pl.when((f == nf - 1) & (i == pl.num_programs(0) - 1))
    def _():
        gather(i + 1, 1 - slot)[1]()


def _moe_experts(x, g, wg, wu, wd, tile_expert, n_used, token_of_slot, *, tile, tf=FFN_COL_TILE):
    N, D = x.shape
    E, _, F = wg.shape
    n_slots = token_of_slot.shape[0] - tile
    assert n_slots % tile == 0 and F % tf == 0
    n_tiles, nf = n_slots // tile, F // tf
    assert tile % nf == 0
    fidx = lambda i, f, nu: jnp.where(i < nu[0], f, nf - 1)
    return pl.pallas_call(
        functools.partial(_moe_experts_kernel, tile=tile, nf=nf),
        grid_spec=pltpu.PrefetchScalarGridSpec(
            num_scalar_prefetch=3,
            grid=(n_tiles, nf),
            in_specs=[pl.BlockSpec(memory_space=pl.ANY),
                      pl.BlockSpec((1, D), lambda i, f, te, nu, tos: (0, 0)),
                      pl.BlockSpec((1, D, tf), lambda i, f, te, nu, tos: (te[i], 0, fidx(i, f, nu))),
                      pl.BlockSpec((1, D, tf), lambda i, f, te, nu, tos: (te[i], 0, fidx(i, f, nu))),
                      pl.BlockSpec((1, tf, D), lambda i, f, te, nu, tos: (te[i], fidx(i, f, nu), 0))],
            out_specs=pl.BlockSpec((tile, D), lambda i, f, te, nu, tos: (i, 0)),
            scratch_shapes=[pltpu.VMEM((2, tile, D), F32), pltpu.VMEM((tile, D), BF16),
                            pltpu.VMEM((tile, D), F32), pltpu.SemaphoreType.DMA((2,))],
        ),
        out_shape=jax.ShapeDtypeStruct((n_slots, D), F32),
        compiler_params=_cparams(("arbitrary", "arbitrary")),
        name="moe_experts",
    )(tile_expert, n_used, token_of_slot, x, g, wg, wu, wd)


def _moe_combine_kernel(slot_ref, x_ref, route_ref, gf_ref, ys_hbm, o_ref, ybuf, sems, *, tile):
    i = pl.program_id(0)
    slot = i % 2

    def gather(t, sl):
        return _row_gather(lambda j: slot_ref[t * 2 * tile + j], ys_hbm,
                           lambda j: ybuf.at[sl, pl.ds(j, 1), :], sems.at[sl], 2 * tile, lambda j: j % 2)

    @pl.when(i == 0)
    def _():
        gather(0, 0)[0]()

    gather(i, slot)[1]()

    @pl.when(i + 1 < pl.num_programs(0))
    def _():
        gather(i + 1, 1 - slot)[0]()

    route = route_ref[...]
    w1, w2 = route[:, ROUTE_W1:ROUTE_W1 + 1], route[:, ROUTE_W2:ROUTE_W2 + 1]
    y = w1 * ybuf[slot, 0:tile, :] + w2 * ybuf[slot, tile:2 * tile, :]
    o_ref[...] = _rmsnorm(x_ref[...] + y, gf_ref[...])


def _moe_combine(x, route, g_final, ys, slots, *, tile):
    N, D = x.shape
    assert N % tile == 0
    return pl.pallas_call(
        functools.partial(_moe_combine_kernel, tile=tile),
        grid_spec=pltpu.PrefetchScalarGridSpec(
            num_scalar_prefetch=1,
            grid=(N // tile,),
            in_specs=[pl.BlockSpec((tile, D), lambda i, s: (i, 0)),
                      pl.BlockSpec((tile, LANES), lambda i, s: (i, 0)),
                      pl.BlockSpec((1, D), lambda i, s: (0, 0)),
                      pl.BlockSpec(memory_space=pl.ANY)],
            out_specs=pl.BlockSpec((tile, D), lambda i, s: (i, 0)),
            scratch_shapes=[pltpu.VMEM((2, 2 * tile, D), F32), pltpu.SemaphoreType.DMA((2,))],
        ),
        out_shape=jax.ShapeDtypeStruct((N, D), F32),
        compiler_params=_cparams(("arbitrary",)),
        name="moe_combine",
    )(slots, x, route, g_final, ys)


def _moe_prompt(x, g, w_router, wg, wu, wd, g_final, *, tile=EXPERT_ROW_TILE, ctile=COMBINE_ROW_TILE):
    N, D = x.shape
    E = wg.shape[0]
    route, counts = _moe_route(x, g, w_router, E)
    cnt = counts[0, :E].astype(jnp.int32)
    padded = (cnt + tile - 1) // tile * tile
    ends = jnp.cumsum(padded)
    off = ends - padded
    i1, i2 = route[:, ROUTE_I1].astype(jnp.int32), route[:, ROUTE_I2].astype(jnp.int32)
    slot1 = off[i1] + route[:, ROUTE_R1].astype(jnp.int32)
    slot2 = off[i2] + route[:, ROUTE_R2].astype(jnp.int32)
    n_slots = (TOP_K * N + E * (tile - 1)) // tile * tile
    rows = jnp.arange(N, dtype=jnp.int32)
    token_of_slot = jnp.zeros((n_slots + tile,), jnp.int32).at[jnp.concatenate([slot1, slot2])].set(
        jnp.concatenate([rows, rows]), unique_indices=True)
    n_used = ends[-1:] // tile
    tile_start = jnp.minimum(jnp.arange(n_slots // tile, dtype=jnp.int32), n_used[0] - 1) * tile
    tile_expert = jnp.sum(tile_start[:, None] >= ends[None, :], axis=1).astype(jnp.int32)
    ys = _moe_experts(x, g, wg, wu, wd, tile_expert, n_used.astype(jnp.int32), token_of_slot, tile=tile)
    slots = jnp.stack([slot1.reshape(-1, ctile), slot2.reshape(-1, ctile)], axis=1).reshape(-1)
    return _moe_combine(x, route, g_final, ys, slots, tile=ctile)


def _nsa_proj_kernel(x_ref, g_ref, wq_ref, wkv_ref, wgt_ref, wck_ref, wcv_ref,
                     q_ref, kvt_ref, ks_ref, vs_ref, kw_ref, vw_ref, gt_ref, kc_ref, vc_ref):
    hb = _rmsnorm(x_ref[...], g_ref[...]).astype(BF16)
    q_ref[...] = _dot(hb, wq_ref[...]).astype(BF16)
    kv = _dot(hb, wkv_ref[...])
    kvt_ref[0] = kv.T
    ks_ref[...] = kv[:, 2 * KV_W:3 * KV_W].astype(BF16)
    vs_ref[...] = kv[:, 3 * KV_W:4 * KV_W].astype(BF16)
    kw_ref[...] = kv[:, 4 * KV_W:5 * KV_W].astype(BF16)
    vw_ref[...] = kv[:, 5 * KV_W:].astype(BF16)
    gt_ref[...] = jax.nn.sigmoid(_dot(hb, wgt_ref[...]))
    tm = kv.shape[0]
    kc_ref[...] = (kv[:, :KV_W] * wck_ref[...]).reshape(tm // CMP_BLOCK, CMP_BLOCK, KV_W).sum(axis=1)
    vc_ref[...] = (kv[:, KV_W:2 * KV_W] * wcv_ref[...]).reshape(tm // CMP_BLOCK, CMP_BLOCK, KV_W).sum(axis=1)


def _nsa_proj(x, g, wq, wkv, wgt, wck, wcv, *, tm, seq):
    N, D = x.shape
    assert N % tm == 0 and tm % (8 * CMP_BLOCK) == 0 and seq % tm == 0 and N % seq == 0
    per_seq = seq // tm
    row = lambda w: pl.BlockSpec((tm, w), lambda i: (i, 0))
    full = lambda a: pl.BlockSpec(a.shape, lambda i: (0, 0))
    out_shape = [jax.ShapeDtypeStruct((N, N_HEADS * HEAD_DIM), BF16),
                 jax.ShapeDtypeStruct((N // seq, 6 * KV_W, seq), F32)]
    out_shape += [jax.ShapeDtypeStruct((N, KV_W), BF16)] * 4
    out_shape += [jax.ShapeDtypeStruct((N, LANES), F32)]
    out_shape += [jax.ShapeDtypeStruct((N // CMP_BLOCK, KV_W), F32)] * 2
    out_specs = [row(N_HEADS * HEAD_DIM),
                 pl.BlockSpec((1, 6 * KV_W, tm), lambda i: (i // per_seq, 0, i % per_seq))]
    out_specs += [row(KV_W)] * 4 + [row(LANES)]
    out_specs += [pl.BlockSpec((tm // CMP_BLOCK, KV_W), lambda i: (i, 0))] * 2
    return pl.pallas_call(
        _nsa_proj_kernel,
        grid=(N // tm,),
        in_specs=[row(D), full(g), full(wq), full(wkv), full(wgt), full(wck), full(wcv)],
        out_specs=out_specs,
        out_shape=out_shape,
        compiler_params=_cparams(("parallel",)),
        name="nsa_proj_prompt",
    )(x, g, wq, wkv, wgt, wck, wcv)


def _nsa_proj_sample_kernel(x_ref, g_ref, w_ref, z_ref):
    z_ref[...] = _dot3(_rmsnorm(x_ref[...], g_ref[...]), w_ref[...])


def _nsa_proj_sample(x, g, w_in, tn=SAMPLE_PROJ_COL_TILE):
    B, D = x.shape
    cols = w_in.shape[1]
    assert cols % tn == 0
    return pl.pallas_call(
        _nsa_proj_sample_kernel,
        grid=(cols // tn,),
        in_specs=[pl.BlockSpec((B, D), lambda j: (0, 0)), pl.BlockSpec((1, D), lambda j: (0, 0)),
                  pl.BlockSpec((D, tn), lambda j: (0, j))],
        out_specs=pl.BlockSpec((B, tn), lambda j: (0, j)),
        out_shape=jax.ShapeDtypeStruct((B, cols), F32),
        compiler_params=_cparams(("parallel",)),
        name="nsa_proj_sample",
    )(x, g, w_in)


def _nsa_prompt_kernel(q_ref, gt_ref, kc_ref, vc_ref, ks_ref, vs_ref, kw_ref, vw_ref, x_ref, wo_ref,
                       o_ref, o_scr, sel_scr, *, tq, kchunk, seq):
    t0 = pl.program_id(1) * tq
    rows = GQA * tq
    scale = HEAD_DIM ** -0.5
    assert scale == 0.125
    n_blocks = seq // SLC_BLOCK
    top_k = min(SLC_TOPK, n_blocks)

    q = q_ref[0]
    gates = gt_ref[0]
    qpos_t = t0 + lax.broadcasted_iota(jnp.int32, (tq, 1), 0)
    qpos = jnp.concatenate([qpos_t] * GQA, axis=0)
    tq_f = qpos.astype(F32)

    lane = lax.broadcasted_iota(jnp.int32, (tq, LANES), 1)
    jblk = lane >> 1
    real = ((lane & 1) == 0) & (jblk < n_blocks)
    blk = qpos_t // SLC_BLOCK
    valid = real & (jblk <= blk)
    forced = (jblk == 0) | (jblk == blk) | (jblk == blk - 1)
    valid_f = jnp.where(valid, 1.0, 0.0)
    valid_b = valid_f.astype(BF16)
    bonus = jnp.where(forced, FORCE_BONUS, 0.0)
    floor = jnp.where(valid, 0.0, jnp.where(real, -1.0, -2.0))
    c_lane = lax.broadcasted_iota(jnp.int32, (1, LANES), 1)
    c_mid = c_lane.astype(F32) * CMP_BLOCK + (CMP_BLOCK - 1) / 2
    c_end = (c_lane + 1) * CMP_BLOCK - 1

    n_chunks = (t0 + tq + kchunk - 1) // kchunk
    wstart = pl.multiple_of(jnp.maximum(t0 - WINDOW, 0), tq)
    wlen = WINDOW + tq

    need_rank = (t0 + tq - 1) // SLC_BLOCK >= top_k
    wpos = wstart + lax.broadcasted_iota(jnp.int32, (1, wlen), 1)
    w_dist = jnp.where((wpos <= qpos_t) & (qpos_t - wpos < WINDOW), (wpos - qpos_t).astype(F32), NEG)

    def with_ones(v):
        return jnp.concatenate([v, jnp.ones_like(v)], axis=1)

    def biased(scores, dist, kv):
        return jnp.concatenate(
            [scores[g * tq:(g + 1) * tq] + _alibi_slope(kv * GQA + g) * dist for g in range(GQA)], axis=0)

    for kv in range(N_KV_HEADS):
        hs = slice(kv * HEAD_DIM, (kv + 1) * HEAD_DIM)
        qs = jnp.concatenate(
            [q[:, (kv * GQA + g) * HEAD_DIM:(kv * GQA + g + 1) * HEAD_DIM] for g in range(GQA)], axis=0) * scale
        slope = _head_slopes(rows, tq, kv * GQA)

        s_c = _dot_nt(qs, kc_ref[0][:, hs]) - slope * (tq_f - c_mid)
        p_c = _masked_softmax(s_c, c_end <= qpos)
        o_c = _dot(p_c.astype(BF16), vc_ref[0][:, hs])

        imp = p_c[0:tq]
        for g in range(1, GQA):
            imp = imp + p_c[g * tq:(g + 1) * tq]
        sel_scr[...] = valid_b

        @pl.when(need_rank)
        def _():
            pair = imp + pltpu.roll(imp, LANES - 1, 1)
            score = valid_f * (pair + bonus) + floor
            score_t = score.T[:2 * n_blocks]
            row_id = lax.broadcasted_iota(jnp.int32, score_t.shape, 0)
            rank_t = jnp.zeros(score_t.shape, jnp.int32)
            for k in range(n_blocks):
                other = score_t[2 * k:2 * k + 1, :]
                beats = (other > score_t) | ((other == score_t) & (row_id > 2 * k))
                rank_t = rank_t + beats.astype(jnp.int32)
            top_t = jnp.where(rank_t < top_k, 1.0, 0.0)
            top = jnp.concatenate([top_t, jnp.zeros((LANES - 2 * n_blocks, tq), F32)], axis=0).T
            sel_scr[...] = (top * valid_f).astype(BF16)

        sel = sel_scr[...]

        def chunk(c, carry):
            m, acc = carry
            k0 = pl.multiple_of(c * kchunk, kchunk)
            kb = ks_ref[0, pl.ds(k0, kchunk), hs]
            vb = with_ones(vs_ref[0, pl.ds(k0, kchunk), hs])
            pos = k0 + lax.broadcasted_iota(jnp.int32, (1, kchunk), 1)
            expand = (lax.broadcasted_iota(jnp.int32, (LANES, kchunk), 0)
                      == 2 * ((k0 + lax.broadcasted_iota(jnp.int32, (LANES, kchunk), 1)) // SLC_BLOCK))
            picked = _dot(sel, jnp.where(expand, 1.0, 0.0).astype(BF16))
            ok = (picked > 0.5) & (pos <= qpos_t)
            s = biased(_dot_nt(qs, kb), jnp.where(ok, (pos - qpos_t).astype(F32), NEG), kv)
            m_new = jnp.maximum(m, jnp.max(s, axis=-1, keepdims=True))
            p = jnp.exp(s - m_new)
            acc = jnp.exp(m - m_new) * acc + _dot(p.astype(BF16), vb)
            return m_new, acc

        init = (jnp.full((rows, 1), NEG, F32), jnp.zeros((rows, 2 * HEAD_DIM), F32))
        _, acc_s = lax.fori_loop(0, n_chunks, chunk, init)
        o_s = acc_s[:, :HEAD_DIM] / acc_s[:, HEAD_DIM:HEAD_DIM + 1]

        kb = kw_ref[0, pl.ds(wstart, wlen), hs]
        vb = with_ones(vw_ref[0, pl.ds(wstart, wlen), hs])
        s_w = biased(_dot_nt(qs, kb), w_dist, kv)
        p_w = jnp.exp(s_w - jnp.max(s_w, axis=-1, keepdims=True))
        pv = _dot(p_w.astype(BF16), vb)
        o_w = pv[:, :HEAD_DIM] / pv[:, HEAD_DIM:HEAD_DIM + 1]

        def gate_col(n):
            base = n * N_HEADS + kv * GQA
            return jnp.concatenate([gates[:, base + g:base + g + 1] for g in range(GQA)], axis=0)

        o = gate_col(0) * o_c + gate_col(1) * o_s + gate_col(2) * o_w
        for g in range(GQA):
            h = kv * GQA + g
            o_scr[:, h * HEAD_DIM:(h + 1) * HEAD_DIM] = o[g * tq:(g + 1) * tq]

    o_ref[0] = x_ref[0] + _dot(o_scr[...].astype(BF16), wo_ref[...])


def _nsa_prompt(q, gates, kc, vc, ks, vs, kw, vw, x, w_o, *, tq=ATTN_Q_TILE, kchunk=ATTN_KEY_CHUNK):
    B, L, D = x.shape
    assert L % kchunk == 0 and kchunk % tq == 0 and L >= WINDOW + tq and L % SLC_BLOCK == 0
    assert 2 * (L // SLC_BLOCK) <= LANES and L // CMP_BLOCK <= LANES and kc.shape[1] == LANES
    tile = lambda w: pl.BlockSpec((1, tq, w), lambda b, t: (b, t, 0))
    whole = lambda a: pl.BlockSpec((1,) + a.shape[1:], lambda b, t: (b, 0, 0))
    return pl.pallas_call(
        functools.partial(_nsa_prompt_kernel, tq=tq, kchunk=kchunk, seq=L),
        grid=(B, L // tq),
        in_specs=[tile(N_HEADS * HEAD_DIM), tile(LANES), whole(kc), whole(vc), whole(ks), whole(vs),
                  whole(kw), whole(vw), tile(D), pl.BlockSpec(w_o.shape, lambda b, t: (0, 0))],
        out_specs=tile(D),
        out_shape=jax.ShapeDtypeStruct((B, L, D), F32),
        scratch_shapes=[pltpu.VMEM((tq, N_HEADS * HEAD_DIM), F32), pltpu.VMEM((tq, LANES), BF16)],
        compiler_params=_cparams(("parallel", "arbitrary")),
        name="nsa_attn_prompt",
    )(q, gates, kc, vc, ks, vs, kw, vw, x, w_o)


CMP_GROUP = 32
MXU_DEPTH = 256


def _cmp_sample_kernel(pt_ref, cache_ref, w_ref, s_ref, o_ref, buf, sems, *, rows):
    b, g = pl.program_id(0), pl.program_id(1)
    n_groups = pl.num_programs(1)
    step = b * n_groups + g
    slot = step % 2

    def page_copy(bb, gg, sl, i):
        page = pt_ref[bb, gg * CMP_GROUP + i]
        return pltpu.make_async_copy(cache_ref.at[page, pl.ds(0, rows), :], buf.at[sl, i], sems.at[sl])

    def start_group(bb, gg, sl):
        for i in range(CMP_GROUP):
            page_copy(bb, gg, sl, i).start()

    @pl.when(step == 0)
    def _():
        start_group(b, g, slot)

    @pl.when(step + 1 < pl.num_programs(0) * n_groups)
    def _():
        nxt = step + 1
        start_group(nxt // n_groups, nxt % n_groups, 1 - slot)

    for i in range(CMP_GROUP):
        page_copy(b, g, slot, i).wait()

    w = w_ref[...]
    acc = jnp.zeros((rows, LANES), F32)
    for j in range(CMP_GROUP // 2):
        p = jnp.concatenate([buf[slot, 2 * j] * w, buf[slot, 2 * j + 1] * w], axis=1)
        hi, lo = _split(p)
        r = _dot(jnp.concatenate([hi, lo], axis=0), s_ref[j])
        acc = acc + (r[:rows] + r[rows:])
    o_ref[0] = acc


def _cmp_sample(page_table, cache_t, w_rows):
    B, n_pages = page_table.shape
    _, _, page = cache_t.shape
    rows = w_rows.shape[0]
    per_page = page // CMP_BLOCK
    assert n_pages % CMP_GROUP == 0 and CMP_GROUP * per_page == LANES and 2 * page == MXU_DEPTH
    k = jnp.arange(2 * page)
    token = (2 * jnp.arange(CMP_GROUP // 2)[:, None] + k[None, :] // page) * per_page + (k[None, :] % page) // CMP_BLOCK
    block_sum = (token[:, :, None] == jnp.arange(LANES)[None, None, :]).astype(BF16)
    return pl.pallas_call(
        functools.partial(_cmp_sample_kernel, rows=rows),
        grid_spec=pltpu.PrefetchScalarGridSpec(
            num_scalar_prefetch=1,
            grid=(B, n_pages // CMP_GROUP),
            in_specs=[pl.BlockSpec(memory_space=pl.ANY),
                      pl.BlockSpec(w_rows.shape, lambda b, g, pt: (0, 0)),
                      pl.BlockSpec(block_sum.shape, lambda b, g, pt: (0, 0, 0))],
            out_specs=pl.BlockSpec((1, rows, LANES), lambda b, g, pt: (b, 0, g)),
            scratch_shapes=[pltpu.VMEM((2, CMP_GROUP, rows, page), F32), pltpu.SemaphoreType.DMA((2,))],
        ),
        out_shape=jax.ShapeDtypeStruct((B, rows, n_pages * per_page), F32),
        compiler_params=_cparams(("arbitrary", "arbitrary")),
        name="cmp_sample",
    )(page_table, cache_t, w_rows, block_sum)


def _select_sample_kernel(q_ref, kcv_ref, oc_ref, idx_ref, *, past):
    scale = HEAD_DIM ** -0.5
    q = q_ref[0]
    nc = kcv_ref.shape[2]
    n_past_blocks = past // SLC_BLOCK
    per_block = SLC_BLOCK // CMP_BLOCK
    row = lax.broadcasted_iota(jnp.int32, (N_HEADS, 1), 0)
    slope = _head_slopes(N_HEADS, 1, 0)
    tok = lax.broadcasted_iota(jnp.int32, (1, nc), 1)
    c_mid = tok.astype(F32) * CMP_BLOCK + (CMP_BLOCK - 1) / 2
    c_ok = (tok + 1) * CMP_BLOCK - 1 <= past
    bias = slope * (float(past) - c_mid)

    s_c = jnp.zeros((N_HEADS, nc), F32)
    for kv in range(N_KV_HEADS):
        s_kv = _dot3(q, kcv_ref[0, kv * HEAD_DIM:(kv + 1) * HEAD_DIM, :])
        s_c = jnp.where(row // GQA == kv, s_kv, s_c)
    p_c = _masked_softmax(s_c * scale - bias, c_ok)
    o_c = jnp.zeros((N_HEADS, HEAD_DIM), F32)
    for kv in range(N_KV_HEADS):
        o_kv = _dot3(p_c, kcv_ref[0, KV_W + kv * HEAD_DIM:KV_W + (kv + 1) * HEAD_DIM, :], nt=True)
        o_c = jnp.where(row // GQA == kv, o_kv, o_c)
    oc_ref[0] = o_c

    imp = p_c
    shift = 1
    while shift < GQA:
        imp = imp + pltpu.roll(imp, shift, 0)
        shift *= 2
    assert per_block == 2
    pair = imp + pltpu.roll(imp, nc - 1, 1)
    jblk = tok // per_block
    real = tok % per_block == 0
    forced = (jblk == 0) | (jblk == n_past_blocks - 1)
    score = jnp.where(real, pair + jnp.where(forced, FORCE_BONUS, 0.0), -2.0)
    own = jnp.float32(FORCE_BONUS)
    rank = (own > score).astype(jnp.int32)
    for k in range(n_past_blocks):
        col = score[:, per_block * k:per_block * k + 1]
        beats = (col > score) | ((col == score) & (jblk > k))
        rank = rank + beats.astype(jnp.int32)
    own_rank = jnp.sum(jnp.where(real & (score >= own), 1, 0), axis=-1, keepdims=True)
    out_lane = lax.broadcasted_iota(jnp.int32, (N_HEADS, LANES), 1)
    idx = jnp.zeros((N_HEADS, LANES), jnp.int32)
    for r in range(SLC_TOPK):
        hit = jnp.sum(jnp.where(real & (rank == r), jblk, 0), axis=-1, keepdims=True)
        hit = hit + jnp.where(own_rank == r, n_past_blocks, 0)
        idx = jnp.where(out_lane == r, hit, idx)
    idx_ref[0] = idx


def _select_sample(q16, kcv, past):
    B = q16.shape[0]
    nc = kcv.shape[2]
    assert past % SLC_BLOCK == 0 and nc == past // CMP_BLOCK and nc % LANES == 0
    assert past // SLC_BLOCK + 1 > SLC_TOPK
    return pl.pallas_call(
        functools.partial(_select_sample_kernel, past=past),
        grid=(B,),
        in_specs=[pl.BlockSpec((1,) + q16.shape[1:], lambda b: (b, 0, 0)),
                  pl.BlockSpec((1,) + kcv.shape[1:], lambda b: (b, 0, 0))],
        out_specs=[pl.BlockSpec((1, N_HEADS, HEAD_DIM), lambda b: (b, 0, 0)),
                   pl.BlockSpec((1, N_HEADS, LANES), lambda b: (b, 0, 0))],
        out_shape=[jax.ShapeDtypeStruct((B, N_HEADS, HEAD_DIM), F32),
                   jax.ShapeDtypeStruct((B, N_HEADS, LANES), jnp.int32)],
        compiler_params=_cparams(("parallel",)),
        name="select_sample",
    )(q16, kcv)


def _attend_sample_kernel(pt_ref, idx_ref, q_ref, oc_ref, gt_ref, new_ref, win_ref, cache_ref,
                          o_ref, kbuf, vbuf, sems, *, past):
    b = pl.program_id(0)
    scale = HEAD_DIM ** -0.5
    n_past_blocks = past // SLC_BLOCK
    page_rows = cache_ref.shape[2]
    blocks_per_page = page_rows // SLC_BLOCK
    nkeys = SLC_TOPK * page_rows

    def block_copies(kv, n):
        blk = idx_ref[(b * N_KV_HEADS + kv) * SLC_TOPK + n]
        in_past = blk < n_past_blocks
        page = pt_ref[b, jnp.minimum(blk, n_past_blocks - 1) // blocks_per_page]
        dst = pl.ds(n * page_rows, page_rows)
        ck = pltpu.make_async_copy(cache_ref.at[page, pl.ds((2 * N_KV_HEADS + kv) * HEAD_DIM, HEAD_DIM), :],
                                   kbuf.at[kv, :, dst], sems.at[0, kv, n])
        cv = pltpu.make_async_copy(cache_ref.at[page, pl.ds((3 * N_KV_HEADS + kv) * HEAD_DIM, HEAD_DIM), :],
                                   vbuf.at[kv, :, dst], sems.at[1, kv, n])
        return blk, in_past, ck, cv

    for kv in range(N_KV_HEADS):
        for n in range(SLC_TOPK):
            _, in_past, ck, cv = block_copies(kv, n)

            @pl.when(in_past)
            def _():
                ck.start()
                cv.start()

            @pl.when(jnp.logical_not(in_past))
            def _():
                kbuf[kv, :, n * page_rows:(n + 1) * page_rows] = jnp.zeros((HEAD_DIM, page_rows), F32)
                vbuf[kv, :, n * page_rows:(n + 1) * page_rows] = jnp.zeros((HEAD_DIM, page_rows), F32)

    q = q_ref[0]
    row = lax.broadcasted_iota(jnp.int32, (N_HEADS, 1), 0)
    slope = _head_slopes(N_HEADS, 1, 0)
    key_lane = lax.broadcasted_iota(jnp.int32, (1, nkeys), 1)

    wb = win_ref.shape[2]
    w_lane = lax.broadcasted_iota(jnp.int32, (1, wb), 1)
    wpos = past - wb + w_lane
    w_ok = (past - wpos < WINDOW) & (wpos >= 0)
    w_bias = slope * (past - wpos).astype(F32)
    s_w = jnp.zeros((N_HEADS, wb), F32)
    s_n = jnp.zeros((N_HEADS, 1), F32)
    for kv in range(N_KV_HEADS):
        mine = row // GQA == kv
        s_kv = _dot3(q, win_ref[0, kv * HEAD_DIM:(kv + 1) * HEAD_DIM, :])
        s_w = jnp.where(mine, s_kv, s_w)
        s_n = jnp.where(mine, jnp.sum(q * new_ref[0, 4, kv:kv + 1, :], axis=-1, keepdims=True), s_n)
    s_w = jnp.where(w_ok, s_w * scale - w_bias, NEG)
    s_n = s_n * scale
    m_w = jnp.maximum(jnp.max(s_w, axis=-1, keepdims=True), s_n)
    p_w = jnp.where(w_ok, jnp.exp(s_w - m_w), 0.0)
    p_n = jnp.exp(s_n - m_w)
    l_w = jnp.maximum(jnp.sum(p_w, axis=-1, keepdims=True) + p_n, 1e-30)
    p_w = p_w / l_w
    p_n = p_n / l_w
    o_w = jnp.zeros((N_HEADS, HEAD_DIM), F32)
    for kv in range(N_KV_HEADS):
        o_kv = (_dot3(p_w, win_ref[0, KV_W + kv * HEAD_DIM:KV_W + (kv + 1) * HEAD_DIM, :], nt=True)
                + p_n * new_ref[0, 5, kv:kv + 1, :])
        o_w = jnp.where(row // GQA == kv, o_kv, o_w)

    for kv in range(N_KV_HEADS):
        for n in range(SLC_TOPK):
            _, in_past, ck, cv = block_copies(kv, n)

            @pl.when(in_past)
            def _():
                ck.wait()
                cv.wait()

    s_s = jnp.zeros((N_HEADS, nkeys), F32)
    pos = jnp.zeros((N_HEADS, nkeys), jnp.int32)
    live = jnp.zeros((N_HEADS, nkeys), jnp.int32)
    own = jnp.zeros((N_HEADS, 1), jnp.int32)
    s_n = jnp.zeros((N_HEADS, 1), F32)
    in_slab = key_lane % page_rows
    for kv in range(N_KV_HEADS):
        mine = row // GQA == kv
        s_kv = _dot3(q, kbuf[kv])
        s_s = jnp.where(mine, s_kv, s_s)
        s_n = jnp.where(mine, jnp.sum(q * new_ref[0, 2, kv:kv + 1, :], axis=-1, keepdims=True), s_n)
        base_kv = jnp.zeros((1, nkeys), jnp.int32)
        half_kv = jnp.zeros((1, nkeys), jnp.int32)
        own_kv = jnp.int32(0)
        for n in range(SLC_TOPK):
            blk = idx_ref[(b * N_KV_HEADS + kv) * SLC_TOPK + n]
            here = key_lane // page_rows == n
            base_kv = jnp.where(here, (blk // blocks_per_page) * page_rows, base_kv)
            half_kv = jnp.where(here, blk % blocks_per_page, half_kv)
            own_kv = own_kv + (blk >= n_past_blocks).astype(jnp.int32)
        pos = jnp.where(mine, base_kv + in_slab, pos)
        live = jnp.where(mine, (in_slab // SLC_BLOCK == half_kv).astype(jnp.int32), live)
        own = jnp.where(mine, own_kv, own)
    s_ok = (live > 0) & (pos < past)
    n_ok = own > 0
    s_s = jnp.where(s_ok, s_s * scale - slope * (past - pos).astype(F32), NEG)
    s_n = jnp.where(n_ok, s_n * scale, NEG)
    m_s = jnp.maximum(jnp.max(s_s, axis=-1, keepdims=True), s_n)
    p_s = jnp.where(s_ok, jnp.exp(s_s - m_s), 0.0)
    p_n = jnp.where(n_ok, jnp.exp(s_n - m_s), 0.0)
    l_s = jnp.maximum(jnp.sum(p_s, axis=-1, keepdims=True) + p_n, 1e-30)
    p_s = p_s / l_s
    p_n = p_n / l_s
    o_s = jnp.zeros((N_HEADS, HEAD_DIM), F32)
    for kv in range(N_KV_HEADS):
        o_kv = _dot3(p_s, vbuf[kv], nt=True) + p_n * new_ref[0, 3, kv:kv + 1, :]
        o_s = jnp.where(row // GQA == kv, o_kv, o_s)

    gt = jax.nn.sigmoid(gt_ref[0])
    o_ref[0] = gt[:, 0:1] * oc_ref[0] + gt[:, 1:2] * o_s + gt[:, 2:3] * o_w


def _attend_sample(page_table, idx, q16, o_c, gates, kv_new, win_t, cache_t, past):
    B = q16.shape[0]
    page_rows = cache_t.shape[2]
    assert win_t.shape[2] <= WINDOW and page_rows % SLC_BLOCK == 0
    blk3 = lambda a: pl.BlockSpec((1,) + a.shape[1:], lambda b, pt, ix: (b,) + (0,) * (a.ndim - 1))
    return pl.pallas_call(
        functools.partial(_attend_sample_kernel, past=past),
        grid_spec=pltpu.PrefetchScalarGridSpec(
            num_scalar_prefetch=2,
            grid=(B,),
            in_specs=[blk3(q16), blk3(o_c), blk3(gates), blk3(kv_new), blk3(win_t),
                      pl.BlockSpec(memory_space=pl.ANY)],
            out_specs=pl.BlockSpec((1, N_HEADS, HEAD_DIM), lambda b, pt, ix: (b, 0, 0)),
            scratch_shapes=[pltpu.VMEM((N_KV_HEADS, HEAD_DIM, SLC_TOPK * page_rows), F32),
                            pltpu.VMEM((N_KV_HEADS, HEAD_DIM, SLC_TOPK * page_rows), F32),
                            pltpu.SemaphoreType.DMA((2, N_KV_HEADS, SLC_TOPK))],
        ),
        out_shape=jax.ShapeDtypeStruct((B, N_HEADS, HEAD_DIM), F32),
        compiler_params=_cparams(("arbitrary",)),
        name="attend_sample",
    )(page_table, idx, q16, o_c, gates, kv_new, win_t, cache_t)


def _linear_res_kernel(x_ref, a_ref, w_ref, o_ref):
    o_ref[...] = x_ref[...] + _dot3(a_ref[...], w_ref[...])


def _linear_res(x, a, w):
    return pl.pallas_call(
        _linear_res_kernel,
        out_shape=jax.ShapeDtypeStruct(x.shape, F32),
        compiler_params=pltpu.CompilerParams(vmem_limit_bytes=VMEM_LIMIT),
        name="out_proj_sample",
    )(x, a, w)


def _expand_cmp_weights(w_cmp, rows):
    w = jnp.repeat(w_cmp, HEAD_DIM, axis=1)
    return jnp.tile(w, (rows // CMP_BLOCK, 1))


def _row(v):
    return v.reshape(1, -1)


def _prompt_mixers(x, norm_mix, w_pool, pool_scale, w_in, w_cmp_k, w_cmp_v, w_out, norm_ffn0, wfg, wfu, wfd):
    B, L, D = x.shape
    q_w = N_HEADS * HEAD_DIM
    x, h_last = _pool_prompt(x, _row(norm_mix[0]), w_pool.astype(BF16), _row(pool_scale))
    new_pool = h_last[None, :, POOL_HALO - (max(POOL_WINDOWS) - 1):, :]
    x = _ffn(x.reshape(B * L, D), _row(norm_ffn0), wfg.astype(BF16), wfu.astype(BF16), wfd.astype(BF16),
             tm=FFN_ROW_TILE)

    tm = PROJ_ROW_TILE
    wq = w_in[:, :q_w].astype(BF16)
    wkv = w_in[:, q_w:q_w + 6 * KV_W].astype(BF16)
    wgt = jnp.pad(w_in[:, q_w + 6 * KV_W:], ((0, 0), (0, LANES - N_BRANCH * N_HEADS))).astype(BF16)
    q, kv_t, ks, vs, kw, vw, gates, kc, vc = _nsa_proj(
        x, _row(norm_mix[1]), wq, wkv, wgt, _expand_cmp_weights(w_cmp_k, tm), _expand_cmp_weights(w_cmp_v, tm),
        tm=tm, seq=L)
    kv_t = kv_t.reshape(B, 6, N_KV_HEADS, HEAD_DIM, L)
    new_kv = kv_t[:, :4].transpose(0, 4, 1, 2, 3)[None]
    new_win = kv_t[:, 4:, :, :, L - min(WINDOW, L):].transpose(0, 4, 1, 2, 3)[None]
    nc = L // CMP_BLOCK
    pad_c = lambda a: jnp.pad(a.reshape(B, nc, KV_W), ((0, 0), (0, LANES - nc), (0, 0))).astype(BF16)
    per_seq = lambda a: a.reshape(B, L, a.shape[-1])
    x = _nsa_prompt(per_seq(q), per_seq(gates), pad_c(kc), pad_c(vc), per_seq(ks), per_seq(vs),
                    per_seq(kw), per_seq(vw), per_seq(x), w_out.astype(BF16))
    return x.reshape(B * L, D), new_pool, new_kv, new_win


def _sample_mixers(x, state_pool, cache, state_win, page_table, norm_mix, w_pool, pool_scale, w_in, w_cmp_k,
                   w_cmp_v, w_out, norm_ffn0, wfg, wfu, wfd):
    SB, D = x.shape
    n_phys, page = cache.shape[:2]
    past = page_table.shape[1] * page
    q_w = N_HEADS * HEAD_DIM
    x, h = _pool_sample(x, state_pool.transpose(1, 0, 2), _row(norm_mix[0]), w_pool, _row(pool_scale))
    new_pool = jnp.concatenate([state_pool[:, 1:], h[:, None]], axis=1)[None]
    x = _ffn(x, _row(norm_ffn0), wfg, wfu, wfd, tm=SB)

    cols = w_in.shape[1]
    z = _nsa_proj_sample(x, _row(norm_mix[1]), jnp.pad(w_in, ((0, 0), (0, -cols % LANES))))
    kv_new = z[:, q_w:q_w + 6 * KV_W].reshape(SB, 6, N_KV_HEADS, HEAD_DIM)
    new_kv = kv_new[None, :, None, :4]
    keep = min(WINDOW, state_win.shape[1] + 1)
    new_win = jnp.concatenate([state_win, kv_new[:, None, 4:]], axis=1)[None, :, -keep:]

    cache_t = cache.transpose(0, 2, 3, 4, 1).reshape(n_phys, 4 * KV_W, page)
    win_t = state_win.transpose(0, 2, 3, 4, 1).reshape(SB, 2 * KV_W, state_win.shape[1])
    w_rows = jnp.concatenate([_expand_cmp_weights(w_cmp_k, page).T, _expand_cmp_weights(w_cmp_v, page).T], axis=0)
    kcv = _cmp_sample(page_table, cache_t, w_rows)
    q16 = z[:, :q_w].reshape(SB, N_HEADS, HEAD_DIM)
    o_c, idx = _select_sample(q16, kcv, past)
    idx = idx[:, GQA - 1::GQA, :SLC_TOPK].reshape(-1)
    gate_logits = z[:, q_w + 6 * KV_W:cols].reshape(SB, N_BRANCH, N_HEADS).transpose(0, 2, 1)
    gate_logits = jnp.pad(gate_logits, ((0, 0), (0, 0), (0, LANES - N_BRANCH)))
    o = _attend_sample(page_table, idx, q16, o_c, gate_logits, kv_new, win_t, cache_t, past)
    return _linear_res(x, o.reshape(SB, q_w), w_out), new_pool, new_kv, new_win


def kernel(x_prompt, x_sample, state_pool, cache_kv, state_win, page_table, norm_mix, w_pool, pool_scale,
           w_nsa_in, w_cmp_k, w_cmp_v, w_nsa_out, norm_ffn, w_ffn_gate, w_ffn_up, w_ffn_down, w_router,
           w_moe_gate, w_moe_up, w_moe_down, norm_final):
    assert x_sample.shape[1] == 1 and norm_mix.shape[0] == 2 and x_prompt.shape[-1] == N_HEADS * HEAD_DIM
    w_rt = jnp.pad(w_router[0], ((0, 0), (0, LANES - w_router.shape[-1])))
    moe_w = (w_moe_gate[0].astype(BF16), w_moe_up[0].astype(BF16), w_moe_down[0].astype(BF16))

    xp, new_pool_prompt, new_kv_prompt, new_win_prompt = _prompt_mixers(
        x_prompt, norm_mix, w_pool[0], pool_scale[0], w_nsa_in[0], w_cmp_k[0], w_cmp_v[0], w_nsa_out[0],
        norm_ffn[0], w_ffn_gate, w_ffn_up, w_ffn_down)
    xs, new_pool_sample, new_kv_sample, new_win_sample = _sample_mixers(
        x_sample[:, 0], state_pool[0], cache_kv[0], state_win[0], page_table, norm_mix, w_pool[0], pool_scale[0],
        w_nsa_in[0], w_cmp_k[0], w_cmp_v[0], w_nsa_out[0], norm_ffn[0], w_ffn_gate, w_ffn_up, w_ffn_down)

    yp = _moe_prompt(xp, _row(norm_ffn[1]), w_rt, *moe_w, _row(norm_final))
    ys = _ffn(xs, _row(norm_ffn[1]), *moe_w, w_rt, _row(norm_final), tm=xs.shape[0], tf=SAMPLE_FFN_COL_TILE)
    return (yp.reshape(x_prompt.shape), ys.reshape(x_sample.shape), new_pool_prompt, new_pool_sample,
            new_kv_prompt, new_kv_sample, new_win_prompt, new_win_sample)
```
